```python
import math
import jax, jax.numpy as jnp
from jax import lax
import numpy as np

D_MODEL = 4096
BATCH = 16
SEQ = 2048
DEPTH = 1

D_FF = 11008
ATTN_HEAD_DIM = 128
N_ATTN_HEADS = D_MODEL // (2 * ATTN_HEAD_DIM)
D_ATTN = N_ATTN_HEADS * ATTN_HEAD_DIM
DILATED_CONFIGS = ((128, 1), (512, 4), (2048, 16))
ATTN_BLOCK = 128
DN_HEAD_DIM = 128
N_DN_HEADS = D_MODEL // (2 * DN_HEAD_DIM)
D_DN = N_DN_HEADS * DN_HEAD_DIM
CONV_WIDTH = 4
CHUNK = 64
D_MIX = D_ATTN + D_DN
IN_SPLITS = (D_ATTN, D_ATTN, D_ATTN, 3 * D_DN, D_DN, N_DN_HEADS, N_DN_HEADS)
D_IN_PROJ = sum(IN_SPLITS)
EPS = 1e-6

kernel_name = "hymba_dilated_swa_gated_deltanet_macaron"


def _rmsnorm(x, w):
    xf = x.astype(jnp.float32)
    y = xf * lax.rsqrt(jnp.mean(xf * xf, axis=-1, keepdims=True) + EPS)
    return (y * w.astype(jnp.float32)).astype(x.dtype)


def _swiglu(h, w_gate, w_up, w_down):
    return (jax.nn.silu(h @ w_gate) * (h @ w_up)) @ w_down


def _band_attention(q, k, v, steps):
    G, L, H, Dh = q.shape
    nb = -(-L // ATTN_BLOCK)
    lp = nb * ATTN_BLOCK
    qb = jnp.pad(q, ((0, 0), (0, lp - L), (0, 0), (0, 0))).reshape(G, nb, ATTN_BLOCK, H, Dh)

    def band(t):
        t = jnp.pad(t, ((0, 0), (ATTN_BLOCK, lp - L), (0, 0), (0, 0)))
        t = t.reshape(G, nb + 1, ATTN_BLOCK, H, Dh)
        return jnp.concatenate([t[:, :-1], t[:, 1:]], axis=2)

    kw, vw = band(k), band(v)
    s = jnp.einsum('gnqhd,gnkhd->gnhqk', qb, kw, preferred_element_type=jnp.float32) * (Dh ** -0.5)
    i = jnp.arange(ATTN_BLOCK)[:, None]
    j = jnp.arange(2 * ATTN_BLOCK)[None, :]
    dist = i + ATTN_BLOCK - j
    kpos = jnp.arange(nb)[:, None, None] * ATTN_BLOCK - ATTN_BLOCK + j
    valid = (dist >= 0) & (dist <= steps) & (kpos >= 0)
    s = jnp.where(valid[:, None], s, -jnp.inf)
    m = jnp.max(s, axis=-1, keepdims=True)
    p = jnp.exp(s - m)
    den = jnp.sum(p, axis=-1)
    num = jnp.einsum('gnhqk,gnkhd->gnqhd', p, vw.astype(jnp.float32))
    m = m[..., 0].transpose(0, 1, 3, 2).reshape(G, lp, H)[:, :L]
    den = den.transpose(0, 1, 3, 2).reshape(G, lp, H)[:, :L]
    num = num.reshape(G, lp, H, Dh)[:, :L]
    return m, num, den


def _dilated_attention(q, k, v):
    B, S, H, Dh = q.shape
    ms, nums, dens = [], [], []
    for window, d in DILATED_CONFIGS:
        L = S // d

        def to_res(t):
            return t.reshape(B, L, d, H, Dh).transpose(0, 2, 1, 3, 4).reshape(B * d, L, H, Dh)

        def from_res(t):
            rest = t.shape[2:]
            return jnp.swapaxes(t.reshape(B, d, L, *rest), 1, 2).reshape(B, S, *rest)

        m, num, den = _band_attention(to_res(q), to_res(k), to_res(v), window // d)
        ms.append(from_res(m)); nums.append(from_res(num)); dens.append(from_res(den))
    m_all = jnp.stack(ms)
    w = jnp.exp(m_all - jnp.max(m_all, axis=0, keepdims=True))
    num = jnp.sum(w[..., None] * jnp.stack(nums), axis=0)
    den = jnp.sum(w * jnp.stack(dens), axis=0)
    return num / den[..., None]


def _causal_conv(x, w):
    S = x.shape[1]
    xp = jnp.pad(x, ((0, 0), (CONV_WIDTH - 1, 0), (0, 0)))
    return sum(w[i] * xp[:, i:i + S] for i in range(CONV_WIDTH))


def _gated_delta_rule(q, k, v, g, beta):
    B, S, H, Dk = q.shape
    Dv = v.shape[-1]
    N = S // CHUNK

    def chunks(t):
        return jnp.swapaxes(t.reshape(B, N, CHUNK, H, *t.shape[3:]), 2, 3)

    q, k, v, g, beta = (chunks(t.astype(jnp.float32)) for t in (q, k, v, g, beta))
    gc = jnp.cumsum(g, axis=-1)
    idx = jnp.arange(CHUNK)
    incl = idx[:, None] >= idx[None, :]
    strict = idx[:, None] > idx[None, :]
    decay = jnp.exp(jnp.where(incl, gc[..., :, None] - gc[..., None, :], -jnp.inf))
    kb = k * beta[..., None]
    kk = jnp.einsum('bnhid,bnhjd->bnhij', kb, k)
    a = jnp.where(strict, kk * decay, 0.0) + jnp.eye(CHUNK, dtype=jnp.float32)
    rhs = jnp.concatenate([kb * jnp.exp(gc)[..., None], v * beta[..., None]], axis=-1)
    sol = lax.linalg.triangular_solve(a, rhs, left_side=True, lower=True, unit_diagonal=True)
    w_c, u_c = sol[..., :Dk], sol[..., Dk:]
    qk = jnp.einsum('bnhid,bnhjd->bnhij', q, k) * decay
    q_dec = q * jnp.exp(gc)[..., None]
    k_dec = k * jnp.exp(gc[..., -1:] - gc)[..., None]
    g_last = jnp.exp(gc[..., -1])

    def step(state, xs):
        wc, uc, qkc, qdc, kdc, glc = xs
        v_new = uc - jnp.einsum('bhcd,bhdv->bhcv', wc, state)
        o = jnp.einsum('bhcd,bhdv->bhcv', qdc, state) + jnp.einsum('bhij,bhjv->bhiv', qkc, v_new)
        state = state * glc[..., None, None] + jnp.einsum('bhcd,bhcv->bhdv', kdc, v_new)
        return state, o

    xs = tuple(jnp.moveaxis(t, 1, 0) for t in (w_c, u_c, qk, q_dec, k_dec, g_last))
    state0 = jnp.zeros((B, H, Dk, Dv), jnp.float32)
    _, o = lax.scan(step, state0, xs)
    return o.transpose(1, 0, 3, 2, 4).reshape(B, S, H, Dv)


def _hybrid_mixer(h, w_in, conv_w, a_log, dt_bias, dn_norm, w_out):
    B, S, _ = h.shape
    proj = h @ w_in
    cuts = [int(c) for c in np.cumsum(IN_SPLITS)[:-1]]
    aq, ak, av, dqkv, dz, db, da = jnp.split(proj, cuts, axis=-1)
    heads_a = lambda t: t.reshape(B, S, N_ATTN_HEADS, ATTN_HEAD_DIM)
    attn = _dilated_attention(heads_a(aq), heads_a(ak), heads_a(av))
    attn = attn.reshape(B, S, D_ATTN).astype(h.dtype)
    dqkv = jax.nn.silu(_causal_conv(dqkv, conv_w))
    dq, dk, dv = jnp.split(dqkv, 3, axis=-1)
    heads_b = lambda t: t.reshape(B, S, N_DN_HEADS, DN_HEAD_DIM).astype(jnp.float32)
    dq, dk, dv = heads_b(dq), heads_b(dk), heads_b(dv)
    l2 = lambda t: t * lax.rsqrt(jnp.sum(t * t, axis=-1, keepdims=True) + EPS)
    dq = l2(dq) * (DN_HEAD_DIM ** -0.5)
    dk = l2(dk)
    beta = jax.nn.sigmoid(db.astype(jnp.float32))
    g = -jnp.exp(a_log.astype(jnp.float32)) * jax.nn.softplus(da.astype(jnp.float32) + dt_bias.astype(jnp.float32))
    o = _gated_delta_rule(dq, dk, dv, g, beta)
    o = o * lax.rsqrt(jnp.mean(o * o, axis=-1, keepdims=True) + EPS) * dn_norm.astype(jnp.float32)
    o = o * jax.nn.silu(heads_b(dz))
    dn = o.reshape(B, S, D_DN).astype(h.dtype)
    return jnp.concatenate([attn, dn], axis=-1) @ w_out


def _fwd_setup_inputs(seed: int = 0) -> dict:
    key = jax.random.key(seed)
    ks = jax.random.split(key, 20)
    f32 = jnp.float32
    nrm = lambda k, shape, fan_in: jax.random.normal(k, shape, f32) * fan_in ** -0.5
    gain = lambda k, n: 1.0 + 0.05 * jax.random.normal(k, (n,), f32)
    dt = jnp.exp(jax.random.uniform(ks[9], (N_DN_HEADS,), f32, math.log(1e-3), math.log(1e-1)))
    return {
        "x": jax.random.normal(ks[0], (BATCH, SEQ, D_MODEL), f32),
        "ffn1_norm": gain(ks[1], D_MODEL),
        "ffn1_w_gate": nrm(ks[2], (D_MODEL, D_FF), D_MODEL),
        "ffn1_w_up": nrm(ks[3], (D_MODEL, D_FF), D_MODEL),
        "ffn1_w_down": nrm(ks[4], (D_FF, D_MODEL), D_FF),
        "mix_norm": gain(ks[5], D_MODEL),
        "w_in": nrm(ks[6], (D_MODEL, D_IN_PROJ), D_MODEL),
        "conv_w": nrm(ks[7], (CONV_WIDTH, 3 * D_DN), CONV_WIDTH),
        "a_log": jnp.log(jax.random.uniform(ks[8], (N_DN_HEADS,), f32, 1.0, 16.0)),
        "dt_bias": dt + jnp.log(-jnp.expm1(-dt)),
        "dn_norm": gain(ks[10], DN_HEAD_DIM),
        "w_out": nrm(ks[11], (D_MIX, D_MODEL), D_MIX),
        "ffn2_norm": gain(ks[12], D_MODEL),
        "ffn2_w_gate": nrm(ks[13], (D_MODEL, D_FF), D_MODEL),
        "ffn2_w_up": nrm(ks[14], (D_MODEL, D_FF), D_MODEL),
        "ffn2_w_down": nrm(ks[15], (D_FF, D_MODEL), D_FF),
        "final_norm": gain(ks[16], D_MODEL),
    }


def _fwd_reference(x, ffn1_norm, ffn1_w_gate, ffn1_w_up, ffn1_w_down, mix_norm, w_in, conv_w,
              a_log, dt_bias, dn_norm, w_out, ffn2_norm, ffn2_w_gate, ffn2_w_up, ffn2_w_down,
              final_norm):
    h = x
    for _ in range(DEPTH):
        h = h + 0.5 * _swiglu(_rmsnorm(h, ffn1_norm), ffn1_w_gate, ffn1_w_up, ffn1_w_down)
        h = h + _hybrid_mixer(_rmsnorm(h, mix_norm), w_in, conv_w, a_log, dt_bias, dn_norm, w_out)
        h = h + 0.5 * _swiglu(_rmsnorm(h, ffn2_norm), ffn2_w_gate, ffn2_w_up, ffn2_w_down)
    return _rmsnorm(h, final_norm)


import jax as _jax
import jax.numpy as _jnp

TWIN_FORMAT = 'train_step'
FWD_PARAMS = ['x', 'ffn1_norm', 'ffn1_w_gate', 'ffn1_w_up', 'ffn1_w_down', 'mix_norm', 'w_in', 'conv_w', 'a_log', 'dt_bias', 'dn_norm', 'w_out', 'ffn2_norm', 'ffn2_w_gate', 'ffn2_w_up', 'ffn2_w_down', 'final_norm']
TWIN_WEIGHTS = ['ffn1_norm', 'ffn1_w_gate', 'ffn1_w_up', 'ffn1_w_down', 'mix_norm', 'w_in', 'conv_w', 'a_log', 'dt_bias', 'dn_norm', 'w_out', 'ffn2_norm', 'ffn2_w_gate', 'ffn2_w_up', 'ffn2_w_down', 'final_norm']
TWIN_DIFF_INPUT = 'x'
TWIN_INPUTS = ['x', 'ffn1_norm', 'ffn1_w_gate', 'ffn1_w_up', 'ffn1_w_down', 'mix_norm', 'w_in', 'conv_w', 'a_log', 'dt_bias', 'dn_norm', 'w_out', 'ffn2_norm', 'ffn2_w_gate', 'ffn2_w_up', 'ffn2_w_down', 'final_norm', 'loss_target', 'm_ffn1_norm', 'm_ffn1_w_gate', 'm_ffn1_w_up', 'm_ffn1_w_down', 'm_mix_norm', 'm_w_in', 'm_conv_w', 'm_a_log', 'm_dt_bias', 'm_dn_norm', 'm_w_out', 'm_ffn2_norm', 'm_ffn2_w_gate', 'm_ffn2_w_up', 'm_ffn2_w_down', 'm_final_norm', 'v_ffn1_norm', 'v_ffn1_w_gate', 'v_ffn1_w_up', 'v_ffn1_w_down', 'v_mix_norm', 'v_w_in', 'v_conv_w', 'v_a_log', 'v_dt_bias', 'v_dn_norm', 'v_w_out', 'v_ffn2_norm', 'v_ffn2_w_gate', 'v_ffn2_w_up', 'v_ffn2_w_down', 'v_final_norm']
TWIN_OUTPUTS = ['loss', 'grad_x', 'grad_ffn1_norm', 'grad_ffn1_w_gate', 'grad_ffn1_w_up', 'grad_ffn1_w_down', 'grad_mix_norm', 'grad_w_in', 'grad_conv_w', 'grad_a_log', 'grad_dt_bias', 'grad_dn_norm', 'grad_w_out', 'grad_ffn2_norm', 'grad_ffn2_w_gate', 'grad_ffn2_w_up', 'grad_ffn2_w_down', 'grad_final_norm', 'delta_ffn1_norm', 'delta_ffn1_w_gate', 'delta_ffn1_w_up', 'delta_ffn1_w_down', 'delta_mix_norm', 'delta_w_in', 'delta_conv_w', 'delta_a_log', 'delta_dt_bias', 'delta_dn_norm', 'delta_w_out', 'delta_ffn2_norm', 'delta_ffn2_w_gate', 'delta_ffn2_w_up', 'delta_ffn2_w_down', 'delta_final_norm', 'new_m_ffn1_norm', 'new_m_ffn1_w_gate', 'new_m_ffn1_w_up', 'new_m_ffn1_w_down', 'new_m_mix_norm', 'new_m_w_in', 'new_m_conv_w', 'new_m_a_log', 'new_m_dt_bias', 'new_m_dn_norm', 'new_m_w_out', 'new_m_ffn2_norm', 'new_m_ffn2_w_gate', 'new_m_ffn2_w_up', 'new_m_ffn2_w_down', 'new_m_final_norm', 'new_v_ffn1_norm', 'new_v_ffn1_w_gate', 'new_v_ffn1_w_up', 'new_v_ffn1_w_down', 'new_v_mix_norm', 'new_v_w_in', 'new_v_conv_w', 'new_v_a_log', 'new_v_dt_bias', 'new_v_dn_norm', 'new_v_w_out', 'new_v_ffn2_norm', 'new_v_ffn2_w_gate', 'new_v_ffn2_w_up', 'new_v_ffn2_w_down', 'new_v_final_norm']
TWIN_LEAF_KINDS = {'loss': 'loss', 'grad_x': 'grad_x', 'grad_ffn1_norm': 'grad_w', 'grad_ffn1_w_gate': 'grad_w', 'grad_ffn1_w_up': 'grad_w', 'grad_ffn1_w_down': 'grad_w', 'grad_mix_norm': 'grad_w', 'grad_w_in': 'grad_w', 'grad_conv_w': 'grad_w', 'grad_a_log': 'grad_w', 'grad_dt_bias': 'grad_w', 'grad_dn_norm': 'grad_w', 'grad_w_out': 'grad_w', 'grad_ffn2_norm': 'grad_w', 'grad_ffn2_w_gate': 'grad_w', 'grad_ffn2_w_up': 'grad_w', 'grad_ffn2_w_down': 'grad_w', 'grad_final_norm': 'grad_w', 'delta_ffn1_norm': 'delta_w', 'delta_ffn1_w_gate': 'delta_w', 'delta_ffn1_w_up': 'delta_w', 'delta_ffn1_w_down': 'delta_w', 'delta_mix_norm': 'delta_w', 'delta_w_in': 'delta_w', 'delta_conv_w': 'delta_w', 'delta_a_log': 'delta_w', 'delta_dt_bias': 'delta_w', 'delta_dn_norm': 'delta_w', 'delta_w_out': 'delta_w', 'delta_ffn2_norm': 'delta_w', 'delta_ffn2_w_gate': 'delta_w', 'delta_ffn2_w_up': 'delta_w', 'delta_ffn2_w_down': 'delta_w', 'delta_final_norm': 'delta_w', 'new_m_ffn1_norm': 'new_m', 'new_m_ffn1_w_gate': 'new_m', 'new_m_ffn1_w_up': 'new_m', 'new_m_ffn1_w_down': 'new_m', 'new_m_mix_norm': 'new_m', 'new_m_w_in': 'new_m', 'new_m_conv_w': 'new_m', 'new_m_a_log': 'new_m', 'new_m_dt_bias': 'new_m', 'new_m_dn_norm': 'new_m', 'new_m_w_out': 'new_m', 'new_m_ffn2_norm': 'new_m', 'new_m_ffn2_w_gate': 'new_m', 'new_m_ffn2_w_up': 'new_m', 'new_m_ffn2_w_down': 'new_m', 'new_m_final_norm': 'new_m', 'new_v_ffn1_norm': 'new_v', 'new_v_ffn1_w_gate': 'new_v', 'new_v_ffn1_w_up': 'new_v', 'new_v_ffn1_w_down': 'new_v', 'new_v_mix_norm': 'new_v', 'new_v_w_in': 'new_v', 'new_v_conv_w': 'new_v', 'new_v_a_log': 'new_v', 'new_v_dt_bias': 'new_v', 'new_v_dn_norm': 'new_v', 'new_v_w_out': 'new_v', 'new_v_ffn2_norm': 'new_v', 'new_v_ffn2_w_gate': 'new_v', 'new_v_ffn2_w_up': 'new_v', 'new_v_ffn2_w_down': 'new_v', 'new_v_final_norm': 'new_v'}


def _forward(args):
    return _fwd_reference(*[args[k] for k in FWD_PARAMS])


def _output_shape():
    def fwd():
        inp = _fwd_setup_inputs(0)
        return _fwd_reference(*[inp[k] for k in FWD_PARAMS])
    out = _jax.eval_shape(fwd)
    return out.shape, out.dtype

N_MICROBATCH = 1
ADAM_LR = 0.001
ADAM_B1 = 0.9
ADAM_B2 = 0.999
ADAM_EPS = 1e-08
ADAM_WD = 0.01
ADAM_STEP = 10
PER_EXAMPLE_BATCH_AXIS = {'x': 0, 'loss_target': 0}
SHARED_INPUTS = []
_WEIGHT_DTYPES = {'ffn1_norm': _jnp.float32, 'ffn1_w_gate': _jnp.float32, 'ffn1_w_up': _jnp.float32, 'ffn1_w_down': _jnp.float32, 'mix_norm': _jnp.float32, 'w_in': _jnp.float32, 'conv_w': _jnp.float32, 'a_log': _jnp.float32, 'dt_bias': _jnp.float32, 'dn_norm': _jnp.float32, 'w_out': _jnp.float32, 'ffn2_norm': _jnp.float32, 'ffn2_w_gate': _jnp.float32, 'ffn2_w_up': _jnp.float32, 'ffn2_w_down': _jnp.float32, 'final_norm': _jnp.float32}
MOMENT_SCALE = {'ffn1_norm': 2.150510e-02, 'ffn1_w_gate': 9.384473e-03, 'ffn1_w_up': 9.084924e-03, 'ffn1_w_down': 1.490441e-02, 'mix_norm': 3.056253e-02, 'w_in': 1.605822e-02, 'conv_w': 1.898194e-02, 'a_log': 9.909448e-02, 'dt_bias': 9.283902e-02, 'dn_norm': 1.022899e-01, 'w_out': 1.798179e-02, 'ffn2_norm': 1.636196e-02, 'ffn2_w_gate': 7.232778e-03, 'ffn2_w_up': 7.018779e-03, 'ffn2_w_down': 1.151010e-02, 'final_norm': 8.006102e+00}


def _to_microbatches(a, axis):
    t = _jnp.moveaxis(a, axis, 0)
    t = t.reshape((N_MICROBATCH, t.shape[0] // N_MICROBATCH) + t.shape[1:])
    return _jnp.moveaxis(t, 1, axis + 1)


def setup_inputs(seed: int = 0) -> dict:
    inp = _fwd_setup_inputs(seed)
    key = _jax.random.fold_in(_jax.random.key(seed), 7919)
    shape, _ = _output_shape()
    out = dict(inp)
    out["loss_target"] = _jax.random.normal(_jax.random.fold_in(key, 0), shape, _jnp.float32)
    for i, name in enumerate(TWIN_WEIGHTS):
        w = inp[name].astype(_jnp.float32)
        if MOMENT_SCALE is None:
            s = _jnp.sqrt(_jnp.mean(_jnp.square(w)) + 1e-30)
        else:
            s = MOMENT_SCALE[name]
        km, kv = _jax.random.split(_jax.random.fold_in(key, i + 1))
        out[name] = w
        out["m_" + name] = s * _jax.random.normal(km, w.shape, _jnp.float32)
        out["v_" + name] = (s * s) * _jax.random.uniform(kv, w.shape, _jnp.float32, 0.5, 1.5)
    if N_MICROBATCH > 1:
        for name, axis in PER_EXAMPLE_BATCH_AXIS.items():
            out[name] = _to_microbatches(out[name], axis)
    return {'x': out['x'], 'ffn1_norm': out['ffn1_norm'], 'ffn1_w_gate': out['ffn1_w_gate'], 'ffn1_w_up': out['ffn1_w_up'], 'ffn1_w_down': out['ffn1_w_down'], 'mix_norm': out['mix_norm'], 'w_in': out['w_in'], 'conv_w': out['conv_w'], 'a_log': out['a_log'], 'dt_bias': out['dt_bias'], 'dn_norm': out['dn_norm'], 'w_out': out['w_out'], 'ffn2_norm': out['ffn2_norm'], 'ffn2_w_gate': out['ffn2_w_gate'], 'ffn2_w_up': out['ffn2_w_up'], 'ffn2_w_down': out['ffn2_w_down'], 'final_norm': out['final_norm'], 'loss_target': out['loss_target'], 'm_ffn1_norm': out['m_ffn1_norm'], 'm_ffn1_w_gate': out['m_ffn1_w_gate'], 'm_ffn1_w_up': out['m_ffn1_w_up'], 'm_ffn1_w_down': out['m_ffn1_w_down'], 'm_mix_norm': out['m_mix_norm'], 'm_w_in': out['m_w_in'], 'm_conv_w': out['m_conv_w'], 'm_a_log': out['m_a_log'], 'm_dt_bias': out['m_dt_bias'], 'm_dn_norm': out['m_dn_norm'], 'm_w_out': out['m_w_out'], 'm_ffn2_norm': out['m_ffn2_norm'], 'm_ffn2_w_gate': out['m_ffn2_w_gate'], 'm_ffn2_w_up': out['m_ffn2_w_up'], 'm_ffn2_w_down': out['m_ffn2_w_down'], 'm_final_norm': out['m_final_norm'], 'v_ffn1_norm': out['v_ffn1_norm'], 'v_ffn1_w_gate': out['v_ffn1_w_gate'], 'v_ffn1_w_up': out['v_ffn1_w_up'], 'v_ffn1_w_down': out['v_ffn1_w_down'], 'v_mix_norm': out['v_mix_norm'], 'v_w_in': out['v_w_in'], 'v_conv_w': out['v_conv_w'], 'v_a_log': out['v_a_log'], 'v_dt_bias': out['v_dt_bias'], 'v_dn_norm': out['v_dn_norm'], 'v_w_out': out['v_w_out'], 'v_ffn2_norm': out['v_ffn2_norm'], 'v_ffn2_w_gate': out['v_ffn2_w_gate'], 'v_ffn2_w_up': out['v_ffn2_w_up'], 'v_ffn2_w_down': out['v_ffn2_w_down'], 'v_final_norm': out['v_final_norm']}


def _loss(weights, diff, rest, loss_target):
    with _jax.named_scope("forward"):
        args = {**rest, TWIN_DIFF_INPUT: diff, **{k: w.astype(_WEIGHT_DTYPES[k]) for k, w in weights.items()}}
        y = _forward(args)
    with _jax.named_scope("loss_head"):
        err = _jnp.square(y.astype(_jnp.float32) - loss_target)
        return 0.5 * _jnp.sum(_jnp.mean(err, axis=-1)) if err.ndim else 0.5 * err


def _adamw(w, g, m, v):
    m = ADAM_B1 * m + (1.0 - ADAM_B1) * g
    v = ADAM_B2 * v + (1.0 - ADAM_B2) * _jnp.square(g)
    m_hat = m / (1.0 - ADAM_B1 ** ADAM_STEP)
    v_hat = v / (1.0 - ADAM_B2 ** ADAM_STEP)
    delta = -ADAM_LR * (m_hat / (_jnp.sqrt(v_hat) + ADAM_EPS) + ADAM_WD * w)
    return delta, m, v


def reference(x, ffn1_norm, ffn1_w_gate, ffn1_w_up, ffn1_w_down, mix_norm, w_in, conv_w, a_log, dt_bias, dn_norm, w_out, ffn2_norm, ffn2_w_gate, ffn2_w_up, ffn2_w_down, final_norm, loss_target, m_ffn1_norm, m_ffn1_w_gate, m_ffn1_w_up, m_ffn1_w_down, m_mix_norm, m_w_in, m_conv_w, m_a_log, m_dt_bias, m_dn_norm, m_w_out, m_ffn2_norm, m_ffn2_w_gate, m_ffn2_w_up, m_ffn2_w_down, m_final_norm, v_ffn1_norm, v_ffn1_w_gate, v_ffn1_w_up, v_ffn1_w_down, v_mix_norm, v_w_in, v_conv_w, v_a_log, v_dt_bias, v_dn_norm, v_w_out, v_ffn2_norm, v_ffn2_w_gate, v_ffn2_w_up, v_ffn2_w_down, v_final_norm):
    given = dict(x=x, ffn1_norm=ffn1_norm, ffn1_w_gate=ffn1_w_gate, ffn1_w_up=ffn1_w_up, ffn1_w_down=ffn1_w_down, mix_norm=mix_norm, w_in=w_in, conv_w=conv_w, a_log=a_log, dt_bias=dt_bias, dn_norm=dn_norm, w_out=w_out, ffn2_norm=ffn2_norm, ffn2_w_gate=ffn2_w_gate, ffn2_w_up=ffn2_w_up, ffn2_w_down=ffn2_w_down, final_norm=final_norm, loss_target=loss_target, m_ffn1_norm=m_ffn1_norm, m_ffn1_w_gate=m_ffn1_w_gate, m_ffn1_w_up=m_ffn1_w_up, m_ffn1_w_down=m_ffn1_w_down, m_mix_norm=m_mix_norm, m_w_in=m_w_in, m_conv_w=m_conv_w, m_a_log=m_a_log, m_dt_bias=m_dt_bias, m_dn_norm=m_dn_norm, m_w_out=m_w_out, m_ffn2_norm=m_ffn2_norm, m_ffn2_w_gate=m_ffn2_w_gate, m_ffn2_w_up=m_ffn2_w_up, m_ffn2_w_down=m_ffn2_w_down, m_final_norm=m_final_norm, v_ffn1_norm=v_ffn1_norm, v_ffn1_w_gate=v_ffn1_w_gate, v_ffn1_w_up=v_ffn1_w_up, v_ffn1_w_down=v_ffn1_w_down, v_mix_norm=v_mix_norm, v_w_in=v_w_in, v_conv_w=v_conv_w, v_a_log=v_a_log, v_dt_bias=v_dt_bias, v_dn_norm=v_dn_norm, v_w_out=v_w_out, v_ffn2_norm=v_ffn2_norm, v_ffn2_w_gate=v_ffn2_w_gate, v_ffn2_w_up=v_ffn2_w_up, v_ffn2_w_down=v_ffn2_w_down, v_final_norm=v_final_norm)
    weights = {n: given[n] for n in TWIN_WEIGHTS}
    shared = {n: given[n] for n in SHARED_INPUTS}
    per_example = {n: given[n] for n in ['x']}
    grad_fn = _jax.value_and_grad(_loss, argnums=(0, 1))

    def one_microbatch(ex, loss_target):
        ex = dict(ex)
        diff = ex.pop(TWIN_DIFF_INPUT)
        return grad_fn(weights, diff, {**shared, **ex}, loss_target)

    if N_MICROBATCH == 1:
        loss, (grad_w, grad_x) = one_microbatch(per_example, given["loss_target"])
    else:
        def body(carry, xs):
            loss_sum, grad_sum = carry
            l_k, (gw_k, gx_k) = one_microbatch(xs[0], xs[1])
            with _jax.named_scope("update"):
                return (loss_sum + l_k, _jax.tree.map(_jnp.add, grad_sum, gw_k)), gx_k

        init = (_jnp.zeros((), _jnp.float32), _jax.tree.map(_jnp.zeros_like, weights))
        (loss, grad_w), grad_x = _jax.lax.scan(body, init, (per_example, given["loss_target"]))
    with _jax.named_scope("update"):
        delta_w, new_m, new_v = {}, {}, {}
        for n in TWIN_WEIGHTS:
            delta_w[n], new_m[n], new_v[n] = _adamw(weights[n], grad_w[n], given["m_" + n], given["v_" + n])
    return (loss, grad_x, *[grad_w[n] for n in TWIN_WEIGHTS], *[delta_w[n] for n in TWIN_WEIGHTS],
            *[new_m[n] for n in TWIN_WEIGHTS], *[new_v[n] for n in TWIN_WEIGHTS])
```

```python
import functools
import math

import jax
import jax.numpy as jnp
from jax import lax
from jax.experimental import pallas as pl
from jax.experimental.pallas import tpu as pltpu

F32 = jnp.float32
BF16 = jnp.bfloat16
N_DEV = 8
HEAD = 128
CHUNK = 64
CONV_WIDTH = 4
EPS = 1e-6
DILATED_CONFIGS = ((128, 1), (512, 4), (2048, 16))
ATTN_BLOCK = 256
NEG = -1e30
ADAM_LR, ADAM_B1, ADAM_B2, ADAM_EPS, ADAM_WD, ADAM_STEP = 0.001, 0.9, 0.999, 1e-08, 0.01, 10
HI = lax.Precision.HIGHEST
MESH = pl.DeviceIdType.MESH
SDS = jax.ShapeDtypeStruct


def _tile(n, pref, align):
    t = (min(n, pref) // align) * align
    while t >= align:
        if n % t == 0:
            return t
        t -= align
    return n


def _params(n_axes, vmem_mb=48):
    return pltpu.CompilerParams(dimension_semantics=("arbitrary",) * n_axes, vmem_limit_bytes=vmem_mb * 2 ** 20)


def _sigmoid(x):
    return 1.0 / (1.0 + jnp.exp(-x))


def _silu(x):
    return x * _sigmoid(x)


def _softplus(x):
    return jnp.maximum(x, 0.0) + jnp.log(1.0 + jnp.exp(-jnp.abs(x)))


_DIMS = {"nn": (((1,), (0,)), ((), ())), "nt": (((1,), (1,)), ((), ())), "tn": (((0,), (0,)), ((), ()))}


def _mm_call(name, grid, mode, pairs, operands, in_specs, out_shape, out_specs, acc_shapes, epilogue, vmem_mb=48):
    dims = _DIMS[mode]
    n_in, n_out = len(operands), len(out_shape)
    nk = grid[-1]

    def body(*refs):
        ins, outs, accs = refs[:n_in], refs[n_in:n_in + n_out], refs[n_in + n_out:]
        k = pl.program_id(len(grid) - 1)

        @pl.when(k == 0)
        def _():
            for acc in accs:
                acc[...] = jnp.zeros_like(acc)

        for a, b, c in pairs:
            accs[c][...] += lax.dot_general(ins[a][...], ins[b][...], dims, preferred_element_type=F32)

        @pl.when(k == nk - 1)
        def _():
            epilogue(ins, outs, [acc[...] for acc in accs])

    return pl.pallas_call(
        body, name=name, grid=grid, in_specs=in_specs, out_specs=out_specs, out_shape=out_shape,
        scratch_shapes=[pltpu.VMEM(s, F32) for s in acc_shapes], compiler_params=_params(len(grid), vmem_mb),
    )(*operands)


def _ffn_up(n, wg, wu):
    t, d = n.shape
    f = wg.shape[2]
    tm, tk = _tile(t, 512, 16), _tile(d, 512, 128)

    def epilogue(ins, outs, accs):
        g, u = accs
        outs[0][...] = g.astype(BF16)
        outs[1][...] = u.astype(BF16)
        outs[2][...] = (_silu(g) * u).astype(BF16)

    w_spec = pl.BlockSpec((None, tk, f), lambda s, m, k: (s, k, 0))
    o_spec = pl.BlockSpec((None, tm, f), lambda s, m, k: (s, m, 0))
    return _mm_call(
        "ffn_up", (N_DEV, t // tm, d // tk), "nn", [(0, 1, 0), (0, 2, 1)], (n, wg, wu),
        [pl.BlockSpec((tm, tk), lambda s, m, k: (m, k)), w_spec, w_spec],
        [SDS((N_DEV, t, f), BF16)] * 3, [o_spec] * 3, [(tm, f)] * 2, epilogue)


def _ffn_down(act, wd, resid, scale):
    _, t, f = act.shape
    d = wd.shape[2]
    tm, tn = _tile(t, 512, 16), _tile(d, 1024, 128)

    def epilogue(ins, outs, accs):
        outs[0][...] = ins[2][...] + scale * accs[0]

    rc = pl.BlockSpec((tm, tn), lambda m, n, s: (m, n))
    return _mm_call(
        "ffn_down", (t // tm, d // tn, N_DEV), "nn", [(0, 1, 0)], (act, wd, resid),
        [pl.BlockSpec((None, tm, f), lambda m, n, s: (s, m, 0)), pl.BlockSpec((None, f, tn), lambda m, n, s: (s, 0, n)), rc],
        [SDS((t, d), F32)], [rc], [(tm, tn)], epilogue)[0]


def _in_proj(n, w):
    t, d = n.shape
    p = w.shape[2]
    tm, tk = _tile(t, 512, 16), _tile(d, 512, 128)

    def epilogue(ins, outs, accs):
        outs[0][...] = accs[0]

    return _mm_call(
        "in_proj", (N_DEV, t // tm, d // tk), "nn", [(0, 1, 0)], (n, w),
        [pl.BlockSpec((tm, tk), lambda s, m, k: (m, k)), pl.BlockSpec((None, tk, p), lambda s, m, k: (s, k, 0))],
        [SDS((N_DEV, t, p), F32)], [pl.BlockSpec((None, tm, p), lambda s, m, k: (s, m, 0))], [(tm, p)], epilogue)[0]


def _out_proj(cat, w, resid):
    t, dm = cat.shape
    d = w.shape[1]
    tm, tn, tk = _tile(t, 512, 16), _tile(d, 1024, 128), _tile(dm, 1024, 128)

    def epilogue(ins, outs, accs):
        outs[0][...] = ins[2][...] + accs[0]

    rc = pl.BlockSpec((tm, tn), lambda m, n, k: (m, n))
    return _mm_call(
        "out_proj", (t // tm, d // tn, dm // tk), "nn", [(0, 1, 0)], (cat, w, resid),
        [pl.BlockSpec((tm, tk), lambda m, n, k: (m, k)), pl.BlockSpec((tk, tn), lambda m, n, k: (k, n)), rc],
        [SDS((t, d), F32)], [rc], [(tm, tn)], epilogue)[0]


def _ffn_bwd_act(dy, wd, gate, up):
    t, d = dy.shape
    f = wd.shape[1]
    tm, tk = _tile(t, 512, 16), _tile(d, 512, 128)

    def epilogue(ins, outs, accs):
        g, u = ins[2][...].astype(F32), ins[3][...].astype(F32)
        sg = _sigmoid(g)
        outs[0][...] = (accs[0] * u * sg * (1.0 + g * (1.0 - sg))).astype(BF16)
        outs[1][...] = (accs[0] * g * sg).astype(BF16)

    o_spec = pl.BlockSpec((None, tm, f), lambda s, m, k: (s, m, 0))
    return _mm_call(
        "ffn_bwd_act", (N_DEV, t // tm, d // tk), "nt", [(0, 1, 0)], (dy, wd, gate, up),
        [pl.BlockSpec((tm, tk), lambda s, m, k: (m, k)), pl.BlockSpec((None, f, tk), lambda s, m, k: (s, 0, k)), o_spec, o_spec],
        [SDS((N_DEV, t, f), BF16)] * 2, [o_spec] * 2, [(tm, f)], epilogue)


def _store_bf16(ins, outs, accs):
    outs[0][...] = accs[0].astype(BF16)


def _store_f32(ins, outs, accs):
    outs[0][...] = accs[0]


def _wgrad_cols(name, a, b):
    t, m = a.shape
    n = b.shape[2]
    tm, tk = _tile(m, 1024, 128), _tile(t, 512, 16)
    return _mm_call(
        name, (N_DEV, m // tm, t // tk), "tn", [(0, 1, 0)], (a, b),
        [pl.BlockSpec((tk, tm), lambda s, i, k: (k, i)), pl.BlockSpec((None, tk, n), lambda s, i, k: (s, k, 0))],
        [SDS((N_DEV, m, n), BF16)], [pl.BlockSpec((None, tm, n), lambda s, i, k: (s, i, 0))], [(tm, n)], _store_bf16)[0]


def _wgrad_rows(name, a, b):
    _, t, m = a.shape
    n = b.shape[1]
    tn, tk = _tile(n, 1024, 128), _tile(t, 512, 16)
    return _mm_call(
        name, (N_DEV, n // tn, t // tk), "tn", [(0, 1, 0)], (a, b),
        [pl.BlockSpec((None, tk, m), lambda s, j, k: (s, k, 0)), pl.BlockSpec((tk, tn), lambda s, j, k: (k, j))],
        [SDS((N_DEV, m, n), BF16)], [pl.BlockSpec((None, m, tn), lambda s, j, k: (s, 0, j))], [(m, tn)], _store_bf16)[0]


def _wgrad_full(name, a, b):
    t, m = a.shape
    n = b.shape[1]
    tm, tn, tk = _tile(m, 1024, 128), _tile(n, 1024, 128), _tile(t, 512, 16)
    return _mm_call(
        name, (m // tm, n // tn, t // tk), "tn", [(0, 1, 0)], (a, b),
        [pl.BlockSpec((tk, tm), lambda i, j, k: (k, i)), pl.BlockSpec((tk, tn), lambda i, j, k: (k, j))],
        [SDS((m, n), BF16)], [pl.BlockSpec((tm, tn), lambda i, j, k: (i, j))], [(tm, tn)], _store_bf16)[0]


def _dgrad_cols(name, grads, weights):
    _, t, n = grads[0].shape
    m = weights[0].shape[1]
    tm, tn = _tile(t, 512, 16), _tile(m, 1024, 128)
    k = len(grads)
    return _mm_call(
        name, (t // tm, m // tn, N_DEV), "nt", [(i, k + i, 0) for i in range(k)], (*grads, *weights),
        [pl.BlockSpec((None, tm, n), lambda i, j, s: (s, i, 0))] * k + [pl.BlockSpec((None, tn, n), lambda i, j, s: (s, j, 0))] * k,
        [SDS((t, m), F32)], [pl.BlockSpec((tm, tn), lambda i, j, s: (i, j))], [(tm, tn)], _store_f32)[0]


def _dgrad_full(name, g, w):
    t, n = g.shape
    m = w.shape[0]
    tm, tn, tk = _tile(t, 512, 16), _tile(m, 1024, 128), _tile(n, 1024, 128)
    return _mm_call(
        name, (t // tm, m // tn, n // tk), "nt", [(0, 1, 0)], (g, w),
        [pl.BlockSpec((tm, tk), lambda i, j, k: (i, k)), pl.BlockSpec((tn, tk), lambda i, j, k: (j, k))],
        [SDS((t, m), F32)], [pl.BlockSpec((tm, tn), lambda i, j, k: (i, j))], [(tm, tn)], _store_f32)[0]


def _rms_fwd(name, h, w):
    t, d = h.shape
    tm = _tile(t, 256, 16)

    def body(h_ref, w_ref, o_ref):
        x = h_ref[...]
        o_ref[...] = (x * lax.rsqrt(jnp.mean(x * x, axis=1, keepdims=True) + EPS) * w_ref[...]).astype(BF16)

    row = pl.BlockSpec((tm, d), lambda i: (i, 0))
    return pl.pallas_call(
        body, name=name, grid=(t // tm,), in_specs=[row, pl.BlockSpec((1, d), lambda i: (0, 0))], out_specs=row,
        out_shape=SDS((t, d), BF16), compiler_params=_params(1))(h, w)


def _rms_bwd(name, h, dn, w, dres, scale):
    t, d = h.shape
    tm = _tile(t, 128, 16)

    def body(h_ref, dn_ref, w_ref, dres_ref, dh_ref, dhb_ref, dw_ref):
        @pl.when(pl.program_id(0) == 0)
        def _():
            dw_ref[...] = jnp.zeros_like(dw_ref)

        x = h_ref[...]
        rstd = lax.rsqrt(jnp.mean(x * x, axis=1, keepdims=True) + EPS)
        nhat = x * rstd
        g = dn_ref[...]
        gw = g * w_ref[...]
        dh = dres_ref[...] + rstd * (gw - nhat * jnp.mean(gw * nhat, axis=1, keepdims=True))
        dh_ref[...] = dh
        dhb_ref[...] = (scale * dh).astype(BF16)
        dw_ref[...] += jnp.sum(g * nhat, axis=0, keepdims=True)

    row = pl.BlockSpec((tm, d), lambda i: (i, 0))
    vec = pl.BlockSpec((1, d), lambda i: (0, 0))
    return pl.pallas_call(
        body, name=name, grid=(t // tm,), in_specs=[row, row, vec, row], out_specs=[row, row, vec],
        out_shape=[SDS((t, d), F32), SDS((t, d), BF16), SDS((1, d), F32)], compiler_params=_params(1))(h, dn, w, dres)


def _loss_head(h, w, target, scale):
    t, d = h.shape
    tm = _tile(t, 128, 16)

    def body(h_ref, w_ref, tg_ref, loss_ref, dh_ref, dhb_ref, dw_ref):
        @pl.when(pl.program_id(0) == 0)
        def _():
            dw_ref[...] = jnp.zeros_like(dw_ref)
            loss_ref[...] = jnp.zeros_like(loss_ref)

        x = h_ref[...]
        rstd = lax.rsqrt(jnp.mean(x * x, axis=1, keepdims=True) + EPS)
        nhat = x * rstd
        wv = w_ref[...]
        err = nhat * wv - tg_ref[...]
        loss_ref[...] += 0.5 * jnp.sum(jnp.mean(err * err, axis=1, keepdims=True), axis=0, keepdims=True)
        g = err * (1.0 / d)
        gw = g * wv
        dh = rstd * (gw - nhat * jnp.mean(gw * nhat, axis=1, keepdims=True))
        dh_ref[...] = dh
        dhb_ref[...] = (scale * dh).astype(BF16)
        dw_ref[...] += jnp.sum(g * nhat, axis=0, keepdims=True)

    row = pl.BlockSpec((tm, d), lambda i: (i, 0))
    vec = pl.BlockSpec((1, d), lambda i: (0, 0))
    return pl.pallas_call(
        body, name="loss_head", grid=(t // tm,), in_specs=[row, vec, row],
        out_specs=[pl.BlockSpec((1, 128), lambda i: (0, 0)), row, row, vec],
        out_shape=[SDS((1, 128), F32), SDS((t, d), F32), SDS((t, d), BF16), SDS((1, d), F32)],
        compiler_params=_params(1))(h, w, target)


def _attn_bias(qi, ki, blk):
    dist = (lax.broadcasted_iota(jnp.int32, (blk, blk), 0) - lax.broadcasted_iota(jnp.int32, (blk, blk), 1)
            + (qi - ki) * blk)
    count = jnp.zeros((blk, blk), F32)
    for window, dil in DILATED_CONFIGS:
        assert dil & (dil - 1) == 0
        seen = (dist >= 0) & (dist <= window) & ((dist & (dil - 1)) == 0)
        count = count + jnp.where(seen, 1.0, 0.0)
    return jnp.where(count > 0.0, jnp.log(jnp.maximum(count, 1.0)), NEG)


def _attn_fwd(qkv, *, batch, seq):
    t, da3 = qkv.shape
    da = da3 // 3
    n_heads = da // HEAD
    blk = _tile(seq, ATTN_BLOCK, 16)
    nq = seq // blk
    sm_scale = HEAD ** -0.5

    def body(q_ref, k_ref, v_ref, o_ref, lse_ref):
        def q_step(qi, _):
            rows = pl.ds(pl.multiple_of(qi * blk, blk), blk)
            q = q_ref[rows, :]

            def kv_step(ki, carry):
                m, l, acc = carry
                cols = pl.ds(pl.multiple_of(ki * blk, blk), blk)
                s = lax.dot_general(q, k_ref[cols, :], _DIMS["nt"], preferred_element_type=F32) * sm_scale
                s = s + _attn_bias(qi, ki, blk)
                m_new = jnp.maximum(m, jnp.max(s, axis=1, keepdims=True))
                alpha = jnp.exp(m - m_new)
                p = jnp.exp(s - m_new)
                l = alpha * l + jnp.sum(p, axis=1, keepdims=True)
                acc = alpha * acc + jnp.dot(p.astype(BF16), v_ref[cols, :], preferred_element_type=F32)
                return m_new, l, acc

            m, l, acc = lax.fori_loop(0, qi + 1, kv_step, (jnp.full((blk, 1), NEG, F32), jnp.zeros((blk, 1), F32),
                                                           jnp.zeros((blk, HEAD), F32)))
            o_ref[rows, :] = acc / l
            lse_ref[rows, :] = jnp.broadcast_to(m + jnp.log(l), (blk, HEAD))
            return 0

        lax.fori_loop(0, nq, q_step, 0)

    col = lambda off: pl.BlockSpec((seq, HEAD), lambda b, h: (b, off + h))
    return pl.pallas_call(
        body, name="attn_fwd", grid=(batch, n_heads), in_specs=[col(0), col(n_heads), col(2 * n_heads)],
        out_specs=[col(0), col(0)], out_shape=[SDS((t, da), F32), SDS((t, da), F32)], compiler_params=_params(2),
    )(qkv, qkv, qkv)


def _attn_bwd(qkv, out, lse, d_out, *, batch, seq):
    t, da = out.shape
    n_heads = da // HEAD
    blk = _tile(seq, ATTN_BLOCK, 16)
    nq = seq // blk
    sm_scale = HEAD ** -0.5

    def body(q_ref, k_ref, v_ref, o_ref, lse_ref, do_ref, dq_ref, dk_ref, dv_ref):
        dk_ref[...] = jnp.zeros_like(dk_ref)
        dv_ref[...] = jnp.zeros_like(dv_ref)

        def q_step(qi, _):
            rows = pl.ds(pl.multiple_of(qi * blk, blk), blk)
            q = q_ref[rows, :]
            do = do_ref[rows, :]
            do_b = do.astype(BF16)
            lse_q = lse_ref[rows, :][:, :1]
            delta = jnp.sum(do * o_ref[rows, :], axis=1, keepdims=True)

            def kv_step(ki, dq):
                cols = pl.ds(pl.multiple_of(ki * blk, blk), blk)
                k = k_ref[cols, :]
                s = lax.dot_general(q, k, _DIMS["nt"], preferred_element_type=F32) * sm_scale
                p = jnp.exp(s + _attn_bias(qi, ki, blk) - lse_q)
                dp = lax.dot_general(do_b, v_ref[cols, :], _DIMS["nt"], preferred_element_type=F32)
                ds = (p * (dp - delta) * sm_scale).astype(BF16)
                dv_ref[cols, :] += lax.dot_general(p.astype(BF16), do_b, _DIMS["tn"], preferred_element_type=F32)
                dk_ref[cols, :] += lax.dot_general(ds, q, _DIMS["tn"], preferred_element_type=F32)
                return dq + jnp.dot(ds, k, preferred_element_type=F32)

            dq_ref[rows, :] = lax.fori_loop(0, qi + 1, kv_step, jnp.zeros((blk, HEAD), F32))
            return 0

        lax.fori_loop(0, nq, q_step, 0)

    col = lambda off: pl.BlockSpec((seq, HEAD), lambda b, h: (b, off + h))
    return pl.pallas_call(
        body, name="attn_bwd", grid=(batch, n_heads),
        in_specs=[col(0), col(n_heads), col(2 * n_heads), col(0), col(0), col(0)], out_specs=[col(0)] * 3,
        out_shape=[SDS((t, da), F32)] * 3, compiler_params=_params(2),
    )(qkv, qkv, qkv, out, lse, d_out)


def _shift_down(x, k, row):
    return x if k == 0 else jnp.where(row >= k, pltpu.roll(x, k, axis=0), 0.0)


def _shift_up(x, k, row):
    n = x.shape[0]
    return x if k == 0 else jnp.where(row < n - k, pltpu.roll(x, n - k, axis=0), 0.0)


def _conv_silu_fwd(x, w, *, batch, seq):
    t, c = x.shape

    def body(x_ref, w_ref, o_ref):
        xv = x_ref[...]
        row = lax.broadcasted_iota(jnp.int32, xv.shape, 0)
        acc = jnp.zeros_like(xv)
        for i in range(CONV_WIDTH):
            acc = acc + w_ref[i:i + 1, :] * _shift_down(xv, CONV_WIDTH - 1 - i, row)
        o_ref[...] = _silu(acc)

    blk = pl.BlockSpec((seq, HEAD), lambda j, b: (b, j))
    return pl.pallas_call(
        body, name="conv_silu_fwd", grid=(c // HEAD, batch), in_specs=[blk, pl.BlockSpec((CONV_WIDTH, HEAD), lambda j, b: (0, j))],
        out_specs=blk, out_shape=SDS((t, c), F32), compiler_params=_params(2))(x, w)


def _conv_silu_bwd(x, w, dy, *, batch, seq):
    t, c = x.shape

    def body(x_ref, w_ref, dy_ref, dx_ref, dw_ref):
        @pl.when(pl.program_id(1) == 0)
        def _():
            dw_ref[...] = jnp.zeros_like(dw_ref)

        xv = x_ref[...]
        row = lax.broadcasted_iota(jnp.int32, xv.shape, 0)
        shifted = [_shift_down(xv, CONV_WIDTH - 1 - i, row) for i in range(CONV_WIDTH)]
        acc = jnp.zeros_like(xv)
        for i in range(CONV_WIDTH):
            acc = acc + w_ref[i:i + 1, :] * shifted[i]
        sg = _sigmoid(acc)
        dc = dy_ref[...] * sg * (1.0 + acc * (1.0 - sg))
        dx = jnp.zeros_like(xv)
        for i in range(CONV_WIDTH):
            dx = dx + w_ref[i:i + 1, :] * _shift_up(dc, CONV_WIDTH - 1 - i, row)
            dw_ref[i:i + 1, :] += jnp.sum(dc * shifted[i], axis=0, keepdims=True)
        dx_ref[...] = dx

    blk = pl.BlockSpec((seq, HEAD), lambda j, b: (b, j))
    wblk = pl.BlockSpec((CONV_WIDTH, HEAD), lambda j, b: (0, j))
    return pl.pallas_call(
        body, name="conv_silu_bwd", grid=(c // HEAD, batch), in_specs=[blk, wblk, blk], out_specs=[blk, wblk],
        out_shape=[SDS((t, c), F32), SDS((CONV_WIDTH, c), F32)], compiler_params=_params(2))(x, w, dy)


def _dot(a, b):
    return jnp.dot(a, b, precision=HI, preferred_element_type=F32)


def _dot_nt(a, b):
    return lax.dot_general(a, b, _DIMS["nt"], precision=HI, preferred_element_type=F32)


def _dot_tn(a, b):
    return lax.dot_general(a, b, _DIMS["tn"], precision=HI, preferred_element_type=F32)


def _dn_chunk(head, n_heads, aq, ak, v, z, dbda, a_log, dt_bias, dn_norm, state):
    c = aq.shape[0]
    lane_g = lax.broadcasted_iota(jnp.int32, dbda.shape, 1)
    db = jnp.sum(jnp.where(lane_g == head, dbda, 0.0), axis=1, keepdims=True)
    da = jnp.sum(jnp.where(lane_g == head + n_heads, dbda, 0.0), axis=1, keepdims=True)
    lane_h = lax.broadcasted_iota(jnp.int32, a_log.shape, 1)
    al = jnp.sum(jnp.where(lane_h == head, a_log, 0.0), axis=1, keepdims=True)
    dtb = jnp.sum(jnp.where(lane_h == head, dt_bias, 0.0), axis=1, keepdims=True)
    beta = _sigmoid(db)
    g = -jnp.exp(al) * _softplus(da + dtb)
    q = aq * lax.rsqrt(jnp.sum(aq * aq, axis=1, keepdims=True) + EPS) * (HEAD ** -0.5)
    k = ak * lax.rsqrt(jnp.sum(ak * ak, axis=1, keepdims=True) + EPS)
    ri = lax.broadcasted_iota(jnp.int32, (c, c), 0)
    ci = lax.broadcasted_iota(jnp.int32, (c, c), 1)
    incl = ri >= ci
    g_row = jnp.sum(jnp.where(ri == ci, g, 0.0), axis=0, keepdims=True)
    gc_col = jnp.sum(jnp.where(incl, g_row, 0.0), axis=1, keepdims=True)
    gc_row = jnp.sum(jnp.where(ri <= ci, g, 0.0), axis=0, keepdims=True)
    gc_last = jnp.sum(g, axis=0, keepdims=True)
    decay = jnp.where(incl, jnp.exp(jnp.where(incl, gc_col - gc_row, 0.0)), 0.0)
    kb = k * beta
    m = -jnp.where(ri > ci, _dot_nt(kb, k) * decay, 0.0)
    x = jnp.where(ri == ci, 1.0, 0.0) + m
    p = m
    for _ in range(int(math.log2(c)) - 1):
        p = _dot(p, p)
        x = x + _dot(x, p)
    egc = jnp.exp(gc_col)
    w_c = _dot(x, kb * egc)
    u_c = _dot(x, v * beta)
    qk = _dot_nt(q, k) * decay
    v_new = u_c - _dot(w_c, state)
    o = _dot(q * egc, state) + _dot(qk, v_new)
    new_state = state * jnp.exp(gc_last) + _dot_tn(k * jnp.exp(gc_last - gc_col), v_new)
    o = o * lax.rsqrt(jnp.mean(o * o, axis=1, keepdims=True) + EPS) * dn_norm
    return o * _silu(z), new_state


def _dn_specs(seq, n_heads, small):
    col = lambda off: pl.BlockSpec((seq, HEAD), lambda b, h: (b, off + h))
    full = [pl.BlockSpec(a.shape, lambda b, h: (0, 0)) for a in small]
    gates = pl.BlockSpec((seq, 2 * n_heads), lambda b, h: (b, 0))
    return col, gates, full


def _dn_fwd(y, z, dbda, a_log, dt_bias, dn_norm, *, batch, seq):
    t, dd = z.shape
    n_heads = dd // HEAD
    n_chunks = seq // CHUNK

    def body(q_ref, k_ref, v_ref, z_ref, g_ref, al_ref, dt_ref, nw_ref, o_ref):
        head = pl.program_id(1)
        al, dtb, nw = al_ref[...], dt_ref[...], nw_ref[...]

        def step(n, state):
            rows = pl.ds(pl.multiple_of(n * CHUNK, CHUNK), CHUNK)
            out, state = _dn_chunk(head, n_heads, q_ref[rows, :], k_ref[rows, :], v_ref[rows, :], z_ref[rows, :],
                                   g_ref[rows, :], al, dtb, nw, state)
            o_ref[rows, :] = out
            return state

        lax.fori_loop(0, n_chunks, step, jnp.zeros((HEAD, HEAD), F32))

    col, gates, full = _dn_specs(seq, n_heads, (a_log, dt_bias, dn_norm))
    return pl.pallas_call(
        body, name="dn_fwd", grid=(batch, n_heads), in_specs=[col(0), col(n_heads), col(2 * n_heads), col(0), gates, *full],
        out_specs=col(0), out_shape=SDS((t, dd), F32), compiler_params=_params(2),
    )(y, y, y, z, dbda, a_log, dt_bias, dn_norm)


def _dn_bwd(y, z, dbda, a_log, dt_bias, dn_norm, d_out, *, batch, seq):
    t, dd = z.shape
    n_heads = dd // HEAD
    n_chunks = seq // CHUNK

    def body(q_ref, k_ref, v_ref, z_ref, g_ref, al_ref, dt_ref, nw_ref, do_ref,
             dq_ref, dk_ref, dv_ref, dz_ref, dg_ref, dal_ref, ddt_ref, dnw_ref, states):
        b, head = pl.program_id(0), pl.program_id(1)
        al, dtb, nw = al_ref[...], dt_ref[...], nw_ref[...]

        @pl.when((b == 0) & (head == 0))
        def _():
            dal_ref[...] = jnp.zeros_like(dal_ref)
            ddt_ref[...] = jnp.zeros_like(ddt_ref)
            dnw_ref[...] = jnp.zeros_like(dnw_ref)

        @pl.when(head == 0)
        def _():
            dg_ref[...] = jnp.zeros_like(dg_ref)

        def chunk(n):
            rows = pl.ds(pl.multiple_of(n * CHUNK, CHUNK), CHUNK)
            return rows, (q_ref[rows, :], k_ref[rows, :], v_ref[rows, :], z_ref[rows, :], g_ref[rows, :], al, dtb, nw)

        def fwd_step(n, state):
            states[n] = state
            return _dn_chunk(head, n_heads, *chunk(n)[1], state)[1]

        lax.fori_loop(0, n_chunks, fwd_step, jnp.zeros((HEAD, HEAD), F32))

        def bwd_step(i, carry):
            d_state, d_al, d_dt, d_nw = carry
            n = n_chunks - 1 - i
            rows, args = chunk(n)
            _, vjp = jax.vjp(functools.partial(_dn_chunk, head, n_heads), *args, states[n])
            gq, gk, gv, gz, gg, gal, gdt, gnw, d_state = vjp((do_ref[rows, :], d_state))
            dq_ref[rows, :] = gq
            dk_ref[rows, :] = gk
            dv_ref[rows, :] = gv
            dz_ref[rows, :] = gz
            dg_ref[rows, :] += gg
            return d_state, d_al + gal, d_dt + gdt, d_nw + gnw

        zero = lambda a: jnp.zeros(a.shape, F32)
        _, d_al, d_dt, d_nw = lax.fori_loop(
            0, n_chunks, bwd_step, (jnp.zeros((HEAD, HEAD), F32), zero(al), zero(dtb), zero(nw)))
        dal_ref[...] += d_al
        ddt_ref[...] += d_dt
        dnw_ref[...] += d_nw

    col, gates, full = _dn_specs(seq, n_heads, (a_log, dt_bias, dn_norm))
    return pl.pallas_call(
        body, name="dn_bwd", grid=(batch, n_heads),
        in_specs=[col(0), col(n_heads), col(2 * n_heads), col(0), gates, *full, col(0)],
        out_specs=[col(0), col(0), col(0), col(0), gates, *full],
        out_shape=[SDS((t, dd), F32)] * 4 + [SDS(dbda.shape, F32), SDS(a_log.shape, F32), SDS(dt_bias.shape, F32),
                                             SDS(dn_norm.shape, F32)],
        scratch_shapes=[pltpu.VMEM((n_chunks, HEAD, HEAD), F32)], compiler_params=_params(2),
    )(y, y, y, z, dbda, a_log, dt_bias, dn_norm, d_out)


def _my_slot():
    return 4 * lax.axis_index("x") + 2 * lax.axis_index("y") + lax.axis_index("c")


def _peer(k):
    x, y, c = lax.axis_index("x"), lax.axis_index("y"), lax.axis_index("c")
    return (x ^ (k >> 2), y ^ ((k >> 1) & 1), c ^ (k & 1)), (4 * x + 2 * y + c) ^ k


def _all_gather(name, block):
    def body(src, dst, send_sems, recv_sems, local_sem):
        me = _my_slot()
        own = pltpu.make_async_copy(src, dst.at[me], local_sem)
        own.start()
        copies = []
        for k in range(1, N_DEV):
            peer, _ = _peer(k)
            copies.append(pltpu.make_async_remote_copy(
                src_ref=src, dst_ref=dst.at[me], send_sem=send_sems.at[k - 1], recv_sem=recv_sems.at[k - 1],
                device_id=peer, device_id_type=MESH))
            copies[-1].start()
        for k in range(1, N_DEV):
            peer, slot = _peer(k)
            pltpu.make_async_remote_copy(
                src_ref=src, dst_ref=dst.at[slot], send_sem=send_sems.at[k - 1], recv_sem=recv_sems.at[k - 1],
                device_id=peer, device_id_type=MESH).wait_recv()
        for cp in copies:
            cp.wait_send()
        own.wait()

    return pl.pallas_call(
        body, name=name, in_specs=[pl.BlockSpec(memory_space=pl.ANY)], out_specs=pl.BlockSpec(memory_space=pl.ANY),
        out_shape=SDS((N_DEV, *block.shape), block.dtype),
        scratch_shapes=[pltpu.SemaphoreType.DMA((N_DEV - 1,)), pltpu.SemaphoreType.DMA((N_DEV - 1,)), pltpu.SemaphoreType.DMA],
    )(block)


def _exchange_slices(name, parts):
    def body(src, dst, send_sems, recv_sems, local_sem):
        me = _my_slot()
        own = pltpu.make_async_copy(src.at[me], dst.at[me], local_sem)
        own.start()
        copies = []
        for k in range(1, N_DEV):
            peer, slot = _peer(k)
            copies.append(pltpu.make_async_remote_copy(
                src_ref=src.at[slot], dst_ref=dst.at[me], send_sem=send_sems.at[k - 1], recv_sem=recv_sems.at[k - 1],
                device_id=peer, device_id_type=MESH))
            copies[-1].start()
        for k in range(1, N_DEV):
            peer, slot = _peer(k)
            pltpu.make_async_remote_copy(
                src_ref=src.at[me], dst_ref=dst.at[slot], send_sem=send_sems.at[k - 1], recv_sem=recv_sems.at[k - 1],
                device_id=peer, device_id_type=MESH).wait_recv()
        for cp in copies:
            cp.wait_send()
        own.wait()

    return pl.pallas_call(
        body, name=name, in_specs=[pl.BlockSpec(memory_space=pl.ANY)], out_specs=pl.BlockSpec(memory_space=pl.ANY),
        out_shape=SDS(parts.shape, parts.dtype),
        scratch_shapes=[pltpu.SemaphoreType.DMA((N_DEV - 1,)), pltpu.SemaphoreType.DMA((N_DEV - 1,)), pltpu.SemaphoreType.DMA],
    )(parts)


def _adamw(name, parts, w, m, v):
    r, c = w.shape
    tr = _tile(r, max(16, (12 * 2 ** 20) // (44 * c)), 16)
    bc1 = 1.0 / (1.0 - ADAM_B1 ** ADAM_STEP)
    bc2 = 1.0 / (1.0 - ADAM_B2 ** ADAM_STEP)

    def body(p_ref, w_ref, m_ref, v_ref, g_ref, d_ref, nm_ref, nv_ref):
        g = p_ref[0].astype(F32)
        for s in range(1, N_DEV):
            g = g + p_ref[s].astype(F32)
        nm = ADAM_B1 * m_ref[...] + (1.0 - ADAM_B1) * g
        nv = ADAM_B2 * v_ref[...] + (1.0 - ADAM_B2) * (g * g)
        g_ref[...] = g
        nm_ref[...] = nm
        nv_ref[...] = nv
        d_ref[...] = -ADAM_LR * ((nm * bc1) / (jnp.sqrt(nv * bc2) + ADAM_EPS) + ADAM_WD * w_ref[...])

    blk = pl.BlockSpec((tr, c), lambda i: (i, 0))
    return pl.pallas_call(
        body, name=name, grid=(r // tr,), in_specs=[pl.BlockSpec((N_DEV, tr, c), lambda i: (0, i, 0)), blk, blk, blk],
        out_specs=[blk] * 4, out_shape=[SDS((r, c), F32)] * 4, compiler_params=_params(1))(parts, w, m, v)


def _ffn_fwd(name, h, norm, wg, wu, wd):
    n = _rms_fwd(name + "_norm", h, norm)
    gate, up, act = _ffn_up(n, wg, wu)
    return _ffn_down(act, wd, h, 0.5), (n, gate, up, act)


def _ffn_bwd(name, h, norm, wg, wu, wd, saved, dh, dy_b, scale_out):
    n, gate, up, act = saved
    d_gate, d_up = _ffn_bwd_act(dy_b, wd, gate, up)
    g_wd = _wgrad_rows(name + "_dwd", act, dy_b)
    g_wg = _wgrad_cols(name + "_dwg", n, d_gate)
    g_wu = _wgrad_cols(name + "_dwu", n, d_up)
    dn = _dgrad_cols(name + "_dn", (d_gate, d_up), (wg, wu))
    dh_in, dh_in_b, g_norm = _rms_bwd(name + "_norm_bwd", h, dn, norm, dh, scale_out)
    return dh_in, dh_in_b, g_norm, g_wg, g_wu, g_wd


def _local_step(x, target, norms, small, conv_w, wts, *, batch, seq):
    n1w, nmw, n2w, nfw = norms
    a_log, dt_bias, dn_norm = small
    wg1, wu1, wd1, w_in, w_out, wg2, wu2, wd2 = wts
    t, d = x.shape
    p = w_in.shape[2]
    d_mix = w_out.shape[0] * w_out.shape[1]
    da = dd = d_mix // 2
    n_dn = dd // HEAD

    h1, saved1 = _ffn_fwd("ffn1", x, n1w, wg1, wu1, wd1)
    nm = _rms_fwd("mix_norm", h1, nmw)
    proj = jnp.swapaxes(_in_proj(nm, w_in), 0, 1).reshape(t, N_DEV * p)
    qkv = proj[:, :3 * da].astype(BF16)
    xd = proj[:, 3 * da:3 * da + 3 * dd]
    z = proj[:, 3 * da + 3 * dd:3 * da + 4 * dd]
    dbda = proj[:, 3 * da + 4 * dd:]
    attn, lse = _attn_fwd(qkv, batch=batch, seq=seq)
    yd = _conv_silu_fwd(xd, conv_w, batch=batch, seq=seq)
    dn_out = _dn_fwd(yd, z, dbda, a_log, dt_bias, dn_norm, batch=batch, seq=seq)
    cat = jnp.concatenate([attn, dn_out], axis=1).astype(BF16)
    w_out2 = w_out.reshape(d_mix, d)
    h2 = _out_proj(cat, w_out2, h1)
    h3, saved2 = _ffn_fwd("ffn2", h2, n2w, wg2, wu2, wd2)

    loss, dh3, dh3_b, g_nf = _loss_head(h3, nfw, target, 0.5)
    dh2, dh2_b, g_n2, g_wg2, g_wu2, g_wd2 = _ffn_bwd("ffn2", h2, n2w, wg2, wu2, wd2, saved2, dh3, dh3_b, 1.0)

    g_wout = _wgrad_full("dw_out", cat, dh2_b).reshape(w_out.shape)
    dcat = _dgrad_full("d_cat", dh2_b, w_out2)
    d_attn, d_dn = dcat[:, :da], dcat[:, da:]
    dq, dk, dv = _attn_bwd(qkv, attn, lse, d_attn, batch=batch, seq=seq)
    gq, gk, gv, gz, g_dbda, g_alog, g_dtb, g_dnn = _dn_bwd(yd, z, dbda, a_log, dt_bias, dn_norm, d_dn, batch=batch, seq=seq)
    d_xd, g_conv = _conv_silu_bwd(xd, conv_w, jnp.concatenate([gq, gk, gv], axis=1), batch=batch, seq=seq)
    dproj = jnp.concatenate([dq, dk, dv, d_xd, gz, g_dbda], axis=1).astype(BF16)
    dproj = jnp.swapaxes(dproj.reshape(t, N_DEV, p), 0, 1)
    g_win = _wgrad_cols("dw_in", nm, dproj)
    dnm = _dgrad_cols("d_mix_in", (dproj,), (w_in,))
    dh1, dh1_b, g_nm = _rms_bwd("mix_norm_bwd", h1, dnm, nmw, dh2, 0.5)

    dx, _, g_n1, g_wg1, g_wu1, g_wd1 = _ffn_bwd("ffn1", x, n1w, wg1, wu1, wd1, saved1, dh1, dh1_b, 1.0)
    return (loss, dx, (g_n1, g_nm, g_n2, g_nf), (g_alog, g_dtb, g_dnn), g_conv,
            (g_wg1, g_wu1, g_wd1, g_win, g_wout, g_wg2, g_wu2, g_wd2))


def _pack_rows(vectors):
    rows, offsets, r = [], [], 0
    for vec in vectors:
        n = -(-vec.size // 128)
        rows.append(jnp.pad(vec.reshape(-1), (0, n * 128 - vec.size)).reshape(n, 128))
        offsets.append((r, vec.size, vec.shape))
        r += n
    pad = -r % 8
    if pad:
        rows.append(jnp.zeros((pad, 128), F32))
    return jnp.concatenate(rows, axis=0), offsets


def _unpack_rows(packed, offsets):
    return [packed[r:r + -(-size // 128)].reshape(-1)[:size].reshape(shape) for r, size, shape in offsets]


def kernel(x, ffn1_norm, ffn1_w_gate, ffn1_w_up, ffn1_w_down, mix_norm, w_in, conv_w, a_log, dt_bias, dn_norm, w_out, ffn2_norm, ffn2_w_gate, ffn2_w_up, ffn2_w_down, final_norm, loss_target, m_ffn1_norm, m_ffn1_w_gate, m_ffn1_w_up, m_ffn1_w_down, m_mix_norm, m_w_in, m_conv_w, m_a_log, m_dt_bias, m_dn_norm, m_w_out, m_ffn2_norm, m_ffn2_w_gate, m_ffn2_w_up, m_ffn2_w_down, m_final_norm, v_ffn1_norm, v_ffn1_w_gate, v_ffn1_w_up, v_ffn1_w_down, v_mix_norm, v_w_in, v_conv_w, v_a_log, v_dt_bias, v_dn_norm, v_w_out, v_ffn2_norm, v_ffn2_w_gate, v_ffn2_w_up, v_ffn2_w_down, v_final_norm):
    batch, seq, d = x.shape
    t = batch * seq
    big = dict(ffn1_w_gate=(ffn1_w_gate, m_ffn1_w_gate, v_ffn1_w_gate), ffn1_w_up=(ffn1_w_up, m_ffn1_w_up, v_ffn1_w_up),
               ffn1_w_down=(ffn1_w_down, m_ffn1_w_down, v_ffn1_w_down), w_in=(w_in, m_w_in, v_w_in),
               w_out=(w_out, m_w_out, v_w_out), ffn2_w_gate=(ffn2_w_gate, m_ffn2_w_gate, v_ffn2_w_gate),
               ffn2_w_up=(ffn2_w_up, m_ffn2_w_up, v_ffn2_w_up), ffn2_w_down=(ffn2_w_down, m_ffn2_w_down, v_ffn2_w_down))
    rep = dict(ffn1_norm=(ffn1_norm, m_ffn1_norm, v_ffn1_norm), mix_norm=(mix_norm, m_mix_norm, v_mix_norm),
               ffn2_norm=(ffn2_norm, m_ffn2_norm, v_ffn2_norm), final_norm=(final_norm, m_final_norm, v_final_norm),
               a_log=(a_log, m_a_log, v_a_log), dt_bias=(dt_bias, m_dt_bias, v_dt_bias), dn_norm=(dn_norm, m_dn_norm, v_dn_norm))

    wts = [_all_gather("gather_" + name, w.astype(BF16)) for name, (w, _, _) in big.items()]
    conv_all = _all_gather("gather_conv_w", conv_w)
    conv_full = jnp.swapaxes(conv_all, 0, 1).reshape(CONV_WIDTH, N_DEV * conv_w.shape[1])

    row = lambda a: a.reshape(1, -1)
    loss, dx, g_norms, g_small, g_conv, g_big = _local_step(
        x.reshape(t, d), loss_target.reshape(t, d), [row(rep[n][0]) for n in ("ffn1_norm", "mix_norm", "ffn2_norm", "final_norm")],
        [row(rep[n][0]) for n in ("a_log", "dt_bias", "dn_norm")], conv_full, wts, batch=batch, seq=seq)

    out = {"grad_x": dx.reshape(x.shape)}
    for (name, (w, m, v)), g in zip(big.items(), g_big):
        parts = _exchange_slices("scatter_" + name, g)
        out["grad_" + name], out["delta_" + name], out["new_m_" + name], out["new_v_" + name] = _adamw("adamw_" + name, parts, w, m, v)
    conv_parts = jnp.swapaxes(g_conv.reshape(CONV_WIDTH, N_DEV, conv_w.shape[1]), 0, 1)
    parts = _exchange_slices("scatter_conv_w", conv_parts)
    out["grad_conv_w"], out["delta_conv_w"], out["new_m_conv_w"], out["new_v_conv_w"] = _adamw("adamw_conv_w", parts, conv_w, m_conv_w, v_conv_w)

    rep_names = list(rep)
    g_rep = [*g_norms, *g_small]
    packed_g, offsets = _pack_rows([*g_rep, loss[:, :1]])
    packed = [_pack_rows([*[rep[n][i] for n in rep_names], jnp.zeros((1, 1), F32)])[0] for i in range(3)]
    parts = _all_gather("gather_small_grads", packed_g)
    res = [_unpack_rows(a, offsets) for a in _adamw("adamw_small", parts, *packed)]
    for i, name in enumerate(rep_names):
        shape = rep[name][0].shape
        out["grad_" + name], out["delta_" + name], out["new_m_" + name], out["new_v_" + name] = (r[i].reshape(shape) for r in res)
    out["loss"] = res[0][-1].reshape(())

    order = ["ffn1_norm", "ffn1_w_gate", "ffn1_w_up", "ffn1_w_down", "mix_norm", "w_in", "conv_w", "a_log", "dt_bias", "dn_norm",
             "w_out", "ffn2_norm", "ffn2_w_gate", "ffn2_w_up", "ffn2_w_down", "final_norm"]
    return (out["loss"], out["grad_x"], *[out["grad_" + n] for n in order], *[out["delta_" + n] for n in order],
            *[out["new_m_" + n] for n in order], *[out["new_v_" + n] for n in order])
```

```python
import functools
import math

import jax
import jax.numpy as jnp
from jax import lax
from jax.experimental import pallas as pl
from jax.experimental.pallas import tpu as pltpu

F32 = jnp.float32
BF16 = jnp.bfloat16
N_DEV = 8
HEAD = 128
CHUNK = 64
CHUNK_BITS = 6
DN_ROWS = 256
CONV_WIDTH = 4
EPS = 1e-6
DILATED_CONFIGS = ((128, 1), (512, 4), (2048, 16))
ATTN_BLOCK = 256
NEG = -1e30
ADAM_LR, ADAM_B1, ADAM_B2, ADAM_EPS, ADAM_WD, ADAM_STEP = 0.001, 0.9, 0.999, 1e-08, 0.01, 10
HI = lax.Precision.HIGHEST
MESH = pl.DeviceIdType.MESH
SDS = jax.ShapeDtypeStruct


def _tile(n, pref, align):
    t = (min(n, pref) // align) * align
    while t >= align:
        if n % t == 0:
            return t
        t -= align
    return n


def _params(n_axes, vmem_mb=48):
    return pltpu.CompilerParams(dimension_semantics=("arbitrary",) * n_axes, vmem_limit_bytes=vmem_mb * 2 ** 20)


def _sigmoid(x):
    return 1.0 / (1.0 + jnp.exp(-x))


def _silu(x):
    return x * _sigmoid(x)


def _softplus(x):
    return jnp.maximum(x, 0.0) + jnp.log(1.0 + jnp.exp(-jnp.abs(x)))


_DIMS = {"nn": (((1,), (0,)), ((), ())), "nt": (((1,), (1,)), ((), ())), "tn": (((0,), (0,)), ((), ()))}


def _mm_call(name, grid, mode, pairs, operands, in_specs, out_shape, out_specs, acc_shapes, epilogue, vmem_mb=48, after=None):
    dims = _DIMS[mode]
    if after is not None:
        operands, in_specs = (*operands, after), [*in_specs, pl.BlockSpec(memory_space=pl.ANY)]
    n_in, n_out = len(operands), len(out_shape)
    nk = grid[-1]

    def body(*refs):
        ins, outs, accs = refs[:n_in], refs[n_in:n_in + n_out], refs[n_in + n_out:]
        k = pl.program_id(len(grid) - 1)

        @pl.when(k == 0)
        def _():
            for acc in accs:
                acc[...] = jnp.zeros_like(acc)

        for a, b, c in pairs:
            accs[c][...] += lax.dot_general(ins[a][...], ins[b][...], dims, preferred_element_type=F32)

        @pl.when(k == nk - 1)
        def _():
            epilogue(ins, outs, [acc[...] for acc in accs])

    return pl.pallas_call(
        body, name=name, grid=grid, in_specs=in_specs, out_specs=out_specs, out_shape=out_shape,
        scratch_shapes=[pltpu.VMEM(s, F32) for s in acc_shapes], compiler_params=_params(len(grid), vmem_mb),
    )(*operands)


def _ffn_up(n, wg, wu):
    t, d = n.shape
    f = wg.shape[2]
    tm, tk = _tile(t, 512, 16), _tile(d, 512, 128)

    def epilogue(ins, outs, accs):
        g, u = accs
        outs[0][...] = g.astype(BF16)
        outs[1][...] = u.astype(BF16)
        outs[2][...] = (_silu(g) * u).astype(BF16)

    w_spec = pl.BlockSpec((None, tk, f), lambda s, m, k: (s, k, 0))
    o_spec = pl.BlockSpec((None, tm, f), lambda s, m, k: (s, m, 0))
    return _mm_call(
        "ffn_up", (N_DEV, t // tm, d // tk), "nn", [(0, 1, 0), (0, 2, 1)], (n, wg, wu),
        [pl.BlockSpec((tm, tk), lambda s, m, k: (m, k)), w_spec, w_spec],
        [SDS((N_DEV, t, f), BF16)] * 3, [o_spec] * 3, [(tm, f)] * 2, epilogue)


def _ffn_down(act, wd, resid, scale):
    _, t, f = act.shape
    d = wd.shape[2]
    tm, tn = _tile(t, 512, 16), _tile(d, 1024, 128)

    def epilogue(ins, outs, accs):
        outs[0][...] = ins[2][...] + scale * accs[0]

    rc = pl.BlockSpec((tm, tn), lambda m, n, s: (m, n))
    return _mm_call(
        "ffn_down", (t // tm, d // tn, N_DEV), "nn", [(0, 1, 0)], (act, wd, resid),
        [pl.BlockSpec((None, tm, f), lambda m, n, s: (s, m, 0)), pl.BlockSpec((None, f, tn), lambda m, n, s: (s, 0, n)), rc],
        [SDS((t, d), F32)], [rc], [(tm, tn)], epilogue)[0]


def _in_proj(n, w):
    t, d = n.shape
    p = w.shape[2]
    tm, tk = _tile(t, 512, 16), _tile(d, 512, 128)

    def epilogue(ins, outs, accs):
        outs[0][...] = accs[0]

    return _mm_call(
        "in_proj", (N_DEV, t // tm, d // tk), "nn", [(0, 1, 0)], (n, w),
        [pl.BlockSpec((tm, tk), lambda s, m, k: (m, k)), pl.BlockSpec((None, tk, p), lambda s, m, k: (s, k, 0))],
        [SDS((N_DEV, t, p), F32)], [pl.BlockSpec((None, tm, p), lambda s, m, k: (s, m, 0))], [(tm, p)], epilogue)[0]


def _out_proj(cat, w, resid):
    t, dm = cat.shape
    d = w.shape[1]
    tm, tn, tk = _tile(t, 512, 16), _tile(d, 1024, 128), _tile(dm, 1024, 128)

    def epilogue(ins, outs, accs):
        outs[0][...] = ins[2][...] + accs[0]

    rc = pl.BlockSpec((tm, tn), lambda m, n, k: (m, n))
    return _mm_call(
        "out_proj", (t // tm, d // tn, dm // tk), "nn", [(0, 1, 0)], (cat, w, resid),
        [pl.BlockSpec((tm, tk), lambda m, n, k: (m, k)), pl.BlockSpec((tk, tn), lambda m, n, k: (k, n)), rc],
        [SDS((t, d), F32)], [rc], [(tm, tn)], epilogue)[0]


def _ffn_bwd_act(dy, wd, gate, up, after=None):
    t, d = dy.shape
    f = wd.shape[1]
    tm, tk = _tile(t, 512, 16), _tile(d, 512, 128)

    def epilogue(ins, outs, accs):
        g, u = ins[2][...].astype(F32), ins[3][...].astype(F32)
        sg = _sigmoid(g)
        outs[0][...] = (accs[0] * u * sg * (1.0 + g * (1.0 - sg))).astype(BF16)
        outs[1][...] = (accs[0] * g * sg).astype(BF16)

    o_spec = pl.BlockSpec((None, tm, f), lambda s, m, k: (s, m, 0))
    return _mm_call(
        "ffn_bwd_act", (N_DEV, t // tm, d // tk), "nt", [(0, 1, 0)], (dy, wd, gate, up),
        [pl.BlockSpec((tm, tk), lambda s, m, k: (m, k)), pl.BlockSpec((None, f, tk), lambda s, m, k: (s, 0, k)), o_spec, o_spec],
        [SDS((N_DEV, t, f), BF16)] * 2, [o_spec] * 2, [(tm, f)], epilogue, after=after)


def _store_bf16(ins, outs, accs):
    outs[0][...] = accs[0].astype(BF16)


def _store_f32(ins, outs, accs):
    outs[0][...] = accs[0]


def _wgrad_cols(name, a, b, after=None):
    t, m = a.shape
    n = b.shape[2]
    tm, tk = _tile(m, 1024, 128), _tile(t, 512, 16)
    return _mm_call(
        name, (N_DEV, m // tm, t // tk), "tn", [(0, 1, 0)], (a, b),
        [pl.BlockSpec((tk, tm), lambda s, i, k: (k, i)), pl.BlockSpec((None, tk, n), lambda s, i, k: (s, k, 0))],
        [SDS((N_DEV, m, n), BF16)], [pl.BlockSpec((None, tm, n), lambda s, i, k: (s, i, 0))], [(tm, n)], _store_bf16,
        after=after)[0]


def _wgrad_rows(name, a, b, after=None):
    _, t, m = a.shape
    n = b.shape[1]
    tn, tk = _tile(n, 1024, 128), _tile(t, 512, 16)
    return _mm_call(
        name, (N_DEV, n // tn, t // tk), "tn", [(0, 1, 0)], (a, b),
        [pl.BlockSpec((None, tk, m), lambda s, j, k: (s, k, 0)), pl.BlockSpec((tk, tn), lambda s, j, k: (k, j))],
        [SDS((N_DEV, m, n), BF16)], [pl.BlockSpec((None, m, tn), lambda s, j, k: (s, 0, j))], [(m, tn)], _store_bf16,
        after=after)[0]


def _wgrad_full(name, a, b, after=None):
    t, m = a.shape
    n = b.shape[1]
    tm, tn, tk = _tile(m, 1024, 128), _tile(n, 1024, 128), _tile(t, 512, 16)
    return _mm_call(
        name, (m // tm, n // tn, t // tk), "tn", [(0, 1, 0)], (a, b),
        [pl.BlockSpec((tk, tm), lambda i, j, k: (k, i)), pl.BlockSpec((tk, tn), lambda i, j, k: (k, j))],
        [SDS((m, n), BF16)], [pl.BlockSpec((tm, tn), lambda i, j, k: (i, j))], [(tm, tn)], _store_bf16, after=after)[0]


def _dgrad_cols(name, grads, weights, after=None):
    _, t, n = grads[0].shape
    m = weights[0].shape[1]
    tm, tn = _tile(t, 512, 16), _tile(m, 1024, 128)
    k = len(grads)
    return _mm_call(
        name, (t // tm, m // tn, N_DEV), "nt", [(i, k + i, 0) for i in range(k)], (*grads, *weights),
        [pl.BlockSpec((None, tm, n), lambda i, j, s: (s, i, 0))] * k + [pl.BlockSpec((None, tn, n), lambda i, j, s: (s, j, 0))] * k,
        [SDS((t, m), F32)], [pl.BlockSpec((tm, tn), lambda i, j, s: (i, j))], [(tm, tn)], _store_f32, after=after)[0]


def _dgrad_full(name, g, w, after=None):
    t, n = g.shape
    m = w.shape[0]
    tm, tn, tk = _tile(t, 512, 16), _tile(m, 1024, 128), _tile(n, 1024, 128)
    return _mm_call(
        name, (t // tm, m // tn, n // tk), "nt", [(0, 1, 0)], (g, w),
        [pl.BlockSpec((tm, tk), lambda i, j, k: (i, k)), pl.BlockSpec((tn, tk), lambda i, j, k: (j, k))],
        [SDS((t, m), F32)], [pl.BlockSpec((tm, tn), lambda i, j, k: (i, j))], [(tm, tn)], _store_f32, after=after)[0]


def _rms_fwd(name, h, w):
    t, d = h.shape
    tm = _tile(t, 256, 16)

    def body(h_ref, w_ref, o_ref):
        x = h_ref[...]
        o_ref[...] = (x * lax.rsqrt(jnp.mean(x * x, axis=1, keepdims=True) + EPS) * w_ref[...]).astype(BF16)

    row = pl.BlockSpec((tm, d), lambda i: (i, 0))
    return pl.pallas_call(
        body, name=name, grid=(t // tm,), in_specs=[row, pl.BlockSpec((1, d), lambda i: (0, 0))], out_specs=row,
        out_shape=SDS((t, d), BF16), compiler_params=_params(1))(h, w)


def _rms_bwd(name, h, dn, w, dres, scale):
    t, d = h.shape
    tm = _tile(t, 128, 16)

    def body(h_ref, dn_ref, w_ref, dres_ref, dh_ref, dhb_ref, dw_ref):
        @pl.when(pl.program_id(0) == 0)
        def _():
            dw_ref[...] = jnp.zeros_like(dw_ref)

        x = h_ref[...]
        rstd = lax.rsqrt(jnp.mean(x * x, axis=1, keepdims=True) + EPS)
        nhat = x * rstd
        g = dn_ref[...]
        gw = g * w_ref[...]
        dh = dres_ref[...] + rstd * (gw - nhat * jnp.mean(gw * nhat, axis=1, keepdims=True))
        dh_ref[...] = dh
        dhb_ref[...] = (scale * dh).astype(BF16)
        dw_ref[...] += jnp.sum(g * nhat, axis=0, keepdims=True)

    row = pl.BlockSpec((tm, d), lambda i: (i, 0))
    vec = pl.BlockSpec((1, d), lambda i: (0, 0))
    return pl.pallas_call(
        body, name=name, grid=(t // tm,), in_specs=[row, row, vec, row], out_specs=[row, row, vec],
        out_shape=[SDS((t, d), F32), SDS((t, d), BF16), SDS((1, d), F32)], compiler_params=_params(1))(h, dn, w, dres)


def _loss_head(h, w, target, scale):
    t, d = h.shape
    tm = _tile(t, 128, 16)

    def body(h_ref, w_ref, tg_ref, loss_ref, dh_ref, dhb_ref, dw_ref):
        @pl.when(pl.program_id(0) == 0)
        def _():
            dw_ref[...] = jnp.zeros_like(dw_ref)
            loss_ref[...] = jnp.zeros_like(loss_ref)

        x = h_ref[...]
        rstd = lax.rsqrt(jnp.mean(x * x, axis=1, keepdims=True) + EPS)
        nhat = x * rstd
        wv = w_ref[...]
        err = nhat * wv - tg_ref[...]
        loss_ref[...] += 0.5 * jnp.sum(jnp.mean(err * err, axis=1, keepdims=True), axis=0, keepdims=True)
        g = err * (1.0 / d)
        gw = g * wv
        dh = rstd * (gw - nhat * jnp.mean(gw * nhat, axis=1, keepdims=True))
        dh_ref[...] = dh
        dhb_ref[...] = (scale * dh).astype(BF16)
        dw_ref[...] += jnp.sum(g * nhat, axis=0, keepdims=True)

    row = pl.BlockSpec((tm, d), lambda i: (i, 0))
    vec = pl.BlockSpec((1, d), lambda i: (0, 0))
    return pl.pallas_call(
        body, name="loss_head", grid=(t // tm,), in_specs=[row, vec, row],
        out_specs=[pl.BlockSpec((1, 128), lambda i: (0, 0)), row, row, vec],
        out_shape=[SDS((1, 128), F32), SDS((t, d), F32), SDS((t, d), BF16), SDS((1, d), F32)],
        compiler_params=_params(1))(h, w, target)


def _attn_bias(qi, ki, blk):
    dist = (lax.broadcasted_iota(jnp.int32, (blk, blk), 0) - lax.broadcasted_iota(jnp.int32, (blk, blk), 1)
            + (qi - ki) * blk)
    count = jnp.zeros((blk, blk), F32)
    for window, dil in DILATED_CONFIGS:
        assert dil & (dil - 1) == 0
        seen = (dist >= 0) & (dist <= window) & ((dist & (dil - 1)) == 0)
        count = count + jnp.where(seen, 1.0, 0.0)
    return jnp.where(count > 0.0, jnp.log(jnp.maximum(count, 1.0)), NEG)


def _attn_fwd(qkv, *, batch, seq):
    t, da3 = qkv.shape
    da = da3 // 3
    n_heads = da // HEAD
    blk = _tile(seq, ATTN_BLOCK, 16)
    nq = seq // blk
    sm_scale = HEAD ** -0.5

    def body(q_ref, k_ref, v_ref, o_ref, lse_ref):
        def q_step(qi, _):
            rows = pl.ds(pl.multiple_of(qi * blk, blk), blk)
            q = q_ref[rows, :]

            def kv_step(ki, carry):
                m, l, acc = carry
                cols = pl.ds(pl.multiple_of(ki * blk, blk), blk)
                s = lax.dot_general(q, k_ref[cols, :], _DIMS["nt"], preferred_element_type=F32) * sm_scale
                s = s + _attn_bias(qi, ki, blk)
                m_new = jnp.maximum(m, jnp.max(s, axis=1, keepdims=True))
                alpha = jnp.exp(m - m_new)
                p = jnp.exp(s - m_new)
                l = alpha * l + jnp.sum(p, axis=1, keepdims=True)
                acc = alpha * acc + jnp.dot(p.astype(BF16), v_ref[cols, :], preferred_element_type=F32)
                return m_new, l, acc

            m, l, acc = lax.fori_loop(0, qi + 1, kv_step, (jnp.full((blk, 1), NEG, F32), jnp.zeros((blk, 1), F32),
                                                           jnp.zeros((blk, HEAD), F32)))
            o_ref[rows, :] = acc / l
            lse_ref[rows, :] = jnp.broadcast_to(m + jnp.log(l), (blk, HEAD))
            return 0

        lax.fori_loop(0, nq, q_step, 0)

    col = lambda off: pl.BlockSpec((seq, HEAD), lambda b, h: (b, off + h))
    return pl.pallas_call(
        body, name="attn_fwd", grid=(batch, n_heads), in_specs=[col(0), col(n_heads), col(2 * n_heads)],
        out_specs=[col(0), col(0)], out_shape=[SDS((t, da), F32), SDS((t, da), F32)], compiler_params=_params(2),
    )(qkv, qkv, qkv)


def _attn_bwd(qkv, out, lse, d_out, *, batch, seq):
    t, da = out.shape
    n_heads = da // HEAD
    blk = _tile(seq, ATTN_BLOCK, 16)
    nq = seq // blk
    sm_scale = HEAD ** -0.5

    def body(q_ref, k_ref, v_ref, o_ref, lse_ref, do_ref, dq_ref, dk_ref, dv_ref):
        dk_ref[...] = jnp.zeros_like(dk_ref)
        dv_ref[...] = jnp.zeros_like(dv_ref)

        def q_step(qi, _):
            rows = pl.ds(pl.multiple_of(qi * blk, blk), blk)
            q = q_ref[rows, :]
            do = do_ref[rows, :]
            do_b = do.astype(BF16)
            lse_q = lse_ref[rows, :][:, :1]
            delta = jnp.sum(do * o_ref[rows, :], axis=1, keepdims=True)

            def kv_step(ki, dq):
                cols = pl.ds(pl.multiple_of(ki * blk, blk), blk)
                k = k_ref[cols, :]
                s = lax.dot_general(q, k, _DIMS["nt"], preferred_element_type=F32) * sm_scale
                p = jnp.exp(s + _attn_bias(qi, ki, blk) - lse_q)
                dp = lax.dot_general(do_b, v_ref[cols, :], _DIMS["nt"], preferred_element_type=F32)
                ds = (p * (dp - delta) * sm_scale).astype(BF16)
                dv_ref[cols, :] += lax.dot_general(p.astype(BF16), do_b, _DIMS["tn"], preferred_element_type=F32)
                dk_ref[cols, :] += lax.dot_general(ds, q, _DIMS["tn"], preferred_element_type=F32)
                return dq + jnp.dot(ds, k, preferred_element_type=F32)

            dq_ref[rows, :] = lax.fori_loop(0, qi + 1, kv_step, jnp.zeros((blk, HEAD), F32))
            return 0

        lax.fori_loop(0, nq, q_step, 0)

    col = lambda off: pl.BlockSpec((seq, HEAD), lambda b, h: (b, off + h))
    return pl.pallas_call(
        body, name="attn_bwd", grid=(batch, n_heads),
        in_specs=[col(0), col(n_heads), col(2 * n_heads), col(0), col(0), col(0)], out_specs=[col(0)] * 3,
        out_shape=[SDS((t, da), F32)] * 3, compiler_params=_params(2),
    )(qkv, qkv, qkv, out, lse, d_out)


def _shift_down(x, k, row):
    return x if k == 0 else jnp.where(row >= k, pltpu.roll(x, k, axis=0), 0.0)


def _shift_up(x, k, row):
    n = x.shape[0]
    return x if k == 0 else jnp.where(row < n - k, pltpu.roll(x, n - k, axis=0), 0.0)


def _conv_silu_fwd(x, w, *, batch, seq):
    t, c = x.shape

    def body(x_ref, w_ref, o_ref):
        xv = x_ref[...]
        row = lax.broadcasted_iota(jnp.int32, xv.shape, 0)
        acc = jnp.zeros_like(xv)
        for i in range(CONV_WIDTH):
            acc = acc + w_ref[i:i + 1, :] * _shift_down(xv, CONV_WIDTH - 1 - i, row)
        o_ref[...] = _silu(acc)

    blk = pl.BlockSpec((seq, HEAD), lambda j, b: (b, j))
    return pl.pallas_call(
        body, name="conv_silu_fwd", grid=(c // HEAD, batch), in_specs=[blk, pl.BlockSpec((CONV_WIDTH, HEAD), lambda j, b: (0, j))],
        out_specs=blk, out_shape=SDS((t, c), F32), compiler_params=_params(2))(x, w)


def _conv_silu_bwd(x, w, dy, *, batch, seq):
    t, c = x.shape

    def body(x_ref, w_ref, dy_ref, dx_ref, dw_ref):
        @pl.when(pl.program_id(1) == 0)
        def _():
            dw_ref[...] = jnp.zeros_like(dw_ref)

        xv = x_ref[...]
        row = lax.broadcasted_iota(jnp.int32, xv.shape, 0)
        shifted = [_shift_down(xv, CONV_WIDTH - 1 - i, row) for i in range(CONV_WIDTH)]
        acc = jnp.zeros_like(xv)
        for i in range(CONV_WIDTH):
            acc = acc + w_ref[i:i + 1, :] * shifted[i]
        sg = _sigmoid(acc)
        dc = dy_ref[...] * sg * (1.0 + acc * (1.0 - sg))
        dx = jnp.zeros_like(xv)
        for i in range(CONV_WIDTH):
            dx = dx + w_ref[i:i + 1, :] * _shift_up(dc, CONV_WIDTH - 1 - i, row)
            dw_ref[i:i + 1, :] += jnp.sum(dc * shifted[i], axis=0, keepdims=True)
        dx_ref[...] = dx

    blk = pl.BlockSpec((seq, HEAD), lambda j, b: (b, j))
    wblk = pl.BlockSpec((CONV_WIDTH, HEAD), lambda j, b: (0, j))
    return pl.pallas_call(
        body, name="conv_silu_bwd", grid=(c // HEAD, batch), in_specs=[blk, wblk, blk], out_specs=[blk, wblk],
        out_shape=[SDS((t, c), F32), SDS((CONV_WIDTH, c), F32)], compiler_params=_params(2))(x, w, dy)


def _dot(a, b):
    return jnp.dot(a, b, precision=HI, preferred_element_type=F32)


def _dot_nt(a, b):
    return lax.dot_general(a, b, _DIMS["nt"], precision=HI, preferred_element_type=F32)


def _dot_tn(a, b):
    return lax.dot_general(a, b, _DIMS["tn"], precision=HI, preferred_element_type=F32)


def _dn_chunk(head, n_heads, aq, ak, v, z, dbda, a_log, dt_bias, dn_norm, state):
    r = aq.shape[0]
    lane_g = lax.broadcasted_iota(jnp.int32, dbda.shape, 1)
    db = jnp.sum(jnp.where(lane_g == head, dbda, 0.0), axis=1, keepdims=True)
    da = jnp.sum(jnp.where(lane_g == head + n_heads, dbda, 0.0), axis=1, keepdims=True)
    lane_h = lax.broadcasted_iota(jnp.int32, a_log.shape, 1)
    al = jnp.sum(jnp.where(lane_h == head, a_log, 0.0), axis=1, keepdims=True)
    dtb = jnp.sum(jnp.where(lane_h == head, dt_bias, 0.0), axis=1, keepdims=True)
    beta = _sigmoid(db)
    g = -jnp.exp(al) * _softplus(da + dtb)
    q = aq * lax.rsqrt(jnp.sum(aq * aq, axis=1, keepdims=True) + EPS) * (HEAD ** -0.5)
    k = ak * lax.rsqrt(jnp.sum(ak * ak, axis=1, keepdims=True) + EPS)
    ri = lax.broadcasted_iota(jnp.int32, (r, r), 0)
    ci = lax.broadcasted_iota(jnp.int32, (r, r), 1)
    same = (ri >> CHUNK_BITS) == (ci >> CHUNK_BITS)
    incl = same & (ri >= ci)
    g_row = jnp.sum(jnp.where(ri == ci, g, 0.0), axis=0, keepdims=True)
    gc_col = jnp.sum(jnp.where(incl, g_row, 0.0), axis=1, keepdims=True)
    gc_row = jnp.sum(jnp.where(same & (ri <= ci), g, 0.0), axis=0, keepdims=True)
    g_all = jnp.sum(jnp.where(same, g_row, 0.0), axis=1, keepdims=True)
    decay = jnp.where(incl, jnp.exp(jnp.where(incl, gc_col - gc_row, 0.0)), 0.0)
    kb = k * beta
    m = -jnp.where(same & (ri > ci), _dot_nt(kb, k) * decay, 0.0)
    x = jnp.where(ri == ci, 1.0, 0.0) + m
    p = m
    for _ in range(int(math.log2(CHUNK)) - 1):
        p = _dot(p, p)
        x = x + _dot(x, p)
    egc = jnp.exp(gc_col)
    w_g = _dot(x, kb * egc)
    u_g = _dot(x, v * beta)
    qk = _dot_nt(q, k) * decay
    q_dec = q * egc
    k_dec = k * jnp.exp(g_all - gc_col)
    carry = jnp.exp(g_all)
    v_new, o_state = [], []
    for c in range(r // CHUNK):
        rows = slice(c * CHUNK, (c + 1) * CHUNK)
        v_new.append(u_g[rows] - _dot(w_g[rows], state))
        o_state.append(_dot(q_dec[rows], state))
        state = state * carry[c * CHUNK:c * CHUNK + 1] + _dot_tn(k_dec[rows], v_new[-1])
    o = jnp.concatenate(o_state, axis=0) + _dot(qk, jnp.concatenate(v_new, axis=0))
    o = o * lax.rsqrt(jnp.mean(o * o, axis=1, keepdims=True) + EPS) * dn_norm
    return o * _silu(z), state


def _dn_specs(seq, n_heads, small):
    col = lambda off: pl.BlockSpec((seq, HEAD), lambda b, h: (b, off + h))
    full = [pl.BlockSpec(a.shape, lambda b, h: (0, 0)) for a in small]
    gates = pl.BlockSpec((seq, 2 * n_heads), lambda b, h: (b, 0))
    return col, gates, full


def _dn_fwd(y, z, dbda, a_log, dt_bias, dn_norm, *, batch, seq):
    t, dd = z.shape
    n_heads = dd // HEAD
    grp = _tile(seq, DN_ROWS, CHUNK)
    n_chunks = seq // grp

    def body(q_ref, k_ref, v_ref, z_ref, g_ref, al_ref, dt_ref, nw_ref, o_ref):
        head = pl.program_id(1)
        al, dtb, nw = al_ref[...], dt_ref[...], nw_ref[...]

        def step(n, state):
            rows = pl.ds(pl.multiple_of(n * grp, grp), grp)
            out, state = _dn_chunk(head, n_heads, q_ref[rows, :], k_ref[rows, :], v_ref[rows, :], z_ref[rows, :],
                                   g_ref[rows, :], al, dtb, nw, state)
            o_ref[rows, :] = out
            return state

        lax.fori_loop(0, n_chunks, step, jnp.zeros((HEAD, HEAD), F32))

    col, gates, full = _dn_specs(seq, n_heads, (a_log, dt_bias, dn_norm))
    return pl.pallas_call(
        body, name="dn_fwd", grid=(batch, n_heads), in_specs=[col(0), col(n_heads), col(2 * n_heads), col(0), gates, *full],
        out_specs=col(0), out_shape=SDS((t, dd), F32), compiler_params=_params(2),
    )(y, y, y, z, dbda, a_log, dt_bias, dn_norm)


def _dn_bwd(y, z, dbda, a_log, dt_bias, dn_norm, d_out, *, batch, seq):
    t, dd = z.shape
    n_heads = dd // HEAD
    grp = _tile(seq, DN_ROWS, CHUNK)
    n_chunks = seq // grp

    def body(q_ref, k_ref, v_ref, z_ref, g_ref, al_ref, dt_ref, nw_ref, do_ref,
             dq_ref, dk_ref, dv_ref, dz_ref, dg_ref, dal_ref, ddt_ref, dnw_ref, states):
        b, head = pl.program_id(0), pl.program_id(1)
        al, dtb, nw = al_ref[...], dt_ref[...], nw_ref[...]

        @pl.when((b == 0) & (head == 0))
        def _():
            dal_ref[...] = jnp.zeros_like(dal_ref)
            ddt_ref[...] = jnp.zeros_like(ddt_ref)
            dnw_ref[...] = jnp.zeros_like(dnw_ref)

        @pl.when(head == 0)
        def _():
            dg_ref[...] = jnp.zeros_like(dg_ref)

        def chunk(n):
            rows = pl.ds(pl.multiple_of(n * grp, grp), grp)
            return rows, (q_ref[rows, :], k_ref[rows, :], v_ref[rows, :], z_ref[rows, :], g_ref[rows, :], al, dtb, nw)

        def fwd_step(n, state):
            states[n] = state
            return _dn_chunk(head, n_heads, *chunk(n)[1], state)[1]

        lax.fori_loop(0, n_chunks, fwd_step, jnp.zeros((HEAD, HEAD), F32))

        def bwd_step(i, carry):
            d_state, d_al, d_dt, d_nw = carry
            n = n_chunks - 1 - i
            rows, args = chunk(n)
            _, vjp = jax.vjp(functools.partial(_dn_chunk, head, n_heads), *args, states[n])
            gq, gk, gv, gz, gg, gal, gdt, gnw, d_state = vjp((do_ref[rows, :], d_state))
            dq_ref[rows, :] = gq
            dk_ref[rows, :] = gk
            dv_ref[rows, :] = gv
            dz_ref[rows, :] = gz
            dg_ref[rows, :] += gg
            return d_state, d_al + gal, d_dt + gdt, d_nw + gnw

        zero = lambda a: jnp.zeros(a.shape, F32)
        _, d_al, d_dt, d_nw = lax.fori_loop(
            0, n_chunks, bwd_step, (jnp.zeros((HEAD, HEAD), F32), zero(al), zero(dtb), zero(nw)))
        dal_ref[...] += d_al
        ddt_ref[...] += d_dt
        dnw_ref[...] += d_nw

    col, gates, full = _dn_specs(seq, n_heads, (a_log, dt_bias, dn_norm))
    return pl.pallas_call(
        body, name="dn_bwd", grid=(batch, n_heads),
        in_specs=[col(0), col(n_heads), col(2 * n_heads), col(0), gates, *full, col(0)],
        out_specs=[col(0), col(0), col(0), col(0), gates, *full],
        out_shape=[SDS((t, dd), F32)] * 4 + [SDS(dbda.shape, F32), SDS(a_log.shape, F32), SDS(dt_bias.shape, F32),
                                             SDS(dn_norm.shape, F32)],
        scratch_shapes=[pltpu.VMEM((n_chunks, HEAD, HEAD), F32)], compiler_params=_params(2),
    )(y, y, y, z, dbda, a_log, dt_bias, dn_norm, d_out)


def _my_slot():
    return 4 * lax.axis_index("x") + 2 * lax.axis_index("y") + lax.axis_index("c")


def _peer(k):
    x, y, c = lax.axis_index("x"), lax.axis_index("y"), lax.axis_index("c")
    return (x ^ (k >> 2), y ^ ((k >> 1) & 1), c ^ (k & 1)), (4 * x + 2 * y + c) ^ k


def _all_gather(name, block, after):
    def body(src, after_ref, dst, send_sems, recv_sems, local_sem):
        me = _my_slot()
        own = pltpu.make_async_copy(src, dst.at[me], local_sem)
        own.start()
        copies = []
        for k in range(1, N_DEV):
            peer, _ = _peer(k)
            copies.append(pltpu.make_async_remote_copy(
                src_ref=src, dst_ref=dst.at[me], send_sem=send_sems.at[k - 1], recv_sem=recv_sems.at[k - 1],
                device_id=peer, device_id_type=MESH))
            copies[-1].start()
        for k in range(1, N_DEV):
            peer, slot = _peer(k)
            pltpu.make_async_remote_copy(
                src_ref=src, dst_ref=dst.at[slot], send_sem=send_sems.at[k - 1], recv_sem=recv_sems.at[k - 1],
                device_id=peer, device_id_type=MESH).wait_recv()
        for cp in copies:
            cp.wait_send()
        own.wait()

    return pl.pallas_call(
        body, name=name, in_specs=[pl.BlockSpec(memory_space=pl.ANY)] * 2, out_specs=pl.BlockSpec(memory_space=pl.ANY),
        out_shape=SDS((N_DEV, *block.shape), block.dtype),
        scratch_shapes=[pltpu.SemaphoreType.DMA((N_DEV - 1,)), pltpu.SemaphoreType.DMA((N_DEV - 1,)), pltpu.SemaphoreType.DMA],
    )(block, after)


def _exchange_slices(name, parts, after):
    def body(src, after_ref, dst, send_sems, recv_sems, local_sem):
        me = _my_slot()
        own = pltpu.make_async_copy(src.at[me], dst.at[me], local_sem)
        own.start()
        copies = []
        for k in range(1, N_DEV):
            peer, slot = _peer(k)
            copies.append(pltpu.make_async_remote_copy(
                src_ref=src.at[slot], dst_ref=dst.at[me], send_sem=send_sems.at[k - 1], recv_sem=recv_sems.at[k - 1],
                device_id=peer, device_id_type=MESH))
            copies[-1].start()
        for k in range(1, N_DEV):
            peer, slot = _peer(k)
            pltpu.make_async_remote_copy(
                src_ref=src.at[me], dst_ref=dst.at[slot], send_sem=send_sems.at[k - 1], recv_sem=recv_sems.at[k - 1],
                device_id=peer, device_id_type=MESH).wait_recv()
        for cp in copies:
            cp.wait_send()
        own.wait()

    return pl.pallas_call(
        body, name=name, in_specs=[pl.BlockSpec(memory_space=pl.ANY)] * 2, out_specs=pl.BlockSpec(memory_space=pl.ANY),
        out_shape=SDS(parts.shape, parts.dtype),
        scratch_shapes=[pltpu.SemaphoreType.DMA((N_DEV - 1,)), pltpu.SemaphoreType.DMA((N_DEV - 1,)), pltpu.SemaphoreType.DMA],
    )(parts, after)


_HBM = pl.BlockSpec(memory_space=pltpu.HBM)
_SEM = pl.BlockSpec(memory_space=pltpu.SEMAPHORE)
_EFFECT = pltpu.SideEffectType.DATAFLOW_SIDE_EFFECTING


def _slot_operand():
    return _my_slot().astype(jnp.int32).reshape(1)


def _cast_place(name, block, dtype):
    r, c = block.shape
    tr = _tile(r, 512, 16)

    def body(me_ref, src_ref, dst_ref):
        dst_ref[...] = src_ref[...].astype(dtype)

    return pl.pallas_call(
        body, name=name, out_shape=SDS((N_DEV, r, c), dtype), compiler_params=_params(1),
        grid_spec=pltpu.PrefetchScalarGridSpec(
            num_scalar_prefetch=1, grid=(r // tr,), in_specs=[pl.BlockSpec((tr, c), lambda i, me: (i, 0))],
            out_specs=pl.BlockSpec((None, tr, c), lambda i, me: (me[0], i, 0))),
    )(_slot_operand(), block)


def _gather_start(name, lands):
    n = len(lands)

    def body(*refs):
        lnds, outs = refs[:n], refs[n:]
        me = _my_slot()
        for i in range(n):
            for k in range(1, N_DEV):
                peer, _ = _peer(k)
                pltpu.make_async_remote_copy(
                    src_ref=lnds[i].at[me], dst_ref=lnds[i].at[me], send_sem=outs[2 * i].at[k - 1],
                    recv_sem=outs[2 * i + 1].at[k - 1], device_id=peer, device_id_type=MESH).start()
        outs[-1][...] = jnp.zeros_like(outs[-1])

    res = pl.pallas_call(
        body, name=name, in_specs=[_HBM] * n,
        out_specs=[_SEM] * (2 * n) + [_HBM] * n + [pl.BlockSpec(memory_space=pltpu.VMEM)],
        out_shape=[pltpu.SemaphoreType.DMA((N_DEV - 1,))] * (2 * n) + [pltpu.HBM(a.shape, a.dtype) for a in lands]
        + [SDS((8, 128), F32)],
        input_output_aliases={i: 2 * n + i for i in range(n)},
        compiler_params=pltpu.CompilerParams(has_side_effects=_EFFECT),
    )(*[pltpu.with_memory_space_constraint(a, pltpu.HBM) for a in lands])
    return [(res[2 * i], res[2 * i + 1], res[2 * n + i]) for i in range(n)], res[-1]


def _gather_wait(name, started, after):
    send_sems, recv_sems, land = started

    def body(land_ref, send_ref, recv_ref, after_ref, land_out):
        me = _my_slot()
        for k in range(1, N_DEV):
            peer, slot = _peer(k)
            copy = pltpu.make_async_remote_copy(
                src_ref=land_ref.at[me], dst_ref=land_ref.at[slot], send_sem=send_ref.at[k - 1],
                recv_sem=recv_ref.at[k - 1], device_id=peer, device_id_type=MESH)
            copy.wait_send()
            copy.wait_recv()

    return pl.pallas_call(
        body, name=name, in_specs=[_HBM, _SEM, _SEM, pl.BlockSpec(memory_space=pl.ANY)], out_specs=[_HBM],
        out_shape=[pltpu.HBM(land.shape, land.dtype)], input_output_aliases={0: 0},
        compiler_params=pltpu.CompilerParams(has_side_effects=_EFFECT),
    )(land, send_sems, recv_sems, after)[0]


def _scatter_start(name, parts):
    land = lax.empty(parts.shape, parts.dtype)

    def body(src, lnd, send_ref, recv_ref, src_out, lnd_out, token):
        me = _my_slot()
        for k in range(1, N_DEV):
            peer, slot = _peer(k)
            pltpu.make_async_remote_copy(
                src_ref=src.at[slot], dst_ref=lnd.at[me], send_sem=send_ref.at[k - 1], recv_sem=recv_ref.at[k - 1],
                device_id=peer, device_id_type=MESH).start()
        token[...] = jnp.zeros_like(token)

    res = pl.pallas_call(
        body, name=name, in_specs=[_HBM, _HBM],
        out_specs=[_SEM, _SEM, _HBM, _HBM, pl.BlockSpec(memory_space=pltpu.VMEM)],
        out_shape=[pltpu.SemaphoreType.DMA((N_DEV - 1,))] * 2 + [pltpu.HBM(parts.shape, parts.dtype)] * 2 + [SDS((8, 128), F32)],
        input_output_aliases={0: 2, 1: 3}, compiler_params=pltpu.CompilerParams(has_side_effects=_EFFECT),
    )(pltpu.with_memory_space_constraint(parts, pltpu.HBM), pltpu.with_memory_space_constraint(land, pltpu.HBM))
    return tuple(res[:4]), res[4]


def _scatter_wait(name, started, after):
    send_sems, recv_sems, parts, land = started

    def body(src_ref, land_ref, send_ref, recv_ref, after_ref, src_out, land_out):
        me = _my_slot()
        for k in range(1, N_DEV):
            peer, slot = _peer(k)
            copy = pltpu.make_async_remote_copy(
                src_ref=src_ref.at[me], dst_ref=land_ref.at[slot], send_sem=send_ref.at[k - 1],
                recv_sem=recv_ref.at[k - 1], device_id=peer, device_id_type=MESH)
            copy.wait_send()
            copy.wait_recv()

    return pl.pallas_call(
        body, name=name, in_specs=[_HBM, _HBM, _SEM, _SEM, pl.BlockSpec(memory_space=pl.ANY)], out_specs=[_HBM, _HBM],
        out_shape=[pltpu.HBM(parts.shape, parts.dtype), pltpu.HBM(land.shape, land.dtype)], input_output_aliases={0: 0, 1: 1},
        compiler_params=pltpu.CompilerParams(has_side_effects=_EFFECT),
    )(parts, land, send_sems, recv_sems, after)


def _adamw(name, landed, own, w, m, v):
    r, c = w.shape
    tr = _tile(r, max(16, (12 * 2 ** 20) // (46 * c)), 16)
    bc1 = 1.0 / (1.0 - ADAM_B1 ** ADAM_STEP)
    bc2 = 1.0 / (1.0 - ADAM_B2 ** ADAM_STEP)

    def body(me_ref, p_ref, own_ref, w_ref, m_ref, v_ref, g_ref, d_ref, nm_ref, nv_ref):
        me = me_ref[0]
        g = jnp.zeros(w_ref.shape, F32)
        for s in range(N_DEV):
            g = g + jnp.where(me == s, own_ref[...], p_ref[s]).astype(F32)
        nm = ADAM_B1 * m_ref[...] + (1.0 - ADAM_B1) * g
        nv = ADAM_B2 * v_ref[...] + (1.0 - ADAM_B2) * (g * g)
        g_ref[...] = g
        nm_ref[...] = nm
        nv_ref[...] = nv
        d_ref[...] = -ADAM_LR * ((nm * bc1) / (jnp.sqrt(nv * bc2) + ADAM_EPS) + ADAM_WD * w_ref[...])

    blk = pl.BlockSpec((tr, c), lambda i, me: (i, 0))
    return pl.pallas_call(
        body, name=name, out_shape=[SDS((r, c), F32)] * 4, compiler_params=_params(1),
        grid_spec=pltpu.PrefetchScalarGridSpec(
            num_scalar_prefetch=1, grid=(r // tr,),
            in_specs=[pl.BlockSpec((N_DEV, tr, c), lambda i, me: (0, i, 0)), pl.BlockSpec((None, tr, c), lambda i, me: (me[0], i, 0)),
                      blk, blk, blk],
            out_specs=[blk] * 4),
    )(_slot_operand(), landed, own, w, m, v)


def _ffn_fwd(name, h, norm, fetch):
    n = _rms_fwd(name + "_norm", h, norm)
    wg, wu = fetch(name + "_w_gate", n), fetch(name + "_w_up", n)
    gate, up, act = _ffn_up(n, wg, wu)
    wd = fetch(name + "_w_down", act)
    return _ffn_down(act, wd, h, 0.5), (n, gate, up, act, wg, wu, wd)


def _ffn_bwd(name, h, norm, saved, dh, dy_b, scale_out, emit):
    n, gate, up, act, wg, wu, wd = saved
    sent = emit(name + "_w_down", _wgrad_rows(name + "_dwd", act, dy_b))
    d_gate, d_up = _ffn_bwd_act(dy_b, wd, gate, up, after=sent)
    sent = emit(name + "_w_gate", _wgrad_cols(name + "_dwg", n, d_gate, after=sent))
    sent = emit(name + "_w_up", _wgrad_cols(name + "_dwu", n, d_up, after=sent))
    dn = _dgrad_cols(name + "_dn", (d_gate, d_up), (wg, wu), after=sent)
    return _rms_bwd(name + "_norm_bwd", h, dn, norm, dh, scale_out)


def _local_step(x, target, norms, small, fetch, emit, *, batch, seq):
    n1w, nmw, n2w, nfw = norms
    a_log, dt_bias, dn_norm = small
    t, d = x.shape

    h1, saved1 = _ffn_fwd("ffn1", x, n1w, fetch)
    nm = _rms_fwd("mix_norm", h1, nmw)
    w_in = fetch("w_in", nm)
    p = w_in.shape[2]
    proj = _in_proj(nm, w_in)
    conv_all = fetch("conv_w", proj)
    proj = jnp.swapaxes(proj, 0, 1).reshape(t, N_DEV * p)
    conv_w = jnp.swapaxes(conv_all, 0, 1).reshape(CONV_WIDTH, N_DEV * conv_all.shape[2])
    dd = conv_w.shape[1] // 3
    da = (N_DEV * p - 4 * dd - 2 * (dd // HEAD)) // 3
    qkv = proj[:, :3 * da].astype(BF16)
    xd = proj[:, 3 * da:3 * da + 3 * dd]
    z = proj[:, 3 * da + 3 * dd:3 * da + 4 * dd]
    dbda = proj[:, 3 * da + 4 * dd:]
    attn, lse = _attn_fwd(qkv, batch=batch, seq=seq)
    yd = _conv_silu_fwd(xd, conv_w, batch=batch, seq=seq)
    dn_out = _dn_fwd(yd, z, dbda, a_log, dt_bias, dn_norm, batch=batch, seq=seq)
    cat = jnp.concatenate([attn, dn_out], axis=1).astype(BF16)
    w_out = fetch("w_out", cat)
    w_out2 = w_out.reshape(da + dd, d)
    h2 = _out_proj(cat, w_out2, h1)
    h3, saved2 = _ffn_fwd("ffn2", h2, n2w, fetch)

    loss, dh3, dh3_b, g_nf = _loss_head(h3, nfw, target, 0.5)
    dh2, dh2_b, g_n2 = _ffn_bwd("ffn2", h2, n2w, saved2, dh3, dh3_b, 1.0, emit)

    sent = emit("w_out", _wgrad_full("dw_out", cat, dh2_b).reshape(w_out.shape))
    dcat = _dgrad_full("d_cat", dh2_b, w_out2, after=sent)
    d_attn, d_dn = dcat[:, :da], dcat[:, da:]
    dq, dk, dv = _attn_bwd(qkv, attn, lse, d_attn, batch=batch, seq=seq)
    gq, gk, gv, gz, g_dbda, g_alog, g_dtb, g_dnn = _dn_bwd(yd, z, dbda, a_log, dt_bias, dn_norm, d_dn, batch=batch, seq=seq)
    d_xd, g_conv = _conv_silu_bwd(xd, conv_w, jnp.concatenate([gq, gk, gv], axis=1), batch=batch, seq=seq)
    dproj = jnp.concatenate([dq, dk, dv, d_xd, gz, g_dbda], axis=1).astype(BF16)
    dproj = jnp.swapaxes(dproj.reshape(t, N_DEV, p), 0, 1)
    sent = emit("w_in", _wgrad_cols("dw_in", nm, dproj))
    dnm = _dgrad_cols("d_mix_in", (dproj,), (w_in,), after=sent)
    dh1, dh1_b, g_nm = _rms_bwd("mix_norm_bwd", h1, dnm, nmw, dh2, 0.5)

    dx, _, g_n1 = _ffn_bwd("ffn1", x, n1w, saved1, dh1, dh1_b, 1.0, emit)
    return loss, dx, (g_n1, g_nm, g_n2, g_nf), (g_alog, g_dtb, g_dnn), g_conv


def _pack_rows(vectors):
    rows, offsets, r = [], [], 0
    for vec in vectors:
        n = -(-vec.size // 128)
        rows.append(jnp.pad(vec.reshape(-1), (0, n * 128 - vec.size)).reshape(n, 128))
        offsets.append((r, vec.size, vec.shape))
        r += n
    pad = -r % 8
    if pad:
        rows.append(jnp.zeros((pad, 128), F32))
    return jnp.concatenate(rows, axis=0), offsets


def _unpack_rows(packed, offsets):
    return [packed[r:r + -(-size // 128)].reshape(-1)[:size].reshape(shape) for r, size, shape in offsets]


def kernel(x, ffn1_norm, ffn1_w_gate, ffn1_w_up, ffn1_w_down, mix_norm, w_in, conv_w, a_log, dt_bias, dn_norm, w_out, ffn2_norm, ffn2_w_gate, ffn2_w_up, ffn2_w_down, final_norm, loss_target, m_ffn1_norm, m_ffn1_w_gate, m_ffn1_w_up, m_ffn1_w_down, m_mix_norm, m_w_in, m_conv_w, m_a_log, m_dt_bias, m_dn_norm, m_w_out, m_ffn2_norm, m_ffn2_w_gate, m_ffn2_w_up, m_ffn2_w_down, m_final_norm, v_ffn1_norm, v_ffn1_w_gate, v_ffn1_w_up, v_ffn1_w_down, v_mix_norm, v_w_in, v_conv_w, v_a_log, v_dt_bias, v_dn_norm, v_w_out, v_ffn2_norm, v_ffn2_w_gate, v_ffn2_w_up, v_ffn2_w_down, v_final_norm):
    batch, seq, d = x.shape
    t = batch * seq
    big = dict(ffn1_w_gate=(ffn1_w_gate, m_ffn1_w_gate, v_ffn1_w_gate), ffn1_w_up=(ffn1_w_up, m_ffn1_w_up, v_ffn1_w_up),
               ffn1_w_down=(ffn1_w_down, m_ffn1_w_down, v_ffn1_w_down), w_in=(w_in, m_w_in, v_w_in),
               w_out=(w_out, m_w_out, v_w_out), ffn2_w_gate=(ffn2_w_gate, m_ffn2_w_gate, v_ffn2_w_gate),
               ffn2_w_up=(ffn2_w_up, m_ffn2_w_up, v_ffn2_w_up), ffn2_w_down=(ffn2_w_down, m_ffn2_w_down, v_ffn2_w_down))
    rep = dict(ffn1_norm=(ffn1_norm, m_ffn1_norm, v_ffn1_norm), mix_norm=(mix_norm, m_mix_norm, v_mix_norm),
               ffn2_norm=(ffn2_norm, m_ffn2_norm, v_ffn2_norm), final_norm=(final_norm, m_final_norm, v_final_norm),
               a_log=(a_log, m_a_log, v_a_log), dt_bias=(dt_bias, m_dt_bias, v_dt_bias), dn_norm=(dn_norm, m_dn_norm, v_dn_norm))

    lands = {"conv_w": _cast_place("place_conv_w", conv_w, F32)}
    lands.update({name: _cast_place("place_" + name, w, BF16) for name, (w, _, _) in big.items()})
    started, token = _gather_start("gather_start", list(lands.values()))
    gathering = dict(zip(lands, started))
    gathered, scattering = {}, {}

    def fetch(name, after):
        if name not in gathered:
            gathered[name] = _gather_wait("gather_wait_" + name, gathering[name], after)
        return gathered[name]

    def emit(name, grad):
        scattering[name], sent = _scatter_start("scatter_start_" + name, grad)
        return sent

    row = lambda a: a.reshape(1, -1)
    norms = [row(rep[n][0]) for n in ("ffn1_norm", "mix_norm", "ffn2_norm", "final_norm")]
    norms[0] = norms[0] + token[0, 0]
    loss, dx, g_norms, g_small, g_conv = _local_step(
        x.reshape(t, d), loss_target.reshape(t, d), norms, [row(rep[n][0]) for n in ("a_log", "dt_bias", "dn_norm")],
        fetch, emit, batch=batch, seq=seq)

    out = {"grad_x": dx.reshape(x.shape)}
    after = dx
    for name in scattering:
        w, m, v = big[name]
        own, landed = _scatter_wait("scatter_wait_" + name, scattering[name], after)
        out["grad_" + name], out["delta_" + name], out["new_m_" + name], out["new_v_" + name] = _adamw("adamw_" + name, landed, own, w, m, v)
        after = out["grad_" + name]
    conv_parts = jnp.swapaxes(g_conv.reshape(CONV_WIDTH, N_DEV, conv_w.shape[1]), 0, 1)
    parts = _exchange_slices("scatter_conv_w", conv_parts, after)
    out["grad_conv_w"], out["delta_conv_w"], out["new_m_conv_w"], out["new_v_conv_w"] = _adamw("adamw_conv_w", parts, parts, conv_w, m_conv_w, v_conv_w)

    rep_names = list(rep)
    g_rep = [*g_norms, *g_small]
    packed_g, offsets = _pack_rows([*g_rep, loss[:, :1]])
    packed = [_pack_rows([*[rep[n][i] for n in rep_names], jnp.zeros((1, 1), F32)])[0] for i in range(3)]
    parts = _all_gather("gather_small_grads", packed_g, out["grad_conv_w"])
    res = [_unpack_rows(a, offsets) for a in _adamw("adamw_small", parts, parts, *packed)]
    for i, name in enumerate(rep_names):
        shape = rep[name][0].shape
        out["grad_" + name], out["delta_" + name], out["new_m_" + name], out["new_v_" + name] = (r[i].reshape(shape) for r in res)
    out["loss"] = res[0][-1].reshape(())

    order = ["ffn1_norm", "ffn1_w_gate", "ffn1_w_up", "ffn1_w_down", "mix_norm", "w_in", "conv_w", "a_log", "dt_bias", "dn_norm",
             "w_out", "ffn2_norm", "ffn2_w_gate", "ffn2_w_up", "ffn2_w_down", "final_norm"]
    return (out["loss"], out["grad_x"], *[out["grad_" + n] for n in order], *[out["delta_" + n] for n in order],
            *[out["new_m_" + n] for n in order], *[out["new_v_" + n] for n in order])
```

```python
import functools
import math

import jax
import jax.numpy as jnp
from jax import lax
from jax.experimental import pallas as pl
from jax.experimental.pallas import tpu as pltpu

F32 = jnp.float32
BF16 = jnp.bfloat16
N_DEV = 8
HEAD = 128
CHUNK = 64
CHUNK_BITS = 6
DN_ROWS = 256
CONV_WIDTH = 4
EPS = 1e-6
DILATED_CONFIGS = ((128, 1), (512, 4), (2048, 16))
ATTN_BLOCK = 256
NEG = -1e30
ADAM_LR, ADAM_B1, ADAM_B2, ADAM_EPS, ADAM_WD, ADAM_STEP = 0.001, 0.9, 0.999, 1e-08, 0.01, 10
HI = lax.Precision.HIGHEST
MESH = pl.DeviceIdType.MESH
SDS = jax.ShapeDtypeStruct


def _tile(n, pref, align):
    t = (min(n, pref) // align) * align
    while t >= align:
        if n % t == 0:
            return t
        t -= align
    return n


def _params(n_axes, vmem_mb=48):
    return pltpu.CompilerParams(dimension_semantics=("arbitrary",) * n_axes, vmem_limit_bytes=vmem_mb * 2 ** 20)


def _sigmoid(x):
    return 1.0 / (1.0 + jnp.exp(-x))


def _silu(x):
    return x * _sigmoid(x)


def _softplus(x):
    return jnp.maximum(x, 0.0) + jnp.log(1.0 + jnp.exp(-jnp.abs(x)))


_DIMS = {"nn": (((1,), (0,)), ((), ())), "nt": (((1,), (1,)), ((), ())), "tn": (((0,), (0,)), ((), ()))}


def _mm_call(name, grid, mode, pairs, operands, in_specs, out_shape, out_specs, acc_shapes, epilogue, vmem_mb=48, after=None):
    dims = _DIMS[mode]
    if after is not None:
        operands, in_specs = (*operands, after), [*in_specs, pl.BlockSpec(memory_space=pl.ANY)]
    n_in, n_out = len(operands), len(out_shape)
    nk = grid[-1]

    def whole(*refs):
        ins, outs = refs[:n_in], refs[n_in:]
        sums = {}
        for a, b, c in pairs:
            prod = lax.dot_general(ins[a][...], ins[b][...], dims, preferred_element_type=F32)
            sums[c] = prod if c not in sums else sums[c] + prod
        epilogue(ins, outs, [sums[c] for c in sorted(sums)])

    if acc_shapes is None:
        return pl.pallas_call(
            whole, name=name, grid=grid, in_specs=in_specs, out_specs=out_specs, out_shape=out_shape,
            compiler_params=_params(len(grid), vmem_mb))(*operands)

    def body(*refs):
        ins, outs, accs = refs[:n_in], refs[n_in:n_in + n_out], refs[n_in + n_out:]
        k = pl.program_id(len(grid) - 1)

        @pl.when(k == 0)
        def _():
            for acc in accs:
                acc[...] = jnp.zeros_like(acc)

        for a, b, c in pairs:
            accs[c][...] += lax.dot_general(ins[a][...], ins[b][...], dims, preferred_element_type=F32)

        @pl.when(k == nk - 1)
        def _():
            epilogue(ins, outs, [acc[...] for acc in accs])

    return pl.pallas_call(
        body, name=name, grid=grid, in_specs=in_specs, out_specs=out_specs, out_shape=out_shape,
        scratch_shapes=[pltpu.VMEM(s, F32) for s in acc_shapes], compiler_params=_params(len(grid), vmem_mb),
    )(*operands)


def _ffn_up(name, n, wg, wu):
    t, d = n.shape
    f = wg.shape[2]
    tm = _tile(t, 256, 16)
    n_spec = pl.BlockSpec((tm, d), lambda s, m, k: (m, 0))
    w_spec = pl.BlockSpec((None, d, f), lambda s, m, k: (s, 0, 0))
    o_spec = pl.BlockSpec((None, tm, f), lambda s, m, k: (s, m, 0))
    o_shape = SDS((N_DEV, t, f), BF16)

    def gate_out(ins, outs, accs):
        outs[0][...] = accs[0].astype(BF16)

    def up_out(ins, outs, accs):
        outs[0][...] = accs[0].astype(BF16)
        outs[1][...] = (_silu(ins[2][...].astype(F32)) * accs[0]).astype(BF16)

    grid = (N_DEV, t // tm, 1)
    gate = _mm_call(name + "_gate", grid, "nn", [(0, 1, 0)], (n, wg), [n_spec, w_spec], [o_shape], [o_spec], None, gate_out)[0]
    up, act = _mm_call(name + "_up", grid, "nn", [(0, 1, 0)], (n, wu, gate), [n_spec, w_spec, o_spec], [o_shape] * 2,
                       [o_spec] * 2, None, up_out)
    return gate, up, act


def _ffn_down(act, wd, resid, scale):
    _, t, f = act.shape
    d = wd.shape[2]
    tm, tn = _tile(t, 512, 16), _tile(d, 1024, 128)

    def epilogue(ins, outs, accs):
        outs[0][...] = ins[2][...] + scale * accs[0]

    rc = pl.BlockSpec((tm, tn), lambda m, n, s: (m, n))
    return _mm_call(
        "ffn_down", (t // tm, d // tn, N_DEV), "nn", [(0, 1, 0)], (act, wd, resid),
        [pl.BlockSpec((None, tm, f), lambda m, n, s: (s, m, 0)), pl.BlockSpec((None, f, tn), lambda m, n, s: (s, 0, n)), rc],
        [SDS((t, d), F32)], [rc], [(tm, tn)], epilogue)[0]


def _in_proj(n, w):
    t, d = n.shape
    p = w.shape[2]
    tm = _tile(t, 256, 16)
    return _mm_call(
        "in_proj", (N_DEV, t // tm, 1), "nn", [(0, 1, 0)], (n, w),
        [pl.BlockSpec((tm, d), lambda s, m, k: (m, 0)), pl.BlockSpec((None, d, p), lambda s, m, k: (s, 0, 0))],
        [SDS((N_DEV, t, p), F32)], [pl.BlockSpec((None, tm, p), lambda s, m, k: (s, m, 0))], None, _store_f32)[0]


def _out_proj(cat, w, resid):
    t, dm = cat.shape
    d = w.shape[1]
    tm, tn = _tile(t, 512, 16), _tile(d, 1024, 128)

    def epilogue(ins, outs, accs):
        outs[0][...] = ins[2][...] + accs[0]

    rc = pl.BlockSpec((tm, tn), lambda n, m, k: (m, n))
    return _mm_call(
        "out_proj", (d // tn, t // tm, 1), "nn", [(0, 1, 0)], (cat, w, resid),
        [pl.BlockSpec((tm, dm), lambda n, m, k: (m, 0)), pl.BlockSpec((dm, tn), lambda n, m, k: (0, n)), rc],
        [SDS((t, d), F32)], [rc], None, epilogue)[0]


def _ffn_bwd_act(dy, wd, gate, up, after=None):
    t, d = dy.shape
    f = wd.shape[1]
    tm = _tile(t, 256, 16)

    def epilogue(ins, outs, accs):
        g, u = ins[2][...].astype(F32), ins[3][...].astype(F32)
        sg = _sigmoid(g)
        outs[0][...] = (accs[0] * u * sg * (1.0 + g * (1.0 - sg))).astype(BF16)
        outs[1][...] = (accs[0] * g * sg).astype(BF16)

    o_spec = pl.BlockSpec((None, tm, f), lambda s, m, k: (s, m, 0))
    return _mm_call(
        "ffn_bwd_act", (N_DEV, t // tm, 1), "nt", [(0, 1, 0)], (dy, wd, gate, up),
        [pl.BlockSpec((tm, d), lambda s, m, k: (m, 0)), pl.BlockSpec((None, f, d), lambda s, m, k: (s, 0, 0)), o_spec, o_spec],
        [SDS((N_DEV, t, f), BF16)] * 2, [o_spec] * 2, None, epilogue, after=after)


def _store_bf16(ins, outs, accs):
    outs[0][...] = accs[0].astype(BF16)


def _store_f32(ins, outs, accs):
    outs[0][...] = accs[0]


def _wgrad_cols(name, a, b, after=None):
    t, m = a.shape
    n = b.shape[2]
    tm, tk = _tile(m, 512 if n <= 1408 else 256, 128), t
    return _mm_call(
        name, (N_DEV, m // tm, t // tk), "tn", [(0, 1, 0)], (a, b),
        [pl.BlockSpec((tk, tm), lambda s, i, k: (k, i)), pl.BlockSpec((None, tk, n), lambda s, i, k: (s, k, 0))],
        [SDS((N_DEV, m, n), BF16)], [pl.BlockSpec((None, tm, n), lambda s, i, k: (s, i, 0))], None, _store_bf16,
        after=after)[0]


def _wgrad_rows(name, a, b, after=None):
    _, t, m = a.shape
    n = b.shape[1]
    tn, tk = _tile(n, 512, 128), t
    return _mm_call(
        name, (N_DEV, n // tn, t // tk), "tn", [(0, 1, 0)], (a, b),
        [pl.BlockSpec((None, tk, m), lambda s, j, k: (s, k, 0)), pl.BlockSpec((tk, tn), lambda s, j, k: (k, j))],
        [SDS((N_DEV, m, n), BF16)], [pl.BlockSpec((None, m, tn), lambda s, j, k: (s, 0, j))], None, _store_bf16,
        after=after)[0]


def _wgrad_full(name, a, b, after=None):
    t, m = a.shape
    n = b.shape[1]
    tm, tn, tk = _tile(m, 512, 128), _tile(n, 1024, 128), t
    return _mm_call(
        name, (m // tm, n // tn, t // tk), "tn", [(0, 1, 0)], (a, b),
        [pl.BlockSpec((tk, tm), lambda i, j, k: (k, i)), pl.BlockSpec((tk, tn), lambda i, j, k: (k, j))],
        [SDS((m, n), BF16)], [pl.BlockSpec((tm, tn), lambda i, j, k: (i, j))], None, _store_bf16, after=after)[0]


def _dgrad_cols(name, grads, weights, after=None):
    _, t, n = grads[0].shape
    m = weights[0].shape[1]
    tm, tn = _tile(t, 512, 16), _tile(m, 1024, 128)
    k = len(grads)
    return _mm_call(
        name, (t // tm, m // tn, N_DEV), "nt", [(i, k + i, 0) for i in range(k)], (*grads, *weights),
        [pl.BlockSpec((None, tm, n), lambda i, j, s: (s, i, 0))] * k + [pl.BlockSpec((None, tn, n), lambda i, j, s: (s, j, 0))] * k,
        [SDS((t, m), F32)], [pl.BlockSpec((tm, tn), lambda i, j, s: (i, j))], [(tm, tn)], _store_f32, after=after)[0]


def _dgrad_full(name, g, w, after=None):
    t, n = g.shape
    m = w.shape[0]
    tm, tn, tk = _tile(t, 512, 16), _tile(m, 1024, 128), n
    return _mm_call(
        name, (m // tn, t // tm, n // tk), "nt", [(0, 1, 0)], (g, w),
        [pl.BlockSpec((tm, tk), lambda j, i, k: (i, k)), pl.BlockSpec((tn, tk), lambda j, i, k: (j, k))],
        [SDS((t, m), F32)], [pl.BlockSpec((tm, tn), lambda j, i, k: (i, j))], None, _store_f32, after=after)[0]


def _rms_fwd(name, h, w):
    t, d = h.shape
    tm = _tile(t, 256, 16)

    def body(h_ref, w_ref, o_ref):
        x = h_ref[...]
        o_ref[...] = (x * lax.rsqrt(jnp.mean(x * x, axis=1, keepdims=True) + EPS) * w_ref[...]).astype(BF16)

    row = pl.BlockSpec((tm, d), lambda i: (i, 0))
    return pl.pallas_call(
        body, name=name, grid=(t // tm,), in_specs=[row, pl.BlockSpec((1, d), lambda i: (0, 0))], out_specs=row,
        out_shape=SDS((t, d), BF16), compiler_params=_params(1))(h, w)


def _rms_bwd(name, h, dn, w, dres, scale):
    t, d = h.shape
    tm = _tile(t, 128, 16)

    def body(h_ref, dn_ref, w_ref, dres_ref, dh_ref, dhb_ref, dw_ref):
        @pl.when(pl.program_id(0) == 0)
        def _():
            dw_ref[...] = jnp.zeros_like(dw_ref)

        x = h_ref[...]
        rstd = lax.rsqrt(jnp.mean(x * x, axis=1, keepdims=True) + EPS)
        nhat = x * rstd
        g = dn_ref[...]
        gw = g * w_ref[...]
        dh = dres_ref[...] + rstd * (gw - nhat * jnp.mean(gw * nhat, axis=1, keepdims=True))
        dh_ref[...] = dh
        dhb_ref[...] = (scale * dh).astype(BF16)
        dw_ref[...] += jnp.sum(g * nhat, axis=0, keepdims=True)

    row = pl.BlockSpec((tm, d), lambda i: (i, 0))
    vec = pl.BlockSpec((1, d), lambda i: (0, 0))
    return pl.pallas_call(
        body, name=name, grid=(t // tm,), in_specs=[row, row, vec, row], out_specs=[row, row, vec],
        out_shape=[SDS((t, d), F32), SDS((t, d), BF16), SDS((1, d), F32)], compiler_params=_params(1))(h, dn, w, dres)


def _loss_head(h, w, target, scale):
    t, d = h.shape
    tm = _tile(t, 128, 16)

    def body(h_ref, w_ref, tg_ref, loss_ref, dh_ref, dhb_ref, dw_ref):
        @pl.when(pl.program_id(0) == 0)
        def _():
            dw_ref[...] = jnp.zeros_like(dw_ref)
            loss_ref[...] = jnp.zeros_like(loss_ref)

        x = h_ref[...]
        rstd = lax.rsqrt(jnp.mean(x * x, axis=1, keepdims=True) + EPS)
        nhat = x * rstd
        wv = w_ref[...]
        err = nhat * wv - tg_ref[...]
        loss_ref[...] += 0.5 * jnp.sum(jnp.mean(err * err, axis=1, keepdims=True), axis=0, keepdims=True)
        g = err * (1.0 / d)
        gw = g * wv
        dh = rstd * (gw - nhat * jnp.mean(gw * nhat, axis=1, keepdims=True))
        dh_ref[...] = dh
        dhb_ref[...] = (scale * dh).astype(BF16)
        dw_ref[...] += jnp.sum(g * nhat, axis=0, keepdims=True)

    row = pl.BlockSpec((tm, d), lambda i: (i, 0))
    vec = pl.BlockSpec((1, d), lambda i: (0, 0))
    return pl.pallas_call(
        body, name="loss_head", grid=(t // tm,), in_specs=[row, vec, row],
        out_specs=[pl.BlockSpec((1, 128), lambda i: (0, 0)), row, row, vec],
        out_shape=[SDS((1, 128), F32), SDS((t, d), F32), SDS((t, d), BF16), SDS((1, d), F32)],
        compiler_params=_params(1))(h, w, target)


def _attn_bias(qi, ki, blk):
    dist = (lax.broadcasted_iota(jnp.int32, (blk, blk), 0) - lax.broadcasted_iota(jnp.int32, (blk, blk), 1)
            + (qi - ki) * blk)
    count = jnp.zeros((blk, blk), F32)
    for window, dil in DILATED_CONFIGS:
        assert dil & (dil - 1) == 0
        seen = (dist >= 0) & (dist <= window) & ((dist & (dil - 1)) == 0)
        count = count + jnp.where(seen, 1.0, 0.0)
    return jnp.where(count > 0.0, jnp.log(jnp.maximum(count, 1.0)), NEG)


def _attn_fwd(qkv, *, batch, seq):
    t, da3 = qkv.shape
    da = da3 // 3
    n_heads = da // HEAD
    blk = _tile(seq, ATTN_BLOCK, 16)
    nq = seq // blk
    sm_scale = HEAD ** -0.5

    def body(q_ref, k_ref, v_ref, o_ref, lse_ref):
        def q_step(qi, _):
            rows = pl.ds(pl.multiple_of(qi * blk, blk), blk)
            q = q_ref[rows, :]

            def kv_step(ki, carry):
                m, l, acc = carry
                cols = pl.ds(pl.multiple_of(ki * blk, blk), blk)
                s = lax.dot_general(q, k_ref[cols, :], _DIMS["nt"], preferred_element_type=F32) * sm_scale
                s = s + _attn_bias(qi, ki, blk)
                m_new = jnp.maximum(m, jnp.max(s, axis=1, keepdims=True))
                alpha = jnp.exp(m - m_new)
                p = jnp.exp(s - m_new)
                l = alpha * l + jnp.sum(p, axis=1, keepdims=True)
                acc = alpha * acc + jnp.dot(p.astype(BF16), v_ref[cols, :], preferred_element_type=F32)
                return m_new, l, acc

            m, l, acc = lax.fori_loop(0, qi + 1, kv_step, (jnp.full((blk, 1), NEG, F32), jnp.zeros((blk, 1), F32),
                                                           jnp.zeros((blk, HEAD), F32)))
            o_ref[rows, :] = acc / l
            lse_ref[rows, :] = jnp.broadcast_to(m + jnp.log(l), (blk, HEAD))
            return 0

        lax.fori_loop(0, nq, q_step, 0)

    col = lambda off: pl.BlockSpec((seq, HEAD), lambda b, h: (b, off + h))
    return pl.pallas_call(
        body, name="attn_fwd", grid=(batch, n_heads), in_specs=[col(0), col(n_heads), col(2 * n_heads)],
        out_specs=[col(0), col(0)], out_shape=[SDS((t, da), F32), SDS((t, da), F32)], compiler_params=_params(2),
    )(qkv, qkv, qkv)


def _attn_bwd(qkv, out, lse, d_out, *, batch, seq):
    t, da = out.shape
    n_heads = da // HEAD
    blk = _tile(seq, ATTN_BLOCK, 16)
    nq = seq // blk
    sm_scale = HEAD ** -0.5

    def body(q_ref, k_ref, v_ref, o_ref, lse_ref, do_ref, dq_ref, dk_ref, dv_ref):
        dk_ref[...] = jnp.zeros_like(dk_ref)
        dv_ref[...] = jnp.zeros_like(dv_ref)

        def q_step(qi, _):
            rows = pl.ds(pl.multiple_of(qi * blk, blk), blk)
            q = q_ref[rows, :]
            do = do_ref[rows, :]
            do_b = do.astype(BF16)
            lse_q = lse_ref[rows, :][:, :1]
            delta = jnp.sum(do * o_ref[rows, :], axis=1, keepdims=True)

            def kv_step(ki, dq):
                cols = pl.ds(pl.multiple_of(ki * blk, blk), blk)
                k = k_ref[cols, :]
                s = lax.dot_general(q, k, _DIMS["nt"], preferred_element_type=F32) * sm_scale
                p = jnp.exp(s + _attn_bias(qi, ki, blk) - lse_q)
                dp = lax.dot_general(do_b, v_ref[cols, :], _DIMS["nt"], preferred_element_type=F32)
                ds = (p * (dp - delta) * sm_scale).astype(BF16)
                dv_ref[cols, :] += lax.dot_general(p.astype(BF16), do_b, _DIMS["tn"], preferred_element_type=F32)
                dk_ref[cols, :] += lax.dot_general(ds, q, _DIMS["tn"], preferred_element_type=F32)
                return dq + jnp.dot(ds, k, preferred_element_type=F32)

            dq_ref[rows, :] = lax.fori_loop(0, qi + 1, kv_step, jnp.zeros((blk, HEAD), F32))
            return 0

        lax.fori_loop(0, nq, q_step, 0)

    col = lambda off: pl.BlockSpec((seq, HEAD), lambda b, h: (b, off + h))
    return pl.pallas_call(
        body, name="attn_bwd", grid=(batch, n_heads),
        in_specs=[col(0), col(n_heads), col(2 * n_heads), col(0), col(0), col(0)], out_specs=[col(0)] * 3,
        out_shape=[SDS((t, da), F32)] * 3, compiler_params=_params(2),
    )(qkv, qkv, qkv, out, lse, d_out)


def _shift_down(x, k, row):
    return x if k == 0 else jnp.where(row >= k, pltpu.roll(x, k, axis=0), 0.0)


def _shift_up(x, k, row):
    n = x.shape[0]
    return x if k == 0 else jnp.where(row < n - k, pltpu.roll(x, n - k, axis=0), 0.0)


def _conv_silu_fwd(x, w, *, batch, seq):
    t, c = x.shape

    def body(x_ref, w_ref, o_ref):
        xv = x_ref[...]
        row = lax.broadcasted_iota(jnp.int32, xv.shape, 0)
        acc = jnp.zeros_like(xv)
        for i in range(CONV_WIDTH):
            acc = acc + w_ref[i:i + 1, :] * _shift_down(xv, CONV_WIDTH - 1 - i, row)
        o_ref[...] = _silu(acc)

    blk = pl.BlockSpec((seq, HEAD), lambda j, b: (b, j))
    return pl.pallas_call(
        body, name="conv_silu_fwd", grid=(c // HEAD, batch), in_specs=[blk, pl.BlockSpec((CONV_WIDTH, HEAD), lambda j, b: (0, j))],
        out_specs=blk, out_shape=SDS((t, c), F32), compiler_params=_params(2))(x, w)


def _conv_silu_bwd(x, w, dy, *, batch, seq):
    t, c = x.shape

    def body(x_ref, w_ref, dy_ref, dx_ref, dw_ref):
        @pl.when(pl.program_id(1) == 0)
        def _():
            dw_ref[...] = jnp.zeros_like(dw_ref)

        xv = x_ref[...]
        row = lax.broadcasted_iota(jnp.int32, xv.shape, 0)
        shifted = [_shift_down(xv, CONV_WIDTH - 1 - i, row) for i in range(CONV_WIDTH)]
        acc = jnp.zeros_like(xv)
        for i in range(CONV_WIDTH):
            acc = acc + w_ref[i:i + 1, :] * shifted[i]
        sg = _sigmoid(acc)
        dc = dy_ref[...] * sg * (1.0 + acc * (1.0 - sg))
        dx = jnp.zeros_like(xv)
        for i in range(CONV_WIDTH):
            dx = dx + w_ref[i:i + 1, :] * _shift_up(dc, CONV_WIDTH - 1 - i, row)
            dw_ref[i:i + 1, :] += jnp.sum(dc * shifted[i], axis=0, keepdims=True)
        dx_ref[...] = dx

    blk = pl.BlockSpec((seq, HEAD), lambda j, b: (b, j))
    wblk = pl.BlockSpec((CONV_WIDTH, HEAD), lambda j, b: (0, j))
    return pl.pallas_call(
        body, name="conv_silu_bwd", grid=(c // HEAD, batch), in_specs=[blk, wblk, blk], out_specs=[blk, wblk],
        out_shape=[SDS((t, c), F32), SDS((CONV_WIDTH, c), F32)], compiler_params=_params(2))(x, w, dy)


def _dot(a, b, mode="nn"):
    return lax.dot_general(a.astype(BF16), b.astype(BF16), _DIMS[mode], preferred_element_type=F32)


def _dot3(a, b):
    a_hi, b_hi = a.astype(BF16), b.astype(BF16)
    a_lo, b_lo = (a - a_hi.astype(F32)).astype(BF16), (b - b_hi.astype(F32)).astype(BF16)
    pass_ = lambda x, y: jnp.dot(x, y, preferred_element_type=F32)
    return pass_(a_hi, b_hi) + pass_(a_hi, b_lo) + pass_(a_lo, b_hi)


def _dot_nt(a, b):
    return _dot(a, b, "nt")


def _dot_tn(a, b):
    return _dot(a, b, "tn")


def _dn_chunk(head, n_heads, aq, ak, v, z, dbda, a_log, dt_bias, dn_norm, state):
    r = aq.shape[0]
    lane_g = lax.broadcasted_iota(jnp.int32, dbda.shape, 1)
    db = jnp.sum(jnp.where(lane_g == head, dbda, 0.0), axis=1, keepdims=True)
    da = jnp.sum(jnp.where(lane_g == head + n_heads, dbda, 0.0), axis=1, keepdims=True)
    lane_h = lax.broadcasted_iota(jnp.int32, a_log.shape, 1)
    al = jnp.sum(jnp.where(lane_h == head, a_log, 0.0), axis=1, keepdims=True)
    dtb = jnp.sum(jnp.where(lane_h == head, dt_bias, 0.0), axis=1, keepdims=True)
    beta = _sigmoid(db)
    g = -jnp.exp(al) * _softplus(da + dtb)
    q = aq * lax.rsqrt(jnp.sum(aq * aq, axis=1, keepdims=True) + EPS) * (HEAD ** -0.5)
    k = ak * lax.rsqrt(jnp.sum(ak * ak, axis=1, keepdims=True) + EPS)
    ri = lax.broadcasted_iota(jnp.int32, (r, r), 0)
    ci = lax.broadcasted_iota(jnp.int32, (r, r), 1)
    same = (ri >> CHUNK_BITS) == (ci >> CHUNK_BITS)
    incl = same & (ri >= ci)
    g_row = jnp.sum(jnp.where(ri == ci, g, 0.0), axis=0, keepdims=True)
    gc_col = jnp.sum(jnp.where(incl, g_row, 0.0), axis=1, keepdims=True)
    gc_row = jnp.sum(jnp.where(same & (ri <= ci), g, 0.0), axis=0, keepdims=True)
    g_all = jnp.sum(jnp.where(same, g_row, 0.0), axis=1, keepdims=True)
    decay = jnp.where(incl, jnp.exp(jnp.where(incl, gc_col - gc_row, 0.0)), 0.0)
    kb = k * beta
    m = -jnp.where(same & (ri > ci), _dot_nt(kb, k) * decay, 0.0)
    x = jnp.where(ri == ci, 1.0, 0.0) + m
    p = m
    for _ in range(int(math.log2(CHUNK)) - 1):
        p = _dot3(p, p)
        x = x + _dot3(x, p)
    egc = jnp.exp(gc_col)
    wu_g = _dot(x, jnp.concatenate([kb * egc, v * beta], axis=1))
    w_g, u_g = wu_g[:, :HEAD], wu_g[:, HEAD:]
    qk = _dot_nt(q, k) * decay
    q_dec = q * egc
    k_dec = k * jnp.exp(g_all - gc_col)
    carry = jnp.exp(g_all)
    v_new, o_state = [], []
    for c in range(r // CHUNK):
        rows = slice(c * CHUNK, (c + 1) * CHUNK)
        v_new.append(u_g[rows] - _dot(w_g[rows], state))
        o_state.append(_dot(q_dec[rows], state))
        state = state * carry[c * CHUNK:c * CHUNK + 1] + _dot_tn(k_dec[rows], v_new[-1])
    o = jnp.concatenate(o_state, axis=0) + _dot(qk, jnp.concatenate(v_new, axis=0))
    o = o * lax.rsqrt(jnp.mean(o * o, axis=1, keepdims=True) + EPS) * dn_norm
    return o * _silu(z), state


def _loop_by_two(n, step, init):
    if n % 2:
        return lax.fori_loop(0, n, step, init)
    return lax.fori_loop(0, n // 2, lambda i, carry: step(2 * i + 1, step(2 * i, carry)), init)


def _dn_specs(seq, n_heads, small):
    col = lambda off: pl.BlockSpec((seq, HEAD), lambda b, h: (b, off + h))
    full = [pl.BlockSpec(a.shape, lambda b, h: (0, 0)) for a in small]
    gates = pl.BlockSpec((seq, 2 * n_heads), lambda b, h: (b, 0))
    return col, gates, full


def _dn_fwd(y, z, dbda, a_log, dt_bias, dn_norm, *, batch, seq):
    t, dd = z.shape
    n_heads = dd // HEAD
    grp = _tile(seq, DN_ROWS, CHUNK)
    n_chunks = seq // grp

    def body(q_ref, k_ref, v_ref, z_ref, g_ref, al_ref, dt_ref, nw_ref, o_ref):
        head = pl.program_id(1)
        al, dtb, nw = al_ref[...], dt_ref[...], nw_ref[...]

        def step(n, state):
            rows = pl.ds(pl.multiple_of(n * grp, grp), grp)
            out, state = _dn_chunk(head, n_heads, q_ref[rows, :], k_ref[rows, :], v_ref[rows, :], z_ref[rows, :],
                                   g_ref[rows, :], al, dtb, nw, state)
            o_ref[rows, :] = out
            return state

        _loop_by_two(n_chunks, step, jnp.zeros((HEAD, HEAD), F32))

    col, gates, full = _dn_specs(seq, n_heads, (a_log, dt_bias, dn_norm))
    return pl.pallas_call(
        body, name="dn_fwd", grid=(batch, n_heads), in_specs=[col(0), col(n_heads), col(2 * n_heads), col(0), gates, *full],
        out_specs=col(0), out_shape=SDS((t, dd), F32), compiler_params=_params(2),
    )(y, y, y, z, dbda, a_log, dt_bias, dn_norm)


def _dn_bwd(y, z, dbda, a_log, dt_bias, dn_norm, d_out, *, batch, seq):
    t, dd = z.shape
    n_heads = dd // HEAD
    grp = _tile(seq, DN_ROWS, CHUNK)
    n_chunks = seq // grp

    def body(q_ref, k_ref, v_ref, z_ref, g_ref, al_ref, dt_ref, nw_ref, do_ref,
             dq_ref, dk_ref, dv_ref, dz_ref, dg_ref, dal_ref, ddt_ref, dnw_ref, states):
        b, head = pl.program_id(0), pl.program_id(1)
        al, dtb, nw = al_ref[...], dt_ref[...], nw_ref[...]

        @pl.when((b == 0) & (head == 0))
        def _():
            dal_ref[...] = jnp.zeros_like(dal_ref)
            ddt_ref[...] = jnp.zeros_like(ddt_ref)
            dnw_ref[...] = jnp.zeros_like(dnw_ref)

        @pl.when(head == 0)
        def _():
            dg_ref[...] = jnp.zeros_like(dg_ref)

        def chunk(n):
            rows = pl.ds(pl.multiple_of(n * grp, grp), grp)
            return rows, (q_ref[rows, :], k_ref[rows, :], v_ref[rows, :], z_ref[rows, :], g_ref[rows, :], al, dtb, nw)

        def fwd_step(n, state):
            states[n] = state
            return _dn_chunk(head, n_heads, *chunk(n)[1], state)[1]

        _loop_by_two(n_chunks, fwd_step, jnp.zeros((HEAD, HEAD), F32))

        def bwd_step(i, carry):
            d_state, d_al, d_dt, d_nw = carry
            n = n_chunks - 1 - i
            rows, args = chunk(n)
            _, vjp = jax.vjp(functools.partial(_dn_chunk, head, n_heads), *args, states[n])
            gq, gk, gv, gz, gg, gal, gdt, gnw, d_state = vjp((do_ref[rows, :], d_state))
            dq_ref[rows, :] = gq
            dk_ref[rows, :] = gk
            dv_ref[rows, :] = gv
            dz_ref[rows, :] = gz
            dg_ref[rows, :] += gg
            return d_state, d_al + gal, d_dt + gdt, d_nw + gnw

        zero = lambda a: jnp.zeros(a.shape, F32)
        _, d_al, d_dt, d_nw = _loop_by_two(
            n_chunks, bwd_step, (jnp.zeros((HEAD, HEAD), F32), zero(al), zero(dtb), zero(nw)))
        dal_ref[...] += d_al
        ddt_ref[...] += d_dt
        dnw_ref[...] += d_nw

    col, gates, full = _dn_specs(seq, n_heads, (a_log, dt_bias, dn_norm))
    return pl.pallas_call(
        body, name="dn_bwd", grid=(batch, n_heads),
        in_specs=[col(0), col(n_heads), col(2 * n_heads), col(0), gates, *full, col(0)],
        out_specs=[col(0), col(0), col(0), col(0), gates, *full],
        out_shape=[SDS((t, dd), F32)] * 4 + [SDS(dbda.shape, F32), SDS(a_log.shape, F32), SDS(dt_bias.shape, F32),
                                             SDS(dn_norm.shape, F32)],
        scratch_shapes=[pltpu.VMEM((n_chunks, HEAD, HEAD), F32)], compiler_params=_params(2),
    )(y, y, y, z, dbda, a_log, dt_bias, dn_norm, d_out)


def _my_slot():
    return 4 * lax.axis_index("x") + 2 * lax.axis_index("y") + lax.axis_index("c")


def _peer(k):
    x, y, c = lax.axis_index("x"), lax.axis_index("y"), lax.axis_index("c")
    return (x ^ (k >> 2), y ^ ((k >> 1) & 1), c ^ (k & 1)), (4 * x + 2 * y + c) ^ k


def _all_gather(name, block, after):
    def body(src, after_ref, dst, send_sems, recv_sems, local_sem):
        me = _my_slot()
        own = pltpu.make_async_copy(src, dst.at[me], local_sem)
        own.start()
        copies = []
        for k in range(1, N_DEV):
            peer, _ = _peer(k)
            copies.append(pltpu.make_async_remote_copy(
                src_ref=src, dst_ref=dst.at[me], send_sem=send_sems.at[k - 1], recv_sem=recv_sems.at[k - 1],
                device_id=peer, device_id_type=MESH))
            copies[-1].start()
        for k in range(1, N_DEV):
            peer, slot = _peer(k)
            pltpu.make_async_remote_copy(
                src_ref=src, dst_ref=dst.at[slot], send_sem=send_sems.at[k - 1], recv_sem=recv_sems.at[k - 1],
                device_id=peer, device_id_type=MESH).wait_recv()
        for cp in copies:
            cp.wait_send()
        own.wait()

    return pl.pallas_call(
        body, name=name, in_specs=[pl.BlockSpec(memory_space=pl.ANY)] * 2, out_specs=pl.BlockSpec(memory_space=pl.ANY),
        out_shape=SDS((N_DEV, *block.shape), block.dtype),
        scratch_shapes=[pltpu.SemaphoreType.DMA((N_DEV - 1,)), pltpu.SemaphoreType.DMA((N_DEV - 1,)), pltpu.SemaphoreType.DMA],
    )(block, after)


def _exchange_slices(name, parts, after):
    def body(src, after_ref, dst, send_sems, recv_sems, local_sem):
        me = _my_slot()
        own = pltpu.make_async_copy(src.at[me], dst.at[me], local_sem)
        own.start()
        copies = []
        for k in range(1, N_DEV):
            peer, slot = _peer(k)
            copies.append(pltpu.make_async_remote_copy(
                src_ref=src.at[slot], dst_ref=dst.at[me], send_sem=send_sems.at[k - 1], recv_sem=recv_sems.at[k - 1],
                device_id=peer, device_id_type=MESH))
            copies[-1].start()
        for k in range(1, N_DEV):
            peer, slot = _peer(k)
            pltpu.make_async_remote_copy(
                src_ref=src.at[me], dst_ref=dst.at[slot], send_sem=send_sems.at[k - 1], recv_sem=recv_sems.at[k - 1],
                device_id=peer, device_id_type=MESH).wait_recv()
        for cp in copies:
            cp.wait_send()
        own.wait()

    return pl.pallas_call(
        body, name=name, in_specs=[pl.BlockSpec(memory_space=pl.ANY)] * 2, out_specs=pl.BlockSpec(memory_space=pl.ANY),
        out_shape=SDS(parts.shape, parts.dtype),
        scratch_shapes=[pltpu.SemaphoreType.DMA((N_DEV - 1,)), pltpu.SemaphoreType.DMA((N_DEV - 1,)), pltpu.SemaphoreType.DMA],
    )(parts, after)


_HBM = pl.BlockSpec(memory_space=pltpu.HBM)
_SEM = pl.BlockSpec(memory_space=pltpu.SEMAPHORE)
_EFFECT = pltpu.SideEffectType.DATAFLOW_SIDE_EFFECTING


def _slot_operand():
    return _my_slot().astype(jnp.int32).reshape(1)


def _cast_place(name, block, dtype):
    r, c = block.shape
    tr = _tile(r, 512, 16)

    def body(me_ref, src_ref, dst_ref):
        dst_ref[...] = src_ref[...].astype(dtype)

    return pl.pallas_call(
        body, name=name, out_shape=SDS((N_DEV, r, c), dtype), compiler_params=_params(1),
        grid_spec=pltpu.PrefetchScalarGridSpec(
            num_scalar_prefetch=1, grid=(r // tr,), in_specs=[pl.BlockSpec((tr, c), lambda i, me: (i, 0))],
            out_specs=pl.BlockSpec((None, tr, c), lambda i, me: (me[0], i, 0))),
    )(_slot_operand(), block)


def _gather_start(name, lands):
    n = len(lands)

    def body(*refs):
        lnds, outs = refs[:n], refs[n:]
        me = _my_slot()
        for i in range(n):
            for k in range(1, N_DEV):
                peer, _ = _peer(k)
                pltpu.make_async_remote_copy(
                    src_ref=lnds[i].at[me], dst_ref=lnds[i].at[me], send_sem=outs[2 * i].at[k - 1],
                    recv_sem=outs[2 * i + 1].at[k - 1], device_id=peer, device_id_type=MESH).start()
        outs[-1][...] = jnp.zeros_like(outs[-1])

    res = pl.pallas_call(
        body, name=name, in_specs=[_HBM] * n,
        out_specs=[_SEM] * (2 * n) + [_HBM] * n + [pl.BlockSpec(memory_space=pltpu.VMEM)],
        out_shape=[pltpu.SemaphoreType.DMA((N_DEV - 1,))] * (2 * n) + [pltpu.HBM(a.shape, a.dtype) for a in lands]
        + [SDS((8, 128), F32)],
        input_output_aliases={i: 2 * n + i for i in range(n)},
        compiler_params=pltpu.CompilerParams(has_side_effects=_EFFECT),
    )(*[pltpu.with_memory_space_constraint(a, pltpu.HBM) for a in lands])
    return [(res[2 * i], res[2 * i + 1], res[2 * n + i]) for i in range(n)], res[-1]


def _gather_wait(name, started, after):
    send_sems, recv_sems, land = started

    def body(land_ref, send_ref, recv_ref, after_ref, land_out):
        me = _my_slot()
        for k in range(1, N_DEV):
            peer, slot = _peer(k)
            copy = pltpu.make_async_remote_copy(
                src_ref=land_ref.at[me], dst_ref=land_ref.at[slot], send_sem=send_ref.at[k - 1],
                recv_sem=recv_ref.at[k - 1], device_id=peer, device_id_type=MESH)
            copy.wait_send()
            copy.wait_recv()

    return pl.pallas_call(
        body, name=name, in_specs=[_HBM, _SEM, _SEM, pl.BlockSpec(memory_space=pl.ANY)], out_specs=[_HBM],
        out_shape=[pltpu.HBM(land.shape, land.dtype)], input_output_aliases={0: 0},
        compiler_params=pltpu.CompilerParams(has_side_effects=_EFFECT),
    )(land, send_sems, recv_sems, after)[0]


def _scatter_start(name, parts):
    land = lax.empty(parts.shape, parts.dtype)

    def body(src, lnd, send_ref, recv_ref, src_out, lnd_out, token):
        me = _my_slot()
        for k in range(1, N_DEV):
            peer, slot = _peer(k)
            pltpu.make_async_remote_copy(
                src_ref=src.at[slot], dst_ref=lnd.at[me], send_sem=send_ref.at[k - 1], recv_sem=recv_ref.at[k - 1],
                device_id=peer, device_id_type=MESH).start()
        token[...] = jnp.zeros_like(token)

    res = pl.pallas_call(
        body, name=name, in_specs=[_HBM, _HBM],
        out_specs=[_SEM, _SEM, _HBM, _HBM, pl.BlockSpec(memory_space=pltpu.VMEM)],
        out_shape=[pltpu.SemaphoreType.DMA((N_DEV - 1,))] * 2 + [pltpu.HBM(parts.shape, parts.dtype)] * 2 + [SDS((8, 128), F32)],
        input_output_aliases={0: 2, 1: 3}, compiler_params=pltpu.CompilerParams(has_side_effects=_EFFECT),
    )(pltpu.with_memory_space_constraint(parts, pltpu.HBM), pltpu.with_memory_space_constraint(land, pltpu.HBM))
    return tuple(res[:4]), res[4]


def _scatter_wait(name, started, after):
    send_sems, recv_sems, parts, land = started

    def body(src_ref, land_ref, send_ref, recv_ref, after_ref, src_out, land_out):
        me = _my_slot()
        for k in range(1, N_DEV):
            peer, slot = _peer(k)
            copy = pltpu.make_async_remote_copy(
                src_ref=src_ref.at[me], dst_ref=land_ref.at[slot], send_sem=send_ref.at[k - 1],
                recv_sem=recv_ref.at[k - 1], device_id=peer, device_id_type=MESH)
            copy.wait_send()
            copy.wait_recv()

    return pl.pallas_call(
        body, name=name, in_specs=[_HBM, _HBM, _SEM, _SEM, pl.BlockSpec(memory_space=pl.ANY)], out_specs=[_HBM, _HBM],
        out_shape=[pltpu.HBM(parts.shape, parts.dtype), pltpu.HBM(land.shape, land.dtype)], input_output_aliases={0: 0, 1: 1},
        compiler_params=pltpu.CompilerParams(has_side_effects=_EFFECT),
    )(parts, land, send_sems, recv_sems, after)


def _adamw(name, landed, own, w, m, v):
    r, c = w.shape
    tr = _tile(r, max(16, (12 * 2 ** 20) // (46 * c)), 16)
    bc1 = 1.0 / (1.0 - ADAM_B1 ** ADAM_STEP)
    bc2 = 1.0 / (1.0 - ADAM_B2 ** ADAM_STEP)

    def body(me_ref, p_ref, own_ref, w_ref, m_ref, v_ref, g_ref, d_ref, nm_ref, nv_ref):
        me = me_ref[0]
        g = jnp.zeros(w_ref.shape, F32)
        for s in range(N_DEV):
            g = g + jnp.where(me == s, own_ref[...], p_ref[s]).astype(F32)
        nm = ADAM_B1 * m_ref[...] + (1.0 - ADAM_B1) * g
        nv = ADAM_B2 * v_ref[...] + (1.0 - ADAM_B2) * (g * g)
        g_ref[...] = g
        nm_ref[...] = nm
        nv_ref[...] = nv
        d_ref[...] = -ADAM_LR * ((nm * bc1) / (jnp.sqrt(nv * bc2) + ADAM_EPS) + ADAM_WD * w_ref[...])

    blk = pl.BlockSpec((tr, c), lambda i, me: (i, 0))
    return pl.pallas_call(
        body, name=name, out_shape=[SDS((r, c), F32)] * 4, compiler_params=_params(1),
        grid_spec=pltpu.PrefetchScalarGridSpec(
            num_scalar_prefetch=1, grid=(r // tr,),
            in_specs=[pl.BlockSpec((N_DEV, tr, c), lambda i, me: (0, i, 0)), pl.BlockSpec((None, tr, c), lambda i, me: (me[0], i, 0)),
                      blk, blk, blk],
            out_specs=[blk] * 4),
    )(_slot_operand(), landed, own, w, m, v)


def _ffn_fwd(name, h, norm, fetch):
    n = _rms_fwd(name + "_norm", h, norm)
    wg, wu = fetch(name + "_w_gate", n), fetch(name + "_w_up", n)
    gate, up, act = _ffn_up(name, n, wg, wu)
    wd = fetch(name + "_w_down", act)
    return _ffn_down(act, wd, h, 0.5), (n, gate, up, act, wg, wu, wd)


def _ffn_bwd(name, h, norm, saved, dh, dy_b, scale_out, emit):
    n, gate, up, act, wg, wu, wd = saved
    sent = emit(name + "_w_down", _wgrad_rows(name + "_dwd", act, dy_b))
    d_gate, d_up = _ffn_bwd_act(dy_b, wd, gate, up, after=sent)
    sent = emit(name + "_w_gate", _wgrad_cols(name + "_dwg", n, d_gate, after=sent))
    sent = emit(name + "_w_up", _wgrad_cols(name + "_dwu", n, d_up, after=sent))
    dn = _dgrad_cols(name + "_dn", (d_gate, d_up), (wg, wu), after=sent)
    return _rms_bwd(name + "_norm_bwd", h, dn, norm, dh, scale_out)


def _local_step(x, target, norms, small, fetch, emit, *, batch, seq):
    n1w, nmw, n2w, nfw = norms
    a_log, dt_bias, dn_norm = small
    t, d = x.shape

    h1, saved1 = _ffn_fwd("ffn1", x, n1w, fetch)
    nm = _rms_fwd("mix_norm", h1, nmw)
    w_in = fetch("w_in", nm)
    p = w_in.shape[2]
    proj = _in_proj(nm, w_in)
    conv_all = fetch("conv_w", proj)
    proj = jnp.swapaxes(proj, 0, 1).reshape(t, N_DEV * p)
    conv_w = jnp.swapaxes(conv_all, 0, 1).reshape(CONV_WIDTH, N_DEV * conv_all.shape[2])
    dd = conv_w.shape[1] // 3
    da = (N_DEV * p - 4 * dd - 2 * (dd // HEAD)) // 3
    qkv = proj[:, :3 * da].astype(BF16)
    xd = proj[:, 3 * da:3 * da + 3 * dd]
    z = proj[:, 3 * da + 3 * dd:3 * da + 4 * dd]
    dbda = proj[:, 3 * da + 4 * dd:]
    attn, lse = _attn_fwd(qkv, batch=batch, seq=seq)
    yd = _conv_silu_fwd(xd, conv_w, batch=batch, seq=seq)
    dn_out = _dn_fwd(yd, z, dbda, a_log, dt_bias, dn_norm, batch=batch, seq=seq)
    cat = jnp.concatenate([attn, dn_out], axis=1).astype(BF16)
    w_out = fetch("w_out", cat)
    w_out2 = w_out.reshape(da + dd, d)
    h2 = _out_proj(cat, w_out2, h1)
    h3, saved2 = _ffn_fwd("ffn2", h2, n2w, fetch)

    loss, dh3, dh3_b, g_nf = _loss_head(h3, nfw, target, 0.5)
    dh2, dh2_b, g_n2 = _ffn_bwd("ffn2", h2, n2w, saved2, dh3, dh3_b, 1.0, emit)

    sent = emit("w_out", _wgrad_full("dw_out", cat, dh2_b).reshape(w_out.shape))
    dcat = _dgrad_full("d_cat", dh2_b, w_out2, after=sent)
    d_attn, d_dn = dcat[:, :da], dcat[:, da:]
    dq, dk, dv = _attn_bwd(qkv, attn, lse, d_attn, batch=batch, seq=seq)
    gq, gk, gv, gz, g_dbda, g_alog, g_dtb, g_dnn = _dn_bwd(yd, z, dbda, a_log, dt_bias, dn_norm, d_dn, batch=batch, seq=seq)
    d_xd, g_conv = _conv_silu_bwd(xd, conv_w, jnp.concatenate([gq, gk, gv], axis=1), batch=batch, seq=seq)
    dproj = jnp.concatenate([dq, dk, dv, d_xd, gz, g_dbda], axis=1).astype(BF16)
    dproj = jnp.swapaxes(dproj.reshape(t, N_DEV, p), 0, 1)
    sent = emit("w_in", _wgrad_cols("dw_in", nm, dproj))
    dnm = _dgrad_cols("d_mix_in", (dproj,), (w_in,), after=sent)
    dh1, dh1_b, g_nm = _rms_bwd("mix_norm_bwd", h1, dnm, nmw, dh2, 0.5)

    dx, _, g_n1 = _ffn_bwd("ffn1", x, n1w, saved1, dh1, dh1_b, 1.0, emit)
    return loss, dx, (g_n1, g_nm, g_n2, g_nf), (g_alog, g_dtb, g_dnn), g_conv


def _pack_rows(vectors):
    rows, offsets, r = [], [], 0
    for vec in vectors:
        n = -(-vec.size // 128)
        rows.append(jnp.pad(vec.reshape(-1), (0, n * 128 - vec.size)).reshape(n, 128))
        offsets.append((r, vec.size, vec.shape))
        r += n
    pad = -r % 8
    if pad:
        rows.append(jnp.zeros((pad, 128), F32))
    return jnp.concatenate(rows, axis=0), offsets


def _unpack_rows(packed, offsets):
    return [packed[r:r + -(-size // 128)].reshape(-1)[:size].reshape(shape) for r, size, shape in offsets]


def kernel(x, ffn1_norm, ffn1_w_gate, ffn1_w_up, ffn1_w_down, mix_norm, w_in, conv_w, a_log, dt_bias, dn_norm, w_out, ffn2_norm, ffn2_w_gate, ffn2_w_up, ffn2_w_down, final_norm, loss_target, m_ffn1_norm, m_ffn1_w_gate, m_ffn1_w_up, m_ffn1_w_down, m_mix_norm, m_w_in, m_conv_w, m_a_log, m_dt_bias, m_dn_norm, m_w_out, m_ffn2_norm, m_ffn2_w_gate, m_ffn2_w_up, m_ffn2_w_down, m_final_norm, v_ffn1_norm, v_ffn1_w_gate, v_ffn1_w_up, v_ffn1_w_down, v_mix_norm, v_w_in, v_conv_w, v_a_log, v_dt_bias, v_dn_norm, v_w_out, v_ffn2_norm, v_ffn2_w_gate, v_ffn2_w_up, v_ffn2_w_down, v_final_norm):
    batch, seq, d = x.shape
    t = batch * seq
    big = dict(ffn1_w_gate=(ffn1_w_gate, m_ffn1_w_gate, v_ffn1_w_gate), ffn1_w_up=(ffn1_w_up, m_ffn1_w_up, v_ffn1_w_up),
               ffn1_w_down=(ffn1_w_down, m_ffn1_w_down, v_ffn1_w_down), w_in=(w_in, m_w_in, v_w_in),
               w_out=(w_out, m_w_out, v_w_out), ffn2_w_gate=(ffn2_w_gate, m_ffn2_w_gate, v_ffn2_w_gate),
               ffn2_w_up=(ffn2_w_up, m_ffn2_w_up, v_ffn2_w_up), ffn2_w_down=(ffn2_w_down, m_ffn2_w_down, v_ffn2_w_down))
    rep = dict(ffn1_norm=(ffn1_norm, m_ffn1_norm, v_ffn1_norm), mix_norm=(mix_norm, m_mix_norm, v_mix_norm),
               ffn2_norm=(ffn2_norm, m_ffn2_norm, v_ffn2_norm), final_norm=(final_norm, m_final_norm, v_final_norm),
               a_log=(a_log, m_a_log, v_a_log), dt_bias=(dt_bias, m_dt_bias, v_dt_bias), dn_norm=(dn_norm, m_dn_norm, v_dn_norm))

    lands = {"conv_w": _cast_place("place_conv_w", conv_w, F32)}
    lands.update({name: _cast_place("place_" + name, w, BF16) for name, (w, _, _) in big.items()})
    started, token = _gather_start("gather_start", list(lands.values()))
    gathering = dict(zip(lands, started))
    gathered, scattering = {}, {}

    def fetch(name, after):
        if name not in gathered:
            gathered[name] = _gather_wait("gather_wait_" + name, gathering[name], after)
        return gathered[name]

    def emit(name, grad):
        scattering[name], sent = _scatter_start("scatter_start_" + name, grad)
        return sent

    row = lambda a: a.reshape(1, -1)
    norms = [row(rep[n][0]) for n in ("ffn1_norm", "mix_norm", "ffn2_norm", "final_norm")]
    norms[0] = norms[0] + token[0, 0]
    loss, dx, g_norms, g_small, g_conv = _local_step(
        x.reshape(t, d), loss_target.reshape(t, d), norms, [row(rep[n][0]) for n in ("a_log", "dt_bias", "dn_norm")],
        fetch, emit, batch=batch, seq=seq)

    out = {"grad_x": dx.reshape(x.shape)}
    after = dx
    for name in scattering:
        w, m, v = big[name]
        own, landed = _scatter_wait("scatter_wait_" + name, scattering[name], after)
        out["grad_" + name], out["delta_" + name], out["new_m_" + name], out["new_v_" + name] = _adamw("adamw_" + name, landed, own, w, m, v)
        after = out["grad_" + name]
    conv_parts = jnp.swapaxes(g_conv.reshape(CONV_WIDTH, N_DEV, conv_w.shape[1]), 0, 1)
    parts = _exchange_slices("scatter_conv_w", conv_parts, after)
    out["grad_conv_w"], out["delta_conv_w"], out["new_m_conv_w"], out["new_v_conv_w"] = _adamw("adamw_conv_w", parts, parts, conv_w, m_conv_w, v_conv_w)

    rep_names = list(rep)
    g_rep = [*g_norms, *g_small]
    packed_g, offsets = _pack_rows([*g_rep, loss[:, :1]])
    packed = [_pack_rows([*[rep[n][i] for n in rep_names], jnp.zeros((1, 1), F32)])[0] for i in range(3)]
    parts = _all_gather("gather_small_grads", packed_g, out["grad_conv_w"])
    res = [_unpack_rows(a, offsets) for a in _adamw("adamw_small", parts, parts, *packed)]
    for i, name in enumerate(rep_names):
        shape = rep[name][0].shape
        out["grad_" + name], out["delta_" + name], out["new_m_" + name], out["new_v_" + name] = (r[i].reshape(shape) for r in res)
    out["loss"] = res[0][-1].reshape(())

    order = ["ffn1_norm", "ffn1_w_gate", "ffn1_w_up", "ffn1_w_down", "mix_norm", "w_in", "conv_w", "a_log", "dt_bias", "dn_norm",
             "w_out", "ffn2_norm", "ffn2_w_gate", "ffn2_w_up", "ffn2_w_down", "final_norm"]
    return (out["loss"], out["grad_x"], *[out["grad_" + n] for n in order], *[out["delta_" + n] for n in order],
            *[out["new_m_" + n] for n in order], *[out["new_v_" + n] for n in order])
```

```python
import functools
import math

import jax
import jax.numpy as jnp
from jax import lax
from jax.experimental import pallas as pl
from jax.experimental.pallas import tpu as pltpu

F32 = jnp.float32
BF16 = jnp.bfloat16
N_DEV = 8
HEAD = 128
CHUNK = 64
CHUNK_BITS = 6
DN_ROWS = 256
CONV_WIDTH = 4
EPS = 1e-6
DILATED_CONFIGS = ((128, 1), (512, 4), (2048, 16))
ATTN_BLOCK = 256
NEG = -1e30
ADAM_LR, ADAM_B1, ADAM_B2, ADAM_EPS, ADAM_WD, ADAM_STEP = 0.001, 0.9, 0.999, 1e-08, 0.01, 10
HI = lax.Precision.HIGHEST
MESH = pl.DeviceIdType.MESH
SDS = jax.ShapeDtypeStruct


def _tile(n, pref, align):
    t = (min(n, pref) // align) * align
    while t >= align:
        if n % t == 0:
            return t
        t -= align
    return n


def _params(n_axes, vmem_mb=48):
    return pltpu.CompilerParams(dimension_semantics=("arbitrary",) * n_axes, vmem_limit_bytes=vmem_mb * 2 ** 20)


def _sigmoid(x):
    return 1.0 / (1.0 + jnp.exp(-x))


def _silu(x):
    return x * _sigmoid(x)


def _softplus(x):
    return jnp.maximum(x, 0.0) + jnp.log(1.0 + jnp.exp(-jnp.abs(x)))


_DIMS = {"nn": (((1,), (0,)), ((), ())), "nt": (((1,), (1,)), ((), ())), "tn": (((0,), (0,)), ((), ()))}


def _mm_call(name, grid, mode, pairs, operands, in_specs, out_shape, out_specs, acc_shapes, epilogue, vmem_mb=48, after=None):
    dims = _DIMS[mode]
    if after is not None:
        operands, in_specs = (*operands, after), [*in_specs, pl.BlockSpec(memory_space=pl.ANY)]
    n_in, n_out = len(operands), len(out_shape)
    nk = grid[-1]

    def whole(*refs):
        ins, outs = refs[:n_in], refs[n_in:]
        sums = {}
        for a, b, c in pairs:
            prod = lax.dot_general(ins[a][...], ins[b][...], dims, preferred_element_type=F32)
            sums[c] = prod if c not in sums else sums[c] + prod
        epilogue(ins, outs, [sums[c] for c in sorted(sums)])

    if acc_shapes is None:
        return pl.pallas_call(
            whole, name=name, grid=grid, in_specs=in_specs, out_specs=out_specs, out_shape=out_shape,
            compiler_params=_params(len(grid), vmem_mb))(*operands)

    def body(*refs):
        ins, outs, accs = refs[:n_in], refs[n_in:n_in + n_out], refs[n_in + n_out:]
        k = pl.program_id(len(grid) - 1)

        @pl.when(k == 0)
        def _():
            for acc in accs:
                acc[...] = jnp.zeros_like(acc)

        sums = {}
        for a, b, c in pairs:
            prod = lax.dot_general(ins[a][...], ins[b][...], dims, preferred_element_type=F32)
            sums[c] = prod if c not in sums else sums[c] + prod
        for c, total in sums.items():
            accs[c][...] += total

        @pl.when(k == nk - 1)
        def _():
            epilogue(ins, outs, [acc[...] for acc in accs])

    return pl.pallas_call(
        body, name=name, grid=grid, in_specs=in_specs, out_specs=out_specs, out_shape=out_shape,
        scratch_shapes=[pltpu.VMEM(s, F32) for s in acc_shapes], compiler_params=_params(len(grid), vmem_mb),
    )(*operands)


def _ffn_proj(name, n, w, gate=None, after=None):
    t, d = n.shape
    f = w.shape[2]
    tm = _tile(t, 256, 16)
    n_spec = pl.BlockSpec((tm, d), lambda s, m, k: (m, 0))
    w_spec = pl.BlockSpec((None, d, f), lambda s, m, k: (s, 0, 0))
    o_spec = pl.BlockSpec((None, tm, f), lambda s, m, k: (s, m, 0))
    o_shape = SDS((N_DEV, t, f), BF16)
    grid = (N_DEV, t // tm, 1)
    if gate is None:
        return _mm_call(name, grid, "nn", [(0, 1, 0)], (n, w), [n_spec, w_spec], [o_shape], [o_spec], None, _store_bf16,
                        after=after)[0]

    def up_out(ins, outs, accs):
        outs[0][...] = accs[0].astype(BF16)
        outs[1][...] = (_silu(ins[2][...].astype(F32)) * accs[0]).astype(BF16)

    return _mm_call(name, grid, "nn", [(0, 1, 0)], (n, w, gate), [n_spec, w_spec, o_spec], [o_shape] * 2, [o_spec] * 2,
                    None, up_out, after=after)


def _ffn_down(act, wd, resid, scale, after=None):
    _, t, f = act.shape
    d = wd.shape[2]
    tm, tn = _tile(t, 512, 16), _tile(d, 1024, 128)

    def epilogue(ins, outs, accs):
        outs[0][...] = ins[2][...] + scale * accs[0]

    rc = pl.BlockSpec((tm, tn), lambda m, n, s: (m, n))
    return _mm_call(
        "ffn_down", (t // tm, d // tn, N_DEV), "nn", [(0, 1, 0)], (act, wd, resid),
        [pl.BlockSpec((None, tm, f), lambda m, n, s: (s, m, 0)), pl.BlockSpec((None, f, tn), lambda m, n, s: (s, 0, n)), rc],
        [SDS((t, d), F32)], [rc], [(tm, tn)], epilogue, after=after)[0]


def _in_proj(n, w):
    t, d = n.shape
    p = w.shape[2]
    tm = _tile(t, 256, 16)
    return _mm_call(
        "in_proj", (N_DEV, t // tm, 1), "nn", [(0, 1, 0)], (n, w),
        [pl.BlockSpec((tm, d), lambda s, m, k: (m, 0)), pl.BlockSpec((None, d, p), lambda s, m, k: (s, 0, 0))],
        [SDS((N_DEV, t, p), F32)], [pl.BlockSpec((None, tm, p), lambda s, m, k: (s, m, 0))], None, _store_f32)[0]


def _out_proj(cat, w, resid, after=None):
    t, dm = cat.shape
    d = w.shape[1]
    tm, tn = _tile(t, 512, 16), _tile(d, 1024, 128)

    def epilogue(ins, outs, accs):
        outs[0][...] = ins[2][...] + accs[0]

    rc = pl.BlockSpec((tm, tn), lambda n, m, k: (m, n))
    return _mm_call(
        "out_proj", (d // tn, t // tm, 1), "nn", [(0, 1, 0)], (cat, w, resid),
        [pl.BlockSpec((tm, dm), lambda n, m, k: (m, 0)), pl.BlockSpec((dm, tn), lambda n, m, k: (0, n)), rc],
        [SDS((t, d), F32)], [rc], None, epilogue, after=after)[0]


def _ffn_bwd_act(dy, wd, gate, up, after=None):
    t, d = dy.shape
    f = wd.shape[1]
    tm = _tile(t, 256, 16)

    def epilogue(ins, outs, accs):
        g, u = ins[2][...].astype(F32), ins[3][...].astype(F32)
        sg = _sigmoid(g)
        outs[0][...] = (accs[0] * u * sg * (1.0 + g * (1.0 - sg))).astype(BF16)
        outs[1][...] = (accs[0] * g * sg).astype(BF16)

    o_spec = pl.BlockSpec((None, tm, f), lambda s, m, k: (s, m, 0))
    return _mm_call(
        "ffn_bwd_act", (N_DEV, t // tm, 1), "nt", [(0, 1, 0)], (dy, wd, gate, up),
        [pl.BlockSpec((tm, d), lambda s, m, k: (m, 0)), pl.BlockSpec((None, f, d), lambda s, m, k: (s, 0, 0)), o_spec, o_spec],
        [SDS((N_DEV, t, f), BF16)] * 2, [o_spec] * 2, None, epilogue, after=after)


def _store_bf16(ins, outs, accs):
    outs[0][...] = accs[0].astype(BF16)


def _store_f32(ins, outs, accs):
    outs[0][...] = accs[0]


def _wgrad_cols(name, a_t, b, after=None):
    m, t = a_t.shape
    n = b.shape[2]
    tm, tk = _tile(m, 512 if n <= 1408 else 256, 128), t
    return _mm_call(
        name, (N_DEV, m // tm, t // tk), "nn", [(0, 1, 0)], (a_t, b),
        [pl.BlockSpec((tm, tk), lambda s, i, k: (i, k)), pl.BlockSpec((None, tk, n), lambda s, i, k: (s, k, 0))],
        [SDS((N_DEV, m, n), BF16)], [pl.BlockSpec((None, tm, n), lambda s, i, k: (s, i, 0))], None, _store_bf16,
        after=after)[0]


def _wgrad_rows(name, a, b, after=None):
    _, t, m = a.shape
    n = b.shape[1]
    tn, tk = _tile(n, 512, 128), t
    return _mm_call(
        name, (N_DEV, n // tn, t // tk), "tn", [(0, 1, 0)], (a, b),
        [pl.BlockSpec((None, tk, m), lambda s, j, k: (s, k, 0)), pl.BlockSpec((tk, tn), lambda s, j, k: (k, j))],
        [SDS((N_DEV, m, n), BF16)], [pl.BlockSpec((None, m, tn), lambda s, j, k: (s, 0, j))], None, _store_bf16,
        after=after)[0]


def _wgrad_full(name, a, b, after=None):
    t, m = a.shape
    n = b.shape[1]
    tm, tn, tk = _tile(m, 512, 128), _tile(n, 1024, 128), t
    return _mm_call(
        name, (m // tm, n // tn, t // tk), "tn", [(0, 1, 0)], (a, b),
        [pl.BlockSpec((tk, tm), lambda i, j, k: (k, i)), pl.BlockSpec((tk, tn), lambda i, j, k: (k, j))],
        [SDS((m, n), BF16)], [pl.BlockSpec((tm, tn), lambda i, j, k: (i, j))], None, _store_bf16, after=after)[0]


def _dgrad_cols(name, grads, weights, after=None):
    _, t, n = grads[0].shape
    m = weights[0].shape[1]
    tm, tn = _tile(t, 512, 16), _tile(m, 1024, 128)
    k = len(grads)
    return _mm_call(
        name, (t // tm, m // tn, N_DEV), "nt", [(i, k + i, 0) for i in range(k)], (*grads, *weights),
        [pl.BlockSpec((None, tm, n), lambda i, j, s: (s, i, 0))] * k + [pl.BlockSpec((None, tn, n), lambda i, j, s: (s, j, 0))] * k,
        [SDS((t, m), F32)], [pl.BlockSpec((tm, tn), lambda i, j, s: (i, j))], [(tm, tn)], _store_f32, after=after)[0]


def _dgrad_full(name, g, w, after=None):
    t, n = g.shape
    m = w.shape[0]
    tm, tn, tk = _tile(t, 512, 16), _tile(m, 1024, 128), n
    return _mm_call(
        name, (m // tn, t // tm, n // tk), "nt", [(0, 1, 0)], (g, w),
        [pl.BlockSpec((tm, tk), lambda j, i, k: (i, k)), pl.BlockSpec((tn, tk), lambda j, i, k: (j, k))],
        [SDS((t, m), F32)], [pl.BlockSpec((tm, tn), lambda j, i, k: (i, j))], None, _store_f32, after=after)[0]


def _rms_fwd(name, h, w):
    t, d = h.shape
    tm = _tile(t, 256, 128)

    def body(h_ref, w_ref, o_ref, ot_ref):
        x = h_ref[...]
        n = (x * lax.rsqrt(jnp.mean(x * x, axis=1, keepdims=True) + EPS) * w_ref[...]).astype(BF16)
        o_ref[...] = n
        ot_ref[...] = n.T

    row = pl.BlockSpec((tm, d), lambda i: (i, 0))
    return pl.pallas_call(
        body, name=name, grid=(t // tm,), in_specs=[row, pl.BlockSpec((1, d), lambda i: (0, 0))],
        out_specs=[row, pl.BlockSpec((d, tm), lambda i: (0, i))], out_shape=[SDS((t, d), BF16), SDS((d, t), BF16)],
        compiler_params=_params(1))(h, w)


def _rms_bwd(name, h, dn, w, dres, scale):
    t, d = h.shape
    tm = _tile(t, 128, 16)

    def body(h_ref, dn_ref, w_ref, dres_ref, dh_ref, dhb_ref, dw_ref):
        @pl.when(pl.program_id(0) == 0)
        def _():
            dw_ref[...] = jnp.zeros_like(dw_ref)

        x = h_ref[...]
        rstd = lax.rsqrt(jnp.mean(x * x, axis=1, keepdims=True) + EPS)
        nhat = x * rstd
        g = dn_ref[...]
        gw = g * w_ref[...]
        dh = dres_ref[...] + rstd * (gw - nhat * jnp.mean(gw * nhat, axis=1, keepdims=True))
        dh_ref[...] = dh
        dhb_ref[...] = (scale * dh).astype(BF16)
        dw_ref[...] += jnp.sum(g * nhat, axis=0, keepdims=True)

    row = pl.BlockSpec((tm, d), lambda i: (i, 0))
    vec = pl.BlockSpec((1, d), lambda i: (0, 0))
    return pl.pallas_call(
        body, name=name, grid=(t // tm,), in_specs=[row, row, vec, row], out_specs=[row, row, vec],
        out_shape=[SDS((t, d), F32), SDS((t, d), BF16), SDS((1, d), F32)], compiler_params=_params(1))(h, dn, w, dres)


def _loss_head(h, w, target, scale):
    t, d = h.shape
    tm = _tile(t, 128, 16)

    def body(h_ref, w_ref, tg_ref, loss_ref, dh_ref, dhb_ref, dw_ref):
        @pl.when(pl.program_id(0) == 0)
        def _():
            dw_ref[...] = jnp.zeros_like(dw_ref)
            loss_ref[...] = jnp.zeros_like(loss_ref)

        x = h_ref[...]
        rstd = lax.rsqrt(jnp.mean(x * x, axis=1, keepdims=True) + EPS)
        nhat = x * rstd
        wv = w_ref[...]
        err = nhat * wv - tg_ref[...]
        loss_ref[...] += 0.5 * jnp.sum(jnp.mean(err * err, axis=1, keepdims=True), axis=0, keepdims=True)
        g = err * (1.0 / d)
        gw = g * wv
        dh = rstd * (gw - nhat * jnp.mean(gw * nhat, axis=1, keepdims=True))
        dh_ref[...] = dh
        dhb_ref[...] = (scale * dh).astype(BF16)
        dw_ref[...] += jnp.sum(g * nhat, axis=0, keepdims=True)

    row = pl.BlockSpec((tm, d), lambda i: (i, 0))
    vec = pl.BlockSpec((1, d), lambda i: (0, 0))
    return pl.pallas_call(
        body, name="loss_head", grid=(t // tm,), in_specs=[row, vec, row],
        out_specs=[pl.BlockSpec((1, 128), lambda i: (0, 0)), row, row, vec],
        out_shape=[SDS((1, 128), F32), SDS((t, d), F32), SDS((t, d), BF16), SDS((1, d), F32)],
        compiler_params=_params(1))(h, w, target)


def _attn_bias(qi, ki, blk):
    dist = (lax.broadcasted_iota(jnp.int32, (blk, blk), 0) - lax.broadcasted_iota(jnp.int32, (blk, blk), 1)
            + (qi - ki) * blk)
    count = jnp.zeros((blk, blk), F32)
    for window, dil in DILATED_CONFIGS:
        assert dil & (dil - 1) == 0
        seen = (dist >= 0) & (dist <= window) & ((dist & (dil - 1)) == 0)
        count = count + jnp.where(seen, 1.0, 0.0)
    return jnp.where(count > 0.0, jnp.log(jnp.maximum(count, 1.0)), NEG)


def _attn_fwd(qkv, *, batch, seq):
    t, da3 = qkv.shape
    da = da3 // 3
    n_heads = da // HEAD
    blk = _tile(seq, ATTN_BLOCK, 16)
    nq = seq // blk
    sm_scale = HEAD ** -0.5

    def body(q_ref, k_ref, v_ref, o_ref, lse_ref):
        def q_step(qi, _):
            rows = pl.ds(pl.multiple_of(qi * blk, blk), blk)
            q = q_ref[rows, :]

            def kv_step(ki, carry):
                m, l, acc = carry
                cols = pl.ds(pl.multiple_of(ki * blk, blk), blk)
                s = lax.dot_general(q, k_ref[cols, :], _DIMS["nt"], preferred_element_type=F32) * sm_scale
                s = s + _attn_bias(qi, ki, blk)
                m_new = jnp.maximum(m, jnp.max(s, axis=1, keepdims=True))
                alpha = jnp.exp(m - m_new)
                p = jnp.exp(s - m_new)
                l = alpha * l + jnp.sum(p, axis=1, keepdims=True)
                acc = alpha * acc + jnp.dot(p.astype(BF16), v_ref[cols, :], preferred_element_type=F32)
                return m_new, l, acc

            m, l, acc = lax.fori_loop(0, qi + 1, kv_step, (jnp.full((blk, 1), NEG, F32), jnp.zeros((blk, 1), F32),
                                                           jnp.zeros((blk, HEAD), F32)))
            o_ref[rows, :] = acc / l
            lse_ref[rows, :] = jnp.broadcast_to(m + jnp.log(l), (blk, HEAD))
            return 0

        lax.fori_loop(0, nq, q_step, 0)

    col = lambda off: pl.BlockSpec((seq, HEAD), lambda b, h: (b, off + h))
    return pl.pallas_call(
        body, name="attn_fwd", grid=(batch, n_heads), in_specs=[col(0), col(n_heads), col(2 * n_heads)],
        out_specs=[col(0), col(0)], out_shape=[SDS((t, da), F32), SDS((t, da), F32)], compiler_params=_params(2),
    )(qkv, qkv, qkv)


def _attn_bwd(qkv, out, lse, d_out, *, batch, seq):
    t, da = out.shape
    n_heads = da // HEAD
    blk = _tile(seq, ATTN_BLOCK, 16)
    nq = seq // blk
    sm_scale = HEAD ** -0.5

    def body(q_ref, k_ref, v_ref, o_ref, lse_ref, do_ref, dq_ref, dk_ref, dv_ref):
        dk_ref[...] = jnp.zeros_like(dk_ref)
        dv_ref[...] = jnp.zeros_like(dv_ref)

        def q_step(qi, _):
            rows = pl.ds(pl.multiple_of(qi * blk, blk), blk)
            q = q_ref[rows, :]
            do = do_ref[rows, :]
            do_b = do.astype(BF16)
            lse_q = lse_ref[rows, :][:, :1]
            delta = jnp.sum(do * o_ref[rows, :], axis=1, keepdims=True)

            def kv_step(ki, dq):
                cols = pl.ds(pl.multiple_of(ki * blk, blk), blk)
                k = k_ref[cols, :]
                s = lax.dot_general(q, k, _DIMS["nt"], preferred_element_type=F32) * sm_scale
                p = jnp.exp(s + _attn_bias(qi, ki, blk) - lse_q)
                dp = lax.dot_general(do_b, v_ref[cols, :], _DIMS["nt"], preferred_element_type=F32)
                ds = (p * (dp - delta) * sm_scale).astype(BF16)
                dv_ref[cols, :] += lax.dot_general(p.astype(BF16), do_b, _DIMS["tn"], preferred_element_type=F32)
                dk_ref[cols, :] += lax.dot_general(ds, q, _DIMS["tn"], preferred_element_type=F32)
                return dq + jnp.dot(ds, k, preferred_element_type=F32)

            dq_ref[rows, :] = lax.fori_loop(0, qi + 1, kv_step, jnp.zeros((blk, HEAD), F32))
            return 0

        lax.fori_loop(0, nq, q_step, 0)

    col = lambda off: pl.BlockSpec((seq, HEAD), lambda b, h: (b, off + h))
    return pl.pallas_call(
        body, name="attn_bwd", grid=(batch, n_heads),
        in_specs=[col(0), col(n_heads), col(2 * n_heads), col(0), col(0), col(0)], out_specs=[col(0)] * 3,
        out_shape=[SDS((t, da), F32)] * 3, compiler_params=_params(2),
    )(qkv, qkv, qkv, out, lse, d_out)


def _shift_down(x, k, row):
    return x if k == 0 else jnp.where(row >= k, pltpu.roll(x, k, axis=0), 0.0)


def _shift_up(x, k, row):
    n = x.shape[0]
    return x if k == 0 else jnp.where(row < n - k, pltpu.roll(x, n - k, axis=0), 0.0)


def _conv_silu_fwd(x, w, *, batch, seq):
    t, c = x.shape

    def body(x_ref, w_ref, o_ref):
        xv = x_ref[...]
        row = lax.broadcasted_iota(jnp.int32, xv.shape, 0)
        acc = jnp.zeros_like(xv)
        for i in range(CONV_WIDTH):
            acc = acc + w_ref[i:i + 1, :] * _shift_down(xv, CONV_WIDTH - 1 - i, row)
        o_ref[...] = _silu(acc)

    blk = pl.BlockSpec((seq, HEAD), lambda j, b: (b, j))
    return pl.pallas_call(
        body, name="conv_silu_fwd", grid=(c // HEAD, batch), in_specs=[blk, pl.BlockSpec((CONV_WIDTH, HEAD), lambda j, b: (0, j))],
        out_specs=blk, out_shape=SDS((t, c), F32), compiler_params=_params(2))(x, w)


def _conv_silu_bwd(x, w, dy, *, batch, seq):
    t, c = x.shape

    def body(x_ref, w_ref, dy_ref, dx_ref, dw_ref):
        @pl.when(pl.program_id(1) == 0)
        def _():
            dw_ref[...] = jnp.zeros_like(dw_ref)

        xv = x_ref[...]
        row = lax.broadcasted_iota(jnp.int32, xv.shape, 0)
        shifted = [_shift_down(xv, CONV_WIDTH - 1 - i, row) for i in range(CONV_WIDTH)]
        acc = jnp.zeros_like(xv)
        for i in range(CONV_WIDTH):
            acc = acc + w_ref[i:i + 1, :] * shifted[i]
        sg = _sigmoid(acc)
        dc = dy_ref[...] * sg * (1.0 + acc * (1.0 - sg))
        dx = jnp.zeros_like(xv)
        for i in range(CONV_WIDTH):
            dx = dx + w_ref[i:i + 1, :] * _shift_up(dc, CONV_WIDTH - 1 - i, row)
            dw_ref[i:i + 1, :] += jnp.sum(dc * shifted[i], axis=0, keepdims=True)
        dx_ref[...] = dx

    blk = pl.BlockSpec((seq, HEAD), lambda j, b: (b, j))
    wblk = pl.BlockSpec((CONV_WIDTH, HEAD), lambda j, b: (0, j))
    return pl.pallas_call(
        body, name="conv_silu_bwd", grid=(c // HEAD, batch), in_specs=[blk, wblk, blk], out_specs=[blk, wblk],
        out_shape=[SDS((t, c), F32), SDS((CONV_WIDTH, c), F32)], compiler_params=_params(2))(x, w, dy)


def _dot(a, b, mode="nn"):
    return lax.dot_general(a.astype(BF16), b.astype(BF16), _DIMS[mode], preferred_element_type=F32)


def _dot3(a, b):
    a_hi, b_hi = a.astype(BF16), b.astype(BF16)
    a_lo, b_lo = (a - a_hi.astype(F32)).astype(BF16), (b - b_hi.astype(F32)).astype(BF16)
    pass_ = lambda x, y: jnp.dot(x, y, preferred_element_type=F32)
    return pass_(a_hi, b_hi) + pass_(a_hi, b_lo) + pass_(a_lo, b_hi)


def _dot_nt(a, b):
    return _dot(a, b, "nt")


def _dot_tn(a, b):
    return _dot(a, b, "tn")


def _dn_chunk(head, n_heads, aq, ak, v, z, dbda, a_log, dt_bias, dn_norm, state):
    r = aq.shape[0]
    lane_g = lax.broadcasted_iota(jnp.int32, dbda.shape, 1)
    db = jnp.sum(jnp.where(lane_g == head, dbda, 0.0), axis=1, keepdims=True)
    da = jnp.sum(jnp.where(lane_g == head + n_heads, dbda, 0.0), axis=1, keepdims=True)
    lane_h = lax.broadcasted_iota(jnp.int32, a_log.shape, 1)
    al = jnp.sum(jnp.where(lane_h == head, a_log, 0.0), axis=1, keepdims=True)
    dtb = jnp.sum(jnp.where(lane_h == head, dt_bias, 0.0), axis=1, keepdims=True)
    beta = _sigmoid(db)
    g = -jnp.exp(al) * _softplus(da + dtb)
    q = aq * lax.rsqrt(jnp.sum(aq * aq, axis=1, keepdims=True) + EPS) * (HEAD ** -0.5)
    k = ak * lax.rsqrt(jnp.sum(ak * ak, axis=1, keepdims=True) + EPS)
    ri = lax.broadcasted_iota(jnp.int32, (r, r), 0)
    ci = lax.broadcasted_iota(jnp.int32, (r, r), 1)
    same = (ri >> CHUNK_BITS) == (ci >> CHUNK_BITS)
    incl = same & (ri >= ci)
    g_row = jnp.sum(jnp.where(ri == ci, g, 0.0), axis=0, keepdims=True)
    gc_col = jnp.sum(jnp.where(incl, g_row, 0.0), axis=1, keepdims=True)
    gc_row = jnp.sum(jnp.where(same & (ri <= ci), g, 0.0), axis=0, keepdims=True)
    g_all = jnp.sum(jnp.where(same, g_row, 0.0), axis=1, keepdims=True)
    decay = jnp.where(incl, jnp.exp(jnp.where(incl, gc_col - gc_row, 0.0)), 0.0)
    kb = k * beta
    m = -jnp.where(same & (ri > ci), _dot_nt(kb, k) * decay, 0.0)
    x = jnp.where(ri == ci, 1.0, 0.0) + m
    p = m
    for _ in range(int(math.log2(CHUNK)) - 1):
        p = _dot3(p, p)
        x = x + _dot3(x, p)
    egc = jnp.exp(gc_col)
    wu_g = _dot(x, jnp.concatenate([kb * egc, v * beta], axis=1))
    w_g, u_g = wu_g[:, :HEAD], wu_g[:, HEAD:]
    qk = _dot_nt(q, k) * decay
    q_dec = q * egc
    k_dec = k * jnp.exp(g_all - gc_col)
    carry = jnp.exp(g_all)
    v_new, o_state = [], []
    for c in range(r // CHUNK):
        rows = slice(c * CHUNK, (c + 1) * CHUNK)
        v_new.append(u_g[rows] - _dot(w_g[rows], state))
        o_state.append(_dot(q_dec[rows], state))
        state = state * carry[c * CHUNK:c * CHUNK + 1] + _dot_tn(k_dec[rows], v_new[-1])
    o = jnp.concatenate(o_state, axis=0) + _dot(qk, jnp.concatenate(v_new, axis=0))
    o = o * lax.rsqrt(jnp.mean(o * o, axis=1, keepdims=True) + EPS) * dn_norm
    return o * _silu(z), state


def _loop_by_two(n, step, init):
    if n % 2:
        return lax.fori_loop(0, n, step, init)
    return lax.fori_loop(0, n // 2, lambda i, carry: step(2 * i + 1, step(2 * i, carry)), init)


def _dn_state_spec(n_groups):
    return pl.BlockSpec((None, None, n_groups, HEAD, HEAD), lambda b, h: (b, h, 0, 0, 0))


def _dn_specs(seq, n_heads, small):
    col = lambda off: pl.BlockSpec((seq, HEAD), lambda b, h: (b, off + h))
    full = [pl.BlockSpec(a.shape, lambda b, h: (0, 0)) for a in small]
    gates = pl.BlockSpec((seq, 2 * n_heads), lambda b, h: (b, 0))
    return col, gates, full


def _dn_fwd(y, z, dbda, a_log, dt_bias, dn_norm, *, batch, seq):
    t, dd = z.shape
    n_heads = dd // HEAD
    grp = _tile(seq, DN_ROWS, CHUNK)
    n_chunks = seq // grp

    def body(q_ref, k_ref, v_ref, z_ref, g_ref, al_ref, dt_ref, nw_ref, o_ref, states):
        head = pl.program_id(1)
        al, dtb, nw = al_ref[...], dt_ref[...], nw_ref[...]

        def step(n, state):
            states[n] = state
            rows = pl.ds(pl.multiple_of(n * grp, grp), grp)
            out, state = _dn_chunk(head, n_heads, q_ref[rows, :], k_ref[rows, :], v_ref[rows, :], z_ref[rows, :],
                                   g_ref[rows, :], al, dtb, nw, state)
            o_ref[rows, :] = out
            return state

        _loop_by_two(n_chunks, step, jnp.zeros((HEAD, HEAD), F32))

    col, gates, full = _dn_specs(seq, n_heads, (a_log, dt_bias, dn_norm))
    return pl.pallas_call(
        body, name="dn_fwd", grid=(batch, n_heads), in_specs=[col(0), col(n_heads), col(2 * n_heads), col(0), gates, *full],
        out_specs=[col(0), _dn_state_spec(n_chunks)],
        out_shape=[SDS((t, dd), F32), SDS((batch, n_heads, n_chunks, HEAD, HEAD), F32)], compiler_params=_params(2),
    )(y, y, y, z, dbda, a_log, dt_bias, dn_norm)


def _dn_bwd(y, z, dbda, a_log, dt_bias, dn_norm, states, d_out, *, batch, seq):
    t, dd = z.shape
    n_heads = dd // HEAD
    grp = _tile(seq, DN_ROWS, CHUNK)
    n_chunks = seq // grp

    def body(q_ref, k_ref, v_ref, z_ref, g_ref, al_ref, dt_ref, nw_ref, states, do_ref,
             dq_ref, dk_ref, dv_ref, dz_ref, dg_ref, dal_ref, ddt_ref, dnw_ref):
        b, head = pl.program_id(0), pl.program_id(1)
        al, dtb, nw = al_ref[...], dt_ref[...], nw_ref[...]

        @pl.when((b == 0) & (head == 0))
        def _():
            dal_ref[...] = jnp.zeros_like(dal_ref)
            ddt_ref[...] = jnp.zeros_like(ddt_ref)
            dnw_ref[...] = jnp.zeros_like(dnw_ref)

        @pl.when(head == 0)
        def _():
            dg_ref[...] = jnp.zeros_like(dg_ref)

        def chunk(n):
            rows = pl.ds(pl.multiple_of(n * grp, grp), grp)
            return rows, (q_ref[rows, :], k_ref[rows, :], v_ref[rows, :], z_ref[rows, :], g_ref[rows, :], al, dtb, nw)

        def bwd_step(i, carry):
            d_state, d_al, d_dt, d_nw = carry
            n = n_chunks - 1 - i
            rows, args = chunk(n)
            _, vjp = jax.vjp(functools.partial(_dn_chunk, head, n_heads), *args, states[n])
            gq, gk, gv, gz, gg, gal, gdt, gnw, d_state = vjp((do_ref[rows, :], d_state))
            dq_ref[rows, :] = gq
            dk_ref[rows, :] = gk
            dv_ref[rows, :] = gv
            dz_ref[rows, :] = gz
            dg_ref[rows, :] += gg
            return d_state, d_al + gal, d_dt + gdt, d_nw + gnw

        zero = lambda a: jnp.zeros(a.shape, F32)
        _, d_al, d_dt, d_nw = _loop_by_two(
            n_chunks, bwd_step, (jnp.zeros((HEAD, HEAD), F32), zero(al), zero(dtb), zero(nw)))
        dal_ref[...] += d_al
        ddt_ref[...] += d_dt
        dnw_ref[...] += d_nw

    col, gates, full = _dn_specs(seq, n_heads, (a_log, dt_bias, dn_norm))
    return pl.pallas_call(
        body, name="dn_bwd", grid=(batch, n_heads),
        in_specs=[col(0), col(n_heads), col(2 * n_heads), col(0), gates, *full, _dn_state_spec(n_chunks), col(0)],
        out_specs=[col(0), col(0), col(0), col(0), gates, *full],
        out_shape=[SDS((t, dd), F32)] * 4 + [SDS(dbda.shape, F32), SDS(a_log.shape, F32), SDS(dt_bias.shape, F32),
                                             SDS(dn_norm.shape, F32)],
        compiler_params=_params(2),
    )(y, y, y, z, dbda, a_log, dt_bias, dn_norm, states, d_out)


def _my_slot():
    return 4 * lax.axis_index("x") + 2 * lax.axis_index("y") + lax.axis_index("c")


def _peer(k):
    x, y, c = lax.axis_index("x"), lax.axis_index("y"), lax.axis_index("c")
    return (x ^ (k >> 2), y ^ ((k >> 1) & 1), c ^ (k & 1)), (4 * x + 2 * y + c) ^ k


def _all_gather(name, block, after):
    def body(src, after_ref, dst, send_sems, recv_sems, local_sem):
        me = _my_slot()
        own = pltpu.make_async_copy(src, dst.at[me], local_sem)
        own.start()
        copies = []
        for k in range(1, N_DEV):
            peer, _ = _peer(k)
            copies.append(pltpu.make_async_remote_copy(
                src_ref=src, dst_ref=dst.at[me], send_sem=send_sems.at[k - 1], recv_sem=recv_sems.at[k - 1],
                device_id=peer, device_id_type=MESH))
            copies[-1].start()
        for k in range(1, N_DEV):
            peer, slot = _peer(k)
            pltpu.make_async_remote_copy(
                src_ref=src, dst_ref=dst.at[slot], send_sem=send_sems.at[k - 1], recv_sem=recv_sems.at[k - 1],
                device_id=peer, device_id_type=MESH).wait_recv()
        for cp in copies:
            cp.wait_send()
        own.wait()

    return pl.pallas_call(
        body, name=name, in_specs=[pl.BlockSpec(memory_space=pl.ANY)] * 2, out_specs=pl.BlockSpec(memory_space=pl.ANY),
        out_shape=SDS((N_DEV, *block.shape), block.dtype),
        scratch_shapes=[pltpu.SemaphoreType.DMA((N_DEV - 1,)), pltpu.SemaphoreType.DMA((N_DEV - 1,)), pltpu.SemaphoreType.DMA],
    )(block, after)


def _exchange_slices(name, parts, after):
    def body(src, after_ref, dst, send_sems, recv_sems, local_sem):
        me = _my_slot()
        own = pltpu.make_async_copy(src.at[me], dst.at[me], local_sem)
        own.start()
        copies = []
        for k in range(1, N_DEV):
            peer, slot = _peer(k)
            copies.append(pltpu.make_async_remote_copy(
                src_ref=src.at[slot], dst_ref=dst.at[me], send_sem=send_sems.at[k - 1], recv_sem=recv_sems.at[k - 1],
                device_id=peer, device_id_type=MESH))
            copies[-1].start()
        for k in range(1, N_DEV):
            peer, slot = _peer(k)
            pltpu.make_async_remote_copy(
                src_ref=src.at[me], dst_ref=dst.at[slot], send_sem=send_sems.at[k - 1], recv_sem=recv_sems.at[k - 1],
                device_id=peer, device_id_type=MESH).wait_recv()
        for cp in copies:
            cp.wait_send()
        own.wait()

    return pl.pallas_call(
        body, name=name, in_specs=[pl.BlockSpec(memory_space=pl.ANY)] * 2, out_specs=pl.BlockSpec(memory_space=pl.ANY),
        out_shape=SDS(parts.shape, parts.dtype),
        scratch_shapes=[pltpu.SemaphoreType.DMA((N_DEV - 1,)), pltpu.SemaphoreType.DMA((N_DEV - 1,)), pltpu.SemaphoreType.DMA],
    )(parts, after)


_HBM = pl.BlockSpec(memory_space=pltpu.HBM)
_SEM = pl.BlockSpec(memory_space=pltpu.SEMAPHORE)
_EFFECT = pltpu.SideEffectType.DATAFLOW_SIDE_EFFECTING


def _slot_operand():
    return _my_slot().astype(jnp.int32).reshape(1)


def _cast_place(name, block, dtype):
    r, c = block.shape
    tr = _tile(r, 512, 16)

    def body(me_ref, src_ref, dst_ref):
        dst_ref[...] = src_ref[...].astype(dtype)

    return pl.pallas_call(
        body, name=name, out_shape=SDS((N_DEV, r, c), dtype), compiler_params=_params(1),
        grid_spec=pltpu.PrefetchScalarGridSpec(
            num_scalar_prefetch=1, grid=(r // tr,), in_specs=[pl.BlockSpec((tr, c), lambda i, me: (i, 0))],
            out_specs=pl.BlockSpec((None, tr, c), lambda i, me: (me[0], i, 0))),
    )(_slot_operand(), block)


_SIBLING = 1
_SAME_CORE = (2, 4, 6)
_OTHER_CORE = (3, 5, 7)


def _gather_start(name, lands):
    n = len(lands)

    def body(*refs):
        lnds, outs = refs[:n], refs[n:]
        me = _my_slot()
        for i in range(n):
            for k in (*_SAME_CORE, _SIBLING):
                peer, _ = _peer(k)
                pltpu.make_async_remote_copy(
                    src_ref=lnds[i].at[me], dst_ref=lnds[i].at[me], send_sem=outs[2 * i].at[k - 1],
                    recv_sem=outs[2 * i + 1].at[k - 1], device_id=peer, device_id_type=MESH).start()
        outs[-1][...] = jnp.zeros_like(outs[-1])

    res = pl.pallas_call(
        body, name=name, in_specs=[_HBM] * n,
        out_specs=[_SEM] * (2 * n) + [_HBM] * n + [pl.BlockSpec(memory_space=pltpu.VMEM)],
        out_shape=[pltpu.SemaphoreType.DMA((N_DEV - 1,))] * (2 * n) + [pltpu.HBM(a.shape, a.dtype) for a in lands]
        + [SDS((8, 128), F32)],
        input_output_aliases={i: 2 * n + i for i in range(n)},
        compiler_params=pltpu.CompilerParams(has_side_effects=_EFFECT),
    )(*[pltpu.with_memory_space_constraint(a, pltpu.HBM) for a in lands])
    return [(res[2 * i], res[2 * i + 1], res[2 * n + i]) for i in range(n)], res[-1]


def _gather_copy(land_ref, send_ref, recv_ref, k, slot, to):
    return pltpu.make_async_remote_copy(
        src_ref=land_ref.at[slot], dst_ref=land_ref.at[slot], send_sem=send_ref.at[k - 1], recv_sem=recv_ref.at[k - 1],
        device_id=to, device_id_type=MESH)


def _gather_arrived(name, started, after):
    send_sems, recv_sems, land = started

    def body(land_ref, send_ref, recv_ref, after_ref, land_out):
        for k in _SAME_CORE:
            peer, slot = _peer(k)
            _gather_copy(land_ref, send_ref, recv_ref, k, slot, peer).wait_recv()

    return pl.pallas_call(
        body, name=name, in_specs=[_HBM, _SEM, _SEM, pl.BlockSpec(memory_space=pl.ANY)], out_specs=[_HBM],
        out_shape=[pltpu.HBM(land.shape, land.dtype)], input_output_aliases={0: 0},
        compiler_params=pltpu.CompilerParams(has_side_effects=_EFFECT),
    )(land, send_sems, recv_sems, after)[0]


def _gather_forward(name, land):
    def body(land_ref, send_ref, recv_ref, land_out, token):
        sibling, _ = _peer(_SIBLING)
        for j, k in enumerate(_SAME_CORE):
            _, slot = _peer(k)
            _gather_copy(land_ref, send_ref, recv_ref, j + 1, slot, sibling).start()
        token[...] = jnp.zeros_like(token)

    res = pl.pallas_call(
        body, name=name, in_specs=[_HBM], out_specs=[_SEM, _SEM, _HBM, pl.BlockSpec(memory_space=pltpu.VMEM)],
        out_shape=[pltpu.SemaphoreType.DMA((len(_SAME_CORE),))] * 2 + [pltpu.HBM(land.shape, land.dtype), SDS((8, 128), F32)],
        input_output_aliases={0: 2}, compiler_params=pltpu.CompilerParams(has_side_effects=_EFFECT),
    )(land)
    return tuple(res[:3]), res[3]


def _gather_wait(name, started, forwarded, after):
    send_sems, recv_sems, _ = started
    send_fwd, recv_fwd, land = forwarded

    def body(land_ref, send_ref, recv_ref, send2_ref, recv2_ref, after_ref, land_out):
        sibling, slot = _peer(_SIBLING)
        _gather_copy(land_ref, send_ref, recv_ref, _SIBLING, slot, sibling).wait_recv()
        for j, k in enumerate(_OTHER_CORE):
            _, slot = _peer(k)
            _gather_copy(land_ref, send2_ref, recv2_ref, j + 1, slot, sibling).wait_recv()
        for k in (_SIBLING, *_SAME_CORE):
            peer, slot = _peer(k)
            _gather_copy(land_ref, send_ref, recv_ref, k, slot, peer).wait_send()
        for j, k in enumerate(_SAME_CORE):
            _, slot = _peer(k)
            _gather_copy(land_ref, send2_ref, recv2_ref, j + 1, slot, sibling).wait_send()

    return pl.pallas_call(
        body, name=name, in_specs=[_HBM, _SEM, _SEM, _SEM, _SEM, pl.BlockSpec(memory_space=pl.ANY)], out_specs=[_HBM],
        out_shape=[pltpu.HBM(land.shape, land.dtype)], input_output_aliases={0: 0},
        compiler_params=pltpu.CompilerParams(has_side_effects=_EFFECT),
    )(land, send_sems, recv_sems, send_fwd, recv_fwd, after)[0]


def _scatter_start(name, parts):
    land = lax.empty(parts.shape, parts.dtype)

    def body(src, lnd, send_ref, recv_ref, src_out, lnd_out, token):
        me = _my_slot()
        for k in range(1, N_DEV):
            peer, slot = _peer(k)
            pltpu.make_async_remote_copy(
                src_ref=src.at[slot], dst_ref=lnd.at[me], send_sem=send_ref.at[k - 1], recv_sem=recv_ref.at[k - 1],
                device_id=peer, device_id_type=MESH).start()
        token[...] = jnp.zeros_like(token)

    res = pl.pallas_call(
        body, name=name, in_specs=[_HBM, _HBM],
        out_specs=[_SEM, _SEM, _HBM, _HBM, pl.BlockSpec(memory_space=pltpu.VMEM)],
        out_shape=[pltpu.SemaphoreType.DMA((N_DEV - 1,))] * 2 + [pltpu.HBM(parts.shape, parts.dtype)] * 2 + [SDS((8, 128), F32)],
        input_output_aliases={0: 2, 1: 3}, compiler_params=pltpu.CompilerParams(has_side_effects=_EFFECT),
    )(pltpu.with_memory_space_constraint(parts, pltpu.HBM), pltpu.with_memory_space_constraint(land, pltpu.HBM))
    return tuple(res[:4]), res[4]


def _scatter_wait(name, started, after):
    send_sems, recv_sems, parts, land = started

    def body(src_ref, land_ref, send_ref, recv_ref, after_ref, src_out, land_out):
        me = _my_slot()
        for k in range(1, N_DEV):
            peer, slot = _peer(k)
            copy = pltpu.make_async_remote_copy(
                src_ref=src_ref.at[me], dst_ref=land_ref.at[slot], send_sem=send_ref.at[k - 1],
                recv_sem=recv_ref.at[k - 1], device_id=peer, device_id_type=MESH)
            copy.wait_send()
            copy.wait_recv()

    return pl.pallas_call(
        body, name=name, in_specs=[_HBM, _HBM, _SEM, _SEM, pl.BlockSpec(memory_space=pl.ANY)], out_specs=[_HBM, _HBM],
        out_shape=[pltpu.HBM(parts.shape, parts.dtype), pltpu.HBM(land.shape, land.dtype)], input_output_aliases={0: 0, 1: 1},
        compiler_params=pltpu.CompilerParams(has_side_effects=_EFFECT),
    )(parts, land, send_sems, recv_sems, after)


def _adamw(name, landed, own, w, m, v):
    r, c = w.shape
    tr = _tile(r, max(16, (12 * 2 ** 20) // (46 * c)), 16)
    bc1 = 1.0 / (1.0 - ADAM_B1 ** ADAM_STEP)
    bc2 = 1.0 / (1.0 - ADAM_B2 ** ADAM_STEP)

    def body(me_ref, p_ref, own_ref, w_ref, m_ref, v_ref, g_ref, d_ref, nm_ref, nv_ref):
        me = me_ref[0]
        g = jnp.zeros(w_ref.shape, F32)
        for s in range(N_DEV):
            g = g + jnp.where(me == s, own_ref[...], p_ref[s]).astype(F32)
        nm = ADAM_B1 * m_ref[...] + (1.0 - ADAM_B1) * g
        nv = ADAM_B2 * v_ref[...] + (1.0 - ADAM_B2) * (g * g)
        g_ref[...] = g
        nm_ref[...] = nm
        nv_ref[...] = nv
        d_ref[...] = -ADAM_LR * ((nm * bc1) / (jnp.sqrt(nv * bc2) + ADAM_EPS) + ADAM_WD * w_ref[...])

    blk = pl.BlockSpec((tr, c), lambda i, me: (i, 0))
    return pl.pallas_call(
        body, name=name, out_shape=[SDS((r, c), F32)] * 4, compiler_params=_params(1),
        grid_spec=pltpu.PrefetchScalarGridSpec(
            num_scalar_prefetch=1, grid=(r // tr,),
            in_specs=[pl.BlockSpec((N_DEV, tr, c), lambda i, me: (0, i, 0)), pl.BlockSpec((None, tr, c), lambda i, me: (me[0], i, 0)),
                      blk, blk, blk],
            out_specs=[blk] * 4),
    )(_slot_operand(), landed, own, w, m, v)


def _ffn_fwd(name, h, norm, fetch, prefetch, landed, ahead=None):
    n, n_t = _rms_fwd(name + "_norm", h, norm)
    wg = fetch(name + "_w_gate", n)
    gate = _ffn_proj(name + "_gate", n, wg)
    sent = prefetch(name + "_w_down", gate) if landed else None
    wu = fetch(name + "_w_up", gate)
    up, act = _ffn_proj(name + "_up", n, wu, gate, after=sent)
    sent = prefetch(ahead, act) if ahead else None
    wd = fetch(name + "_w_down", act)
    return _ffn_down(act, wd, h, 0.5, after=sent), (n_t, gate, up, act, wg, wu, wd)


def _ffn_bwd(name, h, norm, saved, dh, dy_b, scale_out, emit):
    n_t, gate, up, act, wg, wu, wd = saved
    sent = emit(name + "_w_down", _wgrad_rows(name + "_dwd", act, dy_b))
    d_gate, d_up = _ffn_bwd_act(dy_b, wd, gate, up, after=sent)
    sent = emit(name + "_w_gate", _wgrad_cols(name + "_dwg", n_t, d_gate, after=sent))
    sent = emit(name + "_w_up", _wgrad_cols(name + "_dwu", n_t, d_up, after=sent))
    dn = _dgrad_cols(name + "_dn", (d_gate, d_up), (wg, wu), after=sent)
    return _rms_bwd(name + "_norm_bwd", h, dn, norm, dh, scale_out)


def _local_step(x, target, norms, small, fetch, prefetch, emit, *, batch, seq):
    n1w, nmw, n2w, nfw = norms
    a_log, dt_bias, dn_norm = small
    t, d = x.shape

    h1, saved1 = _ffn_fwd("ffn1", x, n1w, fetch, prefetch, False)
    nm, nm_t = _rms_fwd("mix_norm", h1, nmw)
    w_in = fetch("w_in", nm)
    p = w_in.shape[2]
    proj = _in_proj(nm, w_in)
    conv_all = fetch("conv_w", proj)
    proj = jnp.swapaxes(proj, 0, 1).reshape(t, N_DEV * p)
    conv_w = jnp.swapaxes(conv_all, 0, 1).reshape(CONV_WIDTH, N_DEV * conv_all.shape[2])
    dd = conv_w.shape[1] // 3
    da = (N_DEV * p - 4 * dd - 2 * (dd // HEAD)) // 3
    qkv = proj[:, :3 * da].astype(BF16)
    xd = proj[:, 3 * da:3 * da + 3 * dd]
    z = proj[:, 3 * da + 3 * dd:3 * da + 4 * dd]
    dbda = proj[:, 3 * da + 4 * dd:]
    attn, lse = _attn_fwd(qkv, batch=batch, seq=seq)
    yd = _conv_silu_fwd(xd, conv_w, batch=batch, seq=seq)
    dn_out, dn_states = _dn_fwd(yd, z, dbda, a_log, dt_bias, dn_norm, batch=batch, seq=seq)
    cat = jnp.concatenate([attn, dn_out], axis=1).astype(BF16)
    w_out = fetch("w_out", cat)
    w_out2 = w_out.reshape(da + dd, d)
    sent = prefetch("ffn2_w_up", prefetch("ffn2_w_gate", cat))
    h2 = _out_proj(cat, w_out2, h1, after=sent)
    h3, saved2 = _ffn_fwd("ffn2", h2, n2w, fetch, prefetch, True)

    loss, dh3, dh3_b, g_nf = _loss_head(h3, nfw, target, 0.5)
    dh2, dh2_b, g_n2 = _ffn_bwd("ffn2", h2, n2w, saved2, dh3, dh3_b, 1.0, emit)

    sent = emit("w_out", _wgrad_full("dw_out", cat, dh2_b).reshape(w_out.shape))
    dcat = _dgrad_full("d_cat", dh2_b, w_out2, after=sent)
    d_attn, d_dn = dcat[:, :da], dcat[:, da:]
    dq, dk, dv = _attn_bwd(qkv, attn, lse, d_attn, batch=batch, seq=seq)
    gq, gk, gv, gz, g_dbda, g_alog, g_dtb, g_dnn = _dn_bwd(yd, z, dbda, a_log, dt_bias, dn_norm, dn_states, d_dn, batch=batch, seq=seq)
    d_xd, g_conv = _conv_silu_bwd(xd, conv_w, jnp.concatenate([gq, gk, gv], axis=1), batch=batch, seq=seq)
    dproj = jnp.concatenate([dq, dk, dv, d_xd, gz, g_dbda], axis=1).astype(BF16)
    dproj = jnp.swapaxes(dproj.reshape(t, N_DEV, p), 0, 1)
    sent = emit("w_in", _wgrad_cols("dw_in", nm_t, dproj))
    dnm = _dgrad_cols("d_mix_in", (dproj,), (w_in,), after=sent)
    dh1, dh1_b, g_nm = _rms_bwd("mix_norm_bwd", h1, dnm, nmw, dh2, 0.5)

    dx, _, g_n1 = _ffn_bwd("ffn1", x, n1w, saved1, dh1, dh1_b, 1.0, emit)
    return loss, dx, (g_n1, g_nm, g_n2, g_nf), (g_alog, g_dtb, g_dnn), g_conv


def _pack_rows(vectors):
    rows, offsets, r = [], [], 0
    for vec in vectors:
        n = -(-vec.size // 128)
        rows.append(jnp.pad(vec.reshape(-1), (0, n * 128 - vec.size)).reshape(n, 128))
        offsets.append((r, vec.size, vec.shape))
        r += n
    pad = -r % 8
    if pad:
        rows.append(jnp.zeros((pad, 128), F32))
    return jnp.concatenate(rows, axis=0), offsets


def _unpack_rows(packed, offsets):
    return [packed[r:r + -(-size // 128)].reshape(-1)[:size].reshape(shape) for r, size, shape in offsets]


def kernel(x, ffn1_norm, ffn1_w_gate, ffn1_w_up, ffn1_w_down, mix_norm, w_in, conv_w, a_log, dt_bias, dn_norm, w_out, ffn2_norm, ffn2_w_gate, ffn2_w_up, ffn2_w_down, final_norm, loss_target, m_ffn1_norm, m_ffn1_w_gate, m_ffn1_w_up, m_ffn1_w_down, m_mix_norm, m_w_in, m_conv_w, m_a_log, m_dt_bias, m_dn_norm, m_w_out, m_ffn2_norm, m_ffn2_w_gate, m_ffn2_w_up, m_ffn2_w_down, m_final_norm, v_ffn1_norm, v_ffn1_w_gate, v_ffn1_w_up, v_ffn1_w_down, v_mix_norm, v_w_in, v_conv_w, v_a_log, v_dt_bias, v_dn_norm, v_w_out, v_ffn2_norm, v_ffn2_w_gate, v_ffn2_w_up, v_ffn2_w_down, v_final_norm):
    batch, seq, d = x.shape
    t = batch * seq
    big = dict(ffn1_w_gate=(ffn1_w_gate, m_ffn1_w_gate, v_ffn1_w_gate), ffn1_w_up=(ffn1_w_up, m_ffn1_w_up, v_ffn1_w_up),
               ffn1_w_down=(ffn1_w_down, m_ffn1_w_down, v_ffn1_w_down), w_in=(w_in, m_w_in, v_w_in),
               w_out=(w_out, m_w_out, v_w_out), ffn2_w_gate=(ffn2_w_gate, m_ffn2_w_gate, v_ffn2_w_gate),
               ffn2_w_up=(ffn2_w_up, m_ffn2_w_up, v_ffn2_w_up), ffn2_w_down=(ffn2_w_down, m_ffn2_w_down, v_ffn2_w_down))
    rep = dict(ffn1_norm=(ffn1_norm, m_ffn1_norm, v_ffn1_norm), mix_norm=(mix_norm, m_mix_norm, v_mix_norm),
               ffn2_norm=(ffn2_norm, m_ffn2_norm, v_ffn2_norm), final_norm=(final_norm, m_final_norm, v_final_norm),
               a_log=(a_log, m_a_log, v_a_log), dt_bias=(dt_bias, m_dt_bias, v_dt_bias), dn_norm=(dn_norm, m_dn_norm, v_dn_norm))

    lands = {"conv_w": _cast_place("place_conv_w", conv_w, F32)}
    lands.update({name: _cast_place("place_" + name, w, BF16) for name, (w, _, _) in big.items()})
    started, token = _gather_start("gather_start", list(lands.values()))
    gathering = dict(zip(lands, started))
    gathered, scattering = {}, {}

    forwarding = {}

    def prefetch(name, after):
        if name not in forwarding:
            land = _gather_arrived("gather_arrived_" + name, gathering[name], after)
            forwarding[name] = _gather_forward("gather_forward_" + name, land)
        return forwarding[name][1]

    def fetch(name, after):
        if name not in gathered:
            prefetch(name, after)
            gathered[name] = _gather_wait("gather_wait_" + name, gathering[name], forwarding[name][0], after)
        return gathered[name]

    def emit(name, grad):
        scattering[name], sent = _scatter_start("scatter_start_" + name, grad)
        return sent

    row = lambda a: a.reshape(1, -1)
    norms = [row(rep[n][0]) for n in ("ffn1_norm", "mix_norm", "ffn2_norm", "final_norm")]
    norms[0] = norms[0] + token[0, 0]
    loss, dx, g_norms, g_small, g_conv = _local_step(
        x.reshape(t, d), loss_target.reshape(t, d), norms, [row(rep[n][0]) for n in ("a_log", "dt_bias", "dn_norm")],
        fetch, prefetch, emit, batch=batch, seq=seq)

    out = {"grad_x": dx.reshape(x.shape)}
    after = dx
    for name in scattering:
        w, m, v = big[name]
        own, landed = _scatter_wait("scatter_wait_" + name, scattering[name], after)
        out["grad_" + name], out["delta_" + name], out["new_m_" + name], out["new_v_" + name] = _adamw("adamw_" + name, landed, own, w, m, v)
        after = out["grad_" + name]
    conv_parts = jnp.swapaxes(g_conv.reshape(CONV_WIDTH, N_DEV, conv_w.shape[1]), 0, 1)
    parts = _exchange_slices("scatter_conv_w", conv_parts, after)
    out["grad_conv_w"], out["delta_conv_w"], out["new_m_conv_w"], out["new_v_conv_w"] = _adamw("adamw_conv_w", parts, parts, conv_w, m_conv_w, v_conv_w)

    rep_names = list(rep)
    g_rep = [*g_norms, *g_small]
    packed_g, offsets = _pack_rows([*g_rep, loss[:, :1]])
    packed = [_pack_rows([*[rep[n][i] for n in rep_names], jnp.zeros((1, 1), F32)])[0] for i in range(3)]
    parts = _all_gather("gather_small_grads", packed_g, out["grad_conv_w"])
    res = [_unpack_rows(a, offsets) for a in _adamw("adamw_small", parts, parts, *packed)]
    for i, name in enumerate(rep_names):
        shape = rep[name][0].shape
        out["grad_" + name], out["delta_" + name], out["new_m_" + name], out["new_v_" + name] = (r[i].reshape(shape) for r in res)
    out["loss"] = res[0][-1].reshape(())

    order = ["ffn1_norm", "ffn1_w_gate", "ffn1_w_up", "ffn1_w_down", "mix_norm", "w_in", "conv_w", "a_log", "dt_bias", "dn_norm",
             "w_out", "ffn2_norm", "ffn2_w_gate", "ffn2_w_up", "ffn2_w_down", "final_norm"]
    return (out["loss"], out["grad_x"], *[out["grad_" + n] for n in order], *[out["delta_" + n] for n in order],
            *[out["new_m_" + n] for n in order], *[out["new_v_" + n] for n in order])
```

```python
import functools
import math

import jax
import jax.numpy as jnp
from jax import lax
from jax.experimental import pallas as pl
from jax.experimental.pallas import tpu as pltpu

F32 = jnp.float32
BF16 = jnp.bfloat16
N_DEV = 8
HEAD = 128
CHUNK = 64
CHUNK_BITS = 6
DN_ROWS = 256
CONV_WIDTH = 4
EPS = 1e-6
DILATED_CONFIGS = ((128, 1), (512, 4), (2048, 16))
ATTN_BLOCK = 256
NEG = -1e30
ADAM_LR, ADAM_B1, ADAM_B2, ADAM_EPS, ADAM_WD, ADAM_STEP = 0.001, 0.9, 0.999, 1e-08, 0.01, 10
HI = lax.Precision.HIGHEST
MESH = pl.DeviceIdType.MESH
SDS = jax.ShapeDtypeStruct


def _tile(n, pref, align):
    t = (min(n, pref) // align) * align
    while t >= align:
        if n % t == 0:
            return t
        t -= align
    return n


def _params(n_axes, vmem_mb=48):
    return pltpu.CompilerParams(dimension_semantics=("arbitrary",) * n_axes, vmem_limit_bytes=vmem_mb * 2 ** 20)


def _sigmoid(x):
    return 1.0 / (1.0 + jnp.exp(-x))


def _silu(x):
    return x * _sigmoid(x)


def _softplus(x):
    return jnp.maximum(x, 0.0) + jnp.log(1.0 + jnp.exp(-jnp.abs(x)))


_DIMS = {"nn": (((1,), (0,)), ((), ())), "nt": (((1,), (1,)), ((), ())), "tn": (((0,), (0,)), ((), ()))}


def _mm_call(name, grid, mode, pairs, operands, in_specs, out_shape, out_specs, acc_shapes, epilogue, vmem_mb=48, after=None):
    dims = _DIMS[mode]
    if after is not None:
        operands, in_specs = (*operands, after), [*in_specs, pl.BlockSpec(memory_space=pl.ANY)]
    n_in, n_out = len(operands), len(out_shape)
    nk = grid[-1]

    def whole(*refs):
        ins, outs = refs[:n_in], refs[n_in:]
        sums = {}
        for a, b, c in pairs:
            prod = lax.dot_general(ins[a][...], ins[b][...], dims, preferred_element_type=F32)
            sums[c] = prod if c not in sums else sums[c] + prod
        epilogue(ins, outs, [sums[c] for c in sorted(sums)])

    if acc_shapes is None:
        return pl.pallas_call(
            whole, name=name, grid=grid, in_specs=in_specs, out_specs=out_specs, out_shape=out_shape,
            compiler_params=_params(len(grid), vmem_mb))(*operands)

    def body(*refs):
        ins, outs, accs = refs[:n_in], refs[n_in:n_in + n_out], refs[n_in + n_out:]
        k = pl.program_id(len(grid) - 1)

        @pl.when(k == 0)
        def _():
            for acc in accs:
                acc[...] = jnp.zeros_like(acc)

        sums = {}
        for a, b, c in pairs:
            prod = lax.dot_general(ins[a][...], ins[b][...], dims, preferred_element_type=F32)
            sums[c] = prod if c not in sums else sums[c] + prod
        for c, total in sums.items():
            accs[c][...] += total

        @pl.when(k == nk - 1)
        def _():
            epilogue(ins, outs, [acc[...] for acc in accs])

    return pl.pallas_call(
        body, name=name, grid=grid, in_specs=in_specs, out_specs=out_specs, out_shape=out_shape,
        scratch_shapes=[pltpu.VMEM(s, F32) for s in acc_shapes], compiler_params=_params(len(grid), vmem_mb),
    )(*operands)


def _ffn_proj(name, n, w, gate=None, after=None):
    t, d = n.shape
    f = w.shape[2]
    tm = _tile(t, 256, 16)
    n_spec = pl.BlockSpec((tm, d), lambda s, m, k: (m, 0))
    w_spec = pl.BlockSpec((None, d, f), lambda s, m, k: (s, 0, 0))
    o_spec = pl.BlockSpec((None, tm, f), lambda s, m, k: (s, m, 0))
    o_shape = SDS((N_DEV, t, f), BF16)
    grid = (N_DEV, t // tm, 1)
    if gate is None:
        return _mm_call(name, grid, "nn", [(0, 1, 0)], (n, w), [n_spec, w_spec], [o_shape], [o_spec], None, _store_bf16,
                        after=after)[0]

    def up_out(ins, outs, accs):
        outs[0][...] = accs[0].astype(BF16)
        outs[1][...] = (_silu(ins[2][...].astype(F32)) * accs[0]).astype(BF16)

    return _mm_call(name, grid, "nn", [(0, 1, 0)], (n, w, gate), [n_spec, w_spec, o_spec], [o_shape] * 2, [o_spec] * 2,
                    None, up_out, after=after)


def _ffn_down(act, wd, resid, scale, after=None):
    _, t, f = act.shape
    d = wd.shape[2]
    tm, tn = _tile(t, 512, 16), _tile(d, 1024, 128)

    def epilogue(ins, outs, accs):
        outs[0][...] = ins[2][...] + scale * accs[0]

    rc = pl.BlockSpec((tm, tn), lambda m, n, s: (m, n))
    return _mm_call(
        "ffn_down", (t // tm, d // tn, N_DEV), "nn", [(0, 1, 0)], (act, wd, resid),
        [pl.BlockSpec((None, tm, f), lambda m, n, s: (s, m, 0)), pl.BlockSpec((None, f, tn), lambda m, n, s: (s, 0, n)), rc],
        [SDS((t, d), F32)], [rc], [(tm, tn)], epilogue, after=after)[0]


def _in_proj(n, w):
    t, d = n.shape
    p = w.shape[2]
    tm = _tile(t, 256, 16)
    return _mm_call(
        "in_proj", (N_DEV, t // tm, 1), "nn", [(0, 1, 0)], (n, w),
        [pl.BlockSpec((tm, d), lambda s, m, k: (m, 0)), pl.BlockSpec((None, d, p), lambda s, m, k: (s, 0, 0))],
        [SDS((N_DEV, t, p), F32)], [pl.BlockSpec((None, tm, p), lambda s, m, k: (s, m, 0))], None, _store_f32)[0]


def _out_proj(cat, w, resid, after=None):
    t, dm = cat.shape
    d = w.shape[1]
    tm, tn = _tile(t, 512, 16), _tile(d, 1024, 128)

    def epilogue(ins, outs, accs):
        outs[0][...] = ins[2][...] + accs[0]

    rc = pl.BlockSpec((tm, tn), lambda n, m, k: (m, n))
    return _mm_call(
        "out_proj", (d // tn, t // tm, 1), "nn", [(0, 1, 0)], (cat, w, resid),
        [pl.BlockSpec((tm, dm), lambda n, m, k: (m, 0)), pl.BlockSpec((dm, tn), lambda n, m, k: (0, n)), rc],
        [SDS((t, d), F32)], [rc], None, epilogue, after=after)[0]


def _ffn_bwd_act(dy, wd, gate, up, after=None):
    t, d = dy.shape
    f = wd.shape[1]
    tm = _tile(t, 256, 16)

    def epilogue(ins, outs, accs):
        g, u = ins[2][...].astype(F32), ins[3][...].astype(F32)
        sg = _sigmoid(g)
        outs[0][...] = (accs[0] * u * sg * (1.0 + g * (1.0 - sg))).astype(BF16)
        outs[1][...] = (accs[0] * g * sg).astype(BF16)

    o_spec = pl.BlockSpec((None, tm, f), lambda s, m, k: (s, m, 0))
    return _mm_call(
        "ffn_bwd_act", (N_DEV, t // tm, 1), "nt", [(0, 1, 0)], (dy, wd, gate, up),
        [pl.BlockSpec((tm, d), lambda s, m, k: (m, 0)), pl.BlockSpec((None, f, d), lambda s, m, k: (s, 0, 0)), o_spec, o_spec],
        [SDS((N_DEV, t, f), BF16)] * 2, [o_spec] * 2, None, epilogue, after=after)


def _store_bf16(ins, outs, accs):
    outs[0][...] = accs[0].astype(BF16)


def _store_f32(ins, outs, accs):
    outs[0][...] = accs[0]


def _wgrad_cols(name, a_t, b, after=None):
    m, t = a_t.shape
    n = b.shape[2]
    tm, tk = _tile(m, 512 if n <= 1408 else 256, 128), t
    return _mm_call(
        name, (N_DEV, m // tm, t // tk), "nn", [(0, 1, 0)], (a_t, b),
        [pl.BlockSpec((tm, tk), lambda s, i, k: (i, k)), pl.BlockSpec((None, tk, n), lambda s, i, k: (s, k, 0))],
        [SDS((N_DEV, m, n), BF16)], [pl.BlockSpec((None, tm, n), lambda s, i, k: (s, i, 0))], None, _store_bf16,
        after=after)[0]


def _wgrad_rows(name, a, b, after=None):
    _, t, m = a.shape
    n = b.shape[1]
    tn, tk = _tile(n, 512, 128), t
    return _mm_call(
        name, (N_DEV, n // tn, t // tk), "tn", [(0, 1, 0)], (a, b),
        [pl.BlockSpec((None, tk, m), lambda s, j, k: (s, k, 0)), pl.BlockSpec((tk, tn), lambda s, j, k: (k, j))],
        [SDS((N_DEV, m, n), BF16)], [pl.BlockSpec((None, m, tn), lambda s, j, k: (s, 0, j))], None, _store_bf16,
        after=after)[0]


def _wgrad_full(name, a, b, after=None):
    t, m = a.shape
    n = b.shape[1]
    tm, tn, tk = _tile(m, 512, 128), _tile(n, 1024, 128), t
    return _mm_call(
        name, (m // tm, n // tn, t // tk), "tn", [(0, 1, 0)], (a, b),
        [pl.BlockSpec((tk, tm), lambda i, j, k: (k, i)), pl.BlockSpec((tk, tn), lambda i, j, k: (k, j))],
        [SDS((m, n), BF16)], [pl.BlockSpec((tm, tn), lambda i, j, k: (i, j))], None, _store_bf16, after=after)[0]


def _dgrad_cols(name, grads, weights, after=None):
    _, t, n = grads[0].shape
    m = weights[0].shape[1]
    tm, tn = _tile(t, 512, 16), _tile(m, 1024, 128)
    k = len(grads)
    return _mm_call(
        name, (t // tm, m // tn, N_DEV), "nt", [(i, k + i, 0) for i in range(k)], (*grads, *weights),
        [pl.BlockSpec((None, tm, n), lambda i, j, s: (s, i, 0))] * k + [pl.BlockSpec((None, tn, n), lambda i, j, s: (s, j, 0))] * k,
        [SDS((t, m), F32)], [pl.BlockSpec((tm, tn), lambda i, j, s: (i, j))], [(tm, tn)], _store_f32, after=after)[0]


def _dgrad_full(name, g, w, after=None):
    t, n = g.shape
    m = w.shape[0]
    tm, tn, tk = _tile(t, 512, 16), _tile(m, 1024, 128), n
    return _mm_call(
        name, (m // tn, t // tm, n // tk), "nt", [(0, 1, 0)], (g, w),
        [pl.BlockSpec((tm, tk), lambda j, i, k: (i, k)), pl.BlockSpec((tn, tk), lambda j, i, k: (j, k))],
        [SDS((t, m), F32)], [pl.BlockSpec((tm, tn), lambda j, i, k: (i, j))], None, _store_f32, after=after)[0]


def _rms_fwd(name, h, w):
    t, d = h.shape
    tm = _tile(t, 256, 128)

    def body(h_ref, w_ref, o_ref, ot_ref):
        x = h_ref[...]
        n = (x * lax.rsqrt(jnp.mean(x * x, axis=1, keepdims=True) + EPS) * w_ref[...]).astype(BF16)
        o_ref[...] = n
        ot_ref[...] = n.T

    row = pl.BlockSpec((tm, d), lambda i: (i, 0))
    return pl.pallas_call(
        body, name=name, grid=(t // tm,), in_specs=[row, pl.BlockSpec((1, d), lambda i: (0, 0))],
        out_specs=[row, pl.BlockSpec((d, tm), lambda i: (0, i))], out_shape=[SDS((t, d), BF16), SDS((d, t), BF16)],
        compiler_params=_params(1))(h, w)


def _rms_bwd(name, h, dn, w, dres, scale):
    t, d = h.shape
    tm = _tile(t, 128, 16)

    def body(h_ref, dn_ref, w_ref, dres_ref, dh_ref, dhb_ref, dw_ref):
        @pl.when(pl.program_id(0) == 0)
        def _():
            dw_ref[...] = jnp.zeros_like(dw_ref)

        x = h_ref[...]
        rstd = lax.rsqrt(jnp.mean(x * x, axis=1, keepdims=True) + EPS)
        nhat = x * rstd
        g = dn_ref[...]
        gw = g * w_ref[...]
        dh = dres_ref[...] + rstd * (gw - nhat * jnp.mean(gw * nhat, axis=1, keepdims=True))
        dh_ref[...] = dh
        dhb_ref[...] = (scale * dh).astype(BF16)
        dw_ref[...] += jnp.sum(g * nhat, axis=0, keepdims=True)

    row = pl.BlockSpec((tm, d), lambda i: (i, 0))
    vec = pl.BlockSpec((1, d), lambda i: (0, 0))
    return pl.pallas_call(
        body, name=name, grid=(t // tm,), in_specs=[row, row, vec, row], out_specs=[row, row, vec],
        out_shape=[SDS((t, d), F32), SDS((t, d), BF16), SDS((1, d), F32)], compiler_params=_params(1))(h, dn, w, dres)


def _loss_head(h, w, target, scale):
    t, d = h.shape
    tm = _tile(t, 128, 16)

    def body(h_ref, w_ref, tg_ref, loss_ref, dh_ref, dhb_ref, dw_ref):
        @pl.when(pl.program_id(0) == 0)
        def _():
            dw_ref[...] = jnp.zeros_like(dw_ref)
            loss_ref[...] = jnp.zeros_like(loss_ref)

        x = h_ref[...]
        rstd = lax.rsqrt(jnp.mean(x * x, axis=1, keepdims=True) + EPS)
        nhat = x * rstd
        wv = w_ref[...]
        err = nhat * wv - tg_ref[...]
        loss_ref[...] += 0.5 * jnp.sum(jnp.mean(err * err, axis=1, keepdims=True), axis=0, keepdims=True)
        g = err * (1.0 / d)
        gw = g * wv
        dh = rstd * (gw - nhat * jnp.mean(gw * nhat, axis=1, keepdims=True))
        dh_ref[...] = dh
        dhb_ref[...] = (scale * dh).astype(BF16)
        dw_ref[...] += jnp.sum(g * nhat, axis=0, keepdims=True)

    row = pl.BlockSpec((tm, d), lambda i: (i, 0))
    vec = pl.BlockSpec((1, d), lambda i: (0, 0))
    return pl.pallas_call(
        body, name="loss_head", grid=(t // tm,), in_specs=[row, vec, row],
        out_specs=[pl.BlockSpec((1, 128), lambda i: (0, 0)), row, row, vec],
        out_shape=[SDS((1, 128), F32), SDS((t, d), F32), SDS((t, d), BF16), SDS((1, d), F32)],
        compiler_params=_params(1))(h, w, target)


def _attn_bias(delta, blk):
    dist = (lax.broadcasted_iota(jnp.int32, (blk, blk), 0) - lax.broadcasted_iota(jnp.int32, (blk, blk), 1)
            + delta * blk)
    count = jnp.zeros((blk, blk), F32)
    for window, dil in DILATED_CONFIGS:
        assert dil & (dil - 1) == 0
        seen = (dist >= 0) & (dist <= window) & ((dist & (dil - 1)) == 0)
        count = count + jnp.where(seen, 1.0, 0.0)
    return jnp.where(count > 0.0, jnp.log(jnp.maximum(count, 1.0)), NEG)


def _fill_bias_table(table, blk):
    @pl.when((pl.program_id(0) == 0) & (pl.program_id(1) == 0))
    def _():
        for delta in range(table.shape[0]):
            table[delta] = _attn_bias(delta, blk)


def _attn_fwd(qkv, *, batch, seq):
    t, da3 = qkv.shape
    da = da3 // 3
    n_heads = da // HEAD
    blk = _tile(seq, ATTN_BLOCK, 16)
    nq = seq // blk
    sm_scale = HEAD ** -0.5

    def body(q_ref, k_ref, v_ref, o_ref, lse_ref, bias):
        _fill_bias_table(bias, blk)

        def q_step(qi, _):
            rows = pl.ds(pl.multiple_of(qi * blk, blk), blk)
            q = q_ref[rows, :]

            def kv_step(ki, carry):
                m, l, acc = carry
                cols = pl.ds(pl.multiple_of(ki * blk, blk), blk)
                s = lax.dot_general(q, k_ref[cols, :], _DIMS["nt"], preferred_element_type=F32) * sm_scale
                s = s + bias[qi - ki]
                m_new = jnp.maximum(m, jnp.max(s, axis=1, keepdims=True))
                alpha = jnp.exp(m - m_new)
                p = jnp.exp(s - m_new)
                l = alpha * l + jnp.sum(p, axis=1, keepdims=True)
                acc = alpha * acc + jnp.dot(p.astype(BF16), v_ref[cols, :], preferred_element_type=F32)
                return m_new, l, acc

            m, l, acc = lax.fori_loop(0, qi + 1, kv_step, (jnp.full((blk, 1), NEG, F32), jnp.zeros((blk, 1), F32),
                                                           jnp.zeros((blk, HEAD), F32)))
            o_ref[rows, :] = acc / l
            lse_ref[rows, :] = jnp.broadcast_to(m + jnp.log(l), (blk, HEAD))
            return 0

        lax.fori_loop(0, nq, q_step, 0)

    col = lambda off: pl.BlockSpec((seq, HEAD), lambda b, h: (b, off + h))
    return pl.pallas_call(
        body, name="attn_fwd", grid=(batch, n_heads), in_specs=[col(0), col(n_heads), col(2 * n_heads)],
        out_specs=[col(0), col(0)], out_shape=[SDS((t, da), F32), SDS((t, da), F32)],
        scratch_shapes=[pltpu.VMEM((nq, blk, blk), F32)], compiler_params=_params(2),
    )(qkv, qkv, qkv)


def _attn_bwd(qkv, out, lse, d_out, *, batch, seq):
    t, da = out.shape
    n_heads = da // HEAD
    blk = _tile(seq, ATTN_BLOCK, 16)
    nq = seq // blk
    sm_scale = HEAD ** -0.5

    def body(q_ref, k_ref, v_ref, o_ref, lse_ref, do_ref, dq_ref, dk_ref, dv_ref, bias):
        _fill_bias_table(bias, blk)
        dk_ref[...] = jnp.zeros_like(dk_ref)
        dv_ref[...] = jnp.zeros_like(dv_ref)

        def q_step(qi, _):
            rows = pl.ds(pl.multiple_of(qi * blk, blk), blk)
            q = q_ref[rows, :]
            do = do_ref[rows, :]
            do_b = do.astype(BF16)
            lse_q = lse_ref[rows, :][:, :1]
            delta = jnp.sum(do * o_ref[rows, :], axis=1, keepdims=True)

            def kv_step(ki, dq):
                cols = pl.ds(pl.multiple_of(ki * blk, blk), blk)
                k = k_ref[cols, :]
                s = lax.dot_general(q, k, _DIMS["nt"], preferred_element_type=F32) * sm_scale
                p = jnp.exp(s + bias[qi - ki] - lse_q)
                dp = lax.dot_general(do_b, v_ref[cols, :], _DIMS["nt"], preferred_element_type=F32)
                ds = (p * (dp - delta) * sm_scale).astype(BF16)
                dv_ref[cols, :] += lax.dot_general(p.astype(BF16), do_b, _DIMS["tn"], preferred_element_type=F32)
                dk_ref[cols, :] += lax.dot_general(ds, q, _DIMS["tn"], preferred_element_type=F32)
                return dq + jnp.dot(ds, k, preferred_element_type=F32)

            dq_ref[rows, :] = lax.fori_loop(0, qi + 1, kv_step, jnp.zeros((blk, HEAD), F32))
            return 0

        lax.fori_loop(0, nq, q_step, 0)

    col = lambda off: pl.BlockSpec((seq, HEAD), lambda b, h: (b, off + h))
    return pl.pallas_call(
        body, name="attn_bwd", grid=(batch, n_heads),
        in_specs=[col(0), col(n_heads), col(2 * n_heads), col(0), col(0), col(0)], out_specs=[col(0)] * 3,
        out_shape=[SDS((t, da), F32)] * 3, scratch_shapes=[pltpu.VMEM((nq, blk, blk), F32)], compiler_params=_params(2),
    )(qkv, qkv, qkv, out, lse, d_out)


def _shift_down(x, k, row):
    return x if k == 0 else jnp.where(row >= k, pltpu.roll(x, k, axis=0), 0.0)


def _shift_up(x, k, row):
    n = x.shape[0]
    return x if k == 0 else jnp.where(row < n - k, pltpu.roll(x, n - k, axis=0), 0.0)


def _conv_silu_fwd(x, w, *, batch, seq):
    t, c = x.shape

    def body(x_ref, w_ref, o_ref):
        xv = x_ref[...]
        row = lax.broadcasted_iota(jnp.int32, xv.shape, 0)
        acc = jnp.zeros_like(xv)
        for i in range(CONV_WIDTH):
            acc = acc + w_ref[i:i + 1, :] * _shift_down(xv, CONV_WIDTH - 1 - i, row)
        o_ref[...] = _silu(acc)

    blk = pl.BlockSpec((seq, HEAD), lambda j, b: (b, j))
    return pl.pallas_call(
        body, name="conv_silu_fwd", grid=(c // HEAD, batch), in_specs=[blk, pl.BlockSpec((CONV_WIDTH, HEAD), lambda j, b: (0, j))],
        out_specs=blk, out_shape=SDS((t, c), F32), compiler_params=_params(2))(x, w)


def _conv_silu_bwd(x, w, dy, *, batch, seq):
    t, c = x.shape

    def body(x_ref, w_ref, dy_ref, dx_ref, dw_ref):
        @pl.when(pl.program_id(1) == 0)
        def _():
            dw_ref[...] = jnp.zeros_like(dw_ref)

        xv = x_ref[...]
        row = lax.broadcasted_iota(jnp.int32, xv.shape, 0)
        shifted = [_shift_down(xv, CONV_WIDTH - 1 - i, row) for i in range(CONV_WIDTH)]
        acc = jnp.zeros_like(xv)
        for i in range(CONV_WIDTH):
            acc = acc + w_ref[i:i + 1, :] * shifted[i]
        sg = _sigmoid(acc)
        dc = dy_ref[...] * sg * (1.0 + acc * (1.0 - sg))
        dx = jnp.zeros_like(xv)
        for i in range(CONV_WIDTH):
            dx = dx + w_ref[i:i + 1, :] * _shift_up(dc, CONV_WIDTH - 1 - i, row)
            dw_ref[i:i + 1, :] += jnp.sum(dc * shifted[i], axis=0, keepdims=True)
        dx_ref[...] = dx

    blk = pl.BlockSpec((seq, HEAD), lambda j, b: (b, j))
    wblk = pl.BlockSpec((CONV_WIDTH, HEAD), lambda j, b: (0, j))
    return pl.pallas_call(
        body, name="conv_silu_bwd", grid=(c // HEAD, batch), in_specs=[blk, wblk, blk], out_specs=[blk, wblk],
        out_shape=[SDS((t, c), F32), SDS((CONV_WIDTH, c), F32)], compiler_params=_params(2))(x, w, dy)


def _dot(a, b, mode="nn"):
    return lax.dot_general(a.astype(BF16), b.astype(BF16), _DIMS[mode], preferred_element_type=F32)


def _dot3(a, b):
    a_hi, b_hi = a.astype(BF16), b.astype(BF16)
    a_lo, b_lo = (a - a_hi.astype(F32)).astype(BF16), (b - b_hi.astype(F32)).astype(BF16)
    pass_ = lambda x, y: jnp.dot(x, y, preferred_element_type=F32)
    return pass_(a_hi, b_hi) + pass_(a_hi, b_lo) + pass_(a_lo, b_hi)


@jax.custom_vjp
def _nilpotent_inverse(m):
    r = m.shape[0]
    x = jnp.where(lax.broadcasted_iota(jnp.int32, (r, r), 0) == lax.broadcasted_iota(jnp.int32, (r, r), 1), 1.0, 0.0) + m
    p = m
    for _ in range(CHUNK_BITS - 1):
        p = _dot3(p, p)
        x = x + _dot3(x, p)
    return x


def _nilpotent_inverse_fwd(m):
    x = _nilpotent_inverse(m)
    return x, x


def _nilpotent_inverse_bwd(x, g):
    return (_dot(x, _dot(g, x, "nt"), "tn"),)


_nilpotent_inverse.defvjp(_nilpotent_inverse_fwd, _nilpotent_inverse_bwd)


def _dot_nt(a, b):
    return _dot(a, b, "nt")


def _dot_tn(a, b):
    return _dot(a, b, "tn")


def _dn_chunk(head, n_heads, aq, ak, v, z, dbda, a_log, dt_bias, dn_norm, state):
    r = aq.shape[0]
    lane_g = lax.broadcasted_iota(jnp.int32, dbda.shape, 1)
    db = jnp.sum(jnp.where(lane_g == head, dbda, 0.0), axis=1, keepdims=True)
    da = jnp.sum(jnp.where(lane_g == head + n_heads, dbda, 0.0), axis=1, keepdims=True)
    lane_h = lax.broadcasted_iota(jnp.int32, a_log.shape, 1)
    al = jnp.sum(jnp.where(lane_h == head, a_log, 0.0), axis=1, keepdims=True)
    dtb = jnp.sum(jnp.where(lane_h == head, dt_bias, 0.0), axis=1, keepdims=True)
    beta = _sigmoid(db)
    g = -jnp.exp(al) * _softplus(da + dtb)
    q = aq * lax.rsqrt(jnp.sum(aq * aq, axis=1, keepdims=True) + EPS) * (HEAD ** -0.5)
    k = ak * lax.rsqrt(jnp.sum(ak * ak, axis=1, keepdims=True) + EPS)
    ri = lax.broadcasted_iota(jnp.int32, (r, r), 0)
    ci = lax.broadcasted_iota(jnp.int32, (r, r), 1)
    same = (ri >> CHUNK_BITS) == (ci >> CHUNK_BITS)
    incl = same & (ri >= ci)
    g_row = jnp.sum(jnp.where(ri == ci, g, 0.0), axis=0, keepdims=True)
    gc_col = jnp.sum(jnp.where(incl, g_row, 0.0), axis=1, keepdims=True)
    gc_row = jnp.sum(jnp.where(same & (ri <= ci), g, 0.0), axis=0, keepdims=True)
    g_all = jnp.sum(jnp.where(same, g_row, 0.0), axis=1, keepdims=True)
    decay = jnp.where(incl, jnp.exp(jnp.where(incl, gc_col - gc_row, 0.0)), 0.0)
    kb = k * beta
    m = -jnp.where(same & (ri > ci), _dot_nt(kb, k) * decay, 0.0)
    x = _nilpotent_inverse(m)
    egc = jnp.exp(gc_col)
    wu_g = _dot(x, jnp.concatenate([kb * egc, v * beta], axis=1))
    w_g, u_g = wu_g[:, :HEAD], wu_g[:, HEAD:]
    qk = _dot_nt(q, k) * decay
    q_dec = q * egc
    k_dec = k * jnp.exp(g_all - gc_col)
    carry = jnp.exp(g_all)
    v_new, o_state = [], []
    for c in range(r // CHUNK):
        rows = slice(c * CHUNK, (c + 1) * CHUNK)
        v_new.append(u_g[rows] - _dot(w_g[rows], state))
        o_state.append(_dot(q_dec[rows], state))
        state = state * carry[c * CHUNK:c * CHUNK + 1] + _dot_tn(k_dec[rows], v_new[-1])
    o = jnp.concatenate(o_state, axis=0) + _dot(qk, jnp.concatenate(v_new, axis=0))
    o = o * lax.rsqrt(jnp.mean(o * o, axis=1, keepdims=True) + EPS) * dn_norm
    return o * _silu(z), state


def _loop_by_two(n, step, init):
    if n % 2:
        return lax.fori_loop(0, n, step, init)
    return lax.fori_loop(0, n // 2, lambda i, carry: step(2 * i + 1, step(2 * i, carry)), init)


def _dn_state_spec(n_groups):
    return pl.BlockSpec((None, None, n_groups, HEAD, HEAD), lambda b, h: (b, h, 0, 0, 0))


def _dn_specs(seq, n_heads, small):
    col = lambda off: pl.BlockSpec((seq, HEAD), lambda b, h: (b, off + h))
    full = [pl.BlockSpec(a.shape, lambda b, h: (0, 0)) for a in small]
    gates = pl.BlockSpec((seq, 2 * n_heads), lambda b, h: (b, 0))
    return col, gates, full


def _dn_fwd(y, z, dbda, a_log, dt_bias, dn_norm, *, batch, seq):
    t, dd = z.shape
    n_heads = dd // HEAD
    grp = _tile(seq, DN_ROWS, CHUNK)
    n_chunks = seq // grp

    def body(q_ref, k_ref, v_ref, z_ref, g_ref, al_ref, dt_ref, nw_ref, o_ref, states):
        head = pl.program_id(1)
        al, dtb, nw = al_ref[...], dt_ref[...], nw_ref[...]

        def step(n, state):
            states[n] = state
            rows = pl.ds(pl.multiple_of(n * grp, grp), grp)
            out, state = _dn_chunk(head, n_heads, q_ref[rows, :], k_ref[rows, :], v_ref[rows, :], z_ref[rows, :],
                                   g_ref[rows, :], al, dtb, nw, state)
            o_ref[rows, :] = out
            return state

        _loop_by_two(n_chunks, step, jnp.zeros((HEAD, HEAD), F32))

    col, gates, full = _dn_specs(seq, n_heads, (a_log, dt_bias, dn_norm))
    return pl.pallas_call(
        body, name="dn_fwd", grid=(batch, n_heads), in_specs=[col(0), col(n_heads), col(2 * n_heads), col(0), gates, *full],
        out_specs=[col(0), _dn_state_spec(n_chunks)],
        out_shape=[SDS((t, dd), F32), SDS((batch, n_heads, n_chunks, HEAD, HEAD), F32)], compiler_params=_params(2),
    )(y, y, y, z, dbda, a_log, dt_bias, dn_norm)


def _dn_bwd(y, z, dbda, a_log, dt_bias, dn_norm, states, d_out, *, batch, seq):
    t, dd = z.shape
    n_heads = dd // HEAD
    grp = _tile(seq, DN_ROWS, CHUNK)
    n_chunks = seq // grp

    def body(q_ref, k_ref, v_ref, z_ref, g_ref, al_ref, dt_ref, nw_ref, states, do_ref,
             dq_ref, dk_ref, dv_ref, dz_ref, dg_ref, dal_ref, ddt_ref, dnw_ref):
        b, head = pl.program_id(0), pl.program_id(1)
        al, dtb, nw = al_ref[...], dt_ref[...], nw_ref[...]

        @pl.when((b == 0) & (head == 0))
        def _():
            dal_ref[...] = jnp.zeros_like(dal_ref)
            ddt_ref[...] = jnp.zeros_like(ddt_ref)
            dnw_ref[...] = jnp.zeros_like(dnw_ref)

        @pl.when(head == 0)
        def _():
            dg_ref[...] = jnp.zeros_like(dg_ref)

        def chunk(n):
            rows = pl.ds(pl.multiple_of(n * grp, grp), grp)
            return rows, (q_ref[rows, :], k_ref[rows, :], v_ref[rows, :], z_ref[rows, :], g_ref[rows, :], al, dtb, nw)

        def bwd_step(i, carry):
            d_state, d_al, d_dt, d_nw = carry
            n = n_chunks - 1 - i
            rows, args = chunk(n)
            _, vjp = jax.vjp(functools.partial(_dn_chunk, head, n_heads), *args, states[n])
            gq, gk, gv, gz, gg, gal, gdt, gnw, d_state = vjp((do_ref[rows, :], d_state))
            dq_ref[rows, :] = gq
            dk_ref[rows, :] = gk
            dv_ref[rows, :] = gv
            dz_ref[rows, :] = gz
            dg_ref[rows, :] += gg
            return d_state, d_al + gal, d_dt + gdt, d_nw + gnw

        zero = lambda a: jnp.zeros(a.shape, F32)
        _, d_al, d_dt, d_nw = _loop_by_two(
            n_chunks, bwd_step, (jnp.zeros((HEAD, HEAD), F32), zero(al), zero(dtb), zero(nw)))
        dal_ref[...] += d_al
        ddt_ref[...] += d_dt
        dnw_ref[...] += d_nw

    col, gates, full = _dn_specs(seq, n_heads, (a_log, dt_bias, dn_norm))
    return pl.pallas_call(
        body, name="dn_bwd", grid=(batch, n_heads),
        in_specs=[col(0), col(n_heads), col(2 * n_heads), col(0), gates, *full, _dn_state_spec(n_chunks), col(0)],
        out_specs=[col(0), col(0), col(0), col(0), gates, *full],
        out_shape=[SDS((t, dd), F32)] * 4 + [SDS(dbda.shape, F32), SDS(a_log.shape, F32), SDS(dt_bias.shape, F32),
                                             SDS(dn_norm.shape, F32)],
        compiler_params=_params(2),
    )(y, y, y, z, dbda, a_log, dt_bias, dn_norm, states, d_out)


def _my_slot():
    return 4 * lax.axis_index("x") + 2 * lax.axis_index("y") + lax.axis_index("c")


def _peer(k):
    x, y, c = lax.axis_index("x"), lax.axis_index("y"), lax.axis_index("c")
    return (x ^ (k >> 2), y ^ ((k >> 1) & 1), c ^ (k & 1)), (4 * x + 2 * y + c) ^ k


def _all_gather(name, block, after):
    def body(src, after_ref, dst, send_sems, recv_sems, local_sem):
        me = _my_slot()
        own = pltpu.make_async_copy(src, dst.at[me], local_sem)
        own.start()
        copies = []
        for k in range(1, N_DEV):
            peer, _ = _peer(k)
            copies.append(pltpu.make_async_remote_copy(
                src_ref=src, dst_ref=dst.at[me], send_sem=send_sems.at[k - 1], recv_sem=recv_sems.at[k - 1],
                device_id=peer, device_id_type=MESH))
            copies[-1].start()
        for k in range(1, N_DEV):
            peer, slot = _peer(k)
            pltpu.make_async_remote_copy(
                src_ref=src, dst_ref=dst.at[slot], send_sem=send_sems.at[k - 1], recv_sem=recv_sems.at[k - 1],
                device_id=peer, device_id_type=MESH).wait_recv()
        for cp in copies:
            cp.wait_send()
        own.wait()

    return pl.pallas_call(
        body, name=name, in_specs=[pl.BlockSpec(memory_space=pl.ANY)] * 2, out_specs=pl.BlockSpec(memory_space=pl.ANY),
        out_shape=SDS((N_DEV, *block.shape), block.dtype),
        scratch_shapes=[pltpu.SemaphoreType.DMA((N_DEV - 1,)), pltpu.SemaphoreType.DMA((N_DEV - 1,)), pltpu.SemaphoreType.DMA],
    )(block, after)


def _exchange_slices(name, parts, after):
    def body(src, after_ref, dst, send_sems, recv_sems, local_sem):
        me = _my_slot()
        own = pltpu.make_async_copy(src.at[me], dst.at[me], local_sem)
        own.start()
        copies = []
        for k in range(1, N_DEV):
            peer, slot = _peer(k)
            copies.append(pltpu.make_async_remote_copy(
                src_ref=src.at[slot], dst_ref=dst.at[me], send_sem=send_sems.at[k - 1], recv_sem=recv_sems.at[k - 1],
                device_id=peer, device_id_type=MESH))
            copies[-1].start()
        for k in range(1, N_DEV):
            peer, slot = _peer(k)
            pltpu.make_async_remote_copy(
                src_ref=src.at[me], dst_ref=dst.at[slot], send_sem=send_sems.at[k - 1], recv_sem=recv_sems.at[k - 1],
                device_id=peer, device_id_type=MESH).wait_recv()
        for cp in copies:
            cp.wait_send()
        own.wait()

    return pl.pallas_call(
        body, name=name, in_specs=[pl.BlockSpec(memory_space=pl.ANY)] * 2, out_specs=pl.BlockSpec(memory_space=pl.ANY),
        out_shape=SDS(parts.shape, parts.dtype),
        scratch_shapes=[pltpu.SemaphoreType.DMA((N_DEV - 1,)), pltpu.SemaphoreType.DMA((N_DEV - 1,)), pltpu.SemaphoreType.DMA],
    )(parts, after)


_HBM = pl.BlockSpec(memory_space=pltpu.HBM)
_SEM = pl.BlockSpec(memory_space=pltpu.SEMAPHORE)
_EFFECT = pltpu.SideEffectType.DATAFLOW_SIDE_EFFECTING


def _slot_operand():
    return _my_slot().astype(jnp.int32).reshape(1)


def _cast_place(name, block, dtype):
    r, c = block.shape
    tr = _tile(r, 512, 16)

    def body(me_ref, src_ref, dst_ref):
        dst_ref[...] = src_ref[...].astype(dtype)

    return pl.pallas_call(
        body, name=name, out_shape=SDS((N_DEV, r, c), dtype), compiler_params=_params(1),
        grid_spec=pltpu.PrefetchScalarGridSpec(
            num_scalar_prefetch=1, grid=(r // tr,), in_specs=[pl.BlockSpec((tr, c), lambda i, me: (i, 0))],
            out_specs=pl.BlockSpec((None, tr, c), lambda i, me: (me[0], i, 0))),
    )(_slot_operand(), block)


_SIBLING = 1
_SAME_CORE = (2, 4, 6)
_OTHER_CORE = (3, 5, 7)


def _gather_start(name, lands):
    n = len(lands)

    def body(*refs):
        lnds, outs = refs[:n], refs[n:]
        me = _my_slot()
        for i in range(n):
            for k in (*_SAME_CORE, _SIBLING):
                peer, _ = _peer(k)
                pltpu.make_async_remote_copy(
                    src_ref=lnds[i].at[me], dst_ref=lnds[i].at[me], send_sem=outs[2 * i].at[k - 1],
                    recv_sem=outs[2 * i + 1].at[k - 1], device_id=peer, device_id_type=MESH).start()
        outs[-1][...] = jnp.zeros_like(outs[-1])

    res = pl.pallas_call(
        body, name=name, in_specs=[_HBM] * n,
        out_specs=[_SEM] * (2 * n) + [_HBM] * n + [pl.BlockSpec(memory_space=pltpu.VMEM)],
        out_shape=[pltpu.SemaphoreType.DMA((N_DEV - 1,))] * (2 * n) + [pltpu.HBM(a.shape, a.dtype) for a in lands]
        + [SDS((8, 128), F32)],
        input_output_aliases={i: 2 * n + i for i in range(n)},
        compiler_params=pltpu.CompilerParams(has_side_effects=_EFFECT),
    )(*[pltpu.with_memory_space_constraint(a, pltpu.HBM) for a in lands])
    return [(res[2 * i], res[2 * i + 1], res[2 * n + i]) for i in range(n)], res[-1]


def _gather_copy(land_ref, send_ref, recv_ref, k, slot, to):
    return pltpu.make_async_remote_copy(
        src_ref=land_ref.at[slot], dst_ref=land_ref.at[slot], send_sem=send_ref.at[k - 1], recv_sem=recv_ref.at[k - 1],
        device_id=to, device_id_type=MESH)


def _gather_arrived(name, started, after):
    send_sems, recv_sems, land = started

    def body(land_ref, send_ref, recv_ref, after_ref, land_out):
        for k in _SAME_CORE:
            peer, slot = _peer(k)
            _gather_copy(land_ref, send_ref, recv_ref, k, slot, peer).wait_recv()

    return pl.pallas_call(
        body, name=name, in_specs=[_HBM, _SEM, _SEM, pl.BlockSpec(memory_space=pl.ANY)], out_specs=[_HBM],
        out_shape=[pltpu.HBM(land.shape, land.dtype)], input_output_aliases={0: 0},
        compiler_params=pltpu.CompilerParams(has_side_effects=_EFFECT),
    )(land, send_sems, recv_sems, after)[0]


def _gather_forward(name, land):
    def body(land_ref, send_ref, recv_ref, land_out, token):
        sibling, _ = _peer(_SIBLING)
        for j, k in enumerate(_SAME_CORE):
            _, slot = _peer(k)
            _gather_copy(land_ref, send_ref, recv_ref, j + 1, slot, sibling).start()
        token[...] = jnp.zeros_like(token)

    res = pl.pallas_call(
        body, name=name, in_specs=[_HBM], out_specs=[_SEM, _SEM, _HBM, pl.BlockSpec(memory_space=pltpu.VMEM)],
        out_shape=[pltpu.SemaphoreType.DMA((len(_SAME_CORE),))] * 2 + [pltpu.HBM(land.shape, land.dtype), SDS((8, 128), F32)],
        input_output_aliases={0: 2}, compiler_params=pltpu.CompilerParams(has_side_effects=_EFFECT),
    )(land)
    return tuple(res[:3]), res[3]


def _gather_wait(name, started, forwarded, after):
    send_sems, recv_sems, _ = started
    send_fwd, recv_fwd, land = forwarded

    def body(land_ref, send_ref, recv_ref, send2_ref, recv2_ref, after_ref, land_out):
        sibling, slot = _peer(_SIBLING)
        _gather_copy(land_ref, send_ref, recv_ref, _SIBLING, slot, sibling).wait_recv()
        for j, k in enumerate(_OTHER_CORE):
            _, slot = _peer(k)
            _gather_copy(land_ref, send2_ref, recv2_ref, j + 1, slot, sibling).wait_recv()
        for k in (_SIBLING, *_SAME_CORE):
            peer, slot = _peer(k)
            _gather_copy(land_ref, send_ref, recv_ref, k, slot, peer).wait_send()
        for j, k in enumerate(_SAME_CORE):
            _, slot = _peer(k)
            _gather_copy(land_ref, send2_ref, recv2_ref, j + 1, slot, sibling).wait_send()

    return pl.pallas_call(
        body, name=name, in_specs=[_HBM, _SEM, _SEM, _SEM, _SEM, pl.BlockSpec(memory_space=pl.ANY)], out_specs=[_HBM],
        out_shape=[pltpu.HBM(land.shape, land.dtype)], input_output_aliases={0: 0},
        compiler_params=pltpu.CompilerParams(has_side_effects=_EFFECT),
    )(land, send_sems, recv_sems, send_fwd, recv_fwd, after)[0]


def _scatter_start(name, parts):
    land = lax.empty(parts.shape, parts.dtype)

    def body(src, lnd, send_ref, recv_ref, src_out, lnd_out, token):
        me = _my_slot()
        for k in range(1, N_DEV):
            peer, slot = _peer(k)
            pltpu.make_async_remote_copy(
                src_ref=src.at[slot], dst_ref=lnd.at[me], send_sem=send_ref.at[k - 1], recv_sem=recv_ref.at[k - 1],
                device_id=peer, device_id_type=MESH).start()
        token[...] = jnp.zeros_like(token)

    res = pl.pallas_call(
        body, name=name, in_specs=[_HBM, _HBM],
        out_specs=[_SEM, _SEM, _HBM, _HBM, pl.BlockSpec(memory_space=pltpu.VMEM)],
        out_shape=[pltpu.SemaphoreType.DMA((N_DEV - 1,))] * 2 + [pltpu.HBM(parts.shape, parts.dtype)] * 2 + [SDS((8, 128), F32)],
        input_output_aliases={0: 2, 1: 3}, compiler_params=pltpu.CompilerParams(has_side_effects=_EFFECT),
    )(pltpu.with_memory_space_constraint(parts, pltpu.HBM), pltpu.with_memory_space_constraint(land, pltpu.HBM))
    return tuple(res[:4]), res[4]


def _scatter_wait(name, started, after):
    send_sems, recv_sems, parts, land = started

    def body(src_ref, land_ref, send_ref, recv_ref, after_ref, src_out, land_out):
        me = _my_slot()
        for k in range(1, N_DEV):
            peer, slot = _peer(k)
            copy = pltpu.make_async_remote_copy(
                src_ref=src_ref.at[me], dst_ref=land_ref.at[slot], send_sem=send_ref.at[k - 1],
                recv_sem=recv_ref.at[k - 1], device_id=peer, device_id_type=MESH)
            copy.wait_send()
            copy.wait_recv()

    return pl.pallas_call(
        body, name=name, in_specs=[_HBM, _HBM, _SEM, _SEM, pl.BlockSpec(memory_space=pl.ANY)], out_specs=[_HBM, _HBM],
        out_shape=[pltpu.HBM(parts.shape, parts.dtype), pltpu.HBM(land.shape, land.dtype)], input_output_aliases={0: 0, 1: 1},
        compiler_params=pltpu.CompilerParams(has_side_effects=_EFFECT),
    )(parts, land, send_sems, recv_sems, after)


def _adamw(name, landed, own, w, m, v):
    r, c = w.shape
    tr = _tile(r, max(16, (12 * 2 ** 20) // (46 * c)), 16)
    bc1 = 1.0 / (1.0 - ADAM_B1 ** ADAM_STEP)
    bc2 = 1.0 / (1.0 - ADAM_B2 ** ADAM_STEP)

    def body(me_ref, p_ref, own_ref, w_ref, m_ref, v_ref, g_ref, d_ref, nm_ref, nv_ref):
        me = me_ref[0]
        g = jnp.zeros(w_ref.shape, F32)
        for s in range(N_DEV):
            g = g + jnp.where(me == s, own_ref[...], p_ref[s]).astype(F32)
        nm = ADAM_B1 * m_ref[...] + (1.0 - ADAM_B1) * g
        nv = ADAM_B2 * v_ref[...] + (1.0 - ADAM_B2) * (g * g)
        g_ref[...] = g
        nm_ref[...] = nm
        nv_ref[...] = nv
        d_ref[...] = -ADAM_LR * ((nm * bc1) / (jnp.sqrt(nv * bc2) + ADAM_EPS) + ADAM_WD * w_ref[...])

    blk = pl.BlockSpec((tr, c), lambda i, me: (i, 0))
    return pl.pallas_call(
        body, name=name, out_shape=[SDS((r, c), F32)] * 4, compiler_params=_params(1),
        grid_spec=pltpu.PrefetchScalarGridSpec(
            num_scalar_prefetch=1, grid=(r // tr,),
            in_specs=[pl.BlockSpec((N_DEV, tr, c), lambda i, me: (0, i, 0)), pl.BlockSpec((None, tr, c), lambda i, me: (me[0], i, 0)),
                      blk, blk, blk],
            out_specs=[blk] * 4),
    )(_slot_operand(), landed, own, w, m, v)


def _ffn_fwd(name, h, norm, fetch, prefetch, landed, ahead=None):
    n, n_t = _rms_fwd(name + "_norm", h, norm)
    wg = fetch(name + "_w_gate", n)
    gate = _ffn_proj(name + "_gate", n, wg)
    sent = prefetch(name + "_w_down", gate) if landed else None
    wu = fetch(name + "_w_up", gate)
    up, act = _ffn_proj(name + "_up", n, wu, gate, after=sent)
    sent = prefetch(ahead, act) if ahead else None
    wd = fetch(name + "_w_down", act)
    return _ffn_down(act, wd, h, 0.5, after=sent), (n_t, gate, up, act, wg, wu, wd)


def _ffn_bwd(name, h, norm, saved, dh, dy_b, scale_out, emit):
    n_t, gate, up, act, wg, wu, wd = saved
    sent = emit(name + "_w_down", _wgrad_rows(name + "_dwd", act, dy_b))
    d_gate, d_up = _ffn_bwd_act(dy_b, wd, gate, up, after=sent)
    sent = emit(name + "_w_gate", _wgrad_cols(name + "_dwg", n_t, d_gate, after=sent))
    sent = emit(name + "_w_up", _wgrad_cols(name + "_dwu", n_t, d_up, after=sent))
    dn = _dgrad_cols(name + "_dn", (d_gate, d_up), (wg, wu), after=sent)
    return _rms_bwd(name + "_norm_bwd", h, dn, norm, dh, scale_out)


def _local_step(x, target, norms, small, fetch, prefetch, emit, *, batch, seq):
    n1w, nmw, n2w, nfw = norms
    a_log, dt_bias, dn_norm = small
    t, d = x.shape

    h1, saved1 = _ffn_fwd("ffn1", x, n1w, fetch, prefetch, False)
    nm, nm_t = _rms_fwd("mix_norm", h1, nmw)
    w_in = fetch("w_in", nm)
    p = w_in.shape[2]
    proj = _in_proj(nm, w_in)
    conv_all = fetch("conv_w", proj)
    proj = jnp.swapaxes(proj, 0, 1).reshape(t, N_DEV * p)
    conv_w = jnp.swapaxes(conv_all, 0, 1).reshape(CONV_WIDTH, N_DEV * conv_all.shape[2])
    dd = conv_w.shape[1] // 3
    da = (N_DEV * p - 4 * dd - 2 * (dd // HEAD)) // 3
    qkv = proj[:, :3 * da].astype(BF16)
    xd = proj[:, 3 * da:3 * da + 3 * dd]
    z = proj[:, 3 * da + 3 * dd:3 * da + 4 * dd]
    dbda = proj[:, 3 * da + 4 * dd:]
    attn, lse = _attn_fwd(qkv, batch=batch, seq=seq)
    yd = _conv_silu_fwd(xd, conv_w, batch=batch, seq=seq)
    dn_out, dn_states = _dn_fwd(yd, z, dbda, a_log, dt_bias, dn_norm, batch=batch, seq=seq)
    cat = jnp.concatenate([attn, dn_out], axis=1).astype(BF16)
    w_out = fetch("w_out", cat)
    w_out2 = w_out.reshape(da + dd, d)
    sent = prefetch("ffn2_w_up", prefetch("ffn2_w_gate", cat))
    h2 = _out_proj(cat, w_out2, h1, after=sent)
    h3, saved2 = _ffn_fwd("ffn2", h2, n2w, fetch, prefetch, True)

    loss, dh3, dh3_b, g_nf = _loss_head(h3, nfw, target, 0.5)
    dh2, dh2_b, g_n2 = _ffn_bwd("ffn2", h2, n2w, saved2, dh3, dh3_b, 1.0, emit)

    sent = emit("w_out", _wgrad_full("dw_out", cat, dh2_b).reshape(w_out.shape))
    dcat = _dgrad_full("d_cat", dh2_b, w_out2, after=sent)
    d_attn, d_dn = dcat[:, :da], dcat[:, da:]
    dq, dk, dv = _attn_bwd(qkv, attn, lse, d_attn, batch=batch, seq=seq)
    gq, gk, gv, gz, g_dbda, g_alog, g_dtb, g_dnn = _dn_bwd(yd, z, dbda, a_log, dt_bias, dn_norm, dn_states, d_dn, batch=batch, seq=seq)
    d_xd, g_conv = _conv_silu_bwd(xd, conv_w, jnp.concatenate([gq, gk, gv], axis=1), batch=batch, seq=seq)
    dproj = jnp.concatenate([dq, dk, dv, d_xd, gz, g_dbda], axis=1).astype(BF16)
    dproj = jnp.swapaxes(dproj.reshape(t, N_DEV, p), 0, 1)
    sent = emit("w_in", _wgrad_cols("dw_in", nm_t, dproj))
    dnm = _dgrad_cols("d_mix_in", (dproj,), (w_in,), after=sent)
    dh1, dh1_b, g_nm = _rms_bwd("mix_norm_bwd", h1, dnm, nmw, dh2, 0.5)

    dx, _, g_n1 = _ffn_bwd("ffn1", x, n1w, saved1, dh1, dh1_b, 1.0, emit)
    return loss, dx, (g_n1, g_nm, g_n2, g_nf), (g_alog, g_dtb, g_dnn), g_conv


def _pack_rows(vectors):
    rows, offsets, r = [], [], 0
    for vec in vectors:
        n = -(-vec.size // 128)
        rows.append(jnp.pad(vec.reshape(-1), (0, n * 128 - vec.size)).reshape(n, 128))
        offsets.append((r, vec.size, vec.shape))
        r += n
    pad = -r % 8
    if pad:
        rows.append(jnp.zeros((pad, 128), F32))
    return jnp.concatenate(rows, axis=0), offsets


def _unpack_rows(packed, offsets):
    return [packed[r:r + -(-size // 128)].reshape(-1)[:size].reshape(shape) for r, size, shape in offsets]


def kernel(x, ffn1_norm, ffn1_w_gate, ffn1_w_up, ffn1_w_down, mix_norm, w_in, conv_w, a_log, dt_bias, dn_norm, w_out, ffn2_norm, ffn2_w_gate, ffn2_w_up, ffn2_w_down, final_norm, loss_target, m_ffn1_norm, m_ffn1_w_gate, m_ffn1_w_up, m_ffn1_w_down, m_mix_norm, m_w_in, m_conv_w, m_a_log, m_dt_bias, m_dn_norm, m_w_out, m_ffn2_norm, m_ffn2_w_gate, m_ffn2_w_up, m_ffn2_w_down, m_final_norm, v_ffn1_norm, v_ffn1_w_gate, v_ffn1_w_up, v_ffn1_w_down, v_mix_norm, v_w_in, v_conv_w, v_a_log, v_dt_bias, v_dn_norm, v_w_out, v_ffn2_norm, v_ffn2_w_gate, v_ffn2_w_up, v_ffn2_w_down, v_final_norm):
    batch, seq, d = x.shape
    t = batch * seq
    big = dict(ffn1_w_gate=(ffn1_w_gate, m_ffn1_w_gate, v_ffn1_w_gate), ffn1_w_up=(ffn1_w_up, m_ffn1_w_up, v_ffn1_w_up),
               ffn1_w_down=(ffn1_w_down, m_ffn1_w_down, v_ffn1_w_down), w_in=(w_in, m_w_in, v_w_in),
               w_out=(w_out, m_w_out, v_w_out), ffn2_w_gate=(ffn2_w_gate, m_ffn2_w_gate, v_ffn2_w_gate),
               ffn2_w_up=(ffn2_w_up, m_ffn2_w_up, v_ffn2_w_up), ffn2_w_down=(ffn2_w_down, m_ffn2_w_down, v_ffn2_w_down))
    rep = dict(ffn1_norm=(ffn1_norm, m_ffn1_norm, v_ffn1_norm), mix_norm=(mix_norm, m_mix_norm, v_mix_norm),
               ffn2_norm=(ffn2_norm, m_ffn2_norm, v_ffn2_norm), final_norm=(final_norm, m_final_norm, v_final_norm),
               a_log=(a_log, m_a_log, v_a_log), dt_bias=(dt_bias, m_dt_bias, v_dt_bias), dn_norm=(dn_norm, m_dn_norm, v_dn_norm))

    lands = {"conv_w": _cast_place("place_conv_w", conv_w, F32)}
    lands.update({name: _cast_place("place_" + name, w, BF16) for name, (w, _, _) in big.items()})
    started, token = _gather_start("gather_start", list(lands.values()))
    gathering = dict(zip(lands, started))
    gathered, scattering = {}, {}

    forwarding = {}

    def prefetch(name, after):
        if name not in forwarding:
            land = _gather_arrived("gather_arrived_" + name, gathering[name], after)
            forwarding[name] = _gather_forward("gather_forward_" + name, land)
        return forwarding[name][1]

    def fetch(name, after):
        if name not in gathered:
            prefetch(name, after)
            gathered[name] = _gather_wait("gather_wait_" + name, gathering[name], forwarding[name][0], after)
        return gathered[name]

    def emit(name, grad):
        scattering[name], sent = _scatter_start("scatter_start_" + name, grad)
        return sent

    row = lambda a: a.reshape(1, -1)
    norms = [row(rep[n][0]) for n in ("ffn1_norm", "mix_norm", "ffn2_norm", "final_norm")]
    norms[0] = norms[0] + token[0, 0]
    loss, dx, g_norms, g_small, g_conv = _local_step(
        x.reshape(t, d), loss_target.reshape(t, d), norms, [row(rep[n][0]) for n in ("a_log", "dt_bias", "dn_norm")],
        fetch, prefetch, emit, batch=batch, seq=seq)

    out = {"grad_x": dx.reshape(x.shape)}
    after = dx
    for name in scattering:
        w, m, v = big[name]
        own, landed = _scatter_wait("scatter_wait_" + name, scattering[name], after)
        out["grad_" + name], out["delta_" + name], out["new_m_" + name], out["new_v_" + name] = _adamw("adamw_" + name, landed, own, w, m, v)
        after = out["grad_" + name]
    conv_parts = jnp.swapaxes(g_conv.reshape(CONV_WIDTH, N_DEV, conv_w.shape[1]), 0, 1)
    parts = _exchange_slices("scatter_conv_w", conv_parts, after)
    out["grad_conv_w"], out["delta_conv_w"], out["new_m_conv_w"], out["new_v_conv_w"] = _adamw("adamw_conv_w", parts, parts, conv_w, m_conv_w, v_conv_w)

    rep_names = list(rep)
    g_rep = [*g_norms, *g_small]
    packed_g, offsets = _pack_rows([*g_rep, loss[:, :1]])
    packed = [_pack_rows([*[rep[n][i] for n in rep_names], jnp.zeros((1, 1), F32)])[0] for i in range(3)]
    parts = _all_gather("gather_small_grads", packed_g, out["grad_conv_w"])
    res = [_unpack_rows(a, offsets) for a in _adamw("adamw_small", parts, parts, *packed)]
    for i, name in enumerate(rep_names):
        shape = rep[name][0].shape
        out["grad_" + name], out["delta_" + name], out["new_m_" + name], out["new_v_" + name] = (r[i].reshape(shape) for r in res)
    out["loss"] = res[0][-1].reshape(())

    order = ["ffn1_norm", "ffn1_w_gate", "ffn1_w_up", "ffn1_w_down", "mix_norm", "w_in", "conv_w", "a_log", "dt_bias", "dn_norm",
             "w_out", "ffn2_norm", "ffn2_w_gate", "ffn2_w_up", "ffn2_w_down", "final_norm"]
    return (out["loss"], out["grad_x"], *[out["grad_" + n] for n in order], *[out["delta_" + n] for n in order],
            *[out["new_m_" + n] for n in order], *[out["new_v_" + n] for n in order])
```

```python
import functools
import math

import jax
import jax.numpy as jnp
from jax import lax
from jax.experimental import pallas as pl
from jax.experimental.pallas import tpu as pltpu

F32 = jnp.float32
BF16 = jnp.bfloat16
N_DEV = 8
HEAD = 128
CHUNK = 64
CHUNK_BITS = 6
DN_ROWS = 256
DN_SEQ_BLOCK = 1024
CONV_WIDTH = 4
EPS = 1e-6
DILATED_CONFIGS = ((128, 1), (512, 4), (2048, 16))
ATTN_BLOCK = 256
NEG = -1e30
ADAM_LR, ADAM_B1, ADAM_B2, ADAM_EPS, ADAM_WD, ADAM_STEP = 0.001, 0.9, 0.999, 1e-08, 0.01, 10
HI = lax.Precision.HIGHEST
MESH = pl.DeviceIdType.MESH
SDS = jax.ShapeDtypeStruct


def _tile(n, pref, align):
    t = (min(n, pref) // align) * align
    while t >= align:
        if n % t == 0:
            return t
        t -= align
    return n


def _params(n_axes, vmem_mb=48):
    return pltpu.CompilerParams(dimension_semantics=("arbitrary",) * n_axes, vmem_limit_bytes=vmem_mb * 2 ** 20)


def _sigmoid(x):
    return 1.0 / (1.0 + jnp.exp(-x))


def _silu(x):
    return x * _sigmoid(x)


def _softplus(x):
    return jnp.maximum(x, 0.0) + jnp.log(1.0 + jnp.exp(-jnp.abs(x)))


_DIMS = {"nn": (((1,), (0,)), ((), ())), "nt": (((1,), (1,)), ((), ())), "tn": (((0,), (0,)), ((), ()))}


def _mm_call(name, grid, mode, pairs, operands, in_specs, out_shape, out_specs, acc_shapes, epilogue, vmem_mb=48, after=None):
    dims = _DIMS[mode]
    if after is not None:
        operands, in_specs = (*operands, after), [*in_specs, pl.BlockSpec(memory_space=pl.ANY)]
    n_in, n_out = len(operands), len(out_shape)
    nk = grid[-1]

    def whole(*refs):
        ins, outs = refs[:n_in], refs[n_in:]
        sums = {}
        for a, b, c in pairs:
            prod = lax.dot_general(ins[a][...], ins[b][...], dims, preferred_element_type=F32)
            sums[c] = prod if c not in sums else sums[c] + prod
        epilogue(ins, outs, [sums[c] for c in sorted(sums)])

    if acc_shapes is None:
        return pl.pallas_call(
            whole, name=name, grid=grid, in_specs=in_specs, out_specs=out_specs, out_shape=out_shape,
            compiler_params=_params(len(grid), vmem_mb))(*operands)

    def body(*refs):
        ins, outs, accs = refs[:n_in], refs[n_in:n_in + n_out], refs[n_in + n_out:]
        k = pl.program_id(len(grid) - 1)

        @pl.when(k == 0)
        def _():
            for acc in accs:
                acc[...] = jnp.zeros_like(acc)

        sums = {}
        for a, b, c in pairs:
            prod = lax.dot_general(ins[a][...], ins[b][...], dims, preferred_element_type=F32)
            sums[c] = prod if c not in sums else sums[c] + prod
        for c, total in sums.items():
            accs[c][...] += total

        @pl.when(k == nk - 1)
        def _():
            epilogue(ins, outs, [acc[...] for acc in accs])

    return pl.pallas_call(
        body, name=name, grid=grid, in_specs=in_specs, out_specs=out_specs, out_shape=out_shape,
        scratch_shapes=[pltpu.VMEM(s, F32) for s in acc_shapes], compiler_params=_params(len(grid), vmem_mb),
    )(*operands)


def _ffn_proj(name, n, w, gate=None, after=None):
    t, d = n.shape
    f = w.shape[2]
    tm = _tile(t, 256, 16)
    n_spec = pl.BlockSpec((tm, d), lambda s, m, k: (m, 0))
    w_spec = pl.BlockSpec((None, d, f), lambda s, m, k: (s, 0, 0))
    o_spec = pl.BlockSpec((None, tm, f), lambda s, m, k: (s, m, 0))
    o_shape = SDS((N_DEV, t, f), BF16)
    grid = (N_DEV, t // tm, 1)
    if gate is None:
        return _mm_call(name, grid, "nn", [(0, 1, 0)], (n, w), [n_spec, w_spec], [o_shape], [o_spec], None, _store_bf16,
                        after=after)[0]

    def up_out(ins, outs, accs):
        outs[0][...] = accs[0].astype(BF16)
        outs[1][...] = (_silu(ins[2][...].astype(F32)) * accs[0]).astype(BF16)

    return _mm_call(name, grid, "nn", [(0, 1, 0)], (n, w, gate), [n_spec, w_spec, o_spec], [o_shape] * 2, [o_spec] * 2,
                    None, up_out, after=after)


def _ffn_down(act, wd, resid, scale, after=None):
    _, t, f = act.shape
    d = wd.shape[2]
    tm, tn = _tile(t, 512, 16), _tile(d, 1024, 128)

    def epilogue(ins, outs, accs):
        outs[0][...] = ins[2][...] + scale * accs[0]

    rc = pl.BlockSpec((tm, tn), lambda m, n, s: (m, n))
    return _mm_call(
        "ffn_down", (t // tm, d // tn, N_DEV), "nn", [(0, 1, 0)], (act, wd, resid),
        [pl.BlockSpec((None, tm, f), lambda m, n, s: (s, m, 0)), pl.BlockSpec((None, f, tn), lambda m, n, s: (s, 0, n)), rc],
        [SDS((t, d), F32)], [rc], [(tm, tn)], epilogue, after=after)[0]


def _in_proj(n, w):
    t, d = n.shape
    p = w.shape[2]
    tm = _tile(t, 256, 16)
    return _mm_call(
        "in_proj", (N_DEV, t // tm, 1), "nn", [(0, 1, 0)], (n, w),
        [pl.BlockSpec((tm, d), lambda s, m, k: (m, 0)), pl.BlockSpec((None, d, p), lambda s, m, k: (s, 0, 0))],
        [SDS((N_DEV, t, p), F32)], [pl.BlockSpec((None, tm, p), lambda s, m, k: (s, m, 0))], None, _store_f32)[0]


def _out_proj(cat, w, resid, after=None):
    t, dm = cat.shape
    d = w.shape[1]
    tm, tn = _tile(t, 512, 16), _tile(d, 1024, 128)

    def epilogue(ins, outs, accs):
        outs[0][...] = ins[2][...] + accs[0]

    rc = pl.BlockSpec((tm, tn), lambda n, m, k: (m, n))
    return _mm_call(
        "out_proj", (d // tn, t // tm, 1), "nn", [(0, 1, 0)], (cat, w, resid),
        [pl.BlockSpec((tm, dm), lambda n, m, k: (m, 0)), pl.BlockSpec((dm, tn), lambda n, m, k: (0, n)), rc],
        [SDS((t, d), F32)], [rc], None, epilogue, after=after)[0]


def _ffn_bwd_act(dy, wd, gate, up, after=None):
    t, d = dy.shape
    f = wd.shape[1]
    tm = _tile(t, 256, 16)

    def epilogue(ins, outs, accs):
        g, u = ins[2][...].astype(F32), ins[3][...].astype(F32)
        sg = _sigmoid(g)
        outs[0][...] = (accs[0] * u * sg * (1.0 + g * (1.0 - sg))).astype(BF16)
        outs[1][...] = (accs[0] * g * sg).astype(BF16)

    o_spec = pl.BlockSpec((None, tm, f), lambda s, m, k: (s, m, 0))
    return _mm_call(
        "ffn_bwd_act", (N_DEV, t // tm, 1), "nt", [(0, 1, 0)], (dy, wd, gate, up),
        [pl.BlockSpec((tm, d), lambda s, m, k: (m, 0)), pl.BlockSpec((None, f, d), lambda s, m, k: (s, 0, 0)), o_spec, o_spec],
        [SDS((N_DEV, t, f), BF16)] * 2, [o_spec] * 2, None, epilogue, after=after)


def _store_bf16(ins, outs, accs):
    outs[0][...] = accs[0].astype(BF16)


def _store_f32(ins, outs, accs):
    outs[0][...] = accs[0]


def _wgrad_cols(name, a_t, b, after=None):
    m, t = a_t.shape
    n = b.shape[2]
    tm, tk = _tile(m, 512 if n <= 1408 else 256, 128), t
    return _mm_call(
        name, (N_DEV, m // tm, t // tk), "nn", [(0, 1, 0)], (a_t, b),
        [pl.BlockSpec((tm, tk), lambda s, i, k: (i, k)), pl.BlockSpec((None, tk, n), lambda s, i, k: (s, k, 0))],
        [SDS((N_DEV, m, n), BF16)], [pl.BlockSpec((None, tm, n), lambda s, i, k: (s, i, 0))], None, _store_bf16,
        after=after)[0]


def _wgrad_rows(name, a, b, after=None):
    _, t, m = a.shape
    n = b.shape[1]
    tn, tk = _tile(n, 512, 128), t
    return _mm_call(
        name, (N_DEV, n // tn, t // tk), "tn", [(0, 1, 0)], (a, b),
        [pl.BlockSpec((None, tk, m), lambda s, j, k: (s, k, 0)), pl.BlockSpec((tk, tn), lambda s, j, k: (k, j))],
        [SDS((N_DEV, m, n), BF16)], [pl.BlockSpec((None, m, tn), lambda s, j, k: (s, 0, j))], None, _store_bf16,
        after=after)[0]


def _wgrad_full(name, a, b, after=None):
    t, m = a.shape
    n = b.shape[1]
    tm, tn, tk = _tile(m, 512, 128), _tile(n, 1024, 128), t
    return _mm_call(
        name, (m // tm, n // tn, t // tk), "tn", [(0, 1, 0)], (a, b),
        [pl.BlockSpec((tk, tm), lambda i, j, k: (k, i)), pl.BlockSpec((tk, tn), lambda i, j, k: (k, j))],
        [SDS((m, n), BF16)], [pl.BlockSpec((tm, tn), lambda i, j, k: (i, j))], None, _store_bf16, after=after)[0]


def _dgrad_cols(name, grads, weights, after=None):
    _, t, n = grads[0].shape
    m = weights[0].shape[1]
    tm, tn = _tile(t, 512, 16), _tile(m, 1024, 128)
    k = len(grads)
    return _mm_call(
        name, (t // tm, m // tn, N_DEV), "nt", [(i, k + i, 0) for i in range(k)], (*grads, *weights),
        [pl.BlockSpec((None, tm, n), lambda i, j, s: (s, i, 0))] * k + [pl.BlockSpec((None, tn, n), lambda i, j, s: (s, j, 0))] * k,
        [SDS((t, m), F32)], [pl.BlockSpec((tm, tn), lambda i, j, s: (i, j))], [(tm, tn)], _store_f32, after=after)[0]


def _dgrad_full(name, g, w, after=None):
    t, n = g.shape
    m = w.shape[0]
    tm, tn, tk = _tile(t, 512, 16), _tile(m, 1024, 128), n
    return _mm_call(
        name, (m // tn, t // tm, n // tk), "nt", [(0, 1, 0)], (g, w),
        [pl.BlockSpec((tm, tk), lambda j, i, k: (i, k)), pl.BlockSpec((tn, tk), lambda j, i, k: (j, k))],
        [SDS((t, m), F32)], [pl.BlockSpec((tm, tn), lambda j, i, k: (i, j))], None, _store_f32, after=after)[0]


def _rms_fwd(name, h, w):
    t, d = h.shape
    tm = _tile(t, 256, 128)

    def body(h_ref, w_ref, o_ref, ot_ref):
        x = h_ref[...]
        n = (x * lax.rsqrt(jnp.mean(x * x, axis=1, keepdims=True) + EPS) * w_ref[...]).astype(BF16)
        o_ref[...] = n
        ot_ref[...] = n.T

    row = pl.BlockSpec((tm, d), lambda i: (i, 0))
    return pl.pallas_call(
        body, name=name, grid=(t // tm,), in_specs=[row, pl.BlockSpec((1, d), lambda i: (0, 0))],
        out_specs=[row, pl.BlockSpec((d, tm), lambda i: (0, i))], out_shape=[SDS((t, d), BF16), SDS((d, t), BF16)],
        compiler_params=_params(1))(h, w)


def _rms_bwd(name, h, dn, w, dres, scale):
    t, d = h.shape
    tm = _tile(t, 128, 16)

    def body(h_ref, dn_ref, w_ref, dres_ref, dh_ref, dhb_ref, dw_ref):
        @pl.when(pl.program_id(0) == 0)
        def _():
            dw_ref[...] = jnp.zeros_like(dw_ref)

        x = h_ref[...]
        rstd = lax.rsqrt(jnp.mean(x * x, axis=1, keepdims=True) + EPS)
        nhat = x * rstd
        g = dn_ref[...]
        gw = g * w_ref[...]
        dh = dres_ref[...] + rstd * (gw - nhat * jnp.mean(gw * nhat, axis=1, keepdims=True))
        dh_ref[...] = dh
        dhb_ref[...] = (scale * dh).astype(BF16)
        dw_ref[...] += jnp.sum(g * nhat, axis=0, keepdims=True)

    row = pl.BlockSpec((tm, d), lambda i: (i, 0))
    vec = pl.BlockSpec((1, d), lambda i: (0, 0))
    return pl.pallas_call(
        body, name=name, grid=(t // tm,), in_specs=[row, row, vec, row], out_specs=[row, row, vec],
        out_shape=[SDS((t, d), F32), SDS((t, d), BF16), SDS((1, d), F32)], compiler_params=_params(1))(h, dn, w, dres)


def _loss_head(h, w, target, scale):
    t, d = h.shape
    tm = _tile(t, 128, 16)

    def body(h_ref, w_ref, tg_ref, loss_ref, dh_ref, dhb_ref, dw_ref):
        @pl.when(pl.program_id(0) == 0)
        def _():
            dw_ref[...] = jnp.zeros_like(dw_ref)
            loss_ref[...] = jnp.zeros_like(loss_ref)

        x = h_ref[...]
        rstd = lax.rsqrt(jnp.mean(x * x, axis=1, keepdims=True) + EPS)
        nhat = x * rstd
        wv = w_ref[...]
        err = nhat * wv - tg_ref[...]
        loss_ref[...] += 0.5 * jnp.sum(jnp.mean(err * err, axis=1, keepdims=True), axis=0, keepdims=True)
        g = err * (1.0 / d)
        gw = g * wv
        dh = rstd * (gw - nhat * jnp.mean(gw * nhat, axis=1, keepdims=True))
        dh_ref[...] = dh
        dhb_ref[...] = (scale * dh).astype(BF16)
        dw_ref[...] += jnp.sum(g * nhat, axis=0, keepdims=True)

    row = pl.BlockSpec((tm, d), lambda i: (i, 0))
    vec = pl.BlockSpec((1, d), lambda i: (0, 0))
    return pl.pallas_call(
        body, name="loss_head", grid=(t // tm,), in_specs=[row, vec, row],
        out_specs=[pl.BlockSpec((1, 128), lambda i: (0, 0)), row, row, vec],
        out_shape=[SDS((1, 128), F32), SDS((t, d), F32), SDS((t, d), BF16), SDS((1, d), F32)],
        compiler_params=_params(1))(h, w, target)


def _attn_bias(delta, blk):
    dist = (lax.broadcasted_iota(jnp.int32, (blk, blk), 0) - lax.broadcasted_iota(jnp.int32, (blk, blk), 1)
            + delta * blk)
    count = jnp.zeros((blk, blk), F32)
    for window, dil in DILATED_CONFIGS:
        assert dil & (dil - 1) == 0
        seen = (dist >= 0) & (dist <= window) & ((dist & (dil - 1)) == 0)
        count = count + jnp.where(seen, 1.0, 0.0)
    return jnp.where(count > 0.0, jnp.log(jnp.maximum(count, 1.0)), NEG)


def _fill_bias_table(table, blk):
    @pl.when((pl.program_id(0) == 0) & (pl.program_id(1) == 0))
    def _():
        for delta in range(table.shape[0]):
            table[delta] = _attn_bias(delta, blk)


def _attn_fwd(qkv, *, batch, seq):
    t, da3 = qkv.shape
    da = da3 // 3
    n_heads = da // HEAD
    blk = _tile(seq, ATTN_BLOCK, 16)
    nq = seq // blk
    sm_scale = HEAD ** -0.5

    def body(q_ref, k_ref, v_ref, o_ref, lse_ref, bias):
        _fill_bias_table(bias, blk)

        def q_step(qi, _):
            rows = pl.ds(pl.multiple_of(qi * blk, blk), blk)
            q = q_ref[rows, :]

            def kv_step(ki, carry):
                m, l, acc = carry
                cols = pl.ds(pl.multiple_of(ki * blk, blk), blk)
                s = lax.dot_general(q, k_ref[cols, :], _DIMS["nt"], preferred_element_type=F32) * sm_scale
                s = s + bias[qi - ki]
                m_new = jnp.maximum(m, jnp.max(s, axis=1, keepdims=True))
                alpha = jnp.exp(m - m_new)
                p = jnp.exp(s - m_new)
                l = alpha * l + jnp.sum(p, axis=1, keepdims=True)
                acc = alpha * acc + jnp.dot(p.astype(BF16), v_ref[cols, :], preferred_element_type=F32)
                return m_new, l, acc

            m, l, acc = lax.fori_loop(0, qi + 1, kv_step, (jnp.full((blk, 1), NEG, F32), jnp.zeros((blk, 1), F32),
                                                           jnp.zeros((blk, HEAD), F32)))
            o_ref[rows, :] = acc / l
            lse_ref[rows, :] = jnp.broadcast_to(m + jnp.log(l), (blk, HEAD))
            return 0

        lax.fori_loop(0, nq, q_step, 0)

    col = lambda off: pl.BlockSpec((seq, HEAD), lambda b, h: (b, off + h))
    return pl.pallas_call(
        body, name="attn_fwd", grid=(batch, n_heads), in_specs=[col(0), col(n_heads), col(2 * n_heads)],
        out_specs=[col(0), col(0)], out_shape=[SDS((t, da), F32), SDS((t, da), F32)],
        scratch_shapes=[pltpu.VMEM((nq, blk, blk), F32)], compiler_params=_params(2),
    )(qkv, qkv, qkv)


def _attn_bwd(qkv, out, lse, d_out, *, batch, seq):
    t, da = out.shape
    n_heads = da // HEAD
    blk = _tile(seq, ATTN_BLOCK, 16)
    nq = seq // blk
    sm_scale = HEAD ** -0.5

    def body(q_ref, k_ref, v_ref, o_ref, lse_ref, do_ref, dq_ref, dk_ref, dv_ref, bias):
        _fill_bias_table(bias, blk)
        dk_ref[...] = jnp.zeros_like(dk_ref)
        dv_ref[...] = jnp.zeros_like(dv_ref)

        def q_step(qi, _):
            rows = pl.ds(pl.multiple_of(qi * blk, blk), blk)
            q = q_ref[rows, :]
            do = do_ref[rows, :]
            do_b = do.astype(BF16)
            lse_q = lse_ref[rows, :][:, :1]
            delta = jnp.sum(do * o_ref[rows, :], axis=1, keepdims=True)

            def kv_step(ki, dq):
                cols = pl.ds(pl.multiple_of(ki * blk, blk), blk)
                k = k_ref[cols, :]
                s = lax.dot_general(q, k, _DIMS["nt"], preferred_element_type=F32) * sm_scale
                p = jnp.exp(s + bias[qi - ki] - lse_q)
                dp = lax.dot_general(do_b, v_ref[cols, :], _DIMS["nt"], preferred_element_type=F32)
                ds = (p * (dp - delta) * sm_scale).astype(BF16)
                dv_ref[cols, :] += lax.dot_general(p.astype(BF16), do_b, _DIMS["tn"], preferred_element_type=F32)
                dk_ref[cols, :] += lax.dot_general(ds, q, _DIMS["tn"], preferred_element_type=F32)
                return dq + jnp.dot(ds, k, preferred_element_type=F32)

            dq_ref[rows, :] = lax.fori_loop(0, qi + 1, kv_step, jnp.zeros((blk, HEAD), F32))
            return 0

        lax.fori_loop(0, nq, q_step, 0)

    col = lambda off: pl.BlockSpec((seq, HEAD), lambda b, h: (b, off + h))
    return pl.pallas_call(
        body, name="attn_bwd", grid=(batch, n_heads),
        in_specs=[col(0), col(n_heads), col(2 * n_heads), col(0), col(0), col(0)], out_specs=[col(0)] * 3,
        out_shape=[SDS((t, da), F32)] * 3, scratch_shapes=[pltpu.VMEM((nq, blk, blk), F32)], compiler_params=_params(2),
    )(qkv, qkv, qkv, out, lse, d_out)


def _shift_down(x, k, row):
    return x if k == 0 else jnp.where(row >= k, pltpu.roll(x, k, axis=0), 0.0)


def _shift_up(x, k, row):
    n = x.shape[0]
    return x if k == 0 else jnp.where(row < n - k, pltpu.roll(x, n - k, axis=0), 0.0)


def _conv_silu_fwd(x, w, *, batch, seq):
    t, c = x.shape

    def body(x_ref, w_ref, o_ref):
        xv = x_ref[...]
        row = lax.broadcasted_iota(jnp.int32, xv.shape, 0)
        acc = jnp.zeros_like(xv)
        for i in range(CONV_WIDTH):
            acc = acc + w_ref[i:i + 1, :] * _shift_down(xv, CONV_WIDTH - 1 - i, row)
        o_ref[...] = _silu(acc)

    blk = pl.BlockSpec((seq, HEAD), lambda j, b: (b, j))
    return pl.pallas_call(
        body, name="conv_silu_fwd", grid=(c // HEAD, batch), in_specs=[blk, pl.BlockSpec((CONV_WIDTH, HEAD), lambda j, b: (0, j))],
        out_specs=blk, out_shape=SDS((t, c), F32), compiler_params=_params(2))(x, w)


def _conv_silu_bwd(x, w, dy, *, batch, seq):
    t, c = x.shape

    def body(x_ref, w_ref, dy_ref, dx_ref, dw_ref):
        @pl.when(pl.program_id(1) == 0)
        def _():
            dw_ref[...] = jnp.zeros_like(dw_ref)

        xv = x_ref[...]
        row = lax.broadcasted_iota(jnp.int32, xv.shape, 0)
        shifted = [_shift_down(xv, CONV_WIDTH - 1 - i, row) for i in range(CONV_WIDTH)]
        acc = jnp.zeros_like(xv)
        for i in range(CONV_WIDTH):
            acc = acc + w_ref[i:i + 1, :] * shifted[i]
        sg = _sigmoid(acc)
        dc = dy_ref[...] * sg * (1.0 + acc * (1.0 - sg))
        dx = jnp.zeros_like(xv)
        for i in range(CONV_WIDTH):
            dx = dx + w_ref[i:i + 1, :] * _shift_up(dc, CONV_WIDTH - 1 - i, row)
            dw_ref[i:i + 1, :] += jnp.sum(dc * shifted[i], axis=0, keepdims=True)
        dx_ref[...] = dx

    blk = pl.BlockSpec((seq, HEAD), lambda j, b: (b, j))
    wblk = pl.BlockSpec((CONV_WIDTH, HEAD), lambda j, b: (0, j))
    return pl.pallas_call(
        body, name="conv_silu_bwd", grid=(c // HEAD, batch), in_specs=[blk, wblk, blk], out_specs=[blk, wblk],
        out_shape=[SDS((t, c), F32), SDS((CONV_WIDTH, c), F32)], compiler_params=_params(2))(x, w, dy)


def _dot(a, b, mode="nn"):
    return lax.dot_general(a.astype(BF16), b.astype(BF16), _DIMS[mode], preferred_element_type=F32)


def _dot3(a, b):
    a_hi, b_hi = a.astype(BF16), b.astype(BF16)
    a_lo, b_lo = (a - a_hi.astype(F32)).astype(BF16), (b - b_hi.astype(F32)).astype(BF16)
    pass_ = lambda x, y: jnp.dot(x, y, preferred_element_type=F32)
    return pass_(a_hi, b_hi) + pass_(a_hi, b_lo) + pass_(a_lo, b_hi)


@jax.custom_vjp
def _nilpotent_inverse(m):
    r = m.shape[0]
    x = jnp.where(lax.broadcasted_iota(jnp.int32, (r, r), 0) == lax.broadcasted_iota(jnp.int32, (r, r), 1), 1.0, 0.0) + m
    p = m
    for _ in range(CHUNK_BITS - 1):
        p = _dot3(p, p)
        x = x + _dot3(x, p)
    return x


def _nilpotent_inverse_fwd(m):
    x = _nilpotent_inverse(m)
    return x, x


def _nilpotent_inverse_bwd(x, g):
    return (_dot(x, _dot(g, x, "nt"), "tn"),)


_nilpotent_inverse.defvjp(_nilpotent_inverse_fwd, _nilpotent_inverse_bwd)


def _dot_nt(a, b):
    return _dot(a, b, "nt")


def _dot_tn(a, b):
    return _dot(a, b, "tn")


def _dn_chunk(head, n_heads, aq, ak, v, z, dbda, a_log, dt_bias, dn_norm, state):
    r = aq.shape[0]
    lane_g = lax.broadcasted_iota(jnp.int32, dbda.shape, 1)
    db = jnp.sum(jnp.where(lane_g == head, dbda, 0.0), axis=1, keepdims=True)
    da = jnp.sum(jnp.where(lane_g == head + n_heads, dbda, 0.0), axis=1, keepdims=True)
    lane_h = lax.broadcasted_iota(jnp.int32, a_log.shape, 1)
    al = jnp.sum(jnp.where(lane_h == head, a_log, 0.0), axis=1, keepdims=True)
    dtb = jnp.sum(jnp.where(lane_h == head, dt_bias, 0.0), axis=1, keepdims=True)
    beta = _sigmoid(db)
    g = -jnp.exp(al) * _softplus(da + dtb)
    q = aq * lax.rsqrt(jnp.sum(aq * aq, axis=1, keepdims=True) + EPS) * (HEAD ** -0.5)
    k = ak * lax.rsqrt(jnp.sum(ak * ak, axis=1, keepdims=True) + EPS)
    ri = lax.broadcasted_iota(jnp.int32, (r, r), 0)
    ci = lax.broadcasted_iota(jnp.int32, (r, r), 1)
    same = (ri >> CHUNK_BITS) == (ci >> CHUNK_BITS)
    incl = same & (ri >= ci)
    g_row = jnp.sum(jnp.where(ri == ci, g, 0.0), axis=0, keepdims=True)
    gc_col = jnp.sum(jnp.where(incl, g_row, 0.0), axis=1, keepdims=True)
    gc_row = jnp.sum(jnp.where(same & (ri <= ci), g, 0.0), axis=0, keepdims=True)
    g_all = jnp.sum(jnp.where(same, g_row, 0.0), axis=1, keepdims=True)
    decay = jnp.where(incl, jnp.exp(jnp.where(incl, gc_col - gc_row, 0.0)), 0.0)
    kb = k * beta
    m = -jnp.where(same & (ri > ci), _dot_nt(kb, k) * decay, 0.0)
    x = _nilpotent_inverse(m)
    egc = jnp.exp(gc_col)
    wu_g = _dot(x, jnp.concatenate([kb * egc, v * beta], axis=1))
    w_g, u_g = wu_g[:, :HEAD], wu_g[:, HEAD:]
    qk = _dot_nt(q, k) * decay
    q_dec = q * egc
    k_dec = k * jnp.exp(g_all - gc_col)
    carry = jnp.exp(g_all)
    v_new, o_state = [], []
    for c in range(r // CHUNK):
        rows = slice(c * CHUNK, (c + 1) * CHUNK)
        v_new.append(u_g[rows] - _dot(w_g[rows], state))
        o_state.append(_dot(q_dec[rows], state))
        state = state * carry[c * CHUNK:c * CHUNK + 1] + _dot_tn(k_dec[rows], v_new[-1])
    o = jnp.concatenate(o_state, axis=0) + _dot(qk, jnp.concatenate(v_new, axis=0))
    o = o * lax.rsqrt(jnp.mean(o * o, axis=1, keepdims=True) + EPS) * dn_norm
    return o * _silu(z), state


def _dn_chunks(head, n_heads):
    return jax.vmap(functools.partial(_dn_chunk, head, n_heads), in_axes=(0, 0, 0, 0, 0, None, None, None, 0))


def _dn_layout(batch, seq, n_heads, small, reverse):
    grp = _tile(seq, DN_ROWS, CHUNK)
    rows = _tile(seq, DN_SEQ_BLOCK, grp)
    n_blocks, per = seq // rows, rows // grp
    at = (lambda j: n_blocks - 1 - j) if reverse else (lambda j: j)
    col = lambda off: pl.BlockSpec((batch, rows, HEAD), lambda j, h: (0, at(j), off + h))
    gates = pl.BlockSpec((batch, rows, 2 * n_heads), lambda j, h: (0, at(j), 0))
    states = pl.BlockSpec((batch, None, per, HEAD, HEAD), lambda j, h: (0, h, at(j), 0, 0))
    full = [pl.BlockSpec(a.shape, lambda j, h: (0, 0)) for a in small]
    return grp, n_blocks, per, col, gates, states, full


def _dn_fwd(y, z, dbda, a_log, dt_bias, dn_norm):
    batch, seq, dd = z.shape
    n_heads = dd // HEAD
    grp, n_blocks, per, col, gates, st_spec, full = _dn_layout(batch, seq, n_heads, (a_log, dt_bias, dn_norm), False)

    def body(q_ref, k_ref, v_ref, z_ref, g_ref, al_ref, dt_ref, nw_ref, o_ref, st_ref, carry):
        head = pl.program_id(1)
        al, dtb, nw = al_ref[...], dt_ref[...], nw_ref[...]

        @pl.when(pl.program_id(0) == 0)
        def _():
            carry[head] = jnp.zeros((batch, HEAD, HEAD), F32)

        def step(n, states):
            rows = pl.ds(pl.multiple_of(n * grp, grp), grp)
            for b in range(batch):
                st_ref[b, n] = states[b]
            out, states = _dn_chunks(head, n_heads)(q_ref[:, rows, :], k_ref[:, rows, :], v_ref[:, rows, :],
                                                    z_ref[:, rows, :], g_ref[:, rows, :], al, dtb, nw, states)
            o_ref[:, rows, :] = out
            return states

        carry[head] = lax.fori_loop(0, per, step, carry[head])

    return pl.pallas_call(
        body, name="dn_fwd", grid=(n_blocks, n_heads),
        in_specs=[col(0), col(n_heads), col(2 * n_heads), col(0), gates, *full], out_specs=[col(0), st_spec],
        out_shape=[SDS((batch, seq, dd), F32), SDS((batch, n_heads, seq // grp, HEAD, HEAD), F32)],
        scratch_shapes=[pltpu.VMEM((n_heads, batch, HEAD, HEAD), F32)], compiler_params=_params(2),
    )(y, y, y, z, dbda, a_log, dt_bias, dn_norm)


def _dn_bwd(y, z, dbda, a_log, dt_bias, dn_norm, states, d_out):
    batch, seq, dd = z.shape
    n_heads = dd // HEAD
    grp, n_blocks, per, col, gates, st_spec, full = _dn_layout(batch, seq, n_heads, (a_log, dt_bias, dn_norm), True)

    def body(q_ref, k_ref, v_ref, z_ref, g_ref, al_ref, dt_ref, nw_ref, st_ref, do_ref,
             dq_ref, dk_ref, dv_ref, dz_ref, dg_ref, dal_ref, ddt_ref, dnw_ref, carry):
        first, head = pl.program_id(0) == 0, pl.program_id(1)
        al, dtb, nw = al_ref[...], dt_ref[...], nw_ref[...]

        @pl.when(first & (head == 0))
        def _():
            dal_ref[...] = jnp.zeros_like(dal_ref)
            ddt_ref[...] = jnp.zeros_like(ddt_ref)
            dnw_ref[...] = jnp.zeros_like(dnw_ref)

        @pl.when(head == 0)
        def _():
            dg_ref[...] = jnp.zeros_like(dg_ref)

        @pl.when(first)
        def _():
            carry[head] = jnp.zeros((batch, HEAD, HEAD), F32)

        def step(i, acc):
            d_states, d_al, d_dt, d_nw = acc
            n = per - 1 - i
            rows = pl.ds(pl.multiple_of(n * grp, grp), grp)
            states = jnp.stack([st_ref[b, n] for b in range(batch)])
            _, vjp = jax.vjp(_dn_chunks(head, n_heads), q_ref[:, rows, :], k_ref[:, rows, :], v_ref[:, rows, :],
                             z_ref[:, rows, :], g_ref[:, rows, :], al, dtb, nw, states)
            gq, gk, gv, gz, gg, gal, gdt, gnw, d_states = vjp((do_ref[:, rows, :], d_states))
            dq_ref[:, rows, :] = gq
            dk_ref[:, rows, :] = gk
            dv_ref[:, rows, :] = gv
            dz_ref[:, rows, :] = gz
            dg_ref[:, rows, :] += gg
            return d_states, d_al + gal, d_dt + gdt, d_nw + gnw

        zero = lambda a: jnp.zeros(a.shape, F32)
        d_states, d_al, d_dt, d_nw = lax.fori_loop(0, per, step, (carry[head], zero(al), zero(dtb), zero(nw)))
        carry[head] = d_states
        dal_ref[...] += d_al
        ddt_ref[...] += d_dt
        dnw_ref[...] += d_nw

    out3 = SDS((batch, seq, dd), F32)
    return pl.pallas_call(
        body, name="dn_bwd", grid=(n_blocks, n_heads),
        in_specs=[col(0), col(n_heads), col(2 * n_heads), col(0), gates, *full, st_spec, col(0)],
        out_specs=[col(0), col(0), col(0), col(0), gates, *full],
        out_shape=[out3] * 4 + [SDS(dbda.shape, F32), SDS(a_log.shape, F32), SDS(dt_bias.shape, F32), SDS(dn_norm.shape, F32)],
        scratch_shapes=[pltpu.VMEM((n_heads, batch, HEAD, HEAD), F32)], compiler_params=_params(2),
    )(y, y, y, z, dbda, a_log, dt_bias, dn_norm, states, d_out)


def _my_slot():
    return 4 * lax.axis_index("x") + 2 * lax.axis_index("y") + lax.axis_index("c")


def _peer(k):
    x, y, c = lax.axis_index("x"), lax.axis_index("y"), lax.axis_index("c")
    return (x ^ (k >> 2), y ^ ((k >> 1) & 1), c ^ (k & 1)), (4 * x + 2 * y + c) ^ k


def _all_gather(name, block, after):
    def body(src, after_ref, dst, send_sems, recv_sems, local_sem):
        me = _my_slot()
        own = pltpu.make_async_copy(src, dst.at[me], local_sem)
        own.start()
        copies = []
        for k in range(1, N_DEV):
            peer, _ = _peer(k)
            copies.append(pltpu.make_async_remote_copy(
                src_ref=src, dst_ref=dst.at[me], send_sem=send_sems.at[k - 1], recv_sem=recv_sems.at[k - 1],
                device_id=peer, device_id_type=MESH))
            copies[-1].start()
        for k in range(1, N_DEV):
            peer, slot = _peer(k)
            pltpu.make_async_remote_copy(
                src_ref=src, dst_ref=dst.at[slot], send_sem=send_sems.at[k - 1], recv_sem=recv_sems.at[k - 1],
                device_id=peer, device_id_type=MESH).wait_recv()
        for cp in copies:
            cp.wait_send()
        own.wait()

    return pl.pallas_call(
        body, name=name, in_specs=[pl.BlockSpec(memory_space=pl.ANY)] * 2, out_specs=pl.BlockSpec(memory_space=pl.ANY),
        out_shape=SDS((N_DEV, *block.shape), block.dtype),
        scratch_shapes=[pltpu.SemaphoreType.DMA((N_DEV - 1,)), pltpu.SemaphoreType.DMA((N_DEV - 1,)), pltpu.SemaphoreType.DMA],
    )(block, after)


def _exchange_slices(name, parts, after):
    def body(src, after_ref, dst, send_sems, recv_sems, local_sem):
        me = _my_slot()
        own = pltpu.make_async_copy(src.at[me], dst.at[me], local_sem)
        own.start()
        copies = []
        for k in range(1, N_DEV):
            peer, slot = _peer(k)
            copies.append(pltpu.make_async_remote_copy(
                src_ref=src.at[slot], dst_ref=dst.at[me], send_sem=send_sems.at[k - 1], recv_sem=recv_sems.at[k - 1],
                device_id=peer, device_id_type=MESH))
            copies[-1].start()
        for k in range(1, N_DEV):
            peer, slot = _peer(k)
            pltpu.make_async_remote_copy(
                src_ref=src.at[me], dst_ref=dst.at[slot], send_sem=send_sems.at[k - 1], recv_sem=recv_sems.at[k - 1],
                device_id=peer, device_id_type=MESH).wait_recv()
        for cp in copies:
            cp.wait_send()
        own.wait()

    return pl.pallas_call(
        body, name=name, in_specs=[pl.BlockSpec(memory_space=pl.ANY)] * 2, out_specs=pl.BlockSpec(memory_space=pl.ANY),
        out_shape=SDS(parts.shape, parts.dtype),
        scratch_shapes=[pltpu.SemaphoreType.DMA((N_DEV - 1,)), pltpu.SemaphoreType.DMA((N_DEV - 1,)), pltpu.SemaphoreType.DMA],
    )(parts, after)


_HBM = pl.BlockSpec(memory_space=pltpu.HBM)
_SEM = pl.BlockSpec(memory_space=pltpu.SEMAPHORE)
_EFFECT = pltpu.SideEffectType.DATAFLOW_SIDE_EFFECTING


def _slot_operand():
    return _my_slot().astype(jnp.int32).reshape(1)


def _cast_place(name, block, dtype):
    r, c = block.shape
    tr = _tile(r, 512, 16)

    def body(me_ref, src_ref, dst_ref):
        dst_ref[...] = src_ref[...].astype(dtype)

    return pl.pallas_call(
        body, name=name, out_shape=SDS((N_DEV, r, c), dtype), compiler_params=_params(1),
        grid_spec=pltpu.PrefetchScalarGridSpec(
            num_scalar_prefetch=1, grid=(r // tr,), in_specs=[pl.BlockSpec((tr, c), lambda i, me: (i, 0))],
            out_specs=pl.BlockSpec((None, tr, c), lambda i, me: (me[0], i, 0))),
    )(_slot_operand(), block)


_SIBLING = 1
_SAME_CORE = (2, 4, 6)
_OTHER_CORE = (3, 5, 7)


def _gather_start(name, lands):
    n = len(lands)

    def body(*refs):
        lnds, outs = refs[:n], refs[n:]
        me = _my_slot()
        for i in range(n):
            for k in (*_SAME_CORE, _SIBLING):
                peer, _ = _peer(k)
                pltpu.make_async_remote_copy(
                    src_ref=lnds[i].at[me], dst_ref=lnds[i].at[me], send_sem=outs[2 * i].at[k - 1],
                    recv_sem=outs[2 * i + 1].at[k - 1], device_id=peer, device_id_type=MESH).start()
        outs[-1][...] = jnp.zeros_like(outs[-1])

    res = pl.pallas_call(
        body, name=name, in_specs=[_HBM] * n,
        out_specs=[_SEM] * (2 * n) + [_HBM] * n + [pl.BlockSpec(memory_space=pltpu.VMEM)],
        out_shape=[pltpu.SemaphoreType.DMA((N_DEV - 1,))] * (2 * n) + [pltpu.HBM(a.shape, a.dtype) for a in lands]
        + [SDS((8, 128), F32)],
        input_output_aliases={i: 2 * n + i for i in range(n)},
        compiler_params=pltpu.CompilerParams(has_side_effects=_EFFECT),
    )(*[pltpu.with_memory_space_constraint(a, pltpu.HBM) for a in lands])
    return [(res[2 * i], res[2 * i + 1], res[2 * n + i]) for i in range(n)], res[-1]


def _gather_copy(land_ref, send_ref, recv_ref, k, slot, to):
    return pltpu.make_async_remote_copy(
        src_ref=land_ref.at[slot], dst_ref=land_ref.at[slot], send_sem=send_ref.at[k - 1], recv_sem=recv_ref.at[k - 1],
        device_id=to, device_id_type=MESH)


def _gather_arrived(name, started, after):
    send_sems, recv_sems, land = started

    def body(land_ref, send_ref, recv_ref, after_ref, land_out):
        for k in _SAME_CORE:
            peer, slot = _peer(k)
            _gather_copy(land_ref, send_ref, recv_ref, k, slot, peer).wait_recv()

    return pl.pallas_call(
        body, name=name, in_specs=[_HBM, _SEM, _SEM, pl.BlockSpec(memory_space=pl.ANY)], out_specs=[_HBM],
        out_shape=[pltpu.HBM(land.shape, land.dtype)], input_output_aliases={0: 0},
        compiler_params=pltpu.CompilerParams(has_side_effects=_EFFECT),
    )(land, send_sems, recv_sems, after)[0]


def _gather_forward(name, land):
    def body(land_ref, send_ref, recv_ref, land_out, token):
        sibling, _ = _peer(_SIBLING)
        for j, k in enumerate(_SAME_CORE):
            _, slot = _peer(k)
            _gather_copy(land_ref, send_ref, recv_ref, j + 1, slot, sibling).start()
        token[...] = jnp.zeros_like(token)

    res = pl.pallas_call(
        body, name=name, in_specs=[_HBM], out_specs=[_SEM, _SEM, _HBM, pl.BlockSpec(memory_space=pltpu.VMEM)],
        out_shape=[pltpu.SemaphoreType.DMA((len(_SAME_CORE),))] * 2 + [pltpu.HBM(land.shape, land.dtype), SDS((8, 128), F32)],
        input_output_aliases={0: 2}, compiler_params=pltpu.CompilerParams(has_side_effects=_EFFECT),
    )(land)
    return tuple(res[:3]), res[3]


def _gather_wait(name, started, forwarded, after):
    send_sems, recv_sems, _ = started
    send_fwd, recv_fwd, land = forwarded

    def body(land_ref, send_ref, recv_ref, send2_ref, recv2_ref, after_ref, land_out):
        sibling, slot = _peer(_SIBLING)
        _gather_copy(land_ref, send_ref, recv_ref, _SIBLING, slot, sibling).wait_recv()
        for j, k in enumerate(_OTHER_CORE):
            _, slot = _peer(k)
            _gather_copy(land_ref, send2_ref, recv2_ref, j + 1, slot, sibling).wait_recv()
        for k in (_SIBLING, *_SAME_CORE):
            peer, slot = _peer(k)
            _gather_copy(land_ref, send_ref, recv_ref, k, slot, peer).wait_send()
        for j, k in enumerate(_SAME_CORE):
            _, slot = _peer(k)
            _gather_copy(land_ref, send2_ref, recv2_ref, j + 1, slot, sibling).wait_send()

    return pl.pallas_call(
        body, name=name, in_specs=[_HBM, _SEM, _SEM, _SEM, _SEM, pl.BlockSpec(memory_space=pl.ANY)], out_specs=[_HBM],
        out_shape=[pltpu.HBM(land.shape, land.dtype)], input_output_aliases={0: 0},
        compiler_params=pltpu.CompilerParams(has_side_effects=_EFFECT),
    )(land, send_sems, recv_sems, send_fwd, recv_fwd, after)[0]


def _scatter_start(name, parts):
    land = lax.empty(parts.shape, parts.dtype)

    def body(src, lnd, send_ref, recv_ref, src_out, lnd_out, token):
        me = _my_slot()
        for k in range(1, N_DEV):
            peer, slot = _peer(k)
            pltpu.make_async_remote_copy(
                src_ref=src.at[slot], dst_ref=lnd.at[me], send_sem=send_ref.at[k - 1], recv_sem=recv_ref.at[k - 1],
                device_id=peer, device_id_type=MESH).start()
        token[...] = jnp.zeros_like(token)

    res = pl.pallas_call(
        body, name=name, in_specs=[_HBM, _HBM],
        out_specs=[_SEM, _SEM, _HBM, _HBM, pl.BlockSpec(memory_space=pltpu.VMEM)],
        out_shape=[pltpu.SemaphoreType.DMA((N_DEV - 1,))] * 2 + [pltpu.HBM(parts.shape, parts.dtype)] * 2 + [SDS((8, 128), F32)],
        input_output_aliases={0: 2, 1: 3}, compiler_params=pltpu.CompilerParams(has_side_effects=_EFFECT),
    )(pltpu.with_memory_space_constraint(parts, pltpu.HBM), pltpu.with_memory_space_constraint(land, pltpu.HBM))
    return tuple(res[:4]), res[4]


def _scatter_wait(name, started, after):
    send_sems, recv_sems, parts, land = started

    def body(src_ref, land_ref, send_ref, recv_ref, after_ref, src_out, land_out):
        me = _my_slot()
        for k in range(1, N_DEV):
            peer, slot = _peer(k)
            copy = pltpu.make_async_remote_copy(
                src_ref=src_ref.at[me], dst_ref=land_ref.at[slot], send_sem=send_ref.at[k - 1],
                recv_sem=recv_ref.at[k - 1], device_id=peer, device_id_type=MESH)
            copy.wait_send()
            copy.wait_recv()

    return pl.pallas_call(
        body, name=name, in_specs=[_HBM, _HBM, _SEM, _SEM, pl.BlockSpec(memory_space=pl.ANY)], out_specs=[_HBM, _HBM],
        out_shape=[pltpu.HBM(parts.shape, parts.dtype), pltpu.HBM(land.shape, land.dtype)], input_output_aliases={0: 0, 1: 1},
        compiler_params=pltpu.CompilerParams(has_side_effects=_EFFECT),
    )(parts, land, send_sems, recv_sems, after)


def _adamw(name, landed, own, w, m, v):
    r, c = w.shape
    tr = _tile(r, max(16, (12 * 2 ** 20) // (46 * c)), 16)
    bc1 = 1.0 / (1.0 - ADAM_B1 ** ADAM_STEP)
    bc2 = 1.0 / (1.0 - ADAM_B2 ** ADAM_STEP)

    def body(me_ref, p_ref, own_ref, w_ref, m_ref, v_ref, g_ref, d_ref, nm_ref, nv_ref):
        me = me_ref[0]
        g = jnp.zeros(w_ref.shape, F32)
        for s in range(N_DEV):
            g = g + jnp.where(me == s, own_ref[...], p_ref[s]).astype(F32)
        nm = ADAM_B1 * m_ref[...] + (1.0 - ADAM_B1) * g
        nv = ADAM_B2 * v_ref[...] + (1.0 - ADAM_B2) * (g * g)
        g_ref[...] = g
        nm_ref[...] = nm
        nv_ref[...] = nv
        d_ref[...] = -ADAM_LR * ((nm * bc1) / (jnp.sqrt(nv * bc2) + ADAM_EPS) + ADAM_WD * w_ref[...])

    blk = pl.BlockSpec((tr, c), lambda i, me: (i, 0))
    return pl.pallas_call(
        body, name=name, out_shape=[SDS((r, c), F32)] * 4, compiler_params=_params(1),
        grid_spec=pltpu.PrefetchScalarGridSpec(
            num_scalar_prefetch=1, grid=(r // tr,),
            in_specs=[pl.BlockSpec((N_DEV, tr, c), lambda i, me: (0, i, 0)), pl.BlockSpec((None, tr, c), lambda i, me: (me[0], i, 0)),
                      blk, blk, blk],
            out_specs=[blk] * 4),
    )(_slot_operand(), landed, own, w, m, v)


def _ffn_fwd(name, h, norm, fetch, prefetch, landed, ahead=None):
    n, n_t = _rms_fwd(name + "_norm", h, norm)
    wg = fetch(name + "_w_gate", n)
    gate = _ffn_proj(name + "_gate", n, wg)
    sent = prefetch(name + "_w_down", gate) if landed else None
    wu = fetch(name + "_w_up", gate)
    up, act = _ffn_proj(name + "_up", n, wu, gate, after=sent)
    sent = prefetch(ahead, act) if ahead else None
    wd = fetch(name + "_w_down", act)
    return _ffn_down(act, wd, h, 0.5, after=sent), (n_t, gate, up, act, wg, wu, wd)


def _ffn_bwd(name, h, norm, saved, dh, dy_b, scale_out, emit):
    n_t, gate, up, act, wg, wu, wd = saved
    sent = emit(name + "_w_down", _wgrad_rows(name + "_dwd", act, dy_b))
    d_gate, d_up = _ffn_bwd_act(dy_b, wd, gate, up, after=sent)
    sent = emit(name + "_w_gate", _wgrad_cols(name + "_dwg", n_t, d_gate, after=sent))
    sent = emit(name + "_w_up", _wgrad_cols(name + "_dwu", n_t, d_up, after=sent))
    dn = _dgrad_cols(name + "_dn", (d_gate, d_up), (wg, wu), after=sent)
    return _rms_bwd(name + "_norm_bwd", h, dn, norm, dh, scale_out)


def _local_step(x, target, norms, small, fetch, prefetch, emit, *, batch, seq):
    n1w, nmw, n2w, nfw = norms
    a_log, dt_bias, dn_norm = small
    t, d = x.shape

    h1, saved1 = _ffn_fwd("ffn1", x, n1w, fetch, prefetch, False)
    nm, nm_t = _rms_fwd("mix_norm", h1, nmw)
    w_in = fetch("w_in", nm)
    p = w_in.shape[2]
    proj = _in_proj(nm, w_in)
    conv_all = fetch("conv_w", proj)
    proj = jnp.swapaxes(proj, 0, 1).reshape(t, N_DEV * p)
    conv_w = jnp.swapaxes(conv_all, 0, 1).reshape(CONV_WIDTH, N_DEV * conv_all.shape[2])
    dd = conv_w.shape[1] // 3
    da = (N_DEV * p - 4 * dd - 2 * (dd // HEAD)) // 3
    qkv = proj[:, :3 * da].astype(BF16)
    xd = proj[:, 3 * da:3 * da + 3 * dd]
    z = proj[:, 3 * da + 3 * dd:3 * da + 4 * dd]
    dbda = proj[:, 3 * da + 4 * dd:]
    attn, lse = _attn_fwd(qkv, batch=batch, seq=seq)
    yd = _conv_silu_fwd(xd, conv_w, batch=batch, seq=seq)
    per_seq = lambda a: a.reshape(batch, seq, a.shape[1])
    dn_in = (per_seq(yd), per_seq(z), per_seq(dbda), a_log, dt_bias, dn_norm)
    dn_out, dn_states = _dn_fwd(*dn_in)
    cat = jnp.concatenate([attn, dn_out.reshape(t, dd)], axis=1).astype(BF16)
    w_out = fetch("w_out", cat)
    w_out2 = w_out.reshape(da + dd, d)
    sent = prefetch("ffn2_w_up", prefetch("ffn2_w_gate", cat))
    h2 = _out_proj(cat, w_out2, h1, after=sent)
    h3, saved2 = _ffn_fwd("ffn2", h2, n2w, fetch, prefetch, True)

    loss, dh3, dh3_b, g_nf = _loss_head(h3, nfw, target, 0.5)
    dh2, dh2_b, g_n2 = _ffn_bwd("ffn2", h2, n2w, saved2, dh3, dh3_b, 1.0, emit)

    sent = emit("w_out", _wgrad_full("dw_out", cat, dh2_b).reshape(w_out.shape))
    dcat = _dgrad_full("d_cat", dh2_b, w_out2, after=sent)
    d_attn, d_dn = dcat[:, :da], dcat[:, da:]
    dq, dk, dv = _attn_bwd(qkv, attn, lse, d_attn, batch=batch, seq=seq)
    gq, gk, gv, gz, g_dbda, g_alog, g_dtb, g_dnn = _dn_bwd(*dn_in, dn_states, per_seq(d_dn))
    gq, gk, gv, gz, g_dbda = (a.reshape(t, a.shape[2]) for a in (gq, gk, gv, gz, g_dbda))
    d_xd, g_conv = _conv_silu_bwd(xd, conv_w, jnp.concatenate([gq, gk, gv], axis=1), batch=batch, seq=seq)
    dproj = jnp.concatenate([dq, dk, dv, d_xd, gz, g_dbda], axis=1).astype(BF16)
    dproj = jnp.swapaxes(dproj.reshape(t, N_DEV, p), 0, 1)
    sent = emit("w_in", _wgrad_cols("dw_in", nm_t, dproj))
    dnm = _dgrad_cols("d_mix_in", (dproj,), (w_in,), after=sent)
    dh1, dh1_b, g_nm = _rms_bwd("mix_norm_bwd", h1, dnm, nmw, dh2, 0.5)

    dx, _, g_n1 = _ffn_bwd("ffn1", x, n1w, saved1, dh1, dh1_b, 1.0, emit)
    return loss, dx, (g_n1, g_nm, g_n2, g_nf), (g_alog, g_dtb, g_dnn), g_conv


def _pack_rows(vectors):
    rows, offsets, r = [], [], 0
    for vec in vectors:
        n = -(-vec.size // 128)
        rows.append(jnp.pad(vec.reshape(-1), (0, n * 128 - vec.size)).reshape(n, 128))
        offsets.append((r, vec.size, vec.shape))
        r += n
    pad = -r % 8
    if pad:
        rows.append(jnp.zeros((pad, 128), F32))
    return jnp.concatenate(rows, axis=0), offsets


def _unpack_rows(packed, offsets):
    return [packed[r:r + -(-size // 128)].reshape(-1)[:size].reshape(shape) for r, size, shape in offsets]


def kernel(x, ffn1_norm, ffn1_w_gate, ffn1_w_up, ffn1_w_down, mix_norm, w_in, conv_w, a_log, dt_bias, dn_norm, w_out, ffn2_norm, ffn2_w_gate, ffn2_w_up, ffn2_w_down, final_norm, loss_target, m_ffn1_norm, m_ffn1_w_gate, m_ffn1_w_up, m_ffn1_w_down, m_mix_norm, m_w_in, m_conv_w, m_a_log, m_dt_bias, m_dn_norm, m_w_out, m_ffn2_norm, m_ffn2_w_gate, m_ffn2_w_up, m_ffn2_w_down, m_final_norm, v_ffn1_norm, v_ffn1_w_gate, v_ffn1_w_up, v_ffn1_w_down, v_mix_norm, v_w_in, v_conv_w, v_a_log, v_dt_bias, v_dn_norm, v_w_out, v_ffn2_norm, v_ffn2_w_gate, v_ffn2_w_up, v_ffn2_w_down, v_final_norm):
    batch, seq, d = x.shape
    t = batch * seq
    big = dict(ffn1_w_gate=(ffn1_w_gate, m_ffn1_w_gate, v_ffn1_w_gate), ffn1_w_up=(ffn1_w_up, m_ffn1_w_up, v_ffn1_w_up),
               ffn1_w_down=(ffn1_w_down, m_ffn1_w_down, v_ffn1_w_down), w_in=(w_in, m_w_in, v_w_in),
               w_out=(w_out, m_w_out, v_w_out), ffn2_w_gate=(ffn2_w_gate, m_ffn2_w_gate, v_ffn2_w_gate),
               ffn2_w_up=(ffn2_w_up, m_ffn2_w_up, v_ffn2_w_up), ffn2_w_down=(ffn2_w_down, m_ffn2_w_down, v_ffn2_w_down))
    rep = dict(ffn1_norm=(ffn1_norm, m_ffn1_norm, v_ffn1_norm), mix_norm=(mix_norm, m_mix_norm, v_mix_norm),
               ffn2_norm=(ffn2_norm, m_ffn2_norm, v_ffn2_norm), final_norm=(final_norm, m_final_norm, v_final_norm),
               a_log=(a_log, m_a_log, v_a_log), dt_bias=(dt_bias, m_dt_bias, v_dt_bias), dn_norm=(dn_norm, m_dn_norm, v_dn_norm))

    lands = {"conv_w": _cast_place("place_conv_w", conv_w, F32)}
    lands.update({name: _cast_place("place_" + name, w, BF16) for name, (w, _, _) in big.items()})
    started, token = _gather_start("gather_start", list(lands.values()))
    gathering = dict(zip(lands, started))
    gathered, scattering = {}, {}

    forwarding = {}

    def prefetch(name, after):
        if name not in forwarding:
            land = _gather_arrived("gather_arrived_" + name, gathering[name], after)
            forwarding[name] = _gather_forward("gather_forward_" + name, land)
        return forwarding[name][1]

    def fetch(name, after):
        if name not in gathered:
            prefetch(name, after)
            gathered[name] = _gather_wait("gather_wait_" + name, gathering[name], forwarding[name][0], after)
        return gathered[name]

    def emit(name, grad):
        scattering[name], sent = _scatter_start("scatter_start_" + name, grad)
        return sent

    row = lambda a: a.reshape(1, -1)
    norms = [row(rep[n][0]) for n in ("ffn1_norm", "mix_norm", "ffn2_norm", "final_norm")]
    norms[0] = norms[0] + token[0, 0]
    loss, dx, g_norms, g_small, g_conv = _local_step(
        x.reshape(t, d), loss_target.reshape(t, d), norms, [row(rep[n][0]) for n in ("a_log", "dt_bias", "dn_norm")],
        fetch, prefetch, emit, batch=batch, seq=seq)

    out = {"grad_x": dx.reshape(x.shape)}
    after = dx
    for name in scattering:
        w, m, v = big[name]
        own, landed = _scatter_wait("scatter_wait_" + name, scattering[name], after)
        out["grad_" + name], out["delta_" + name], out["new_m_" + name], out["new_v_" + name] = _adamw("adamw_" + name, landed, own, w, m, v)
        after = out["grad_" + name]
    conv_parts = jnp.swapaxes(g_conv.reshape(CONV_WIDTH, N_DEV, conv_w.shape[1]), 0, 1)
    parts = _exchange_slices("scatter_conv_w", conv_parts, after)
    out["grad_conv_w"], out["delta_conv_w"], out["new_m_conv_w"], out["new_v_conv_w"] = _adamw("adamw_conv_w", parts, parts, conv_w, m_conv_w, v_conv_w)

    rep_names = list(rep)
    g_rep = [*g_norms, *g_small]
    packed_g, offsets = _pack_rows([*g_rep, loss[:, :1]])
    packed = [_pack_rows([*[rep[n][i] for n in rep_names], jnp.zeros((1, 1), F32)])[0] for i in range(3)]
    parts = _all_gather("gather_small_grads", packed_g, out["grad_conv_w"])
    res = [_unpack_rows(a, offsets) for a in _adamw("adamw_small", parts, parts, *packed)]
    for i, name in enumerate(rep_names):
        shape = rep[name][0].shape
        out["grad_" + name], out["delta_" + name], out["new_m_" + name], out["new_v_" + name] = (r[i].reshape(shape) for r in res)
    out["loss"] = res[0][-1].reshape(())

    order = ["ffn1_norm", "ffn1_w_gate", "ffn1_w_up", "ffn1_w_down", "mix_norm", "w_in", "conv_w", "a_log", "dt_bias", "dn_norm",
             "w_out", "ffn2_norm", "ffn2_w_gate", "ffn2_w_up", "ffn2_w_down", "final_norm"]
    return (out["loss"], out["grad_x"], *[out["grad_" + n] for n in order], *[out["delta_" + n] for n in order],
            *[out["new_m_" + n] for n in order], *[out["new_v_" + n] for n in order])
```

```python
import functools
import math

import jax
import jax.numpy as jnp
from jax import lax
from jax.experimental import pallas as pl
from jax.experimental.pallas import tpu as pltpu

F32 = jnp.float32
BF16 = jnp.bfloat16
N_DEV = 8
HEAD = 128
CHUNK = 64
CHUNK_BITS = 6
DN_ROWS = 256
DN_SEQ_BLOCK = 1024
CONV_WIDTH = 4
EPS = 1e-6
DILATED_CONFIGS = ((128, 1), (512, 4), (2048, 16))
ATTN_BLOCK = 256
NEG = -1e30
ADAM_LR, ADAM_B1, ADAM_B2, ADAM_EPS, ADAM_WD, ADAM_STEP = 0.001, 0.9, 0.999, 1e-08, 0.01, 10
HI = lax.Precision.HIGHEST
MESH = pl.DeviceIdType.MESH
SDS = jax.ShapeDtypeStruct


def _tile(n, pref, align):
    t = (min(n, pref) // align) * align
    while t >= align:
        if n % t == 0:
            return t
        t -= align
    return n


def _params(n_axes, vmem_mb=48):
    return pltpu.CompilerParams(dimension_semantics=("arbitrary",) * n_axes, vmem_limit_bytes=vmem_mb * 2 ** 20)


def _sigmoid(x):
    return 1.0 / (1.0 + jnp.exp(-x))


def _silu(x):
    return x * _sigmoid(x)


def _softplus(x):
    return jnp.maximum(x, 0.0) + jnp.log(1.0 + jnp.exp(-jnp.abs(x)))


_DIMS = {"nn": (((1,), (0,)), ((), ())), "nt": (((1,), (1,)), ((), ())), "tn": (((0,), (0,)), ((), ()))}


def _mm_call(name, grid, mode, pairs, operands, in_specs, out_shape, out_specs, acc_shapes, epilogue, vmem_mb=48, after=None):
    dims = _DIMS[mode]
    if after is not None:
        operands, in_specs = (*operands, after), [*in_specs, pl.BlockSpec(memory_space=pl.ANY)]
    n_in, n_out = len(operands), len(out_shape)
    nk = grid[-1]

    def whole(*refs):
        ins, outs = refs[:n_in], refs[n_in:]
        sums = {}
        for a, b, c in pairs:
            prod = lax.dot_general(ins[a][...], ins[b][...], dims, preferred_element_type=F32)
            sums[c] = prod if c not in sums else sums[c] + prod
        epilogue(ins, outs, [sums[c] for c in sorted(sums)])

    if acc_shapes is None:
        return pl.pallas_call(
            whole, name=name, grid=grid, in_specs=in_specs, out_specs=out_specs, out_shape=out_shape,
            compiler_params=_params(len(grid), vmem_mb))(*operands)

    def body(*refs):
        ins, outs, accs = refs[:n_in], refs[n_in:n_in + n_out], refs[n_in + n_out:]
        k = pl.program_id(len(grid) - 1)

        @pl.when(k == 0)
        def _():
            for acc in accs:
                acc[...] = jnp.zeros_like(acc)

        sums = {}
        for a, b, c in pairs:
            prod = lax.dot_general(ins[a][...], ins[b][...], dims, preferred_element_type=F32)
            sums[c] = prod if c not in sums else sums[c] + prod
        for c, total in sums.items():
            accs[c][...] += total

        @pl.when(k == nk - 1)
        def _():
            epilogue(ins, outs, [acc[...] for acc in accs])

    return pl.pallas_call(
        body, name=name, grid=grid, in_specs=in_specs, out_specs=out_specs, out_shape=out_shape,
        scratch_shapes=[pltpu.VMEM(s, F32) for s in acc_shapes], compiler_params=_params(len(grid), vmem_mb),
    )(*operands)


def _ffn_proj(name, n, w, gate=None, after=None):
    t, d = n.shape
    f = w.shape[1]
    tm = _tile(t, 256, 16)
    n_spec = pl.BlockSpec((tm, d), lambda s, m, k: (m, 0))
    w_spec = pl.BlockSpec((None, f, d), lambda s, m, k: (s, 0, 0))
    o_spec = pl.BlockSpec((None, tm, f), lambda s, m, k: (s, m, 0))
    o_shape = SDS((N_DEV, t, f), BF16)
    grid = (N_DEV, t // tm, 1)
    if gate is None:
        return _mm_call(name, grid, "nt", [(0, 1, 0)], (n, w), [n_spec, w_spec], [o_shape], [o_spec], None, _store_bf16,
                        after=after)[0]

    def up_out(ins, outs, accs):
        outs[0][...] = accs[0].astype(BF16)
        outs[1][...] = (_silu(ins[2][...].astype(F32)) * accs[0]).astype(BF16)

    return _mm_call(name, grid, "nt", [(0, 1, 0)], (n, w, gate), [n_spec, w_spec, o_spec], [o_shape] * 2, [o_spec] * 2,
                    None, up_out, after=after)


def _ffn_down(act, wd, resid, scale, after=None):
    _, t, f = act.shape
    d = wd.shape[2]
    tm, tn = _tile(t, 1024, 16), _tile(d, 1024, 128)

    def epilogue(ins, outs, accs):
        outs[0][...] = ins[2][...] + scale * accs[0]

    rc = pl.BlockSpec((tm, tn), lambda m, n, s: (m, n))
    return _mm_call(
        "ffn_down", (t // tm, d // tn, N_DEV), "nn", [(0, 1, 0)], (act, wd, resid),
        [pl.BlockSpec((None, tm, f), lambda m, n, s: (s, m, 0)), pl.BlockSpec((None, f, tn), lambda m, n, s: (s, 0, n)), rc],
        [SDS((t, d), F32)], [rc], [(tm, tn)], epilogue, after=after)[0]


def _in_proj(n, w):
    t, d = n.shape
    p = w.shape[1]
    tm = _tile(t, 256, 16)
    return _mm_call(
        "in_proj", (N_DEV, t // tm, 1), "nt", [(0, 1, 0)], (n, w),
        [pl.BlockSpec((tm, d), lambda s, m, k: (m, 0)), pl.BlockSpec((None, p, d), lambda s, m, k: (s, 0, 0))],
        [SDS((N_DEV, t, p), F32)], [pl.BlockSpec((None, tm, p), lambda s, m, k: (s, m, 0))], None, _store_f32)[0]


def _out_proj(cat, w, resid, after=None):
    t, dm = cat.shape
    d = w.shape[1]
    tm, tn = _tile(t, 512, 16), _tile(d, 1024, 128)

    def epilogue(ins, outs, accs):
        outs[0][...] = ins[2][...] + accs[0]

    rc = pl.BlockSpec((tm, tn), lambda n, m, k: (m, n))
    return _mm_call(
        "out_proj", (d // tn, t // tm, 1), "nn", [(0, 1, 0)], (cat, w, resid),
        [pl.BlockSpec((tm, dm), lambda n, m, k: (m, 0)), pl.BlockSpec((dm, tn), lambda n, m, k: (0, n)), rc],
        [SDS((t, d), F32)], [rc], None, epilogue, after=after)[0]


def _ffn_bwd_act(dy, wd, gate, up, after=None):
    t, d = dy.shape
    f = wd.shape[1]
    tm = _tile(t, 256, 16)

    def epilogue(ins, outs, accs):
        g, u = ins[2][...].astype(F32), ins[3][...].astype(F32)
        sg = _sigmoid(g)
        outs[0][...] = (accs[0] * u * sg * (1.0 + g * (1.0 - sg))).astype(BF16)
        outs[1][...] = (accs[0] * g * sg).astype(BF16)

    o_spec = pl.BlockSpec((None, tm, f), lambda s, m, k: (s, m, 0))
    return _mm_call(
        "ffn_bwd_act", (N_DEV, t // tm, 1), "nt", [(0, 1, 0)], (dy, wd, gate, up),
        [pl.BlockSpec((tm, d), lambda s, m, k: (m, 0)), pl.BlockSpec((None, f, d), lambda s, m, k: (s, 0, 0)), o_spec, o_spec],
        [SDS((N_DEV, t, f), BF16)] * 2, [o_spec] * 2, None, epilogue, after=after)


def _store_bf16(ins, outs, accs):
    outs[0][...] = accs[0].astype(BF16)


def _store_f32(ins, outs, accs):
    outs[0][...] = accs[0]


def _wgrad_rows(name, a, b, after=None):
    _, t, m = a.shape
    n = b.shape[1]
    tn, tk = _tile(n, 512 if m <= 1408 else 256, 128), t
    return _mm_call(
        name, (N_DEV, n // tn, t // tk), "tn", [(0, 1, 0)], (a, b),
        [pl.BlockSpec((None, tk, m), lambda s, j, k: (s, k, 0)), pl.BlockSpec((tk, tn), lambda s, j, k: (k, j))],
        [SDS((N_DEV, m, n), BF16)], [pl.BlockSpec((None, m, tn), lambda s, j, k: (s, 0, j))], None, _store_bf16,
        after=after)[0]


def _wgrad_full(name, a, b, after=None):
    t, m = a.shape
    n = b.shape[1]
    tm, tn, tk = _tile(m, 512, 128), _tile(n, 1024, 128), t
    return _mm_call(
        name, (m // tm, n // tn, t // tk), "tn", [(0, 1, 0)], (a, b),
        [pl.BlockSpec((tk, tm), lambda i, j, k: (k, i)), pl.BlockSpec((tk, tn), lambda i, j, k: (k, j))],
        [SDS((m, n), BF16)], [pl.BlockSpec((tm, tn), lambda i, j, k: (i, j))], None, _store_bf16, after=after)[0]


def _dgrad_cols(name, grads, weights, after=None):
    _, t, n = grads[0].shape
    m = weights[0].shape[2]
    tm, tn = _tile(t, 1024, 16), _tile(m, 1024, 128)
    k = len(grads)
    return _mm_call(
        name, (t // tm, m // tn, N_DEV), "nn", [(i, k + i, 0) for i in range(k)], (*grads, *weights),
        [pl.BlockSpec((None, tm, n), lambda i, j, s: (s, i, 0))] * k + [pl.BlockSpec((None, n, tn), lambda i, j, s: (s, 0, j))] * k,
        [SDS((t, m), F32)], [pl.BlockSpec((tm, tn), lambda i, j, s: (i, j))], [(tm, tn)], _store_f32, after=after)[0]


def _dgrad_full(name, g, w, after=None):
    t, n = g.shape
    m = w.shape[0]
    tm, tn, tk = _tile(t, 512, 16), _tile(m, 1024, 128), n
    return _mm_call(
        name, (m // tn, t // tm, n // tk), "nt", [(0, 1, 0)], (g, w),
        [pl.BlockSpec((tm, tk), lambda j, i, k: (i, k)), pl.BlockSpec((tn, tk), lambda j, i, k: (j, k))],
        [SDS((t, m), F32)], [pl.BlockSpec((tm, tn), lambda j, i, k: (i, j))], None, _store_f32, after=after)[0]


def _rms_fwd(name, h, w):
    t, d = h.shape
    tm = _tile(t, 256, 16)

    def body(h_ref, w_ref, o_ref):
        x = h_ref[...]
        o_ref[...] = (x * lax.rsqrt(jnp.mean(x * x, axis=1, keepdims=True) + EPS) * w_ref[...]).astype(BF16)

    row = pl.BlockSpec((tm, d), lambda i: (i, 0))
    return pl.pallas_call(
        body, name=name, grid=(t // tm,), in_specs=[row, pl.BlockSpec((1, d), lambda i: (0, 0))], out_specs=row,
        out_shape=SDS((t, d), BF16), compiler_params=_params(1))(h, w)


def _rms_bwd(name, h, dn, w, dres, scale):
    t, d = h.shape
    tm = _tile(t, 128, 16)

    def body(h_ref, dn_ref, w_ref, dres_ref, dh_ref, dhb_ref, dw_ref):
        @pl.when(pl.program_id(0) == 0)
        def _():
            dw_ref[...] = jnp.zeros_like(dw_ref)

        x = h_ref[...]
        rstd = lax.rsqrt(jnp.mean(x * x, axis=1, keepdims=True) + EPS)
        nhat = x * rstd
        g = dn_ref[...]
        gw = g * w_ref[...]
        dh = dres_ref[...] + rstd * (gw - nhat * jnp.mean(gw * nhat, axis=1, keepdims=True))
        dh_ref[...] = dh
        dhb_ref[...] = (scale * dh).astype(BF16)
        dw_ref[...] += jnp.sum(g * nhat, axis=0, keepdims=True)

    row = pl.BlockSpec((tm, d), lambda i: (i, 0))
    vec = pl.BlockSpec((1, d), lambda i: (0, 0))
    return pl.pallas_call(
        body, name=name, grid=(t // tm,), in_specs=[row, row, vec, row], out_specs=[row, row, vec],
        out_shape=[SDS((t, d), F32), SDS((t, d), BF16), SDS((1, d), F32)], compiler_params=_params(1))(h, dn, w, dres)


def _loss_head(h, w, target, scale):
    t, d = h.shape
    tm = _tile(t, 128, 16)

    def body(h_ref, w_ref, tg_ref, loss_ref, dh_ref, dhb_ref, dw_ref):
        @pl.when(pl.program_id(0) == 0)
        def _():
            dw_ref[...] = jnp.zeros_like(dw_ref)
            loss_ref[...] = jnp.zeros_like(loss_ref)

        x = h_ref[...]
        rstd = lax.rsqrt(jnp.mean(x * x, axis=1, keepdims=True) + EPS)
        nhat = x * rstd
        wv = w_ref[...]
        err = nhat * wv - tg_ref[...]
        loss_ref[...] += 0.5 * jnp.sum(jnp.mean(err * err, axis=1, keepdims=True), axis=0, keepdims=True)
        g = err * (1.0 / d)
        gw = g * wv
        dh = rstd * (gw - nhat * jnp.mean(gw * nhat, axis=1, keepdims=True))
        dh_ref[...] = dh
        dhb_ref[...] = (scale * dh).astype(BF16)
        dw_ref[...] += jnp.sum(g * nhat, axis=0, keepdims=True)

    row = pl.BlockSpec((tm, d), lambda i: (i, 0))
    vec = pl.BlockSpec((1, d), lambda i: (0, 0))
    return pl.pallas_call(
        body, name="loss_head", grid=(t // tm,), in_specs=[row, vec, row],
        out_specs=[pl.BlockSpec((1, 128), lambda i: (0, 0)), row, row, vec],
        out_shape=[SDS((1, 128), F32), SDS((t, d), F32), SDS((t, d), BF16), SDS((1, d), F32)],
        compiler_params=_params(1))(h, w, target)


def _attn_bias(delta, blk):
    dist = (lax.broadcasted_iota(jnp.int32, (blk, blk), 0) - lax.broadcasted_iota(jnp.int32, (blk, blk), 1)
            + delta * blk)
    count = jnp.zeros((blk, blk), F32)
    for window, dil in DILATED_CONFIGS:
        assert dil & (dil - 1) == 0
        seen = (dist >= 0) & (dist <= window) & ((dist & (dil - 1)) == 0)
        count = count + jnp.where(seen, 1.0, 0.0)
    return jnp.where(count > 0.0, jnp.log(jnp.maximum(count, 1.0)), NEG)


def _fill_bias_table(table, blk):
    @pl.when((pl.program_id(0) == 0) & (pl.program_id(1) == 0))
    def _():
        for delta in range(table.shape[0]):
            table[delta] = _attn_bias(delta, blk)


def _attn_fwd(qkv, *, batch, seq):
    t, da3 = qkv.shape
    da = da3 // 3
    n_heads = da // HEAD
    blk = _tile(seq, ATTN_BLOCK, 16)
    nq = seq // blk
    sm_scale = HEAD ** -0.5

    def body(q_ref, k_ref, v_ref, o_ref, lse_ref, bias):
        _fill_bias_table(bias, blk)

        def q_step(qi, _):
            rows = pl.ds(pl.multiple_of(qi * blk, blk), blk)
            q = q_ref[rows, :]

            def kv_step(ki, carry):
                m, l, acc = carry
                cols = pl.ds(pl.multiple_of(ki * blk, blk), blk)
                s = lax.dot_general(q, k_ref[cols, :], _DIMS["nt"], preferred_element_type=F32) * sm_scale
                s = s + bias[qi - ki]
                m_new = jnp.maximum(m, jnp.max(s, axis=1, keepdims=True))
                alpha = jnp.exp(m - m_new)
                p = jnp.exp(s - m_new)
                l = alpha * l + jnp.sum(p, axis=1, keepdims=True)
                acc = alpha * acc + jnp.dot(p.astype(BF16), v_ref[cols, :], preferred_element_type=F32)
                return m_new, l, acc

            m, l, acc = lax.fori_loop(0, qi + 1, kv_step, (jnp.full((blk, 1), NEG, F32), jnp.zeros((blk, 1), F32),
                                                           jnp.zeros((blk, HEAD), F32)))
            o_ref[rows, :] = acc / l
            lse_ref[rows, :] = jnp.broadcast_to(m + jnp.log(l), (blk, HEAD))
            return 0

        lax.fori_loop(0, nq, q_step, 0)

    col = lambda off: pl.BlockSpec((seq, HEAD), lambda b, h: (b, off + h))
    return pl.pallas_call(
        body, name="attn_fwd", grid=(batch, n_heads), in_specs=[col(0), col(n_heads), col(2 * n_heads)],
        out_specs=[col(0), col(0)], out_shape=[SDS((t, da), F32), SDS((t, da), F32)],
        scratch_shapes=[pltpu.VMEM((nq, blk, blk), F32)], compiler_params=_params(2),
    )(qkv, qkv, qkv)


def _attn_bwd(qkv, out, lse, d_out, *, batch, seq):
    t, da = out.shape
    n_heads = da // HEAD
    blk = _tile(seq, ATTN_BLOCK, 16)
    nq = seq // blk
    sm_scale = HEAD ** -0.5

    def body(q_ref, k_ref, v_ref, o_ref, lse_ref, do_ref, dq_ref, dk_ref, dv_ref, bias):
        _fill_bias_table(bias, blk)
        dk_ref[...] = jnp.zeros_like(dk_ref)
        dv_ref[...] = jnp.zeros_like(dv_ref)

        def q_step(qi, _):
            rows = pl.ds(pl.multiple_of(qi * blk, blk), blk)
            q = q_ref[rows, :]
            do = do_ref[rows, :]
            do_b = do.astype(BF16)
            lse_q = lse_ref[rows, :][:, :1]
            delta = jnp.sum(do * o_ref[rows, :], axis=1, keepdims=True)

            def kv_step(ki, dq):
                cols = pl.ds(pl.multiple_of(ki * blk, blk), blk)
                k = k_ref[cols, :]
                s = lax.dot_general(q, k, _DIMS["nt"], preferred_element_type=F32) * sm_scale
                p = jnp.exp(s + bias[qi - ki] - lse_q)
                dp = lax.dot_general(do_b, v_ref[cols, :], _DIMS["nt"], preferred_element_type=F32)
                ds = (p * (dp - delta) * sm_scale).astype(BF16)
                dv_ref[cols, :] += lax.dot_general(p.astype(BF16), do_b, _DIMS["tn"], preferred_element_type=F32)
                dk_ref[cols, :] += lax.dot_general(ds, q, _DIMS["tn"], preferred_element_type=F32)
                return dq + jnp.dot(ds, k, preferred_element_type=F32)

            dq_ref[rows, :] = lax.fori_loop(0, qi + 1, kv_step, jnp.zeros((blk, HEAD), F32))
            return 0

        lax.fori_loop(0, nq, q_step, 0)

    col = lambda off: pl.BlockSpec((seq, HEAD), lambda b, h: (b, off + h))
    return pl.pallas_call(
        body, name="attn_bwd", grid=(batch, n_heads),
        in_specs=[col(0), col(n_heads), col(2 * n_heads), col(0), col(0), col(0)], out_specs=[col(0)] * 3,
        out_shape=[SDS((t, da), F32)] * 3, scratch_shapes=[pltpu.VMEM((nq, blk, blk), F32)], compiler_params=_params(2),
    )(qkv, qkv, qkv, out, lse, d_out)


def _shift_down(x, k, row):
    return x if k == 0 else jnp.where(row >= k, pltpu.roll(x, k, axis=0), 0.0)


def _shift_up(x, k, row):
    n = x.shape[0]
    return x if k == 0 else jnp.where(row < n - k, pltpu.roll(x, n - k, axis=0), 0.0)


def _conv_silu_fwd(x, w, *, batch, seq):
    t, c = x.shape

    def body(x_ref, w_ref, o_ref):
        xv = x_ref[...]
        row = lax.broadcasted_iota(jnp.int32, xv.shape, 0)
        acc = jnp.zeros_like(xv)
        for i in range(CONV_WIDTH):
            acc = acc + w_ref[i:i + 1, :] * _shift_down(xv, CONV_WIDTH - 1 - i, row)
        o_ref[...] = _silu(acc)

    blk = pl.BlockSpec((seq, HEAD), lambda j, b: (b, j))
    return pl.pallas_call(
        body, name="conv_silu_fwd", grid=(c // HEAD, batch), in_specs=[blk, pl.BlockSpec((CONV_WIDTH, HEAD), lambda j, b: (0, j))],
        out_specs=blk, out_shape=SDS((t, c), F32), compiler_params=_params(2))(x, w)


def _conv_silu_bwd(x, w, dy, *, batch, seq):
    t, c = x.shape

    def body(x_ref, w_ref, dy_ref, dx_ref, dw_ref):
        @pl.when(pl.program_id(1) == 0)
        def _():
            dw_ref[...] = jnp.zeros_like(dw_ref)

        xv = x_ref[...]
        row = lax.broadcasted_iota(jnp.int32, xv.shape, 0)
        shifted = [_shift_down(xv, CONV_WIDTH - 1 - i, row) for i in range(CONV_WIDTH)]
        acc = jnp.zeros_like(xv)
        for i in range(CONV_WIDTH):
            acc = acc + w_ref[i:i + 1, :] * shifted[i]
        sg = _sigmoid(acc)
        dc = dy_ref[...] * sg * (1.0 + acc * (1.0 - sg))
        dx = jnp.zeros_like(xv)
        for i in range(CONV_WIDTH):
            dx = dx + w_ref[i:i + 1, :] * _shift_up(dc, CONV_WIDTH - 1 - i, row)
            dw_ref[i:i + 1, :] += jnp.sum(dc * shifted[i], axis=0, keepdims=True)
        dx_ref[...] = dx

    blk = pl.BlockSpec((seq, HEAD), lambda j, b: (b, j))
    wblk = pl.BlockSpec((CONV_WIDTH, HEAD), lambda j, b: (0, j))
    return pl.pallas_call(
        body, name="conv_silu_bwd", grid=(c // HEAD, batch), in_specs=[blk, wblk, blk], out_specs=[blk, wblk],
        out_shape=[SDS((t, c), F32), SDS((CONV_WIDTH, c), F32)], compiler_params=_params(2))(x, w, dy)


def _dot(a, b, mode="nn"):
    return lax.dot_general(a.astype(BF16), b.astype(BF16), _DIMS[mode], preferred_element_type=F32)


def _dot3(a, b):
    a_hi, b_hi = a.astype(BF16), b.astype(BF16)
    a_lo, b_lo = (a - a_hi.astype(F32)).astype(BF16), (b - b_hi.astype(F32)).astype(BF16)
    pass_ = lambda x, y: jnp.dot(x, y, preferred_element_type=F32)
    return pass_(a_hi, b_hi) + pass_(a_hi, b_lo) + pass_(a_lo, b_hi)


@jax.custom_vjp
def _nilpotent_inverse(m):
    r = m.shape[0]
    x = jnp.where(lax.broadcasted_iota(jnp.int32, (r, r), 0) == lax.broadcasted_iota(jnp.int32, (r, r), 1), 1.0, 0.0) + m
    p = m
    for _ in range(CHUNK_BITS - 1):
        p = _dot3(p, p)
        x = x + _dot3(x, p)
    return x


def _nilpotent_inverse_fwd(m):
    x = _nilpotent_inverse(m)
    return x, x


def _nilpotent_inverse_bwd(x, g):
    return (_dot(x, _dot(g, x, "nt"), "tn"),)


_nilpotent_inverse.defvjp(_nilpotent_inverse_fwd, _nilpotent_inverse_bwd)


def _dot_nt(a, b):
    return _dot(a, b, "nt")


def _dot_tn(a, b):
    return _dot(a, b, "tn")


def _dn_chunk(head, n_heads, aq, ak, v, z, dbda, a_log, dt_bias, dn_norm, state):
    r = aq.shape[0]
    lane_g = lax.broadcasted_iota(jnp.int32, dbda.shape, 1)
    db = jnp.sum(jnp.where(lane_g == head, dbda, 0.0), axis=1, keepdims=True)
    da = jnp.sum(jnp.where(lane_g == head + n_heads, dbda, 0.0), axis=1, keepdims=True)
    lane_h = lax.broadcasted_iota(jnp.int32, a_log.shape, 1)
    al = jnp.sum(jnp.where(lane_h == head, a_log, 0.0), axis=1, keepdims=True)
    dtb = jnp.sum(jnp.where(lane_h == head, dt_bias, 0.0), axis=1, keepdims=True)
    beta = _sigmoid(db)
    g = -jnp.exp(al) * _softplus(da + dtb)
    q = aq * lax.rsqrt(jnp.sum(aq * aq, axis=1, keepdims=True) + EPS) * (HEAD ** -0.5)
    k = ak * lax.rsqrt(jnp.sum(ak * ak, axis=1, keepdims=True) + EPS)
    ri = lax.broadcasted_iota(jnp.int32, (r, r), 0)
    ci = lax.broadcasted_iota(jnp.int32, (r, r), 1)
    same = (ri >> CHUNK_BITS) == (ci >> CHUNK_BITS)
    incl = same & (ri >= ci)
    g_row = jnp.sum(jnp.where(ri == ci, g, 0.0), axis=0, keepdims=True)
    gc_col = jnp.sum(jnp.where(incl, g_row, 0.0), axis=1, keepdims=True)
    gc_row = jnp.sum(jnp.where(same & (ri <= ci), g, 0.0), axis=0, keepdims=True)
    g_all = jnp.sum(jnp.where(same, g_row, 0.0), axis=1, keepdims=True)
    decay = jnp.where(incl, jnp.exp(jnp.where(incl, gc_col - gc_row, 0.0)), 0.0)
    kb = k * beta
    m = -jnp.where(same & (ri > ci), _dot_nt(kb, k) * decay, 0.0)
    x = _nilpotent_inverse(m)
    egc = jnp.exp(gc_col)
    wu_g = _dot(x, jnp.concatenate([kb * egc, v * beta], axis=1))
    w_g, u_g = wu_g[:, :HEAD], wu_g[:, HEAD:]
    qk = _dot_nt(q, k) * decay
    q_dec = q * egc
    k_dec = k * jnp.exp(g_all - gc_col)
    carry = jnp.exp(g_all)
    v_new, o_state = [], []
    for c in range(r // CHUNK):
        rows = slice(c * CHUNK, (c + 1) * CHUNK)
        v_new.append(u_g[rows] - _dot(w_g[rows], state))
        o_state.append(_dot(q_dec[rows], state))
        state = state * carry[c * CHUNK:c * CHUNK + 1] + _dot_tn(k_dec[rows], v_new[-1])
    o = jnp.concatenate(o_state, axis=0) + _dot(qk, jnp.concatenate(v_new, axis=0))
    o = o * lax.rsqrt(jnp.mean(o * o, axis=1, keepdims=True) + EPS) * dn_norm
    return o * _silu(z), state


def _dn_chunks(head, n_heads):
    return jax.vmap(functools.partial(_dn_chunk, head, n_heads), in_axes=(0, 0, 0, 0, 0, None, None, None, 0))


def _dn_layout(batch, seq, n_heads, small, reverse):
    grp = _tile(seq, DN_ROWS, CHUNK)
    rows = _tile(seq, DN_SEQ_BLOCK, grp)
    n_blocks, per = seq // rows, rows // grp
    at = (lambda j: n_blocks - 1 - j) if reverse else (lambda j: j)
    col = lambda off: pl.BlockSpec((batch, rows, HEAD), lambda j, h: (0, at(j), off + h))
    gates = pl.BlockSpec((batch, rows, 2 * n_heads), lambda j, h: (0, at(j), 0))
    states = pl.BlockSpec((batch, None, per, HEAD, HEAD), lambda j, h: (0, h, at(j), 0, 0))
    full = [pl.BlockSpec(a.shape, lambda j, h: (0, 0)) for a in small]
    return grp, n_blocks, per, col, gates, states, full


def _dn_fwd(y, z, dbda, a_log, dt_bias, dn_norm):
    batch, seq, dd = z.shape
    n_heads = dd // HEAD
    grp, n_blocks, per, col, gates, st_spec, full = _dn_layout(batch, seq, n_heads, (a_log, dt_bias, dn_norm), False)

    def body(q_ref, k_ref, v_ref, z_ref, g_ref, al_ref, dt_ref, nw_ref, o_ref, st_ref, carry):
        head = pl.program_id(1)
        al, dtb, nw = al_ref[...], dt_ref[...], nw_ref[...]

        @pl.when(pl.program_id(0) == 0)
        def _():
            carry[head] = jnp.zeros((batch, HEAD, HEAD), F32)

        def step(n, states):
            rows = pl.ds(pl.multiple_of(n * grp, grp), grp)
            for b in range(batch):
                st_ref[b, n] = states[b]
            out, states = _dn_chunks(head, n_heads)(q_ref[:, rows, :], k_ref[:, rows, :], v_ref[:, rows, :],
                                                    z_ref[:, rows, :], g_ref[:, rows, :], al, dtb, nw, states)
            o_ref[:, rows, :] = out
            return states

        carry[head] = lax.fori_loop(0, per, step, carry[head])

    return pl.pallas_call(
        body, name="dn_fwd", grid=(n_blocks, n_heads),
        in_specs=[col(0), col(n_heads), col(2 * n_heads), col(0), gates, *full], out_specs=[col(0), st_spec],
        out_shape=[SDS((batch, seq, dd), F32), SDS((batch, n_heads, seq // grp, HEAD, HEAD), F32)],
        scratch_shapes=[pltpu.VMEM((n_heads, batch, HEAD, HEAD), F32)], compiler_params=_params(2),
    )(y, y, y, z, dbda, a_log, dt_bias, dn_norm)


def _dn_bwd(y, z, dbda, a_log, dt_bias, dn_norm, states, d_out):
    batch, seq, dd = z.shape
    n_heads = dd // HEAD
    grp, n_blocks, per, col, gates, st_spec, full = _dn_layout(batch, seq, n_heads, (a_log, dt_bias, dn_norm), True)

    def body(q_ref, k_ref, v_ref, z_ref, g_ref, al_ref, dt_ref, nw_ref, st_ref, do_ref,
             dq_ref, dk_ref, dv_ref, dz_ref, dg_ref, dal_ref, ddt_ref, dnw_ref, carry):
        first, head = pl.program_id(0) == 0, pl.program_id(1)
        al, dtb, nw = al_ref[...], dt_ref[...], nw_ref[...]

        @pl.when(first & (head == 0))
        def _():
            dal_ref[...] = jnp.zeros_like(dal_ref)
            ddt_ref[...] = jnp.zeros_like(ddt_ref)
            dnw_ref[...] = jnp.zeros_like(dnw_ref)

        @pl.when(head == 0)
        def _():
            dg_ref[...] = jnp.zeros_like(dg_ref)

        @pl.when(first)
        def _():
            carry[head] = jnp.zeros((batch, HEAD, HEAD), F32)

        def step(i, acc):
            d_states, d_al, d_dt, d_nw = acc
            n = per - 1 - i
            rows = pl.ds(pl.multiple_of(n * grp, grp), grp)
            states = jnp.stack([st_ref[b, n] for b in range(batch)])
            _, vjp = jax.vjp(_dn_chunks(head, n_heads), q_ref[:, rows, :], k_ref[:, rows, :], v_ref[:, rows, :],
                             z_ref[:, rows, :], g_ref[:, rows, :], al, dtb, nw, states)
            gq, gk, gv, gz, gg, gal, gdt, gnw, d_states = vjp((do_ref[:, rows, :], d_states))
            dq_ref[:, rows, :] = gq
            dk_ref[:, rows, :] = gk
            dv_ref[:, rows, :] = gv
            dz_ref[:, rows, :] = gz
            dg_ref[:, rows, :] += gg
            return d_states, d_al + gal, d_dt + gdt, d_nw + gnw

        zero = lambda a: jnp.zeros(a.shape, F32)
        d_states, d_al, d_dt, d_nw = lax.fori_loop(0, per, step, (carry[head], zero(al), zero(dtb), zero(nw)))
        carry[head] = d_states
        dal_ref[...] += d_al
        ddt_ref[...] += d_dt
        dnw_ref[...] += d_nw

    out3 = SDS((batch, seq, dd), F32)
    return pl.pallas_call(
        body, name="dn_bwd", grid=(n_blocks, n_heads),
        in_specs=[col(0), col(n_heads), col(2 * n_heads), col(0), gates, *full, st_spec, col(0)],
        out_specs=[col(0), col(0), col(0), col(0), gates, *full],
        out_shape=[out3] * 4 + [SDS(dbda.shape, F32), SDS(a_log.shape, F32), SDS(dt_bias.shape, F32), SDS(dn_norm.shape, F32)],
        scratch_shapes=[pltpu.VMEM((n_heads, batch, HEAD, HEAD), F32)], compiler_params=_params(2),
    )(y, y, y, z, dbda, a_log, dt_bias, dn_norm, states, d_out)


def _my_slot():
    return 4 * lax.axis_index("x") + 2 * lax.axis_index("y") + lax.axis_index("c")


def _peer(k):
    x, y, c = lax.axis_index("x"), lax.axis_index("y"), lax.axis_index("c")
    return (x ^ (k >> 2), y ^ ((k >> 1) & 1), c ^ (k & 1)), (4 * x + 2 * y + c) ^ k


def _all_gather(name, block, after):
    def body(src, after_ref, dst, send_sems, recv_sems, local_sem):
        me = _my_slot()
        own = pltpu.make_async_copy(src, dst.at[me], local_sem)
        own.start()
        copies = []
        for k in range(1, N_DEV):
            peer, _ = _peer(k)
            copies.append(pltpu.make_async_remote_copy(
                src_ref=src, dst_ref=dst.at[me], send_sem=send_sems.at[k - 1], recv_sem=recv_sems.at[k - 1],
                device_id=peer, device_id_type=MESH))
            copies[-1].start()
        for k in range(1, N_DEV):
            peer, slot = _peer(k)
            pltpu.make_async_remote_copy(
                src_ref=src, dst_ref=dst.at[slot], send_sem=send_sems.at[k - 1], recv_sem=recv_sems.at[k - 1],
                device_id=peer, device_id_type=MESH).wait_recv()
        for cp in copies:
            cp.wait_send()
        own.wait()

    return pl.pallas_call(
        body, name=name, in_specs=[pl.BlockSpec(memory_space=pl.ANY)] * 2, out_specs=pl.BlockSpec(memory_space=pl.ANY),
        out_shape=SDS((N_DEV, *block.shape), block.dtype),
        scratch_shapes=[pltpu.SemaphoreType.DMA((N_DEV - 1,)), pltpu.SemaphoreType.DMA((N_DEV - 1,)), pltpu.SemaphoreType.DMA],
    )(block, after)


def _exchange_slices(name, parts, after):
    def body(src, after_ref, dst, send_sems, recv_sems, local_sem):
        me = _my_slot()
        own = pltpu.make_async_copy(src.at[me], dst.at[me], local_sem)
        own.start()
        copies = []
        for k in range(1, N_DEV):
            peer, slot = _peer(k)
            copies.append(pltpu.make_async_remote_copy(
                src_ref=src.at[slot], dst_ref=dst.at[me], send_sem=send_sems.at[k - 1], recv_sem=recv_sems.at[k - 1],
                device_id=peer, device_id_type=MESH))
            copies[-1].start()
        for k in range(1, N_DEV):
            peer, slot = _peer(k)
            pltpu.make_async_remote_copy(
                src_ref=src.at[me], dst_ref=dst.at[slot], send_sem=send_sems.at[k - 1], recv_sem=recv_sems.at[k - 1],
                device_id=peer, device_id_type=MESH).wait_recv()
        for cp in copies:
            cp.wait_send()
        own.wait()

    return pl.pallas_call(
        body, name=name, in_specs=[pl.BlockSpec(memory_space=pl.ANY)] * 2, out_specs=pl.BlockSpec(memory_space=pl.ANY),
        out_shape=SDS(parts.shape, parts.dtype),
        scratch_shapes=[pltpu.SemaphoreType.DMA((N_DEV - 1,)), pltpu.SemaphoreType.DMA((N_DEV - 1,)), pltpu.SemaphoreType.DMA],
    )(parts, after)


_HBM = pl.BlockSpec(memory_space=pltpu.HBM)
_SEM = pl.BlockSpec(memory_space=pltpu.SEMAPHORE)
_EFFECT = pltpu.SideEffectType.DATAFLOW_SIDE_EFFECTING


def _slot_operand():
    return _my_slot().astype(jnp.int32).reshape(1)


def _col_tile(r, c, bytes_per_element):
    return _tile(c, max(128, (12 * 2 ** 20) // (bytes_per_element * r) // 128 * 128), 128)


def _cast_place(name, block, dtype):
    r, c = block.shape
    tc = _col_tile(r, c, 6)

    def body(me_ref, src_ref, dst_ref):
        dst_ref[...] = src_ref[...].astype(dtype)

    return pl.pallas_call(
        body, name=name, out_shape=SDS((N_DEV, r, c), dtype), compiler_params=_params(1),
        grid_spec=pltpu.PrefetchScalarGridSpec(
            num_scalar_prefetch=1, grid=(c // tc,), in_specs=[pl.BlockSpec((r, tc), lambda i, me: (0, i))],
            out_specs=pl.BlockSpec((None, r, tc), lambda i, me: (me[0], 0, i))),
    )(_slot_operand(), block)


_SIBLING = 1
_SAME_CORE = (2, 4, 6)
_OTHER_CORE = (3, 5, 7)


def _gather_start(name, lands):
    n = len(lands)

    def body(*refs):
        lnds, outs = refs[:n], refs[n:]
        me = _my_slot()
        for i in range(n):
            for k in (*_SAME_CORE, _SIBLING):
                peer, _ = _peer(k)
                pltpu.make_async_remote_copy(
                    src_ref=lnds[i].at[me], dst_ref=lnds[i].at[me], send_sem=outs[2 * i].at[k - 1],
                    recv_sem=outs[2 * i + 1].at[k - 1], device_id=peer, device_id_type=MESH).start()
        outs[-1][...] = jnp.zeros_like(outs[-1])

    res = pl.pallas_call(
        body, name=name, in_specs=[_HBM] * n,
        out_specs=[_SEM] * (2 * n) + [_HBM] * n + [pl.BlockSpec(memory_space=pltpu.VMEM)],
        out_shape=[pltpu.SemaphoreType.DMA((N_DEV - 1,))] * (2 * n) + [pltpu.HBM(a.shape, a.dtype) for a in lands]
        + [SDS((8, 128), F32)],
        input_output_aliases={i: 2 * n + i for i in range(n)},
        compiler_params=pltpu.CompilerParams(has_side_effects=_EFFECT),
    )(*[pltpu.with_memory_space_constraint(a, pltpu.HBM) for a in lands])
    return [(res[2 * i], res[2 * i + 1], res[2 * n + i]) for i in range(n)], res[-1]


def _gather_copy(land_ref, send_ref, recv_ref, k, slot, to):
    return pltpu.make_async_remote_copy(
        src_ref=land_ref.at[slot], dst_ref=land_ref.at[slot], send_sem=send_ref.at[k - 1], recv_sem=recv_ref.at[k - 1],
        device_id=to, device_id_type=MESH)


def _gather_arrived(name, started, after):
    send_sems, recv_sems, land = started

    def body(land_ref, send_ref, recv_ref, after_ref, land_out):
        for k in _SAME_CORE:
            peer, slot = _peer(k)
            _gather_copy(land_ref, send_ref, recv_ref, k, slot, peer).wait_recv()

    return pl.pallas_call(
        body, name=name, in_specs=[_HBM, _SEM, _SEM, pl.BlockSpec(memory_space=pl.ANY)], out_specs=[_HBM],
        out_shape=[pltpu.HBM(land.shape, land.dtype)], input_output_aliases={0: 0},
        compiler_params=pltpu.CompilerParams(has_side_effects=_EFFECT),
    )(land, send_sems, recv_sems, after)[0]


def _gather_forward(name, land):
    def body(land_ref, send_ref, recv_ref, land_out, token):
        sibling, _ = _peer(_SIBLING)
        for j, k in enumerate(_SAME_CORE):
            _, slot = _peer(k)
            _gather_copy(land_ref, send_ref, recv_ref, j + 1, slot, sibling).start()
        token[...] = jnp.zeros_like(token)

    res = pl.pallas_call(
        body, name=name, in_specs=[_HBM], out_specs=[_SEM, _SEM, _HBM, pl.BlockSpec(memory_space=pltpu.VMEM)],
        out_shape=[pltpu.SemaphoreType.DMA((len(_SAME_CORE),))] * 2 + [pltpu.HBM(land.shape, land.dtype), SDS((8, 128), F32)],
        input_output_aliases={0: 2}, compiler_params=pltpu.CompilerParams(has_side_effects=_EFFECT),
    )(land)
    return tuple(res[:3]), res[3]


def _gather_wait(name, started, forwarded, after):
    send_sems, recv_sems, _ = started
    send_fwd, recv_fwd, land = forwarded

    def body(land_ref, send_ref, recv_ref, send2_ref, recv2_ref, after_ref, land_out):
        sibling, slot = _peer(_SIBLING)
        _gather_copy(land_ref, send_ref, recv_ref, _SIBLING, slot, sibling).wait_recv()
        for j, k in enumerate(_OTHER_CORE):
            _, slot = _peer(k)
            _gather_copy(land_ref, send2_ref, recv2_ref, j + 1, slot, sibling).wait_recv()
        for k in (_SIBLING, *_SAME_CORE):
            peer, slot = _peer(k)
            _gather_copy(land_ref, send_ref, recv_ref, k, slot, peer).wait_send()
        for j, k in enumerate(_SAME_CORE):
            _, slot = _peer(k)
            _gather_copy(land_ref, send2_ref, recv2_ref, j + 1, slot, sibling).wait_send()

    return pl.pallas_call(
        body, name=name, in_specs=[_HBM, _SEM, _SEM, _SEM, _SEM, pl.BlockSpec(memory_space=pl.ANY)], out_specs=[_HBM],
        out_shape=[pltpu.HBM(land.shape, land.dtype)], input_output_aliases={0: 0},
        compiler_params=pltpu.CompilerParams(has_side_effects=_EFFECT),
    )(land, send_sems, recv_sems, send_fwd, recv_fwd, after)[0]


def _scatter_start(name, parts):
    land = lax.empty(parts.shape, parts.dtype)

    def body(src, lnd, send_ref, recv_ref, src_out, lnd_out, token):
        me = _my_slot()
        for k in range(1, N_DEV):
            peer, slot = _peer(k)
            pltpu.make_async_remote_copy(
                src_ref=src.at[slot], dst_ref=lnd.at[me], send_sem=send_ref.at[k - 1], recv_sem=recv_ref.at[k - 1],
                device_id=peer, device_id_type=MESH).start()
        token[...] = jnp.zeros_like(token)

    res = pl.pallas_call(
        body, name=name, in_specs=[_HBM, _HBM],
        out_specs=[_SEM, _SEM, _HBM, _HBM, pl.BlockSpec(memory_space=pltpu.VMEM)],
        out_shape=[pltpu.SemaphoreType.DMA((N_DEV - 1,))] * 2 + [pltpu.HBM(parts.shape, parts.dtype)] * 2 + [SDS((8, 128), F32)],
        input_output_aliases={0: 2, 1: 3}, compiler_params=pltpu.CompilerParams(has_side_effects=_EFFECT),
    )(pltpu.with_memory_space_constraint(parts, pltpu.HBM), pltpu.with_memory_space_constraint(land, pltpu.HBM))
    return tuple(res[:4]), res[4]


def _scatter_wait(name, started, after):
    send_sems, recv_sems, parts, land = started

    def body(src_ref, land_ref, send_ref, recv_ref, after_ref, src_out, land_out):
        me = _my_slot()
        for k in range(1, N_DEV):
            peer, slot = _peer(k)
            copy = pltpu.make_async_remote_copy(
                src_ref=src_ref.at[me], dst_ref=land_ref.at[slot], send_sem=send_ref.at[k - 1],
                recv_sem=recv_ref.at[k - 1], device_id=peer, device_id_type=MESH)
            copy.wait_send()
            copy.wait_recv()

    return pl.pallas_call(
        body, name=name, in_specs=[_HBM, _HBM, _SEM, _SEM, pl.BlockSpec(memory_space=pl.ANY)], out_specs=[_HBM, _HBM],
        out_shape=[pltpu.HBM(parts.shape, parts.dtype), pltpu.HBM(land.shape, land.dtype)], input_output_aliases={0: 0, 1: 1},
        compiler_params=pltpu.CompilerParams(has_side_effects=_EFFECT),
    )(parts, land, send_sems, recv_sems, after)


def _adamw(name, landed, own, w, m, v):
    r, c = w.shape
    tc = _col_tile(r, c, 46)
    bc1 = 1.0 / (1.0 - ADAM_B1 ** ADAM_STEP)
    bc2 = 1.0 / (1.0 - ADAM_B2 ** ADAM_STEP)

    def body(me_ref, p_ref, own_ref, w_ref, m_ref, v_ref, g_ref, d_ref, nm_ref, nv_ref):
        me = me_ref[0]
        g = jnp.zeros(w_ref.shape, F32)
        for s in range(N_DEV):
            g = g + jnp.where(me == s, own_ref[...], p_ref[s]).astype(F32)
        nm = ADAM_B1 * m_ref[...] + (1.0 - ADAM_B1) * g
        nv = ADAM_B2 * v_ref[...] + (1.0 - ADAM_B2) * (g * g)
        g_ref[...] = g
        nm_ref[...] = nm
        nv_ref[...] = nv
        d_ref[...] = -ADAM_LR * ((nm * bc1) / (jnp.sqrt(nv * bc2) + ADAM_EPS) + ADAM_WD * w_ref[...])

    blk = pl.BlockSpec((r, tc), lambda i, me: (0, i))
    return pl.pallas_call(
        body, name=name, out_shape=[SDS((r, c), F32)] * 4, compiler_params=_params(1),
        grid_spec=pltpu.PrefetchScalarGridSpec(
            num_scalar_prefetch=1, grid=(c // tc,),
            in_specs=[pl.BlockSpec((N_DEV, r, tc), lambda i, me: (0, 0, i)), pl.BlockSpec((None, r, tc), lambda i, me: (me[0], 0, i)),
                      blk, blk, blk],
            out_specs=[blk] * 4),
    )(_slot_operand(), landed, own, w, m, v)


def _ffn_fwd(name, h, norm, fetch, prefetch, landed, ahead=None):
    n = _rms_fwd(name + "_norm", h, norm)
    wg = fetch(name + "_w_gate", n)
    gate = _ffn_proj(name + "_gate", n, wg)
    sent = prefetch(name + "_w_down", gate) if landed else None
    wu = fetch(name + "_w_up", gate)
    up, act = _ffn_proj(name + "_up", n, wu, gate, after=sent)
    sent = prefetch(ahead, act) if ahead else None
    wd = fetch(name + "_w_down", act)
    return _ffn_down(act, wd, h, 0.5, after=sent), (n, gate, up, act, wg, wu, wd)


def _ffn_bwd(name, h, norm, saved, dh, dy_b, scale_out, emit):
    n, gate, up, act, wg, wu, wd = saved
    sent = emit(name + "_w_down", _wgrad_rows(name + "_dwd", act, dy_b))
    d_gate, d_up = _ffn_bwd_act(dy_b, wd, gate, up, after=sent)
    sent = emit(name + "_w_gate", _wgrad_rows(name + "_dwg", d_gate, n, after=sent))
    sent = emit(name + "_w_up", _wgrad_rows(name + "_dwu", d_up, n, after=sent))
    dn = _dgrad_cols(name + "_dn", (d_gate, d_up), (wg, wu), after=sent)
    return _rms_bwd(name + "_norm_bwd", h, dn, norm, dh, scale_out)


def _local_step(x, target, norms, small, fetch, prefetch, emit, *, batch, seq):
    n1w, nmw, n2w, nfw = norms
    a_log, dt_bias, dn_norm = small
    t, d = x.shape

    h1, saved1 = _ffn_fwd("ffn1", x, n1w, fetch, prefetch, False)
    nm = _rms_fwd("mix_norm", h1, nmw)
    w_in = fetch("w_in", nm)
    p = w_in.shape[1]
    proj = _in_proj(nm, w_in)
    conv_all = fetch("conv_w", proj)
    proj = jnp.swapaxes(proj, 0, 1).reshape(t, N_DEV * p)
    conv_w = jnp.swapaxes(conv_all, 0, 1).reshape(CONV_WIDTH, N_DEV * conv_all.shape[2])
    dd = conv_w.shape[1] // 3
    da = (N_DEV * p - 4 * dd - 2 * (dd // HEAD)) // 3
    qkv = proj[:, :3 * da].astype(BF16)
    xd = proj[:, 3 * da:3 * da + 3 * dd]
    z = proj[:, 3 * da + 3 * dd:3 * da + 4 * dd]
    dbda = proj[:, 3 * da + 4 * dd:]
    attn, lse = _attn_fwd(qkv, batch=batch, seq=seq)
    yd = _conv_silu_fwd(xd, conv_w, batch=batch, seq=seq)
    per_seq = lambda a: a.reshape(batch, seq, a.shape[1])
    dn_in = (per_seq(yd), per_seq(z), per_seq(dbda), a_log, dt_bias, dn_norm)
    dn_out, dn_states = _dn_fwd(*dn_in)
    cat = jnp.concatenate([attn, dn_out.reshape(t, dd)], axis=1).astype(BF16)
    w_out = fetch("w_out", cat)
    w_out2 = w_out.reshape(da + dd, d)
    sent = prefetch("ffn2_w_up", prefetch("ffn2_w_gate", cat))
    h2 = _out_proj(cat, w_out2, h1, after=sent)
    h3, saved2 = _ffn_fwd("ffn2", h2, n2w, fetch, prefetch, True)

    loss, dh3, dh3_b, g_nf = _loss_head(h3, nfw, target, 0.5)
    dh2, dh2_b, g_n2 = _ffn_bwd("ffn2", h2, n2w, saved2, dh3, dh3_b, 1.0, emit)

    sent = emit("w_out", _wgrad_full("dw_out", cat, dh2_b).reshape(w_out.shape))
    dcat = _dgrad_full("d_cat", dh2_b, w_out2, after=sent)
    d_attn, d_dn = dcat[:, :da], dcat[:, da:]
    dq, dk, dv = _attn_bwd(qkv, attn, lse, d_attn, batch=batch, seq=seq)
    gq, gk, gv, gz, g_dbda, g_alog, g_dtb, g_dnn = _dn_bwd(*dn_in, dn_states, per_seq(d_dn))
    gq, gk, gv, gz, g_dbda = (a.reshape(t, a.shape[2]) for a in (gq, gk, gv, gz, g_dbda))
    d_xd, g_conv = _conv_silu_bwd(xd, conv_w, jnp.concatenate([gq, gk, gv], axis=1), batch=batch, seq=seq)
    dproj = jnp.concatenate([dq, dk, dv, d_xd, gz, g_dbda], axis=1).astype(BF16)
    dproj = jnp.swapaxes(dproj.reshape(t, N_DEV, p), 0, 1)
    sent = emit("w_in", _wgrad_rows("dw_in", dproj, nm))
    dnm = _dgrad_cols("d_mix_in", (dproj,), (w_in,), after=sent)
    dh1, dh1_b, g_nm = _rms_bwd("mix_norm_bwd", h1, dnm, nmw, dh2, 0.5)

    dx, _, g_n1 = _ffn_bwd("ffn1", x, n1w, saved1, dh1, dh1_b, 1.0, emit)
    return loss, dx, (g_n1, g_nm, g_n2, g_nf), (g_alog, g_dtb, g_dnn), g_conv


def _pack_rows(vectors):
    rows, offsets, r = [], [], 0
    for vec in vectors:
        n = -(-vec.size // 128)
        rows.append(jnp.pad(vec.reshape(-1), (0, n * 128 - vec.size)).reshape(n, 128))
        offsets.append((r, vec.size, vec.shape))
        r += n
    pad = -r % 8
    if pad:
        rows.append(jnp.zeros((pad, 128), F32))
    return jnp.concatenate(rows, axis=0), offsets


def _unpack_rows(packed, offsets):
    return [packed[r:r + -(-size // 128)].reshape(-1)[:size].reshape(shape) for r, size, shape in offsets]


def kernel(x, ffn1_norm, ffn1_w_gate, ffn1_w_up, ffn1_w_down, mix_norm, w_in, conv_w, a_log, dt_bias, dn_norm, w_out, ffn2_norm, ffn2_w_gate, ffn2_w_up, ffn2_w_down, final_norm, loss_target, m_ffn1_norm, m_ffn1_w_gate, m_ffn1_w_up, m_ffn1_w_down, m_mix_norm, m_w_in, m_conv_w, m_a_log, m_dt_bias, m_dn_norm, m_w_out, m_ffn2_norm, m_ffn2_w_gate, m_ffn2_w_up, m_ffn2_w_down, m_final_norm, v_ffn1_norm, v_ffn1_w_gate, v_ffn1_w_up, v_ffn1_w_down, v_mix_norm, v_w_in, v_conv_w, v_a_log, v_dt_bias, v_dn_norm, v_w_out, v_ffn2_norm, v_ffn2_w_gate, v_ffn2_w_up, v_ffn2_w_down, v_final_norm):
    batch, seq, d = x.shape
    t = batch * seq
    big = dict(ffn1_w_gate=(ffn1_w_gate, m_ffn1_w_gate, v_ffn1_w_gate), ffn1_w_up=(ffn1_w_up, m_ffn1_w_up, v_ffn1_w_up),
               ffn1_w_down=(ffn1_w_down, m_ffn1_w_down, v_ffn1_w_down), w_in=(w_in, m_w_in, v_w_in),
               w_out=(w_out, m_w_out, v_w_out), ffn2_w_gate=(ffn2_w_gate, m_ffn2_w_gate, v_ffn2_w_gate),
               ffn2_w_up=(ffn2_w_up, m_ffn2_w_up, v_ffn2_w_up), ffn2_w_down=(ffn2_w_down, m_ffn2_w_down, v_ffn2_w_down))
    by_columns = ("ffn1_w_gate", "ffn1_w_up", "w_in", "ffn2_w_gate", "ffn2_w_up")
    for name in by_columns:
        big[name] = tuple(a.T for a in big[name])
    rep = dict(ffn1_norm=(ffn1_norm, m_ffn1_norm, v_ffn1_norm), mix_norm=(mix_norm, m_mix_norm, v_mix_norm),
               ffn2_norm=(ffn2_norm, m_ffn2_norm, v_ffn2_norm), final_norm=(final_norm, m_final_norm, v_final_norm),
               a_log=(a_log, m_a_log, v_a_log), dt_bias=(dt_bias, m_dt_bias, v_dt_bias), dn_norm=(dn_norm, m_dn_norm, v_dn_norm))

    lands = {"conv_w": _cast_place("place_conv_w", conv_w, F32)}
    lands.update({name: _cast_place("place_" + name, w, BF16) for name, (w, _, _) in big.items()})
    started, token = _gather_start("gather_start", list(lands.values()))
    gathering = dict(zip(lands, started))
    gathered, scattering = {}, {}

    forwarding = {}

    def prefetch(name, after):
        if name not in forwarding:
            land = _gather_arrived("gather_arrived_" + name, gathering[name], after)
            forwarding[name] = _gather_forward("gather_forward_" + name, land)
        return forwarding[name][1]

    def fetch(name, after):
        if name not in gathered:
            prefetch(name, after)
            gathered[name] = _gather_wait("gather_wait_" + name, gathering[name], forwarding[name][0], after)
        return gathered[name]

    def emit(name, grad):
        scattering[name], sent = _scatter_start("scatter_start_" + name, grad)
        return sent

    row = lambda a: a.reshape(1, -1)
    norms = [row(rep[n][0]) for n in ("ffn1_norm", "mix_norm", "ffn2_norm", "final_norm")]
    norms[0] = norms[0] + token[0, 0]
    loss, dx, g_norms, g_small, g_conv = _local_step(
        x.reshape(t, d), loss_target.reshape(t, d), norms, [row(rep[n][0]) for n in ("a_log", "dt_bias", "dn_norm")],
        fetch, prefetch, emit, batch=batch, seq=seq)

    out = {"grad_x": dx.reshape(x.shape)}
    after = dx
    for name in scattering:
        w, m, v = big[name]
        own, landed = _scatter_wait("scatter_wait_" + name, scattering[name], after)
        res = _adamw("adamw_" + name, landed, own, w, m, v)
        after = res[0]
        out["grad_" + name], out["delta_" + name], out["new_m_" + name], out["new_v_" + name] = (
            [a.T for a in res] if name in by_columns else res)
    conv_parts = jnp.swapaxes(g_conv.reshape(CONV_WIDTH, N_DEV, conv_w.shape[1]), 0, 1)
    parts = _exchange_slices("scatter_conv_w", conv_parts, after)
    out["grad_conv_w"], out["delta_conv_w"], out["new_m_conv_w"], out["new_v_conv_w"] = _adamw("adamw_conv_w", parts, parts, conv_w, m_conv_w, v_conv_w)

    rep_names = list(rep)
    g_rep = [*g_norms, *g_small]
    packed_g, offsets = _pack_rows([*g_rep, loss[:, :1]])
    packed = [_pack_rows([*[rep[n][i] for n in rep_names], jnp.zeros((1, 1), F32)])[0] for i in range(3)]
    parts = _all_gather("gather_small_grads", packed_g, out["grad_conv_w"])
    res = [_unpack_rows(a, offsets) for a in _adamw("adamw_small", parts, parts, *packed)]
    for i, name in enumerate(rep_names):
        shape = rep[name][0].shape
        out["grad_" + name], out["delta_" + name], out["new_m_" + name], out["new_v_" + name] = (r[i].reshape(shape) for r in res)
    out["loss"] = res[0][-1].reshape(())

    order = ["ffn1_norm", "ffn1_w_gate", "ffn1_w_up", "ffn1_w_down", "mix_norm", "w_in", "conv_w", "a_log", "dt_bias", "dn_norm",
             "w_out", "ffn2_norm", "ffn2_w_gate", "ffn2_w_up", "ffn2_w_down", "final_norm"]
    return (out["loss"], out["grad_x"], *[out["grad_" + n] for n in order], *[out["delta_" + n] for n in order],
            *[out["new_m_" + n] for n in order], *[out["new_v_" + n] for n in order])
```

```python
import functools

import jax
import jax.numpy as jnp
from jax import lax
from jax.experimental import pallas as pl
from jax.experimental.pallas import tpu as pltpu

F32 = jnp.float32
BF16 = jnp.bfloat16
N_DEV = 8
HEAD = 128
CHUNK = 64
CHUNK_BITS = 6
DN_ROWS = 256
DN_SEQ_BLOCK = 1024
CONV_WIDTH = 4
EPS = 1e-6
DILATED_CONFIGS = ((128, 1), (512, 4), (2048, 16))
ATTN_BLOCK = 1024
NEG = -1e30
ADAM_LR, ADAM_B1, ADAM_B2, ADAM_EPS, ADAM_WD, ADAM_STEP = 0.001, 0.9, 0.999, 1e-08, 0.01, 10
MESH = pl.DeviceIdType.MESH
SDS = jax.ShapeDtypeStruct


def _tile(n, pref, align):
    t = (min(n, pref) // align) * align
    while t >= align:
        if n % t == 0:
            return t
        t -= align
    return n


def _params(n_axes, vmem_mb=48):
    return pltpu.CompilerParams(dimension_semantics=("arbitrary",) * n_axes, vmem_limit_bytes=vmem_mb * 2 ** 20)


def _sigmoid(x):
    return 1.0 / (1.0 + jnp.exp(-x))


def _silu(x):
    return x * _sigmoid(x)


def _softplus(x):
    return jnp.maximum(x, 0.0) + jnp.log(1.0 + jnp.exp(-jnp.abs(x)))


_DIMS = {"nn": (((1,), (0,)), ((), ())), "nt": (((1,), (1,)), ((), ())), "tn": (((0,), (0,)), ((), ()))}


def _mm_call(name, grid, mode, pairs, operands, in_specs, out_shape, out_specs, acc_shapes, epilogue, vmem_mb=48, after=None):
    dims = _DIMS[mode]
    if after is not None:
        operands, in_specs = (*operands, after), [*in_specs, pl.BlockSpec(memory_space=pl.ANY)]
    n_in, n_out = len(operands), len(out_shape)
    nk = grid[-1]

    def whole(*refs):
        ins, outs = refs[:n_in], refs[n_in:]
        sums = {}
        for a, b, c in pairs:
            prod = lax.dot_general(ins[a][...], ins[b][...], dims, preferred_element_type=F32)
            sums[c] = prod if c not in sums else sums[c] + prod
        epilogue(ins, outs, [sums[c] for c in sorted(sums)])

    if acc_shapes is None:
        return pl.pallas_call(
            whole, name=name, grid=grid, in_specs=in_specs, out_specs=out_specs, out_shape=out_shape,
            compiler_params=_params(len(grid), vmem_mb))(*operands)

    def body(*refs):
        ins, outs, accs = refs[:n_in], refs[n_in:n_in + n_out], refs[n_in + n_out:]
        k = pl.program_id(len(grid) - 1)

        @pl.when(k == 0)
        def _():
            for acc in accs:
                acc[...] = jnp.zeros_like(acc)

        sums = {}
        for a, b, c in pairs:
            prod = lax.dot_general(ins[a][...], ins[b][...], dims, preferred_element_type=F32)
            sums[c] = prod if c not in sums else sums[c] + prod
        for c, total in sums.items():
            accs[c][...] += total

        @pl.when(k == nk - 1)
        def _():
            epilogue(ins, outs, [acc[...] for acc in accs])

    return pl.pallas_call(
        body, name=name, grid=grid, in_specs=in_specs, out_specs=out_specs, out_shape=out_shape,
        scratch_shapes=[pltpu.VMEM(s, F32) for s in acc_shapes], compiler_params=_params(len(grid), vmem_mb),
    )(*operands)


def _ffn_proj(name, n, w, gate=None, after=None):
    t, d = n.shape
    f = w.shape[1]
    tm = _tile(t, 256, 16)
    n_spec = pl.BlockSpec((tm, d), lambda s, m, k: (m, 0))
    w_spec = pl.BlockSpec((None, f, d), lambda s, m, k: (s, 0, 0))
    o_spec = pl.BlockSpec((None, tm, f), lambda s, m, k: (s, m, 0))
    o_shape = SDS((N_DEV, t, f), BF16)
    grid = (N_DEV, t // tm, 1)
    if gate is None:
        return _mm_call(name, grid, "nt", [(0, 1, 0)], (n, w), [n_spec, w_spec], [o_shape], [o_spec], None, _store_bf16,
                        after=after)[0]

    def up_out(ins, outs, accs):
        outs[0][...] = accs[0].astype(BF16)
        outs[1][...] = (_silu(ins[2][...].astype(F32)) * accs[0]).astype(BF16)

    return _mm_call(name, grid, "nt", [(0, 1, 0)], (n, w, gate), [n_spec, w_spec, o_spec], [o_shape] * 2, [o_spec] * 2,
                    None, up_out, after=after)


def _ffn_down(act, wd, resid, scale, after=None):
    _, t, f = act.shape
    d = wd.shape[2]
    tm, tn = _tile(t, 1024, 16), _tile(d, 1024, 128)

    def epilogue(ins, outs, accs):
        outs[0][...] = ins[2][...] + scale * accs[0]

    rc = pl.BlockSpec((tm, tn), lambda m, n, s: (m, n))
    return _mm_call(
        "ffn_down", (t // tm, d // tn, N_DEV), "nn", [(0, 1, 0)], (act, wd, resid),
        [pl.BlockSpec((None, tm, f), lambda m, n, s: (s, m, 0)), pl.BlockSpec((None, f, tn), lambda m, n, s: (s, 0, n)), rc],
        [SDS((t, d), F32)], [rc], [(tm, tn)], epilogue, after=after)[0]


def _in_proj(n, w):
    t, d = n.shape
    p = w.shape[1]
    tm = _tile(t, 256, 16)
    return _mm_call(
        "in_proj", (N_DEV, t // tm, 1), "nt", [(0, 1, 0)], (n, w),
        [pl.BlockSpec((tm, d), lambda s, m, k: (m, 0)), pl.BlockSpec((None, p, d), lambda s, m, k: (s, 0, 0))],
        [SDS((N_DEV, t, p), F32)], [pl.BlockSpec((None, tm, p), lambda s, m, k: (s, m, 0))], None, _store_f32)[0]


def _out_proj(cat, w, resid, after=None):
    t, dm = cat.shape
    d = w.shape[1]
    tm, tn = _tile(t, 512, 16), _tile(d, 1024, 128)

    def epilogue(ins, outs, accs):
        outs[0][...] = ins[2][...] + accs[0]

    rc = pl.BlockSpec((tm, tn), lambda n, m, k: (m, n))
    return _mm_call(
        "out_proj", (d // tn, t // tm, 1), "nn", [(0, 1, 0)], (cat, w, resid),
        [pl.BlockSpec((tm, dm), lambda n, m, k: (m, 0)), pl.BlockSpec((dm, tn), lambda n, m, k: (0, n)), rc],
        [SDS((t, d), F32)], [rc], None, epilogue, after=after)[0]


def _ffn_bwd_act(dy, wd, gate, up, after=None):
    t, d = dy.shape
    f = wd.shape[1]
    tm = _tile(t, 256, 16)

    def epilogue(ins, outs, accs):
        g, u = ins[2][...].astype(F32), ins[3][...].astype(F32)
        sg = _sigmoid(g)
        outs[0][...] = (accs[0] * u * sg * (1.0 + g * (1.0 - sg))).astype(BF16)
        outs[1][...] = (accs[0] * g * sg).astype(BF16)

    o_spec = pl.BlockSpec((None, tm, f), lambda s, m, k: (s, m, 0))
    return _mm_call(
        "ffn_bwd_act", (N_DEV, t // tm, 1), "nt", [(0, 1, 0)], (dy, wd, gate, up),
        [pl.BlockSpec((tm, d), lambda s, m, k: (m, 0)), pl.BlockSpec((None, f, d), lambda s, m, k: (s, 0, 0)), o_spec, o_spec],
        [SDS((N_DEV, t, f), BF16)] * 2, [o_spec] * 2, None, epilogue, after=after)


def _store_bf16(ins, outs, accs):
    outs[0][...] = accs[0].astype(BF16)


def _store_f32(ins, outs, accs):
    outs[0][...] = accs[0]


def _wgrad_rows(name, a, b, after=None):
    _, t, m = a.shape
    n = b.shape[1]
    tn, tk = _tile(n, 512 if m <= 1408 else 256, 128), t
    return _mm_call(
        name, (N_DEV, n // tn, t // tk), "tn", [(0, 1, 0)], (a, b),
        [pl.BlockSpec((None, tk, m), lambda s, j, k: (s, k, 0)), pl.BlockSpec((tk, tn), lambda s, j, k: (k, j))],
        [SDS((N_DEV, m, n), BF16)], [pl.BlockSpec((None, m, tn), lambda s, j, k: (s, 0, j))], None, _store_bf16,
        after=after)[0]


def _wgrad_full(name, a, b, after=None):
    t, m = a.shape
    n = b.shape[1]
    tm, tn, tk = _tile(m, 512, 128), _tile(n, 1024, 128), t
    return _mm_call(
        name, (m // tm, n // tn, t // tk), "tn", [(0, 1, 0)], (a, b),
        [pl.BlockSpec((tk, tm), lambda i, j, k: (k, i)), pl.BlockSpec((tk, tn), lambda i, j, k: (k, j))],
        [SDS((m, n), BF16)], [pl.BlockSpec((tm, tn), lambda i, j, k: (i, j))], None, _store_bf16, after=after)[0]


def _dgrad_cols(name, grads, weights, after=None):
    _, t, n = grads[0].shape
    m = weights[0].shape[2]
    tm, tn = _tile(t, 1024, 16), _tile(m, 1024, 128)
    k = len(grads)
    return _mm_call(
        name, (t // tm, m // tn, N_DEV), "nn", [(i, k + i, 0) for i in range(k)], (*grads, *weights),
        [pl.BlockSpec((None, tm, n), lambda i, j, s: (s, i, 0))] * k + [pl.BlockSpec((None, n, tn), lambda i, j, s: (s, 0, j))] * k,
        [SDS((t, m), F32)], [pl.BlockSpec((tm, tn), lambda i, j, s: (i, j))], [(tm, tn)], _store_f32, after=after)[0]


def _dgrad_full(name, g, w, after=None):
    t, n = g.shape
    m = w.shape[0]
    tm, tn, tk = _tile(t, 512, 16), _tile(m, 1024, 128), n
    return _mm_call(
        name, (m // tn, t // tm, n // tk), "nt", [(0, 1, 0)], (g, w),
        [pl.BlockSpec((tm, tk), lambda j, i, k: (i, k)), pl.BlockSpec((tn, tk), lambda j, i, k: (j, k))],
        [SDS((t, m), F32)], [pl.BlockSpec((tm, tn), lambda j, i, k: (i, j))], None, _store_f32, after=after)[0]


def _rms_fwd(name, h, w):
    t, d = h.shape
    tm = _tile(t, 256, 16)

    def body(h_ref, w_ref, o_ref):
        x = h_ref[...]
        o_ref[...] = (x * lax.rsqrt(jnp.mean(x * x, axis=1, keepdims=True) + EPS) * w_ref[...]).astype(BF16)

    row = pl.BlockSpec((tm, d), lambda i: (i, 0))
    return pl.pallas_call(
        body, name=name, grid=(t // tm,), in_specs=[row, pl.BlockSpec((1, d), lambda i: (0, 0))], out_specs=row,
        out_shape=SDS((t, d), BF16), compiler_params=_params(1))(h, w)


def _rms_bwd(name, h, dn, w, dres, scale):
    t, d = h.shape
    tm = _tile(t, 128, 16)

    def body(h_ref, dn_ref, w_ref, dres_ref, dh_ref, dhb_ref, dw_ref):
        @pl.when(pl.program_id(0) == 0)
        def _():
            dw_ref[...] = jnp.zeros_like(dw_ref)

        x = h_ref[...]
        rstd = lax.rsqrt(jnp.mean(x * x, axis=1, keepdims=True) + EPS)
        nhat = x * rstd
        g = dn_ref[...]
        gw = g * w_ref[...]
        dh = dres_ref[...] + rstd * (gw - nhat * jnp.mean(gw * nhat, axis=1, keepdims=True))
        dh_ref[...] = dh
        dhb_ref[...] = (scale * dh).astype(BF16)
        dw_ref[...] += jnp.sum(g * nhat, axis=0, keepdims=True)

    row = pl.BlockSpec((tm, d), lambda i: (i, 0))
    vec = pl.BlockSpec((1, d), lambda i: (0, 0))
    return pl.pallas_call(
        body, name=name, grid=(t // tm,), in_specs=[row, row, vec, row], out_specs=[row, row, vec],
        out_shape=[SDS((t, d), F32), SDS((t, d), BF16), SDS((1, d), F32)], compiler_params=_params(1))(h, dn, w, dres)


def _loss_head(h, w, target, scale):
    t, d = h.shape
    tm = _tile(t, 128, 16)

    def body(h_ref, w_ref, tg_ref, loss_ref, dh_ref, dhb_ref, dw_ref):
        @pl.when(pl.program_id(0) == 0)
        def _():
            dw_ref[...] = jnp.zeros_like(dw_ref)
            loss_ref[...] = jnp.zeros_like(loss_ref)

        x = h_ref[...]
        rstd = lax.rsqrt(jnp.mean(x * x, axis=1, keepdims=True) + EPS)
        nhat = x * rstd
        wv = w_ref[...]
        err = nhat * wv - tg_ref[...]
        loss_ref[...] += 0.5 * jnp.sum(jnp.mean(err * err, axis=1, keepdims=True), axis=0, keepdims=True)
        g = err * (1.0 / d)
        gw = g * wv
        dh = rstd * (gw - nhat * jnp.mean(gw * nhat, axis=1, keepdims=True))
        dh_ref[...] = dh
        dhb_ref[...] = (scale * dh).astype(BF16)
        dw_ref[...] += jnp.sum(g * nhat, axis=0, keepdims=True)

    row = pl.BlockSpec((tm, d), lambda i: (i, 0))
    vec = pl.BlockSpec((1, d), lambda i: (0, 0))
    return pl.pallas_call(
        body, name="loss_head", grid=(t // tm,), in_specs=[row, vec, row],
        out_specs=[pl.BlockSpec((1, 128), lambda i: (0, 0)), row, row, vec],
        out_shape=[SDS((1, 128), F32), SDS((t, d), F32), SDS((t, d), BF16), SDS((1, d), F32)],
        compiler_params=_params(1))(h, w, target)


def _attn_bias(delta, blk):
    dist = (lax.broadcasted_iota(jnp.int32, (blk, blk), 0) - lax.broadcasted_iota(jnp.int32, (blk, blk), 1)
            + delta * blk)
    count = jnp.zeros((blk, blk), F32)
    for window, dil in DILATED_CONFIGS:
        assert dil & (dil - 1) == 0
        seen = (dist >= 0) & (dist <= window) & ((dist & (dil - 1)) == 0)
        count = count + jnp.where(seen, 1.0, 0.0)
    return jnp.where(count > 0.0, jnp.log(jnp.maximum(count, 1.0)), NEG)


def _fill_bias_table(table, blk):
    @pl.when((pl.program_id(0) == 0) & (pl.program_id(1) == 0))
    def _():
        for delta in range(table.shape[0]):
            table[delta] = _attn_bias(delta, blk)


def _attn_fwd(qkv, *, batch, seq):
    t, da3 = qkv.shape
    da = da3 // 3
    n_heads = da // HEAD
    blk = _tile(seq, ATTN_BLOCK, 16)
    nq = seq // blk
    sm_scale = HEAD ** -0.5

    def body(q_ref, k_ref, v_ref, o_ref, lse_ref, bias):
        _fill_bias_table(bias, blk)

        def q_step(qi, _):
            rows = pl.ds(pl.multiple_of(qi * blk, blk), blk)
            q = q_ref[rows, :]

            def kv_step(ki, carry):
                m, l, acc = carry
                cols = pl.ds(pl.multiple_of(ki * blk, blk), blk)
                s = lax.dot_general(q, k_ref[cols, :], _DIMS["nt"], preferred_element_type=F32) * sm_scale
                s = s + bias[qi - ki]
                m_new = jnp.maximum(m, jnp.max(s, axis=1, keepdims=True))
                alpha = jnp.exp(m - m_new)
                p = jnp.exp(s - m_new)
                l = alpha * l + jnp.sum(p, axis=1, keepdims=True)
                acc = alpha * acc + jnp.dot(p.astype(BF16), v_ref[cols, :], preferred_element_type=F32)
                return m_new, l, acc

            m, l, acc = lax.fori_loop(0, qi + 1, kv_step, (jnp.full((blk, 1), NEG, F32), jnp.zeros((blk, 1), F32),
                                                           jnp.zeros((blk, HEAD), F32)))
            o_ref[rows, :] = acc / l
            lse_ref[rows, :] = jnp.broadcast_to(m + jnp.log(l), (blk, HEAD))
            return 0

        lax.fori_loop(0, nq, q_step, 0)

    col = lambda off: pl.BlockSpec((seq, HEAD), lambda b, h: (b, off + h))
    return pl.pallas_call(
        body, name="attn_fwd", grid=(batch, n_heads), in_specs=[col(0), col(n_heads), col(2 * n_heads)],
        out_specs=[col(0), col(0)], out_shape=[SDS((t, da), F32), SDS((t, da), F32)],
        scratch_shapes=[pltpu.VMEM((nq, blk, blk), F32)], compiler_params=_params(2),
    )(qkv, qkv, qkv)


def _attn_bwd(qkv, out, lse, d_out, *, batch, seq):
    t, da = out.shape
    n_heads = da // HEAD
    blk = _tile(seq, ATTN_BLOCK, 16)
    nq = seq // blk
    sm_scale = HEAD ** -0.5

    def body(q_ref, k_ref, v_ref, o_ref, lse_ref, do_ref, dq_ref, dk_ref, dv_ref, bias):
        _fill_bias_table(bias, blk)
        dk_ref[...] = jnp.zeros_like(dk_ref)
        dv_ref[...] = jnp.zeros_like(dv_ref)

        def q_step(qi, _):
            rows = pl.ds(pl.multiple_of(qi * blk, blk), blk)
            q = q_ref[rows, :]
            do = do_ref[rows, :]
            do_b = do.astype(BF16)
            lse_q = lse_ref[rows, :][:, :1]
            delta = jnp.sum(do * o_ref[rows, :], axis=1, keepdims=True)

            def kv_step(ki, dq):
                cols = pl.ds(pl.multiple_of(ki * blk, blk), blk)
                k = k_ref[cols, :]
                s = lax.dot_general(q, k, _DIMS["nt"], preferred_element_type=F32) * sm_scale
                p = jnp.exp(s + bias[qi - ki] - lse_q)
                dp = lax.dot_general(do_b, v_ref[cols, :], _DIMS["nt"], preferred_element_type=F32)
                ds = (p * (dp - delta) * sm_scale).astype(BF16)
                dv_ref[cols, :] += lax.dot_general(p.astype(BF16), do_b, _DIMS["tn"], preferred_element_type=F32)
                dk_ref[cols, :] += lax.dot_general(ds, q, _DIMS["tn"], preferred_element_type=F32)
                return dq + jnp.dot(ds, k, preferred_element_type=F32)

            dq_ref[rows, :] = lax.fori_loop(0, qi + 1, kv_step, jnp.zeros((blk, HEAD), F32))
            return 0

        lax.fori_loop(0, nq, q_step, 0)

    col = lambda off: pl.BlockSpec((seq, HEAD), lambda b, h: (b, off + h))
    return pl.pallas_call(
        body, name="attn_bwd", grid=(batch, n_heads),
        in_specs=[col(0), col(n_heads), col(2 * n_heads), col(0), col(0), col(0)], out_specs=[col(0)] * 3,
        out_shape=[SDS((t, da), F32)] * 3, scratch_shapes=[pltpu.VMEM((nq, blk, blk), F32)], compiler_params=_params(2),
    )(qkv, qkv, qkv, out, lse, d_out)


def _shift_down(x, k, row):
    return x if k == 0 else jnp.where(row >= k, pltpu.roll(x, k, axis=0), 0.0)


def _shift_up(x, k, row):
    n = x.shape[0]
    return x if k == 0 else jnp.where(row < n - k, pltpu.roll(x, n - k, axis=0), 0.0)


def _conv_silu_fwd(x, col0, w, *, batch, seq):
    t, c = x.shape[0], w.shape[1]
    first = col0 // HEAD

    def body(x_ref, w_ref, o_ref):
        xv = x_ref[...]
        row = lax.broadcasted_iota(jnp.int32, xv.shape, 0)
        acc = jnp.zeros_like(xv)
        for i in range(CONV_WIDTH):
            acc = acc + w_ref[i:i + 1, :] * _shift_down(xv, CONV_WIDTH - 1 - i, row)
        o_ref[...] = _silu(acc)

    blk = pl.BlockSpec((seq, HEAD), lambda j, b: (b, j))
    return pl.pallas_call(
        body, name="conv_silu_fwd", grid=(c // HEAD, batch),
        in_specs=[pl.BlockSpec((seq, HEAD), lambda j, b: (b, first + j)), pl.BlockSpec((CONV_WIDTH, HEAD), lambda j, b: (0, j))],
        out_specs=blk, out_shape=SDS((t, c), F32), compiler_params=_params(2))(x, w)


def _conv_silu_bwd(x, col0, w, dy, *, batch, seq):
    t, c = x.shape[0], w.shape[1]
    first = col0 // HEAD

    def body(x_ref, w_ref, dy_ref, dx_ref, dw_ref):
        @pl.when(pl.program_id(1) == 0)
        def _():
            dw_ref[...] = jnp.zeros_like(dw_ref)

        xv = x_ref[...]
        row = lax.broadcasted_iota(jnp.int32, xv.shape, 0)
        shifted = [_shift_down(xv, CONV_WIDTH - 1 - i, row) for i in range(CONV_WIDTH)]
        acc = jnp.zeros_like(xv)
        for i in range(CONV_WIDTH):
            acc = acc + w_ref[i:i + 1, :] * shifted[i]
        sg = _sigmoid(acc)
        dc = dy_ref[...] * sg * (1.0 + acc * (1.0 - sg))
        dx = jnp.zeros_like(xv)
        for i in range(CONV_WIDTH):
            dx = dx + w_ref[i:i + 1, :] * _shift_up(dc, CONV_WIDTH - 1 - i, row)
            dw_ref[i:i + 1, :] += jnp.sum(dc * shifted[i], axis=0, keepdims=True)
        dx_ref[...] = dx

    blk = pl.BlockSpec((seq, HEAD), lambda j, b: (b, j))
    wblk = pl.BlockSpec((CONV_WIDTH, HEAD), lambda j, b: (0, j))
    return pl.pallas_call(
        body, name="conv_silu_bwd", grid=(c // HEAD, batch),
        in_specs=[pl.BlockSpec((seq, HEAD), lambda j, b: (b, first + j)), wblk, blk], out_specs=[blk, wblk],
        out_shape=[SDS((t, c), F32), SDS((CONV_WIDTH, c), F32)], compiler_params=_params(2))(x, w, dy)


def _dot(a, b, mode="nn"):
    return lax.dot_general(a.astype(BF16), b.astype(BF16), _DIMS[mode], preferred_element_type=F32)


def _dot3(a, b):
    a_hi, b_hi = a.astype(BF16), b.astype(BF16)
    a_lo, b_lo = (a - a_hi.astype(F32)).astype(BF16), (b - b_hi.astype(F32)).astype(BF16)
    pass_ = lambda x, y: jnp.dot(x, y, preferred_element_type=F32)
    return pass_(a_hi, b_hi) + pass_(a_hi, b_lo) + pass_(a_lo, b_hi)


@jax.custom_vjp
def _nilpotent_inverse(m):
    r = m.shape[0]
    x = jnp.where(lax.broadcasted_iota(jnp.int32, (r, r), 0) == lax.broadcasted_iota(jnp.int32, (r, r), 1), 1.0, 0.0) + m
    p = m
    for _ in range(CHUNK_BITS - 1):
        p = _dot3(p, p)
        x = x + _dot3(x, p)
    return x


def _nilpotent_inverse_fwd(m):
    x = _nilpotent_inverse(m)
    return x, x


def _nilpotent_inverse_bwd(x, g):
    return (_dot(x, _dot(g, x, "nt"), "tn"),)


_nilpotent_inverse.defvjp(_nilpotent_inverse_fwd, _nilpotent_inverse_bwd)


def _dot_nt(a, b):
    return _dot(a, b, "nt")


def _dot_tn(a, b):
    return _dot(a, b, "tn")


def _dn_chunk(head, n_heads, aq, ak, v, z, dbda, a_log, dt_bias, dn_norm, state):
    r = aq.shape[0]
    lane_g = lax.broadcasted_iota(jnp.int32, dbda.shape, 1)
    db = jnp.sum(jnp.where(lane_g == head, dbda, 0.0), axis=1, keepdims=True)
    da = jnp.sum(jnp.where(lane_g == head + n_heads, dbda, 0.0), axis=1, keepdims=True)
    lane_h = lax.broadcasted_iota(jnp.int32, a_log.shape, 1)
    al = jnp.sum(jnp.where(lane_h == head, a_log, 0.0), axis=1, keepdims=True)
    dtb = jnp.sum(jnp.where(lane_h == head, dt_bias, 0.0), axis=1, keepdims=True)
    beta = _sigmoid(db)
    g = -jnp.exp(al) * _softplus(da + dtb)
    q = aq * lax.rsqrt(jnp.sum(aq * aq, axis=1, keepdims=True) + EPS) * (HEAD ** -0.5)
    k = ak * lax.rsqrt(jnp.sum(ak * ak, axis=1, keepdims=True) + EPS)
    ri = lax.broadcasted_iota(jnp.int32, (r, r), 0)
    ci = lax.broadcasted_iota(jnp.int32, (r, r), 1)
    same = (ri >> CHUNK_BITS) == (ci >> CHUNK_BITS)
    incl = same & (ri >= ci)
    g_row = jnp.sum(jnp.where(ri == ci, g, 0.0), axis=0, keepdims=True)
    gc_col = jnp.sum(jnp.where(incl, g_row, 0.0), axis=1, keepdims=True)
    gc_row = jnp.sum(jnp.where(same & (ri <= ci), g, 0.0), axis=0, keepdims=True)
    g_all = jnp.sum(jnp.where(same, g_row, 0.0), axis=1, keepdims=True)
    decay = jnp.where(incl, jnp.exp(jnp.where(incl, gc_col - gc_row, 0.0)), 0.0)
    kb = k * beta
    m = -jnp.where(same & (ri > ci), _dot_nt(kb, k) * decay, 0.0)
    x = _nilpotent_inverse(m)
    egc = jnp.exp(gc_col)
    wu_g = _dot(x, jnp.concatenate([kb * egc, v * beta], axis=1))
    w_g, u_g = wu_g[:, :HEAD], wu_g[:, HEAD:]
    qk = _dot_nt(q, k) * decay
    q_dec = q * egc
    k_dec = k * jnp.exp(g_all - gc_col)
    carry = jnp.exp(g_all)
    v_new, o_state = [], []
    for c in range(r // CHUNK):
        rows = slice(c * CHUNK, (c + 1) * CHUNK)
        v_new.append(u_g[rows] - _dot(w_g[rows], state))
        o_state.append(_dot(q_dec[rows], state))
        state = state * carry[c * CHUNK:c * CHUNK + 1] + _dot_tn(k_dec[rows], v_new[-1])
    o = jnp.concatenate(o_state, axis=0) + _dot(qk, jnp.concatenate(v_new, axis=0))
    o = o * lax.rsqrt(jnp.mean(o * o, axis=1, keepdims=True) + EPS) * dn_norm
    return o * _silu(z), state


def _dn_chunks(head, n_heads):
    return jax.vmap(functools.partial(_dn_chunk, head, n_heads), in_axes=(0, 0, 0, 0, 0, None, None, None, 0))


def _dn_layout(batch, seq, n_heads, small, reverse):
    grp = _tile(seq, DN_ROWS, CHUNK)
    rows = _tile(seq, DN_SEQ_BLOCK, grp)
    n_blocks, per = seq // rows, rows // grp
    at = (lambda j: n_blocks - 1 - j) if reverse else (lambda j: j)
    col = lambda off: pl.BlockSpec((batch, rows, HEAD), lambda j, h: (0, at(j), off + h))
    gates = pl.BlockSpec((batch, rows, 2 * n_heads), lambda j, h: (0, at(j), 0))
    states = pl.BlockSpec((batch, None, per, HEAD, HEAD), lambda j, h: (0, h, at(j), 0, 0))
    full = [pl.BlockSpec(a.shape, lambda j, h: (0, 0)) for a in small]
    return grp, n_blocks, per, col, gates, states, full


def _dn_fwd(y, z, dbda, a_log, dt_bias, dn_norm):
    batch, seq, dd = z.shape
    n_heads = dd // HEAD
    grp, n_blocks, per, col, gates, st_spec, full = _dn_layout(batch, seq, n_heads, (a_log, dt_bias, dn_norm), False)

    def body(q_ref, k_ref, v_ref, z_ref, g_ref, al_ref, dt_ref, nw_ref, o_ref, st_ref, carry):
        head = pl.program_id(1)
        al, dtb, nw = al_ref[...], dt_ref[...], nw_ref[...]

        @pl.when(pl.program_id(0) == 0)
        def _():
            carry[head] = jnp.zeros((batch, HEAD, HEAD), F32)

        def step(n, states):
            rows = pl.ds(pl.multiple_of(n * grp, grp), grp)
            for b in range(batch):
                st_ref[b, n] = states[b]
            out, states = _dn_chunks(head, n_heads)(q_ref[:, rows, :], k_ref[:, rows, :], v_ref[:, rows, :],
                                                    z_ref[:, rows, :], g_ref[:, rows, :], al, dtb, nw, states)
            o_ref[:, rows, :] = out
            return states

        carry[head] = lax.fori_loop(0, per, step, carry[head])

    return pl.pallas_call(
        body, name="dn_fwd", grid=(n_blocks, n_heads),
        in_specs=[col(0), col(n_heads), col(2 * n_heads), col(0), gates, *full], out_specs=[col(0), st_spec],
        out_shape=[SDS((batch, seq, dd), F32), SDS((batch, n_heads, seq // grp, HEAD, HEAD), F32)],
        scratch_shapes=[pltpu.VMEM((n_heads, batch, HEAD, HEAD), F32)], compiler_params=_params(2),
    )(y, y, y, z, dbda, a_log, dt_bias, dn_norm)


def _dn_bwd(y, z, dbda, a_log, dt_bias, dn_norm, states, d_out):
    batch, seq, dd = z.shape
    n_heads = dd // HEAD
    grp, n_blocks, per, col, gates, st_spec, full = _dn_layout(batch, seq, n_heads, (a_log, dt_bias, dn_norm), True)

    def body(q_ref, k_ref, v_ref, z_ref, g_ref, al_ref, dt_ref, nw_ref, st_ref, do_ref,
             dq_ref, dk_ref, dv_ref, dz_ref, dg_ref, dal_ref, ddt_ref, dnw_ref, carry):
        first, head = pl.program_id(0) == 0, pl.program_id(1)
        al, dtb, nw = al_ref[...], dt_ref[...], nw_ref[...]

        @pl.when(first & (head == 0))
        def _():
            dal_ref[...] = jnp.zeros_like(dal_ref)
            ddt_ref[...] = jnp.zeros_like(ddt_ref)
            dnw_ref[...] = jnp.zeros_like(dnw_ref)

        @pl.when(head == 0)
        def _():
            dg_ref[...] = jnp.zeros_like(dg_ref)

        @pl.when(first)
        def _():
            carry[head] = jnp.zeros((batch, HEAD, HEAD), F32)

        def step(i, acc):
            d_states, d_al, d_dt, d_nw = acc
            n = per - 1 - i
            rows = pl.ds(pl.multiple_of(n * grp, grp), grp)
            states = jnp.stack([st_ref[b, n] for b in range(batch)])
            _, vjp = jax.vjp(_dn_chunks(head, n_heads), q_ref[:, rows, :], k_ref[:, rows, :], v_ref[:, rows, :],
                             z_ref[:, rows, :], g_ref[:, rows, :], al, dtb, nw, states)
            gq, gk, gv, gz, gg, gal, gdt, gnw, d_states = vjp((do_ref[:, rows, :], d_states))
            dq_ref[:, rows, :] = gq
            dk_ref[:, rows, :] = gk
            dv_ref[:, rows, :] = gv
            dz_ref[:, rows, :] = gz
            dg_ref[:, rows, :] += gg
            return d_states, d_al + gal, d_dt + gdt, d_nw + gnw

        zero = lambda a: jnp.zeros(a.shape, F32)
        d_states, d_al, d_dt, d_nw = lax.fori_loop(0, per, step, (carry[head], zero(al), zero(dtb), zero(nw)))
        carry[head] = d_states
        dal_ref[...] += d_al
        ddt_ref[...] += d_dt
        dnw_ref[...] += d_nw

    out3 = SDS((batch, seq, dd), F32)
    return pl.pallas_call(
        body, name="dn_bwd", grid=(n_blocks, n_heads),
        in_specs=[col(0), col(n_heads), col(2 * n_heads), col(0), gates, *full, st_spec, col(0)],
        out_specs=[col(0), col(0), col(0), col(0), gates, *full],
        out_shape=[out3] * 4 + [SDS(dbda.shape, F32), SDS(a_log.shape, F32), SDS(dt_bias.shape, F32), SDS(dn_norm.shape, F32)],
        scratch_shapes=[pltpu.VMEM((n_heads, batch, HEAD, HEAD), F32)], compiler_params=_params(2),
    )(y, y, y, z, dbda, a_log, dt_bias, dn_norm, states, d_out)


def _my_slot():
    return 4 * lax.axis_index("x") + 2 * lax.axis_index("y") + lax.axis_index("c")


def _peer(k):
    x, y, c = lax.axis_index("x"), lax.axis_index("y"), lax.axis_index("c")
    return (x ^ (k >> 2), y ^ ((k >> 1) & 1), c ^ (k & 1)), (4 * x + 2 * y + c) ^ k


def _all_gather(name, block, after):
    def body(src, after_ref, dst, send_sems, recv_sems, local_sem):
        me = _my_slot()
        own = pltpu.make_async_copy(src, dst.at[me], local_sem)
        own.start()
        copies = []
        for k in range(1, N_DEV):
            peer, _ = _peer(k)
            copies.append(pltpu.make_async_remote_copy(
                src_ref=src, dst_ref=dst.at[me], send_sem=send_sems.at[k - 1], recv_sem=recv_sems.at[k - 1],
                device_id=peer, device_id_type=MESH))
            copies[-1].start()
        for k in range(1, N_DEV):
            peer, slot = _peer(k)
            pltpu.make_async_remote_copy(
                src_ref=src, dst_ref=dst.at[slot], send_sem=send_sems.at[k - 1], recv_sem=recv_sems.at[k - 1],
                device_id=peer, device_id_type=MESH).wait_recv()
        for cp in copies:
            cp.wait_send()
        own.wait()

    return pl.pallas_call(
        body, name=name, in_specs=[pl.BlockSpec(memory_space=pl.ANY)] * 2, out_specs=pl.BlockSpec(memory_space=pl.ANY),
        out_shape=SDS((N_DEV, *block.shape), block.dtype),
        scratch_shapes=[pltpu.SemaphoreType.DMA((N_DEV - 1,)), pltpu.SemaphoreType.DMA((N_DEV - 1,)), pltpu.SemaphoreType.DMA],
    )(block, after)


def _exchange_slices(name, parts, after):
    def body(src, after_ref, dst, send_sems, recv_sems, local_sem):
        me = _my_slot()
        own = pltpu.make_async_copy(src.at[me], dst.at[me], local_sem)
        own.start()
        copies = []
        for k in range(1, N_DEV):
            peer, slot = _peer(k)
            copies.append(pltpu.make_async_remote_copy(
                src_ref=src.at[slot], dst_ref=dst.at[me], send_sem=send_sems.at[k - 1], recv_sem=recv_sems.at[k - 1],
                device_id=peer, device_id_type=MESH))
            copies[-1].start()
        for k in range(1, N_DEV):
            peer, slot = _peer(k)
            pltpu.make_async_remote_copy(
                src_ref=src.at[me], dst_ref=dst.at[slot], send_sem=send_sems.at[k - 1], recv_sem=recv_sems.at[k - 1],
                device_id=peer, device_id_type=MESH).wait_recv()
        for cp in copies:
            cp.wait_send()
        own.wait()

    return pl.pallas_call(
        body, name=name, in_specs=[pl.BlockSpec(memory_space=pl.ANY)] * 2, out_specs=pl.BlockSpec(memory_space=pl.ANY),
        out_shape=SDS(parts.shape, parts.dtype),
        scratch_shapes=[pltpu.SemaphoreType.DMA((N_DEV - 1,)), pltpu.SemaphoreType.DMA((N_DEV - 1,)), pltpu.SemaphoreType.DMA],
    )(parts, after)


_HBM = pl.BlockSpec(memory_space=pltpu.HBM)
_SEM = pl.BlockSpec(memory_space=pltpu.SEMAPHORE)
_EFFECT = pltpu.SideEffectType.DATAFLOW_SIDE_EFFECTING


def _slot_operand():
    return _my_slot().astype(jnp.int32).reshape(1)


def _col_tile(r, c, bytes_per_element):
    return _tile(c, max(128, (12 * 2 ** 20) // (bytes_per_element * r) // 128 * 128), 128)


def _cast_place(name, block, dtype):
    r, c = block.shape
    tc = _col_tile(r, c, 6)

    def body(me_ref, src_ref, dst_ref):
        dst_ref[...] = src_ref[...].astype(dtype)

    return pl.pallas_call(
        body, name=name, out_shape=SDS((N_DEV, r, c), dtype), compiler_params=_params(1),
        grid_spec=pltpu.PrefetchScalarGridSpec(
            num_scalar_prefetch=1, grid=(c // tc,), in_specs=[pl.BlockSpec((r, tc), lambda i, me: (0, i))],
            out_specs=pl.BlockSpec((None, r, tc), lambda i, me: (me[0], 0, i))),
    )(_slot_operand(), block)


_SIBLING = 1
_SAME_CORE = (2, 4, 6)
_OTHER_CORE = (3, 5, 7)


def _gather_start(name, lands):
    n = len(lands)

    def body(*refs):
        lnds, outs = refs[:n], refs[n:]
        me = _my_slot()
        for i in range(n):
            for k in (*_SAME_CORE, _SIBLING):
                peer, _ = _peer(k)
                pltpu.make_async_remote_copy(
                    src_ref=lnds[i].at[me], dst_ref=lnds[i].at[me], send_sem=outs[2 * i].at[k - 1],
                    recv_sem=outs[2 * i + 1].at[k - 1], device_id=peer, device_id_type=MESH).start()
        outs[-1][...] = jnp.zeros_like(outs[-1])

    res = pl.pallas_call(
        body, name=name, in_specs=[_HBM] * n,
        out_specs=[_SEM] * (2 * n) + [_HBM] * n + [pl.BlockSpec(memory_space=pltpu.VMEM)],
        out_shape=[pltpu.SemaphoreType.DMA((N_DEV - 1,))] * (2 * n) + [pltpu.HBM(a.shape, a.dtype) for a in lands]
        + [SDS((8, 128), F32)],
        input_output_aliases={i: 2 * n + i for i in range(n)},
        compiler_params=pltpu.CompilerParams(has_side_effects=_EFFECT),
    )(*[pltpu.with_memory_space_constraint(a, pltpu.HBM) for a in lands])
    return [(res[2 * i], res[2 * i + 1], res[2 * n + i]) for i in range(n)], res[-1]


def _gather_copy(land_ref, send_ref, recv_ref, k, slot, to):
    return pltpu.make_async_remote_copy(
        src_ref=land_ref.at[slot], dst_ref=land_ref.at[slot], send_sem=send_ref.at[k - 1], recv_sem=recv_ref.at[k - 1],
        device_id=to, device_id_type=MESH)


def _gather_arrived(name, started, after):
    send_sems, recv_sems, land = started

    def body(land_ref, send_ref, recv_ref, after_ref, land_out):
        for k in _SAME_CORE:
            peer, slot = _peer(k)
            _gather_copy(land_ref, send_ref, recv_ref, k, slot, peer).wait_recv()

    return pl.pallas_call(
        body, name=name, in_specs=[_HBM, _SEM, _SEM, pl.BlockSpec(memory_space=pl.ANY)], out_specs=[_HBM],
        out_shape=[pltpu.HBM(land.shape, land.dtype)], input_output_aliases={0: 0},
        compiler_params=pltpu.CompilerParams(has_side_effects=_EFFECT),
    )(land, send_sems, recv_sems, after)[0]


def _gather_forward(name, land):
    def body(land_ref, send_ref, recv_ref, land_out, token):
        sibling, _ = _peer(_SIBLING)
        for j, k in enumerate(_SAME_CORE):
            _, slot = _peer(k)
            _gather_copy(land_ref, send_ref, recv_ref, j + 1, slot, sibling).start()
        token[...] = jnp.zeros_like(token)

    res = pl.pallas_call(
        body, name=name, in_specs=[_HBM], out_specs=[_SEM, _SEM, _HBM, pl.BlockSpec(memory_space=pltpu.VMEM)],
        out_shape=[pltpu.SemaphoreType.DMA((len(_SAME_CORE),))] * 2 + [pltpu.HBM(land.shape, land.dtype), SDS((8, 128), F32)],
        input_output_aliases={0: 2}, compiler_params=pltpu.CompilerParams(has_side_effects=_EFFECT),
    )(land)
    return tuple(res[:3]), res[3]


def _gather_wait(name, started, forwarded, after):
    send_sems, recv_sems, _ = started
    send_fwd, recv_fwd, land = forwarded

    def body(land_ref, send_ref, recv_ref, send2_ref, recv2_ref, after_ref, land_out):
        sibling, slot = _peer(_SIBLING)
        _gather_copy(land_ref, send_ref, recv_ref, _SIBLING, slot, sibling).wait_recv()
        for j, k in enumerate(_OTHER_CORE):
            _, slot = _peer(k)
            _gather_copy(land_ref, send2_ref, recv2_ref, j + 1, slot, sibling).wait_recv()
        for k in (_SIBLING, *_SAME_CORE):
            peer, slot = _peer(k)
            _gather_copy(land_ref, send_ref, recv_ref, k, slot, peer).wait_send()
        for j, k in enumerate(_SAME_CORE):
            _, slot = _peer(k)
            _gather_copy(land_ref, send2_ref, recv2_ref, j + 1, slot, sibling).wait_send()

    return pl.pallas_call(
        body, name=name, in_specs=[_HBM, _SEM, _SEM, _SEM, _SEM, pl.BlockSpec(memory_space=pl.ANY)], out_specs=[_HBM],
        out_shape=[pltpu.HBM(land.shape, land.dtype)], input_output_aliases={0: 0},
        compiler_params=pltpu.CompilerParams(has_side_effects=_EFFECT),
    )(land, send_sems, recv_sems, send_fwd, recv_fwd, after)[0]


def _scatter_start(name, parts):
    land = lax.empty(parts.shape, parts.dtype)

    def body(src, lnd, send_ref, recv_ref, src_out, lnd_out, token):
        me = _my_slot()
        for k in range(1, N_DEV):
            peer, slot = _peer(k)
            pltpu.make_async_remote_copy(
                src_ref=src.at[slot], dst_ref=lnd.at[me], send_sem=send_ref.at[k - 1], recv_sem=recv_ref.at[k - 1],
                device_id=peer, device_id_type=MESH).start()
        token[...] = jnp.zeros_like(token)

    res = pl.pallas_call(
        body, name=name, in_specs=[_HBM, _HBM],
        out_specs=[_SEM, _SEM, _HBM, _HBM, pl.BlockSpec(memory_space=pltpu.VMEM)],
        out_shape=[pltpu.SemaphoreType.DMA((N_DEV - 1,))] * 2 + [pltpu.HBM(parts.shape, parts.dtype)] * 2 + [SDS((8, 128), F32)],
        input_output_aliases={0: 2, 1: 3}, compiler_params=pltpu.CompilerParams(has_side_effects=_EFFECT),
    )(pltpu.with_memory_space_constraint(parts, pltpu.HBM), pltpu.with_memory_space_constraint(land, pltpu.HBM))
    return tuple(res[:4]), res[4]


def _scatter_wait(name, started, after):
    send_sems, recv_sems, parts, land = started

    def body(src_ref, land_ref, send_ref, recv_ref, after_ref, src_out, land_out):
        me = _my_slot()
        for k in range(1, N_DEV):
            peer, slot = _peer(k)
            copy = pltpu.make_async_remote_copy(
                src_ref=src_ref.at[me], dst_ref=land_ref.at[slot], send_sem=send_ref.at[k - 1],
                recv_sem=recv_ref.at[k - 1], device_id=peer, device_id_type=MESH)
            copy.wait_send()
            copy.wait_recv()

    return pl.pallas_call(
        body, name=name, in_specs=[_HBM, _HBM, _SEM, _SEM, pl.BlockSpec(memory_space=pl.ANY)], out_specs=[_HBM, _HBM],
        out_shape=[pltpu.HBM(parts.shape, parts.dtype), pltpu.HBM(land.shape, land.dtype)], input_output_aliases={0: 0, 1: 1},
        compiler_params=pltpu.CompilerParams(has_side_effects=_EFFECT),
    )(parts, land, send_sems, recv_sems, after)


def _adamw(name, landed, own, w, m, v):
    r, c = w.shape
    tc = _col_tile(r, c, 46)
    bc1 = 1.0 / (1.0 - ADAM_B1 ** ADAM_STEP)
    bc2 = 1.0 / (1.0 - ADAM_B2 ** ADAM_STEP)

    def body(me_ref, p_ref, own_ref, w_ref, m_ref, v_ref, g_ref, d_ref, nm_ref, nv_ref):
        me = me_ref[0]
        g = jnp.zeros(w_ref.shape, F32)
        for s in range(N_DEV):
            g = g + jnp.where(me == s, own_ref[...], p_ref[s]).astype(F32)
        nm = ADAM_B1 * m_ref[...] + (1.0 - ADAM_B1) * g
        nv = ADAM_B2 * v_ref[...] + (1.0 - ADAM_B2) * (g * g)
        g_ref[...] = g
        nm_ref[...] = nm
        nv_ref[...] = nv
        d_ref[...] = -ADAM_LR * ((nm * bc1) / (jnp.sqrt(nv * bc2) + ADAM_EPS) + ADAM_WD * w_ref[...])

    blk = pl.BlockSpec((r, tc), lambda i, me: (0, i))
    return pl.pallas_call(
        body, name=name, out_shape=[SDS((r, c), F32)] * 4, compiler_params=_params(1),
        grid_spec=pltpu.PrefetchScalarGridSpec(
            num_scalar_prefetch=1, grid=(c // tc,),
            in_specs=[pl.BlockSpec((N_DEV, r, tc), lambda i, me: (0, 0, i)), pl.BlockSpec((None, r, tc), lambda i, me: (me[0], 0, i)),
                      blk, blk, blk],
            out_specs=[blk] * 4),
    )(_slot_operand(), landed, own, w, m, v)


def _ffn_fwd(name, h, norm, fetch, prefetch, landed, ahead=None):
    n = _rms_fwd(name + "_norm", h, norm)
    wg = fetch(name + "_w_gate", n)
    gate = _ffn_proj(name + "_gate", n, wg)
    sent = prefetch(name + "_w_down", gate) if landed else None
    wu = fetch(name + "_w_up", gate)
    up, act = _ffn_proj(name + "_up", n, wu, gate, after=sent)
    sent = prefetch(ahead, act) if ahead else None
    wd = fetch(name + "_w_down", act)
    return _ffn_down(act, wd, h, 0.5, after=sent), (n, gate, up, act, wg, wu, wd)


def _ffn_bwd(name, h, norm, saved, dh, dy_b, scale_out, emit):
    n, gate, up, act, wg, wu, wd = saved
    sent = emit(name + "_w_down", _wgrad_rows(name + "_dwd", act, dy_b))
    d_gate, d_up = _ffn_bwd_act(dy_b, wd, gate, up, after=sent)
    sent = emit(name + "_w_gate", _wgrad_rows(name + "_dwg", d_gate, n, after=sent))
    sent = emit(name + "_w_up", _wgrad_rows(name + "_dwu", d_up, n, after=sent))
    dn = _dgrad_cols(name + "_dn", (d_gate, d_up), (wg, wu), after=sent)
    return _rms_bwd(name + "_norm_bwd", h, dn, norm, dh, scale_out)


def _local_step(x, target, norms, small, fetch, prefetch, emit, *, batch, seq):
    n1w, nmw, n2w, nfw = norms
    a_log, dt_bias, dn_norm = small
    t, d = x.shape

    h1, saved1 = _ffn_fwd("ffn1", x, n1w, fetch, prefetch, False)
    nm = _rms_fwd("mix_norm", h1, nmw)
    w_in = fetch("w_in", nm)
    p = w_in.shape[1]
    proj = _in_proj(nm, w_in)
    conv_all = fetch("conv_w", proj)
    proj = jnp.swapaxes(proj, 0, 1).reshape(t, N_DEV * p)
    conv_w = jnp.swapaxes(conv_all, 0, 1).reshape(CONV_WIDTH, N_DEV * conv_all.shape[2])
    dd = conv_w.shape[1] // 3
    da = (N_DEV * p - 4 * dd - 2 * (dd // HEAD)) // 3
    qkv = proj[:, :3 * da].astype(BF16)
    z = proj[:, 3 * da + 3 * dd:3 * da + 4 * dd]
    dbda = proj[:, 3 * da + 4 * dd:]
    attn, lse = _attn_fwd(qkv, batch=batch, seq=seq)
    yd = _conv_silu_fwd(proj, 3 * da, conv_w, batch=batch, seq=seq)
    per_seq = lambda a: a.reshape(batch, seq, a.shape[1])
    dn_in = (per_seq(yd), per_seq(z), per_seq(dbda), a_log, dt_bias, dn_norm)
    dn_out, dn_states = _dn_fwd(*dn_in)
    cat = jnp.concatenate([attn, dn_out.reshape(t, dd)], axis=1).astype(BF16)
    w_out = fetch("w_out", cat)
    w_out2 = w_out.reshape(da + dd, d)
    sent = prefetch("ffn2_w_up", prefetch("ffn2_w_gate", cat))
    h2 = _out_proj(cat, w_out2, h1, after=sent)
    h3, saved2 = _ffn_fwd("ffn2", h2, n2w, fetch, prefetch, True)

    loss, dh3, dh3_b, g_nf = _loss_head(h3, nfw, target, 0.5)
    dh2, dh2_b, g_n2 = _ffn_bwd("ffn2", h2, n2w, saved2, dh3, dh3_b, 1.0, emit)

    sent = emit("w_out", _wgrad_full("dw_out", cat, dh2_b).reshape(w_out.shape))
    d_attn = _dgrad_full("d_attn", dh2_b, w_out2[:da], after=sent)
    d_dn = _dgrad_full("d_dn", dh2_b, w_out2[da:])
    dq, dk, dv = _attn_bwd(qkv, attn, lse, d_attn, batch=batch, seq=seq)
    gq, gk, gv, gz, g_dbda, g_alog, g_dtb, g_dnn = _dn_bwd(*dn_in, dn_states, per_seq(d_dn))
    gq, gk, gv, gz, g_dbda = (a.reshape(t, a.shape[2]) for a in (gq, gk, gv, gz, g_dbda))
    d_xd, g_conv = _conv_silu_bwd(proj, 3 * da, conv_w, jnp.concatenate([gq, gk, gv], axis=1), batch=batch, seq=seq)
    dproj = jnp.concatenate([dq, dk, dv, d_xd, gz, g_dbda], axis=1).astype(BF16)
    dproj = jnp.swapaxes(dproj.reshape(t, N_DEV, p), 0, 1)
    sent = emit("w_in", _wgrad_rows("dw_in", dproj, nm))
    dnm = _dgrad_cols("d_mix_in", (dproj,), (w_in,), after=sent)
    dh1, dh1_b, g_nm = _rms_bwd("mix_norm_bwd", h1, dnm, nmw, dh2, 0.5)

    dx, _, g_n1 = _ffn_bwd("ffn1", x, n1w, saved1, dh1, dh1_b, 1.0, emit)
    return loss, dx, (g_n1, g_nm, g_n2, g_nf), (g_alog, g_dtb, g_dnn), g_conv


def _pack_rows(vectors):
    rows, offsets, r = [], [], 0
    for vec in vectors:
        n = -(-vec.size // 128)
        rows.append(jnp.pad(vec.reshape(-1), (0, n * 128 - vec.size)).reshape(n, 128))
        offsets.append((r, vec.size, vec.shape))
        r += n
    pad = -r % 8
    if pad:
        rows.append(jnp.zeros((pad, 128), F32))
    return jnp.concatenate(rows, axis=0), offsets


def _unpack_rows(packed, offsets):
    return [packed[r:r + -(-size // 128)].reshape(-1)[:size].reshape(shape) for r, size, shape in offsets]


def kernel(x, ffn1_norm, ffn1_w_gate, ffn1_w_up, ffn1_w_down, mix_norm, w_in, conv_w, a_log, dt_bias, dn_norm, w_out, ffn2_norm, ffn2_w_gate, ffn2_w_up, ffn2_w_down, final_norm, loss_target, m_ffn1_norm, m_ffn1_w_gate, m_ffn1_w_up, m_ffn1_w_down, m_mix_norm, m_w_in, m_conv_w, m_a_log, m_dt_bias, m_dn_norm, m_w_out, m_ffn2_norm, m_ffn2_w_gate, m_ffn2_w_up, m_ffn2_w_down, m_final_norm, v_ffn1_norm, v_ffn1_w_gate, v_ffn1_w_up, v_ffn1_w_down, v_mix_norm, v_w_in, v_conv_w, v_a_log, v_dt_bias, v_dn_norm, v_w_out, v_ffn2_norm, v_ffn2_w_gate, v_ffn2_w_up, v_ffn2_w_down, v_final_norm):
    batch, seq, d = x.shape
    t = batch * seq
    big = dict(ffn1_w_gate=(ffn1_w_gate, m_ffn1_w_gate, v_ffn1_w_gate), ffn1_w_up=(ffn1_w_up, m_ffn1_w_up, v_ffn1_w_up),
               ffn1_w_down=(ffn1_w_down, m_ffn1_w_down, v_ffn1_w_down), w_in=(w_in, m_w_in, v_w_in),
               w_out=(w_out, m_w_out, v_w_out), ffn2_w_gate=(ffn2_w_gate, m_ffn2_w_gate, v_ffn2_w_gate),
               ffn2_w_up=(ffn2_w_up, m_ffn2_w_up, v_ffn2_w_up), ffn2_w_down=(ffn2_w_down, m_ffn2_w_down, v_ffn2_w_down))
    by_columns = ("ffn1_w_gate", "ffn1_w_up", "w_in", "ffn2_w_gate", "ffn2_w_up")
    for name in by_columns:
        big[name] = tuple(a.T for a in big[name])
    rep = dict(ffn1_norm=(ffn1_norm, m_ffn1_norm, v_ffn1_norm), mix_norm=(mix_norm, m_mix_norm, v_mix_norm),
               ffn2_norm=(ffn2_norm, m_ffn2_norm, v_ffn2_norm), final_norm=(final_norm, m_final_norm, v_final_norm),
               a_log=(a_log, m_a_log, v_a_log), dt_bias=(dt_bias, m_dt_bias, v_dt_bias), dn_norm=(dn_norm, m_dn_norm, v_dn_norm))

    lands = {"conv_w": _cast_place("place_conv_w", conv_w, F32)}
    lands.update({name: _cast_place("place_" + name, w, BF16) for name, (w, _, _) in big.items()})
    started, token = _gather_start("gather_start", list(lands.values()))
    gathering = dict(zip(lands, started))
    gathered, scattering = {}, {}

    forwarding = {}

    def prefetch(name, after):
        if name not in forwarding:
            land = _gather_arrived("gather_arrived_" + name, gathering[name], after)
            forwarding[name] = _gather_forward("gather_forward_" + name, land)
        return forwarding[name][1]

    def fetch(name, after):
        if name not in gathered:
            prefetch(name, after)
            gathered[name] = _gather_wait("gather_wait_" + name, gathering[name], forwarding[name][0], after)
        return gathered[name]

    def emit(name, grad):
        scattering[name], sent = _scatter_start("scatter_start_" + name, grad)
        return sent

    row = lambda a: a.reshape(1, -1)
    norms = [row(rep[n][0]) for n in ("ffn1_norm", "mix_norm", "ffn2_norm", "final_norm")]
    norms[0] = norms[0] + token[0, 0]
    loss, dx, g_norms, g_small, g_conv = _local_step(
        x.reshape(t, d), loss_target.reshape(t, d), norms, [row(rep[n][0]) for n in ("a_log", "dt_bias", "dn_norm")],
        fetch, prefetch, emit, batch=batch, seq=seq)

    out = {"grad_x": dx.reshape(x.shape)}
    after = dx
    for name in scattering:
        w, m, v = big[name]
        own, landed = _scatter_wait("scatter_wait_" + name, scattering[name], after)
        res = _adamw("adamw_" + name, landed, own, w, m, v)
        after = res[0]
        out["grad_" + name], out["delta_" + name], out["new_m_" + name], out["new_v_" + name] = (
            [a.T for a in res] if name in by_columns else res)
    conv_parts = jnp.swapaxes(g_conv.reshape(CONV_WIDTH, N_DEV, conv_w.shape[1]), 0, 1)
    parts = _exchange_slices("scatter_conv_w", conv_parts, after)
    out["grad_conv_w"], out["delta_conv_w"], out["new_m_conv_w"], out["new_v_conv_w"] = _adamw("adamw_conv_w", parts, parts, conv_w, m_conv_w, v_conv_w)

    rep_names = list(rep)
    g_rep = [*g_norms, *g_small]
    packed_g, offsets = _pack_rows([*g_rep, loss[:, :1]])
    packed = [_pack_rows([*[rep[n][i] for n in rep_names], jnp.zeros((1, 1), F32)])[0] for i in range(3)]
    parts = _all_gather("gather_small_grads", packed_g, out["grad_conv_w"])
    res = [_unpack_rows(a, offsets) for a in _adamw("adamw_small", parts, parts, *packed)]
    for i, name in enumerate(rep_names):
        shape = rep[name][0].shape
        out["grad_" + name], out["delta_" + name], out["new_m_" + name], out["new_v_" + name] = (r[i].reshape(shape) for r in res)
    out["loss"] = res[0][-1].reshape(())

    order = ["ffn1_norm", "ffn1_w_gate", "ffn1_w_up", "ffn1_w_down", "mix_norm", "w_in", "conv_w", "a_log", "dt_bias", "dn_norm",
             "w_out", "ffn2_norm", "ffn2_w_gate", "ffn2_w_up", "ffn2_w_down", "final_norm"]
    return (out["loss"], out["grad_x"], *[out["grad_" + n] for n in order], *[out["delta_" + n] for n in order],
            *[out["new_m_" + n] for n in order], *[out["new_v_" + n] for n in order])
```

```python
import functools

import jax
import jax.numpy as jnp
from jax import lax
from jax.experimental import pallas as pl
from jax.experimental.pallas import tpu as pltpu

F32 = jnp.float32
BF16 = jnp.bfloat16
N_DEV = 8
HEAD = 128
CHUNK = 128
CHUNK_BITS = 7
DN_ROWS = 256
DN_SEQ_BLOCK = 1024
CONV_WIDTH = 4
EPS = 1e-6
DILATED_CONFIGS = ((128, 1), (512, 4), (2048, 16))
ATTN_BLOCK = 1024
NEG = -1e30
ADAM_LR, ADAM_B1, ADAM_B2, ADAM_EPS, ADAM_WD, ADAM_STEP = 0.001, 0.9, 0.999, 1e-08, 0.01, 10
MESH = pl.DeviceIdType.MESH
SDS = jax.ShapeDtypeStruct


def _tile(n, pref, align):
    t = (min(n, pref) // align) * align
    while t >= align:
        if n % t == 0:
            return t
        t -= align
    return n


def _params(n_axes, vmem_mb=48):
    return pltpu.CompilerParams(dimension_semantics=("arbitrary",) * n_axes, vmem_limit_bytes=vmem_mb * 2 ** 20)


def _sigmoid(x):
    return 1.0 / (1.0 + jnp.exp(-x))


def _silu(x):
    return x * _sigmoid(x)


def _softplus(x):
    return jnp.maximum(x, 0.0) + jnp.log(1.0 + jnp.exp(-jnp.abs(x)))


_DIMS = {"nn": (((1,), (0,)), ((), ())), "nt": (((1,), (1,)), ((), ())), "tn": (((0,), (0,)), ((), ()))}


def _mm_call(name, grid, mode, pairs, operands, in_specs, out_shape, out_specs, acc_shapes, epilogue, vmem_mb=48, after=None):
    dims = _DIMS[mode]
    if after is not None:
        operands, in_specs = (*operands, after), [*in_specs, pl.BlockSpec(memory_space=pl.ANY)]
    n_in, n_out = len(operands), len(out_shape)
    nk = grid[-1]

    def whole(*refs):
        ins, outs = refs[:n_in], refs[n_in:]
        sums = {}
        for a, b, c in pairs:
            prod = lax.dot_general(ins[a][...], ins[b][...], dims, preferred_element_type=F32)
            sums[c] = prod if c not in sums else sums[c] + prod
        epilogue(ins, outs, [sums[c] for c in sorted(sums)])

    if acc_shapes is None:
        return pl.pallas_call(
            whole, name=name, grid=grid, in_specs=in_specs, out_specs=out_specs, out_shape=out_shape,
            compiler_params=_params(len(grid), vmem_mb))(*operands)

    def body(*refs):
        ins, outs, accs = refs[:n_in], refs[n_in:n_in + n_out], refs[n_in + n_out:]
        k = pl.program_id(len(grid) - 1)

        @pl.when(k == 0)
        def _():
            for acc in accs:
                acc[...] = jnp.zeros_like(acc)

        sums = {}
        for a, b, c in pairs:
            prod = lax.dot_general(ins[a][...], ins[b][...], dims, preferred_element_type=F32)
            sums[c] = prod if c not in sums else sums[c] + prod
        for c, total in sums.items():
            accs[c][...] += total

        @pl.when(k == nk - 1)
        def _():
            epilogue(ins, outs, [acc[...] for acc in accs])

    return pl.pallas_call(
        body, name=name, grid=grid, in_specs=in_specs, out_specs=out_specs, out_shape=out_shape,
        scratch_shapes=[pltpu.VMEM(s, F32) for s in acc_shapes], compiler_params=_params(len(grid), vmem_mb),
    )(*operands)


def _ffn_proj(name, n, w, gate=None, after=None):
    t, d = n.shape
    f = w.shape[1]
    tm = _tile(t, 256, 16)
    n_spec = pl.BlockSpec((tm, d), lambda s, m, k: (m, 0))
    w_spec = pl.BlockSpec((None, f, d), lambda s, m, k: (s, 0, 0))
    o_spec = pl.BlockSpec((None, tm, f), lambda s, m, k: (s, m, 0))
    o_shape = SDS((N_DEV, t, f), BF16)
    grid = (N_DEV, t // tm, 1)
    if gate is None:
        return _mm_call(name, grid, "nt", [(0, 1, 0)], (n, w), [n_spec, w_spec], [o_shape], [o_spec], None, _store_bf16,
                        after=after)[0]

    def up_out(ins, outs, accs):
        outs[0][...] = accs[0].astype(BF16)
        outs[1][...] = (_silu(ins[2][...].astype(F32)) * accs[0]).astype(BF16)

    return _mm_call(name, grid, "nt", [(0, 1, 0)], (n, w, gate), [n_spec, w_spec, o_spec], [o_shape] * 2, [o_spec] * 2,
                    None, up_out, after=after)


def _ffn_down(act, wd, resid, scale, after=None):
    _, t, f = act.shape
    d = wd.shape[2]
    tm, tn = _tile(t, 1024, 16), _tile(d, 1024, 128)

    def epilogue(ins, outs, accs):
        outs[0][...] = ins[2][...] + scale * accs[0]

    rc = pl.BlockSpec((tm, tn), lambda m, n, s: (m, n))
    return _mm_call(
        "ffn_down", (t // tm, d // tn, N_DEV), "nn", [(0, 1, 0)], (act, wd, resid),
        [pl.BlockSpec((None, tm, f), lambda m, n, s: (s, m, 0)), pl.BlockSpec((None, f, tn), lambda m, n, s: (s, 0, n)), rc],
        [SDS((t, d), F32)], [rc], [(tm, tn)], epilogue, after=after)[0]


def _in_proj(n, w):
    t, d = n.shape
    p = w.shape[1]
    tm = _tile(t, 256, 16)
    return _mm_call(
        "in_proj", (N_DEV, t // tm, 1), "nt", [(0, 1, 0)], (n, w),
        [pl.BlockSpec((tm, d), lambda s, m, k: (m, 0)), pl.BlockSpec((None, p, d), lambda s, m, k: (s, 0, 0))],
        [SDS((N_DEV, t, p), F32)], [pl.BlockSpec((None, tm, p), lambda s, m, k: (s, m, 0))], None, _store_f32)[0]


def _out_proj(cat, w, resid, after=None):
    t, dm = cat.shape
    d = w.shape[1]
    tm, tn = _tile(t, 512, 16), _tile(d, 1024, 128)

    def epilogue(ins, outs, accs):
        outs[0][...] = ins[2][...] + accs[0]

    rc = pl.BlockSpec((tm, tn), lambda n, m, k: (m, n))
    return _mm_call(
        "out_proj", (d // tn, t // tm, 1), "nn", [(0, 1, 0)], (cat, w, resid),
        [pl.BlockSpec((tm, dm), lambda n, m, k: (m, 0)), pl.BlockSpec((dm, tn), lambda n, m, k: (0, n)), rc],
        [SDS((t, d), F32)], [rc], None, epilogue, after=after)[0]


def _ffn_bwd_act(dy, wd, gate, up, after=None):
    t, d = dy.shape
    f = wd.shape[1]
    tm = _tile(t, 256, 16)

    def epilogue(ins, outs, accs):
        g, u = ins[2][...].astype(F32), ins[3][...].astype(F32)
        sg = _sigmoid(g)
        outs[0][...] = (accs[0] * u * sg * (1.0 + g * (1.0 - sg))).astype(BF16)
        outs[1][...] = (accs[0] * g * sg).astype(BF16)

    o_spec = pl.BlockSpec((None, tm, f), lambda s, m, k: (s, m, 0))
    return _mm_call(
        "ffn_bwd_act", (N_DEV, t // tm, 1), "nt", [(0, 1, 0)], (dy, wd, gate, up),
        [pl.BlockSpec((tm, d), lambda s, m, k: (m, 0)), pl.BlockSpec((None, f, d), lambda s, m, k: (s, 0, 0)), o_spec, o_spec],
        [SDS((N_DEV, t, f), BF16)] * 2, [o_spec] * 2, None, epilogue, after=after)


def _store_bf16(ins, outs, accs):
    outs[0][...] = accs[0].astype(BF16)


def _store_f32(ins, outs, accs):
    outs[0][...] = accs[0]


def _wgrad_rows(name, a, b, after=None):
    _, t, m = a.shape
    n = b.shape[1]
    tn, tk = _tile(n, 512 if m <= 1408 else 256, 128), t
    return _mm_call(
        name, (N_DEV, n // tn, t // tk), "tn", [(0, 1, 0)], (a, b),
        [pl.BlockSpec((None, tk, m), lambda s, j, k: (s, k, 0)), pl.BlockSpec((tk, tn), lambda s, j, k: (k, j))],
        [SDS((N_DEV, m, n), BF16)], [pl.BlockSpec((None, m, tn), lambda s, j, k: (s, 0, j))], None, _store_bf16,
        after=after)[0]


def _wgrad_full(name, a, b, after=None):
    t, m = a.shape
    n = b.shape[1]
    tm, tn, tk = _tile(m, 512, 128), _tile(n, 1024, 128), t
    return _mm_call(
        name, (m // tm, n // tn, t // tk), "tn", [(0, 1, 0)], (a, b),
        [pl.BlockSpec((tk, tm), lambda i, j, k: (k, i)), pl.BlockSpec((tk, tn), lambda i, j, k: (k, j))],
        [SDS((m, n), BF16)], [pl.BlockSpec((tm, tn), lambda i, j, k: (i, j))], None, _store_bf16, after=after)[0]


def _dgrad_cols(name, grads, weights, after=None):
    _, t, n = grads[0].shape
    m = weights[0].shape[2]
    tm, tn = _tile(t, 1024, 16), _tile(m, 1024, 128)
    k = len(grads)
    return _mm_call(
        name, (t // tm, m // tn, N_DEV), "nn", [(i, k + i, 0) for i in range(k)], (*grads, *weights),
        [pl.BlockSpec((None, tm, n), lambda i, j, s: (s, i, 0))] * k + [pl.BlockSpec((None, n, tn), lambda i, j, s: (s, 0, j))] * k,
        [SDS((t, m), F32)], [pl.BlockSpec((tm, tn), lambda i, j, s: (i, j))], [(tm, tn)], _store_f32, after=after)[0]


def _dgrad_full(name, g, w, after=None):
    t, n = g.shape
    m = w.shape[0]
    tm, tn, tk = _tile(t, 512, 16), _tile(m, 1024, 128), n
    return _mm_call(
        name, (m // tn, t // tm, n // tk), "nt", [(0, 1, 0)], (g, w),
        [pl.BlockSpec((tm, tk), lambda j, i, k: (i, k)), pl.BlockSpec((tn, tk), lambda j, i, k: (j, k))],
        [SDS((t, m), F32)], [pl.BlockSpec((tm, tn), lambda j, i, k: (i, j))], None, _store_f32, after=after)[0]


def _rms_fwd(name, h, w):
    t, d = h.shape
    tm = _tile(t, 256, 16)

    def body(h_ref, w_ref, o_ref):
        x = h_ref[...]
        o_ref[...] = (x * lax.rsqrt(jnp.mean(x * x, axis=1, keepdims=True) + EPS) * w_ref[...]).astype(BF16)

    row = pl.BlockSpec((tm, d), lambda i: (i, 0))
    return pl.pallas_call(
        body, name=name, grid=(t // tm,), in_specs=[row, pl.BlockSpec((1, d), lambda i: (0, 0))], out_specs=row,
        out_shape=SDS((t, d), BF16), compiler_params=_params(1))(h, w)


def _rms_bwd(name, h, dn, w, dres, scale):
    t, d = h.shape
    tm = _tile(t, 128, 16)

    def body(h_ref, dn_ref, w_ref, dres_ref, dh_ref, dhb_ref, dw_ref):
        @pl.when(pl.program_id(0) == 0)
        def _():
            dw_ref[...] = jnp.zeros_like(dw_ref)

        x = h_ref[...]
        rstd = lax.rsqrt(jnp.mean(x * x, axis=1, keepdims=True) + EPS)
        nhat = x * rstd
        g = dn_ref[...]
        gw = g * w_ref[...]
        dh = dres_ref[...] + rstd * (gw - nhat * jnp.mean(gw * nhat, axis=1, keepdims=True))
        dh_ref[...] = dh
        dhb_ref[...] = (scale * dh).astype(BF16)
        dw_ref[...] += jnp.sum(g * nhat, axis=0, keepdims=True)

    row = pl.BlockSpec((tm, d), lambda i: (i, 0))
    vec = pl.BlockSpec((1, d), lambda i: (0, 0))
    return pl.pallas_call(
        body, name=name, grid=(t // tm,), in_specs=[row, row, vec, row], out_specs=[row, row, vec],
        out_shape=[SDS((t, d), F32), SDS((t, d), BF16), SDS((1, d), F32)], compiler_params=_params(1))(h, dn, w, dres)


def _loss_head(h, w, target, scale):
    t, d = h.shape
    tm = _tile(t, 128, 16)

    def body(h_ref, w_ref, tg_ref, loss_ref, dh_ref, dhb_ref, dw_ref):
        @pl.when(pl.program_id(0) == 0)
        def _():
            dw_ref[...] = jnp.zeros_like(dw_ref)
            loss_ref[...] = jnp.zeros_like(loss_ref)

        x = h_ref[...]
        rstd = lax.rsqrt(jnp.mean(x * x, axis=1, keepdims=True) + EPS)
        nhat = x * rstd
        wv = w_ref[...]
        err = nhat * wv - tg_ref[...]
        loss_ref[...] += 0.5 * jnp.sum(jnp.mean(err * err, axis=1, keepdims=True), axis=0, keepdims=True)
        g = err * (1.0 / d)
        gw = g * wv
        dh = rstd * (gw - nhat * jnp.mean(gw * nhat, axis=1, keepdims=True))
        dh_ref[...] = dh
        dhb_ref[...] = (scale * dh).astype(BF16)
        dw_ref[...] += jnp.sum(g * nhat, axis=0, keepdims=True)

    row = pl.BlockSpec((tm, d), lambda i: (i, 0))
    vec = pl.BlockSpec((1, d), lambda i: (0, 0))
    return pl.pallas_call(
        body, name="loss_head", grid=(t // tm,), in_specs=[row, vec, row],
        out_specs=[pl.BlockSpec((1, 128), lambda i: (0, 0)), row, row, vec],
        out_shape=[SDS((1, 128), F32), SDS((t, d), F32), SDS((t, d), BF16), SDS((1, d), F32)],
        compiler_params=_params(1))(h, w, target)


def _attn_bias(delta, blk):
    dist = (lax.broadcasted_iota(jnp.int32, (blk, blk), 0) - lax.broadcasted_iota(jnp.int32, (blk, blk), 1)
            + delta * blk)
    count = jnp.zeros((blk, blk), F32)
    for window, dil in DILATED_CONFIGS:
        assert dil & (dil - 1) == 0
        seen = (dist >= 0) & (dist <= window) & ((dist & (dil - 1)) == 0)
        count = count + jnp.where(seen, 1.0, 0.0)
    return jnp.where(count > 0.0, jnp.log(jnp.maximum(count, 1.0)), NEG)


def _fill_bias_table(table, blk):
    @pl.when((pl.program_id(0) == 0) & (pl.program_id(1) == 0))
    def _():
        for delta in range(table.shape[0]):
            table[delta] = _attn_bias(delta, blk)


def _attn_fwd(qkv, *, batch, seq):
    t, da3 = qkv.shape
    da = da3 // 3
    n_heads = da // HEAD
    blk = _tile(seq, ATTN_BLOCK, 16)
    nq = seq // blk
    sm_scale = HEAD ** -0.5

    def body(q_ref, k_ref, v_ref, o_ref, lse_ref, bias):
        _fill_bias_table(bias, blk)

        def q_step(qi, _):
            rows = pl.ds(pl.multiple_of(qi * blk, blk), blk)
            q = q_ref[rows, :]

            def kv_step(ki, carry):
                m, l, acc = carry
                cols = pl.ds(pl.multiple_of(ki * blk, blk), blk)
                s = lax.dot_general(q, k_ref[cols, :], _DIMS["nt"], preferred_element_type=F32) * sm_scale
                s = s + bias[qi - ki]
                m_new = jnp.maximum(m, jnp.max(s, axis=1, keepdims=True))
                alpha = jnp.exp(m - m_new)
                p = jnp.exp(s - m_new)
                l = alpha * l + jnp.sum(p, axis=1, keepdims=True)
                acc = alpha * acc + jnp.dot(p.astype(BF16), v_ref[cols, :], preferred_element_type=F32)
                return m_new, l, acc

            m, l, acc = lax.fori_loop(0, qi + 1, kv_step, (jnp.full((blk, 1), NEG, F32), jnp.zeros((blk, 1), F32),
                                                           jnp.zeros((blk, HEAD), F32)))
            o_ref[rows, :] = acc / l
            lse_ref[rows, :] = jnp.broadcast_to(m + jnp.log(l), (blk, HEAD))
            return 0

        lax.fori_loop(0, nq, q_step, 0)

    col = lambda off: pl.BlockSpec((seq, HEAD), lambda b, h: (b, off + h))
    return pl.pallas_call(
        body, name="attn_fwd", grid=(batch, n_heads), in_specs=[col(0), col(n_heads), col(2 * n_heads)],
        out_specs=[col(0), col(0)], out_shape=[SDS((t, da), F32), SDS((t, da), F32)],
        scratch_shapes=[pltpu.VMEM((nq, blk, blk), F32)], compiler_params=_params(2),
    )(qkv, qkv, qkv)


def _attn_bwd(qkv, out, lse, d_out, *, batch, seq):
    t, da = out.shape
    n_heads = da // HEAD
    blk = _tile(seq, ATTN_BLOCK, 16)
    nq = seq // blk
    sm_scale = HEAD ** -0.5

    def body(q_ref, k_ref, v_ref, o_ref, lse_ref, do_ref, dq_ref, dk_ref, dv_ref, bias):
        _fill_bias_table(bias, blk)
        dk_ref[...] = jnp.zeros_like(dk_ref)
        dv_ref[...] = jnp.zeros_like(dv_ref)

        def q_step(qi, _):
            rows = pl.ds(pl.multiple_of(qi * blk, blk), blk)
            q = q_ref[rows, :]
            do = do_ref[rows, :]
            do_b = do.astype(BF16)
            lse_q = lse_ref[rows, :][:, :1]
            delta = jnp.sum(do * o_ref[rows, :], axis=1, keepdims=True)

            def kv_step(ki, dq):
                cols = pl.ds(pl.multiple_of(ki * blk, blk), blk)
                k = k_ref[cols, :]
                s = lax.dot_general(q, k, _DIMS["nt"], preferred_element_type=F32) * sm_scale
                p = jnp.exp(s + bias[qi - ki] - lse_q)
                dp = lax.dot_general(do_b, v_ref[cols, :], _DIMS["nt"], preferred_element_type=F32)
                ds = (p * (dp - delta) * sm_scale).astype(BF16)
                dv_ref[cols, :] += lax.dot_general(p.astype(BF16), do_b, _DIMS["tn"], preferred_element_type=F32)
                dk_ref[cols, :] += lax.dot_general(ds, q, _DIMS["tn"], preferred_element_type=F32)
                return dq + jnp.dot(ds, k, preferred_element_type=F32)

            dq_ref[rows, :] = lax.fori_loop(0, qi + 1, kv_step, jnp.zeros((blk, HEAD), F32))
            return 0

        lax.fori_loop(0, nq, q_step, 0)

    col = lambda off: pl.BlockSpec((seq, HEAD), lambda b, h: (b, off + h))
    return pl.pallas_call(
        body, name="attn_bwd", grid=(batch, n_heads),
        in_specs=[col(0), col(n_heads), col(2 * n_heads), col(0), col(0), col(0)], out_specs=[col(0)] * 3,
        out_shape=[SDS((t, da), F32)] * 3, scratch_shapes=[pltpu.VMEM((nq, blk, blk), F32)], compiler_params=_params(2),
    )(qkv, qkv, qkv, out, lse, d_out)


def _shift_down(x, k, row):
    return x if k == 0 else jnp.where(row >= k, pltpu.roll(x, k, axis=0), 0.0)


def _shift_up(x, k, row):
    n = x.shape[0]
    return x if k == 0 else jnp.where(row < n - k, pltpu.roll(x, n - k, axis=0), 0.0)


def _conv_silu_fwd(x, col0, w, *, batch, seq):
    t, c = x.shape[0], w.shape[1]
    first = col0 // HEAD

    def body(x_ref, w_ref, o_ref):
        xv = x_ref[...]
        row = lax.broadcasted_iota(jnp.int32, xv.shape, 0)
        acc = jnp.zeros_like(xv)
        for i in range(CONV_WIDTH):
            acc = acc + w_ref[i:i + 1, :] * _shift_down(xv, CONV_WIDTH - 1 - i, row)
        o_ref[...] = _silu(acc)

    blk = pl.BlockSpec((seq, HEAD), lambda j, b: (b, j))
    return pl.pallas_call(
        body, name="conv_silu_fwd", grid=(c // HEAD, batch),
        in_specs=[pl.BlockSpec((seq, HEAD), lambda j, b: (b, first + j)), pl.BlockSpec((CONV_WIDTH, HEAD), lambda j, b: (0, j))],
        out_specs=blk, out_shape=SDS((t, c), F32), compiler_params=_params(2))(x, w)


def _conv_silu_bwd(x, col0, w, dy, *, batch, seq):
    t, c = x.shape[0], w.shape[1]
    first = col0 // HEAD

    def body(x_ref, w_ref, dy_ref, dx_ref, dw_ref):
        @pl.when(pl.program_id(1) == 0)
        def _():
            dw_ref[...] = jnp.zeros_like(dw_ref)

        xv = x_ref[...]
        row = lax.broadcasted_iota(jnp.int32, xv.shape, 0)
        shifted = [_shift_down(xv, CONV_WIDTH - 1 - i, row) for i in range(CONV_WIDTH)]
        acc = jnp.zeros_like(xv)
        for i in range(CONV_WIDTH):
            acc = acc + w_ref[i:i + 1, :] * shifted[i]
        sg = _sigmoid(acc)
        dc = dy_ref[...] * sg * (1.0 + acc * (1.0 - sg))
        dx = jnp.zeros_like(xv)
        for i in range(CONV_WIDTH):
            dx = dx + w_ref[i:i + 1, :] * _shift_up(dc, CONV_WIDTH - 1 - i, row)
            dw_ref[i:i + 1, :] += jnp.sum(dc * shifted[i], axis=0, keepdims=True)
        dx_ref[...] = dx

    blk = pl.BlockSpec((seq, HEAD), lambda j, b: (b, j))
    wblk = pl.BlockSpec((CONV_WIDTH, HEAD), lambda j, b: (0, j))
    return pl.pallas_call(
        body, name="conv_silu_bwd", grid=(c // HEAD, batch),
        in_specs=[pl.BlockSpec((seq, HEAD), lambda j, b: (b, first + j)), wblk, blk], out_specs=[blk, wblk],
        out_shape=[SDS((t, c), F32), SDS((CONV_WIDTH, c), F32)], compiler_params=_params(2))(x, w, dy)


def _dot(a, b, mode="nn"):
    return lax.dot_general(a.astype(BF16), b.astype(BF16), _DIMS[mode], preferred_element_type=F32)


def _dot3(a, b):
    a_hi, b_hi = a.astype(BF16), b.astype(BF16)
    a_lo, b_lo = (a - a_hi.astype(F32)).astype(BF16), (b - b_hi.astype(F32)).astype(BF16)
    pass_ = lambda x, y: jnp.dot(x, y, preferred_element_type=F32)
    return pass_(a_hi, b_hi) + pass_(a_hi, b_lo) + pass_(a_lo, b_hi)


@jax.custom_vjp
def _nilpotent_inverse(m):
    r = m.shape[0]
    x = jnp.where(lax.broadcasted_iota(jnp.int32, (r, r), 0) == lax.broadcasted_iota(jnp.int32, (r, r), 1), 1.0, 0.0) + m
    p = m
    for _ in range(CHUNK_BITS - 1):
        p = _dot3(p, p)
        x = x + _dot3(x, p)
    return x


def _nilpotent_inverse_fwd(m):
    x = _nilpotent_inverse(m)
    return x, x


def _nilpotent_inverse_bwd(x, g):
    return (_dot(x, _dot(g, x, "nt"), "tn"),)


_nilpotent_inverse.defvjp(_nilpotent_inverse_fwd, _nilpotent_inverse_bwd)


def _dot_nt(a, b):
    return _dot(a, b, "nt")


def _dot_tn(a, b):
    return _dot(a, b, "tn")


def _dn_chunk(head, n_heads, aq, ak, v, z, dbda, a_log, dt_bias, dn_norm, state):
    r = aq.shape[0]
    lane_g = lax.broadcasted_iota(jnp.int32, dbda.shape, 1)
    db = jnp.sum(jnp.where(lane_g == head, dbda, 0.0), axis=1, keepdims=True)
    da = jnp.sum(jnp.where(lane_g == head + n_heads, dbda, 0.0), axis=1, keepdims=True)
    lane_h = lax.broadcasted_iota(jnp.int32, a_log.shape, 1)
    al = jnp.sum(jnp.where(lane_h == head, a_log, 0.0), axis=1, keepdims=True)
    dtb = jnp.sum(jnp.where(lane_h == head, dt_bias, 0.0), axis=1, keepdims=True)
    beta = _sigmoid(db)
    g = -jnp.exp(al) * _softplus(da + dtb)
    q = aq * lax.rsqrt(jnp.sum(aq * aq, axis=1, keepdims=True) + EPS) * (HEAD ** -0.5)
    k = ak * lax.rsqrt(jnp.sum(ak * ak, axis=1, keepdims=True) + EPS)
    ri = lax.broadcasted_iota(jnp.int32, (r, r), 0)
    ci = lax.broadcasted_iota(jnp.int32, (r, r), 1)
    same = (ri >> CHUNK_BITS) == (ci >> CHUNK_BITS)
    incl = same & (ri >= ci)
    g_row = jnp.sum(jnp.where(ri == ci, g, 0.0), axis=0, keepdims=True)
    gc_col = jnp.sum(jnp.where(incl, g_row, 0.0), axis=1, keepdims=True)
    gc_row = jnp.sum(jnp.where(same & (ri <= ci), g, 0.0), axis=0, keepdims=True)
    g_all = jnp.sum(jnp.where(same, g_row, 0.0), axis=1, keepdims=True)
    decay = jnp.where(incl, jnp.exp(jnp.where(incl, gc_col - gc_row, 0.0)), 0.0)
    kb = k * beta
    m = -jnp.where(same & (ri > ci), _dot_nt(kb, k) * decay, 0.0)
    x = _nilpotent_inverse(m)
    egc = jnp.exp(gc_col)
    wu_g = _dot(x, jnp.concatenate([kb * egc, v * beta], axis=1))
    w_g, u_g = wu_g[:, :HEAD], wu_g[:, HEAD:]
    qk = _dot_nt(q, k) * decay
    q_dec = q * egc
    k_dec = k * jnp.exp(g_all - gc_col)
    carry = jnp.exp(g_all)
    v_new, o_state = [], []
    for c in range(r // CHUNK):
        rows = slice(c * CHUNK, (c + 1) * CHUNK)
        v_new.append(u_g[rows] - _dot(w_g[rows], state))
        o_state.append(_dot(q_dec[rows], state))
        state = state * carry[c * CHUNK:c * CHUNK + 1] + _dot_tn(k_dec[rows], v_new[-1])
    o = jnp.concatenate(o_state, axis=0) + _dot(qk, jnp.concatenate(v_new, axis=0))
    o = o * lax.rsqrt(jnp.mean(o * o, axis=1, keepdims=True) + EPS) * dn_norm
    return o * _silu(z), state


def _dn_chunks(head, n_heads):
    return jax.vmap(functools.partial(_dn_chunk, head, n_heads), in_axes=(0, 0, 0, 0, 0, None, None, None, 0))


def _dn_layout(batch, seq, n_heads, small, reverse):
    grp = _tile(seq, DN_ROWS, CHUNK)
    rows = _tile(seq, DN_SEQ_BLOCK, grp)
    n_blocks, per = seq // rows, rows // grp
    at = (lambda j: n_blocks - 1 - j) if reverse else (lambda j: j)
    col = lambda off: pl.BlockSpec((batch, rows, HEAD), lambda j, h: (0, at(j), off + h))
    gates = pl.BlockSpec((batch, rows, 2 * n_heads), lambda j, h: (0, at(j), 0))
    states = pl.BlockSpec((batch, None, per, HEAD, HEAD), lambda j, h: (0, h, at(j), 0, 0))
    full = [pl.BlockSpec(a.shape, lambda j, h: (0, 0)) for a in small]
    return grp, n_blocks, per, col, gates, states, full


def _dn_fwd(y, z, dbda, a_log, dt_bias, dn_norm):
    batch, seq, dd = z.shape
    n_heads = dd // HEAD
    grp, n_blocks, per, col, gates, st_spec, full = _dn_layout(batch, seq, n_heads, (a_log, dt_bias, dn_norm), False)

    def body(q_ref, k_ref, v_ref, z_ref, g_ref, al_ref, dt_ref, nw_ref, o_ref, st_ref, carry):
        head = pl.program_id(1)
        al, dtb, nw = al_ref[...], dt_ref[...], nw_ref[...]

        @pl.when(pl.program_id(0) == 0)
        def _():
            carry[head] = jnp.zeros((batch, HEAD, HEAD), F32)

        def step(n, states):
            rows = pl.ds(pl.multiple_of(n * grp, grp), grp)
            for b in range(batch):
                st_ref[b, n] = states[b]
            out, states = _dn_chunks(head, n_heads)(q_ref[:, rows, :], k_ref[:, rows, :], v_ref[:, rows, :],
                                                    z_ref[:, rows, :], g_ref[:, rows, :], al, dtb, nw, states)
            o_ref[:, rows, :] = out
            return states

        carry[head] = lax.fori_loop(0, per, step, carry[head])

    return pl.pallas_call(
        body, name="dn_fwd", grid=(n_blocks, n_heads),
        in_specs=[col(0), col(n_heads), col(2 * n_heads), col(0), gates, *full], out_specs=[col(0), st_spec],
        out_shape=[SDS((batch, seq, dd), F32), SDS((batch, n_heads, seq // grp, HEAD, HEAD), F32)],
        scratch_shapes=[pltpu.VMEM((n_heads, batch, HEAD, HEAD), F32)], compiler_params=_params(2),
    )(y, y, y, z, dbda, a_log, dt_bias, dn_norm)


def _dn_bwd(y, z, dbda, a_log, dt_bias, dn_norm, states, d_out):
    batch, seq, dd = z.shape
    n_heads = dd // HEAD
    grp, n_blocks, per, col, gates, st_spec, full = _dn_layout(batch, seq, n_heads, (a_log, dt_bias, dn_norm), True)

    def body(q_ref, k_ref, v_ref, z_ref, g_ref, al_ref, dt_ref, nw_ref, st_ref, do_ref,
             dq_ref, dk_ref, dv_ref, dz_ref, dg_ref, dal_ref, ddt_ref, dnw_ref, carry):
        first, head = pl.program_id(0) == 0, pl.program_id(1)
        al, dtb, nw = al_ref[...], dt_ref[...], nw_ref[...]

        @pl.when(first & (head == 0))
        def _():
            dal_ref[...] = jnp.zeros_like(dal_ref)
            ddt_ref[...] = jnp.zeros_like(ddt_ref)
            dnw_ref[...] = jnp.zeros_like(dnw_ref)

        @pl.when(head == 0)
        def _():
            dg_ref[...] = jnp.zeros_like(dg_ref)

        @pl.when(first)
        def _():
            carry[head] = jnp.zeros((batch, HEAD, HEAD), F32)

        def step(i, acc):
            d_states, d_al, d_dt, d_nw = acc
            n = per - 1 - i
            rows = pl.ds(pl.multiple_of(n * grp, grp), grp)
            states = jnp.stack([st_ref[b, n] for b in range(batch)])
            _, vjp = jax.vjp(_dn_chunks(head, n_heads), q_ref[:, rows, :], k_ref[:, rows, :], v_ref[:, rows, :],
                             z_ref[:, rows, :], g_ref[:, rows, :], al, dtb, nw, states)
            gq, gk, gv, gz, gg, gal, gdt, gnw, d_states = vjp((do_ref[:, rows, :], d_states))
            dq_ref[:, rows, :] = gq
            dk_ref[:, rows, :] = gk
            dv_ref[:, rows, :] = gv
            dz_ref[:, rows, :] = gz
            dg_ref[:, rows, :] += gg
            return d_states, d_al + gal, d_dt + gdt, d_nw + gnw

        zero = lambda a: jnp.zeros(a.shape, F32)
        d_states, d_al, d_dt, d_nw = lax.fori_loop(0, per, step, (carry[head], zero(al), zero(dtb), zero(nw)))
        carry[head] = d_states
        dal_ref[...] += d_al
        ddt_ref[...] += d_dt
        dnw_ref[...] += d_nw

    out3 = SDS((batch, seq, dd), F32)
    return pl.pallas_call(
        body, name="dn_bwd", grid=(n_blocks, n_heads),
        in_specs=[col(0), col(n_heads), col(2 * n_heads), col(0), gates, *full, st_spec, col(0)],
        out_specs=[col(0), col(0), col(0), col(0), gates, *full],
        out_shape=[out3] * 4 + [SDS(dbda.shape, F32), SDS(a_log.shape, F32), SDS(dt_bias.shape, F32), SDS(dn_norm.shape, F32)],
        scratch_shapes=[pltpu.VMEM((n_heads, batch, HEAD, HEAD), F32)], compiler_params=_params(2),
    )(y, y, y, z, dbda, a_log, dt_bias, dn_norm, states, d_out)


def _my_slot():
    return 4 * lax.axis_index("x") + 2 * lax.axis_index("y") + lax.axis_index("c")


def _peer(k):
    x, y, c = lax.axis_index("x"), lax.axis_index("y"), lax.axis_index("c")
    return (x ^ (k >> 2), y ^ ((k >> 1) & 1), c ^ (k & 1)), (4 * x + 2 * y + c) ^ k


def _all_gather(name, block, after):
    def body(src, after_ref, dst, send_sems, recv_sems, local_sem):
        me = _my_slot()
        own = pltpu.make_async_copy(src, dst.at[me], local_sem)
        own.start()
        copies = []
        for k in range(1, N_DEV):
            peer, _ = _peer(k)
            copies.append(pltpu.make_async_remote_copy(
                src_ref=src, dst_ref=dst.at[me], send_sem=send_sems.at[k - 1], recv_sem=recv_sems.at[k - 1],
                device_id=peer, device_id_type=MESH))
            copies[-1].start()
        for k in range(1, N_DEV):
            peer, slot = _peer(k)
            pltpu.make_async_remote_copy(
                src_ref=src, dst_ref=dst.at[slot], send_sem=send_sems.at[k - 1], recv_sem=recv_sems.at[k - 1],
                device_id=peer, device_id_type=MESH).wait_recv()
        for cp in copies:
            cp.wait_send()
        own.wait()

    return pl.pallas_call(
        body, name=name, in_specs=[pl.BlockSpec(memory_space=pl.ANY)] * 2, out_specs=pl.BlockSpec(memory_space=pl.ANY),
        out_shape=SDS((N_DEV, *block.shape), block.dtype),
        scratch_shapes=[pltpu.SemaphoreType.DMA((N_DEV - 1,)), pltpu.SemaphoreType.DMA((N_DEV - 1,)), pltpu.SemaphoreType.DMA],
    )(block, after)


def _exchange_slices(name, parts, after):
    def body(src, after_ref, dst, send_sems, recv_sems, local_sem):
        me = _my_slot()
        own = pltpu.make_async_copy(src.at[me], dst.at[me], local_sem)
        own.start()
        copies = []
        for k in range(1, N_DEV):
            peer, slot = _peer(k)
            copies.append(pltpu.make_async_remote_copy(
                src_ref=src.at[slot], dst_ref=dst.at[me], send_sem=send_sems.at[k - 1], recv_sem=recv_sems.at[k - 1],
                device_id=peer, device_id_type=MESH))
            copies[-1].start()
        for k in range(1, N_DEV):
            peer, slot = _peer(k)
            pltpu.make_async_remote_copy(
                src_ref=src.at[me], dst_ref=dst.at[slot], send_sem=send_sems.at[k - 1], recv_sem=recv_sems.at[k - 1],
                device_id=peer, device_id_type=MESH).wait_recv()
        for cp in copies:
            cp.wait_send()
        own.wait()

    return pl.pallas_call(
        body, name=name, in_specs=[pl.BlockSpec(memory_space=pl.ANY)] * 2, out_specs=pl.BlockSpec(memory_space=pl.ANY),
        out_shape=SDS(parts.shape, parts.dtype),
        scratch_shapes=[pltpu.SemaphoreType.DMA((N_DEV - 1,)), pltpu.SemaphoreType.DMA((N_DEV - 1,)), pltpu.SemaphoreType.DMA],
    )(parts, after)


_HBM = pl.BlockSpec(memory_space=pltpu.HBM)
_SEM = pl.BlockSpec(memory_space=pltpu.SEMAPHORE)
_EFFECT = pltpu.SideEffectType.DATAFLOW_SIDE_EFFECTING


def _slot_operand():
    return _my_slot().astype(jnp.int32).reshape(1)


def _col_tile(r, c, bytes_per_element):
    return _tile(c, max(128, (12 * 2 ** 20) // (bytes_per_element * r) // 128 * 128), 128)


def _cast_place(name, block, dtype):
    r, c = block.shape
    tc = _col_tile(r, c, 6)

    def body(me_ref, src_ref, dst_ref):
        dst_ref[...] = src_ref[...].astype(dtype)

    return pl.pallas_call(
        body, name=name, out_shape=SDS((N_DEV, r, c), dtype), compiler_params=_params(1),
        grid_spec=pltpu.PrefetchScalarGridSpec(
            num_scalar_prefetch=1, grid=(c // tc,), in_specs=[pl.BlockSpec((r, tc), lambda i, me: (0, i))],
            out_specs=pl.BlockSpec((None, r, tc), lambda i, me: (me[0], 0, i))),
    )(_slot_operand(), block)


_SIBLING = 1
_SAME_CORE = (2, 4, 6)
_OTHER_CORE = (3, 5, 7)


def _gather_start(name, lands):
    n = len(lands)

    def body(*refs):
        lnds, outs = refs[:n], refs[n:]
        me = _my_slot()
        for i in range(n):
            for k in (*_SAME_CORE, _SIBLING):
                peer, _ = _peer(k)
                pltpu.make_async_remote_copy(
                    src_ref=lnds[i].at[me], dst_ref=lnds[i].at[me], send_sem=outs[2 * i].at[k - 1],
                    recv_sem=outs[2 * i + 1].at[k - 1], device_id=peer, device_id_type=MESH).start()
        outs[-1][...] = jnp.zeros_like(outs[-1])

    res = pl.pallas_call(
        body, name=name, in_specs=[_HBM] * n,
        out_specs=[_SEM] * (2 * n) + [_HBM] * n + [pl.BlockSpec(memory_space=pltpu.VMEM)],
        out_shape=[pltpu.SemaphoreType.DMA((N_DEV - 1,))] * (2 * n) + [pltpu.HBM(a.shape, a.dtype) for a in lands]
        + [SDS((8, 128), F32)],
        input_output_aliases={i: 2 * n + i for i in range(n)},
        compiler_params=pltpu.CompilerParams(has_side_effects=_EFFECT),
    )(*[pltpu.with_memory_space_constraint(a, pltpu.HBM) for a in lands])
    return [(res[2 * i], res[2 * i + 1], res[2 * n + i]) for i in range(n)], res[-1]


def _gather_copy(land_ref, send_ref, recv_ref, k, slot, to):
    return pltpu.make_async_remote_copy(
        src_ref=land_ref.at[slot], dst_ref=land_ref.at[slot], send_sem=send_ref.at[k - 1], recv_sem=recv_ref.at[k - 1],
        device_id=to, device_id_type=MESH)


def _gather_arrived(name, started, after):
    send_sems, recv_sems, land = started

    def body(land_ref, send_ref, recv_ref, after_ref, land_out):
        for k in _SAME_CORE:
            peer, slot = _peer(k)
            _gather_copy(land_ref, send_ref, recv_ref, k, slot, peer).wait_recv()

    return pl.pallas_call(
        body, name=name, in_specs=[_HBM, _SEM, _SEM, pl.BlockSpec(memory_space=pl.ANY)], out_specs=[_HBM],
        out_shape=[pltpu.HBM(land.shape, land.dtype)], input_output_aliases={0: 0},
        compiler_params=pltpu.CompilerParams(has_side_effects=_EFFECT),
    )(land, send_sems, recv_sems, after)[0]


def _gather_forward(name, land):
    def body(land_ref, send_ref, recv_ref, land_out, token):
        sibling, _ = _peer(_SIBLING)
        for j, k in enumerate(_SAME_CORE):
            _, slot = _peer(k)
            _gather_copy(land_ref, send_ref, recv_ref, j + 1, slot, sibling).start()
        token[...] = jnp.zeros_like(token)

    res = pl.pallas_call(
        body, name=name, in_specs=[_HBM], out_specs=[_SEM, _SEM, _HBM, pl.BlockSpec(memory_space=pltpu.VMEM)],
        out_shape=[pltpu.SemaphoreType.DMA((len(_SAME_CORE),))] * 2 + [pltpu.HBM(land.shape, land.dtype), SDS((8, 128), F32)],
        input_output_aliases={0: 2}, compiler_params=pltpu.CompilerParams(has_side_effects=_EFFECT),
    )(land)
    return tuple(res[:3]), res[3]


def _gather_wait(name, started, forwarded, after):
    send_sems, recv_sems, _ = started
    send_fwd, recv_fwd, land = forwarded

    def body(land_ref, send_ref, recv_ref, send2_ref, recv2_ref, after_ref, land_out):
        sibling, slot = _peer(_SIBLING)
        _gather_copy(land_ref, send_ref, recv_ref, _SIBLING, slot, sibling).wait_recv()
        for j, k in enumerate(_OTHER_CORE):
            _, slot = _peer(k)
            _gather_copy(land_ref, send2_ref, recv2_ref, j + 1, slot, sibling).wait_recv()
        for k in (_SIBLING, *_SAME_CORE):
            peer, slot = _peer(k)
            _gather_copy(land_ref, send_ref, recv_ref, k, slot, peer).wait_send()
        for j, k in enumerate(_SAME_CORE):
            _, slot = _peer(k)
            _gather_copy(land_ref, send2_ref, recv2_ref, j + 1, slot, sibling).wait_send()

    return pl.pallas_call(
        body, name=name, in_specs=[_HBM, _SEM, _SEM, _SEM, _SEM, pl.BlockSpec(memory_space=pl.ANY)], out_specs=[_HBM],
        out_shape=[pltpu.HBM(land.shape, land.dtype)], input_output_aliases={0: 0},
        compiler_params=pltpu.CompilerParams(has_side_effects=_EFFECT),
    )(land, send_sems, recv_sems, send_fwd, recv_fwd, after)[0]


def _scatter_start(name, parts):
    land = lax.empty(parts.shape, parts.dtype)

    def body(src, lnd, send_ref, recv_ref, src_out, lnd_out, token):
        me = _my_slot()
        for k in range(1, N_DEV):
            peer, slot = _peer(k)
            pltpu.make_async_remote_copy(
                src_ref=src.at[slot], dst_ref=lnd.at[me], send_sem=send_ref.at[k - 1], recv_sem=recv_ref.at[k - 1],
                device_id=peer, device_id_type=MESH).start()
        token[...] = jnp.zeros_like(token)

    res = pl.pallas_call(
        body, name=name, in_specs=[_HBM, _HBM],
        out_specs=[_SEM, _SEM, _HBM, _HBM, pl.BlockSpec(memory_space=pltpu.VMEM)],
        out_shape=[pltpu.SemaphoreType.DMA((N_DEV - 1,))] * 2 + [pltpu.HBM(parts.shape, parts.dtype)] * 2 + [SDS((8, 128), F32)],
        input_output_aliases={0: 2, 1: 3}, compiler_params=pltpu.CompilerParams(has_side_effects=_EFFECT),
    )(pltpu.with_memory_space_constraint(parts, pltpu.HBM), pltpu.with_memory_space_constraint(land, pltpu.HBM))
    return tuple(res[:4]), res[4]


def _scatter_wait(name, started, after):
    send_sems, recv_sems, parts, land = started

    def body(src_ref, land_ref, send_ref, recv_ref, after_ref, src_out, land_out):
        me = _my_slot()
        for k in range(1, N_DEV):
            peer, slot = _peer(k)
            copy = pltpu.make_async_remote_copy(
                src_ref=src_ref.at[me], dst_ref=land_ref.at[slot], send_sem=send_ref.at[k - 1],
                recv_sem=recv_ref.at[k - 1], device_id=peer, device_id_type=MESH)
            copy.wait_send()
            copy.wait_recv()

    return pl.pallas_call(
        body, name=name, in_specs=[_HBM, _HBM, _SEM, _SEM, pl.BlockSpec(memory_space=pl.ANY)], out_specs=[_HBM, _HBM],
        out_shape=[pltpu.HBM(parts.shape, parts.dtype), pltpu.HBM(land.shape, land.dtype)], input_output_aliases={0: 0, 1: 1},
        compiler_params=pltpu.CompilerParams(has_side_effects=_EFFECT),
    )(parts, land, send_sems, recv_sems, after)


def _adamw(name, landed, own, w, m, v):
    r, c = w.shape
    tc = _col_tile(r, c, 46)
    bc1 = 1.0 / (1.0 - ADAM_B1 ** ADAM_STEP)
    bc2 = 1.0 / (1.0 - ADAM_B2 ** ADAM_STEP)

    def body(me_ref, p_ref, own_ref, w_ref, m_ref, v_ref, g_ref, d_ref, nm_ref, nv_ref):
        me = me_ref[0]
        g = jnp.zeros(w_ref.shape, F32)
        for s in range(N_DEV):
            g = g + jnp.where(me == s, own_ref[...], p_ref[s]).astype(F32)
        nm = ADAM_B1 * m_ref[...] + (1.0 - ADAM_B1) * g
        nv = ADAM_B2 * v_ref[...] + (1.0 - ADAM_B2) * (g * g)
        g_ref[...] = g
        nm_ref[...] = nm
        nv_ref[...] = nv
        d_ref[...] = -ADAM_LR * ((nm * bc1) / (jnp.sqrt(nv * bc2) + ADAM_EPS) + ADAM_WD * w_ref[...])

    blk = pl.BlockSpec((r, tc), lambda i, me: (0, i))
    return pl.pallas_call(
        body, name=name, out_shape=[SDS((r, c), F32)] * 4, compiler_params=_params(1),
        grid_spec=pltpu.PrefetchScalarGridSpec(
            num_scalar_prefetch=1, grid=(c // tc,),
            in_specs=[pl.BlockSpec((N_DEV, r, tc), lambda i, me: (0, 0, i)), pl.BlockSpec((None, r, tc), lambda i, me: (me[0], 0, i)),
                      blk, blk, blk],
            out_specs=[blk] * 4),
    )(_slot_operand(), landed, own, w, m, v)


def _ffn_fwd(name, h, norm, fetch, prefetch, landed, ahead=None):
    n = _rms_fwd(name + "_norm", h, norm)
    wg = fetch(name + "_w_gate", n)
    gate = _ffn_proj(name + "_gate", n, wg)
    sent = prefetch(name + "_w_down", gate) if landed else None
    wu = fetch(name + "_w_up", gate)
    up, act = _ffn_proj(name + "_up", n, wu, gate, after=sent)
    sent = prefetch(ahead, act) if ahead else None
    wd = fetch(name + "_w_down", act)
    return _ffn_down(act, wd, h, 0.5, after=sent), (n, gate, up, act, wg, wu, wd)


def _ffn_bwd(name, h, norm, saved, dh, dy_b, scale_out, emit):
    n, gate, up, act, wg, wu, wd = saved
    sent = emit(name + "_w_down", _wgrad_rows(name + "_dwd", act, dy_b))
    d_gate, d_up = _ffn_bwd_act(dy_b, wd, gate, up, after=sent)
    sent = emit(name + "_w_gate", _wgrad_rows(name + "_dwg", d_gate, n, after=sent))
    sent = emit(name + "_w_up", _wgrad_rows(name + "_dwu", d_up, n, after=sent))
    dn = _dgrad_cols(name + "_dn", (d_gate, d_up), (wg, wu), after=sent)
    return _rms_bwd(name + "_norm_bwd", h, dn, norm, dh, scale_out)


def _local_step(x, target, norms, small, fetch, prefetch, emit, *, batch, seq):
    n1w, nmw, n2w, nfw = norms
    a_log, dt_bias, dn_norm = small
    t, d = x.shape

    h1, saved1 = _ffn_fwd("ffn1", x, n1w, fetch, prefetch, False)
    nm = _rms_fwd("mix_norm", h1, nmw)
    w_in = fetch("w_in", nm)
    p = w_in.shape[1]
    proj = _in_proj(nm, w_in)
    conv_all = fetch("conv_w", proj)
    proj = jnp.swapaxes(proj, 0, 1).reshape(t, N_DEV * p)
    conv_w = jnp.swapaxes(conv_all, 0, 1).reshape(CONV_WIDTH, N_DEV * conv_all.shape[2])
    dd = conv_w.shape[1] // 3
    da = (N_DEV * p - 4 * dd - 2 * (dd // HEAD)) // 3
    qkv = proj[:, :3 * da].astype(BF16)
    z = proj[:, 3 * da + 3 * dd:3 * da + 4 * dd]
    dbda = proj[:, 3 * da + 4 * dd:]
    attn, lse = _attn_fwd(qkv, batch=batch, seq=seq)
    yd = _conv_silu_fwd(proj, 3 * da, conv_w, batch=batch, seq=seq)
    per_seq = lambda a: a.reshape(batch, seq, a.shape[1])
    dn_in = (per_seq(yd), per_seq(z), per_seq(dbda), a_log, dt_bias, dn_norm)
    dn_out, dn_states = _dn_fwd(*dn_in)
    cat = jnp.concatenate([attn, dn_out.reshape(t, dd)], axis=1).astype(BF16)
    w_out = fetch("w_out", cat)
    w_out2 = w_out.reshape(da + dd, d)
    sent = prefetch("ffn2_w_up", prefetch("ffn2_w_gate", cat))
    h2 = _out_proj(cat, w_out2, h1, after=sent)
    h3, saved2 = _ffn_fwd("ffn2", h2, n2w, fetch, prefetch, True)

    loss, dh3, dh3_b, g_nf = _loss_head(h3, nfw, target, 0.5)
    dh2, dh2_b, g_n2 = _ffn_bwd("ffn2", h2, n2w, saved2, dh3, dh3_b, 1.0, emit)

    sent = emit("w_out", _wgrad_full("dw_out", cat, dh2_b).reshape(w_out.shape))
    d_attn = _dgrad_full("d_attn", dh2_b, w_out2[:da], after=sent)
    d_dn = _dgrad_full("d_dn", dh2_b, w_out2[da:])
    dq, dk, dv = _attn_bwd(qkv, attn, lse, d_attn, batch=batch, seq=seq)
    gq, gk, gv, gz, g_dbda, g_alog, g_dtb, g_dnn = _dn_bwd(*dn_in, dn_states, per_seq(d_dn))
    gq, gk, gv, gz, g_dbda = (a.reshape(t, a.shape[2]) for a in (gq, gk, gv, gz, g_dbda))
    d_xd, g_conv = _conv_silu_bwd(proj, 3 * da, conv_w, jnp.concatenate([gq, gk, gv], axis=1), batch=batch, seq=seq)
    dproj = jnp.concatenate([dq, dk, dv, d_xd, gz, g_dbda], axis=1).astype(BF16)
    dproj = jnp.swapaxes(dproj.reshape(t, N_DEV, p), 0, 1)
    sent = emit("w_in", _wgrad_rows("dw_in", dproj, nm))
    dnm = _dgrad_cols("d_mix_in", (dproj,), (w_in,), after=sent)
    dh1, dh1_b, g_nm = _rms_bwd("mix_norm_bwd", h1, dnm, nmw, dh2, 0.5)

    dx, _, g_n1 = _ffn_bwd("ffn1", x, n1w, saved1, dh1, dh1_b, 1.0, emit)
    return loss, dx, (g_n1, g_nm, g_n2, g_nf), (g_alog, g_dtb, g_dnn), g_conv


def _pack_rows(vectors):
    rows, offsets, r = [], [], 0
    for vec in vectors:
        n = -(-vec.size // 128)
        rows.append(jnp.pad(vec.reshape(-1), (0, n * 128 - vec.size)).reshape(n, 128))
        offsets.append((r, vec.size, vec.shape))
        r += n
    pad = -r % 8
    if pad:
        rows.append(jnp.zeros((pad, 128), F32))
    return jnp.concatenate(rows, axis=0), offsets


def _unpack_rows(packed, offsets):
    return [packed[r:r + -(-size // 128)].reshape(-1)[:size].reshape(shape) for r, size, shape in offsets]


def kernel(x, ffn1_norm, ffn1_w_gate, ffn1_w_up, ffn1_w_down, mix_norm, w_in, conv_w, a_log, dt_bias, dn_norm, w_out, ffn2_norm, ffn2_w_gate, ffn2_w_up, ffn2_w_down, final_norm, loss_target, m_ffn1_norm, m_ffn1_w_gate, m_ffn1_w_up, m_ffn1_w_down, m_mix_norm, m_w_in, m_conv_w, m_a_log, m_dt_bias, m_dn_norm, m_w_out, m_ffn2_norm, m_ffn2_w_gate, m_ffn2_w_up, m_ffn2_w_down, m_final_norm, v_ffn1_norm, v_ffn1_w_gate, v_ffn1_w_up, v_ffn1_w_down, v_mix_norm, v_w_in, v_conv_w, v_a_log, v_dt_bias, v_dn_norm, v_w_out, v_ffn2_norm, v_ffn2_w_gate, v_ffn2_w_up, v_ffn2_w_down, v_final_norm):
    batch, seq, d = x.shape
    t = batch * seq
    big = dict(ffn1_w_gate=(ffn1_w_gate, m_ffn1_w_gate, v_ffn1_w_gate), ffn1_w_up=(ffn1_w_up, m_ffn1_w_up, v_ffn1_w_up),
               ffn1_w_down=(ffn1_w_down, m_ffn1_w_down, v_ffn1_w_down), w_in=(w_in, m_w_in, v_w_in),
               w_out=(w_out, m_w_out, v_w_out), ffn2_w_gate=(ffn2_w_gate, m_ffn2_w_gate, v_ffn2_w_gate),
               ffn2_w_up=(ffn2_w_up, m_ffn2_w_up, v_ffn2_w_up), ffn2_w_down=(ffn2_w_down, m_ffn2_w_down, v_ffn2_w_down))
    by_columns = ("ffn1_w_gate", "ffn1_w_up", "w_in", "ffn2_w_gate", "ffn2_w_up")
    for name in by_columns:
        big[name] = tuple(a.T for a in big[name])
    rep = dict(ffn1_norm=(ffn1_norm, m_ffn1_norm, v_ffn1_norm), mix_norm=(mix_norm, m_mix_norm, v_mix_norm),
               ffn2_norm=(ffn2_norm, m_ffn2_norm, v_ffn2_norm), final_norm=(final_norm, m_final_norm, v_final_norm),
               a_log=(a_log, m_a_log, v_a_log), dt_bias=(dt_bias, m_dt_bias, v_dt_bias), dn_norm=(dn_norm, m_dn_norm, v_dn_norm))

    lands = {"conv_w": _cast_place("place_conv_w", conv_w, F32)}
    lands.update({name: _cast_place("place_" + name, w, BF16) for name, (w, _, _) in big.items()})
    started, token = _gather_start("gather_start", list(lands.values()))
    gathering = dict(zip(lands, started))
    gathered, scattering = {}, {}

    forwarding = {}

    def prefetch(name, after):
        if name not in forwarding:
            land = _gather_arrived("gather_arrived_" + name, gathering[name], after)
            forwarding[name] = _gather_forward("gather_forward_" + name, land)
        return forwarding[name][1]

    def fetch(name, after):
        if name not in gathered:
            prefetch(name, after)
            gathered[name] = _gather_wait("gather_wait_" + name, gathering[name], forwarding[name][0], after)
        return gathered[name]

    def emit(name, grad):
        scattering[name], sent = _scatter_start("scatter_start_" + name, grad)
        return sent

    row = lambda a: a.reshape(1, -1)
    norms = [row(rep[n][0]) for n in ("ffn1_norm", "mix_norm", "ffn2_norm", "final_norm")]
    norms[0] = norms[0] + token[0, 0]
    loss, dx, g_norms, g_small, g_conv = _local_step(
        x.reshape(t, d), loss_target.reshape(t, d), norms, [row(rep[n][0]) for n in ("a_log", "dt_bias", "dn_norm")],
        fetch, prefetch, emit, batch=batch, seq=seq)

    out = {"grad_x": dx.reshape(x.shape)}
    after = dx
    for name in scattering:
        w, m, v = big[name]
        own, landed = _scatter_wait("scatter_wait_" + name, scattering[name], after)
        res = _adamw("adamw_" + name, landed, own, w, m, v)
        after = res[0]
        out["grad_" + name], out["delta_" + name], out["new_m_" + name], out["new_v_" + name] = (
            [a.T for a in res] if name in by_columns else res)
    conv_parts = jnp.swapaxes(g_conv.reshape(CONV_WIDTH, N_DEV, conv_w.shape[1]), 0, 1)
    parts = _exchange_slices("scatter_conv_w", conv_parts, after)
    out["grad_conv_w"], out["delta_conv_w"], out["new_m_conv_w"], out["new_v_conv_w"] = _adamw("adamw_conv_w", parts, parts, conv_w, m_conv_w, v_conv_w)

    rep_names = list(rep)
    g_rep = [*g_norms, *g_small]
    packed_g, offsets = _pack_rows([*g_rep, loss[:, :1]])
    packed = [_pack_rows([*[rep[n][i] for n in rep_names], jnp.zeros((1, 1), F32)])[0] for i in range(3)]
    parts = _all_gather("gather_small_grads", packed_g, out["grad_conv_w"])
    res = [_unpack_rows(a, offsets) for a in _adamw("adamw_small", parts, parts, *packed)]
    for i, name in enumerate(rep_names):
        shape = rep[name][0].shape
        out["grad_" + name], out["delta_" + name], out["new_m_" + name], out["new_v_" + name] = (r[i].reshape(shape) for r in res)
    out["loss"] = res[0][-1].reshape(())

    order = ["ffn1_norm", "ffn1_w_gate", "ffn1_w_up", "ffn1_w_down", "mix_norm", "w_in", "conv_w", "a_log", "dt_bias", "dn_norm",
             "w_out", "ffn2_norm", "ffn2_w_gate", "ffn2_w_up", "ffn2_w_down", "final_norm"]
    return (out["loss"], out["grad_x"], *[out["grad_" + n] for n in order], *[out["delta_" + n] for n in order],
            *[out["new_m_" + n] for n in order], *[out["new_v_" + n] for n in order])
```

```python
import functools

import jax
import jax.numpy as jnp
from jax import lax
from jax.experimental import pallas as pl
from jax.experimental.pallas import tpu as pltpu

F32 = jnp.float32
BF16 = jnp.bfloat16
N_DEV = 8
HEAD = 128
CHUNK = 128
CHUNK_BITS = 7
DN_ROWS = 256
DN_SEQ_BLOCK = 1024
CONV_WIDTH = 4
EPS = 1e-6
DILATED_CONFIGS = ((128, 1), (512, 4), (2048, 16))
ATTN_BLOCK = 1024
NEG = -1e30
ADAM_LR, ADAM_B1, ADAM_B2, ADAM_EPS, ADAM_WD, ADAM_STEP = 0.001, 0.9, 0.999, 1e-08, 0.01, 10
MESH = pl.DeviceIdType.MESH
SDS = jax.ShapeDtypeStruct


def _tile(n, pref, align):
    t = (min(n, pref) // align) * align
    while t >= align:
        if n % t == 0:
            return t
        t -= align
    return n


def _params(n_axes, vmem_mb=48):
    return pltpu.CompilerParams(dimension_semantics=("arbitrary",) * n_axes, vmem_limit_bytes=vmem_mb * 2 ** 20)


def _sigmoid(x):
    return 1.0 / (1.0 + jnp.exp(-x))


def _silu(x):
    return x * _sigmoid(x)


def _softplus(x):
    return jnp.maximum(x, 0.0) + jnp.log(1.0 + jnp.exp(-jnp.abs(x)))


_DIMS = {"nn": (((1,), (0,)), ((), ())), "nt": (((1,), (1,)), ((), ())), "tn": (((0,), (0,)), ((), ()))}


def _mm_call(name, grid, mode, pairs, operands, in_specs, out_shape, out_specs, acc_shapes, epilogue, vmem_mb=48, after=None):
    dims = _DIMS[mode]
    if after is not None:
        operands, in_specs = (*operands, after), [*in_specs, pl.BlockSpec(memory_space=pl.ANY)]
    n_in, n_out = len(operands), len(out_shape)
    nk = grid[-1]

    def whole(*refs):
        ins, outs = refs[:n_in], refs[n_in:]
        sums = {}
        for a, b, c in pairs:
            prod = lax.dot_general(ins[a][...], ins[b][...], dims, preferred_element_type=F32)
            sums[c] = prod if c not in sums else sums[c] + prod
        epilogue(ins, outs, [sums[c] for c in sorted(sums)])

    if acc_shapes is None:
        return pl.pallas_call(
            whole, name=name, grid=grid, in_specs=in_specs, out_specs=out_specs, out_shape=out_shape,
            compiler_params=_params(len(grid), vmem_mb))(*operands)

    def body(*refs):
        ins, outs, accs = refs[:n_in], refs[n_in:n_in + n_out], refs[n_in + n_out:]
        k = pl.program_id(len(grid) - 1)

        @pl.when(k == 0)
        def _():
            for acc in accs:
                acc[...] = jnp.zeros_like(acc)

        sums = {}
        for a, b, c in pairs:
            prod = lax.dot_general(ins[a][...], ins[b][...], dims, preferred_element_type=F32)
            sums[c] = prod if c not in sums else sums[c] + prod
        for c, total in sums.items():
            accs[c][...] += total

        @pl.when(k == nk - 1)
        def _():
            epilogue(ins, outs, [acc[...] for acc in accs])

    return pl.pallas_call(
        body, name=name, grid=grid, in_specs=in_specs, out_specs=out_specs, out_shape=out_shape,
        scratch_shapes=[pltpu.VMEM(s, F32) for s in acc_shapes], compiler_params=_params(len(grid), vmem_mb),
    )(*operands)


def _ffn_proj(name, n, w, gate=None, after=None):
    t, d = n.shape
    f = w.shape[1]
    tm = _tile(t, 256, 16)
    n_spec = pl.BlockSpec((tm, d), lambda s, m, k: (m, 0))
    w_spec = pl.BlockSpec((None, f, d), lambda s, m, k: (s, 0, 0))
    o_spec = pl.BlockSpec((None, tm, f), lambda s, m, k: (s, m, 0))
    o_shape = SDS((N_DEV, t, f), BF16)
    grid = (N_DEV, t // tm, 1)
    if gate is None:
        return _mm_call(name, grid, "nt", [(0, 1, 0)], (n, w), [n_spec, w_spec], [o_shape], [o_spec], None, _store_bf16,
                        after=after)[0]

    def up_out(ins, outs, accs):
        outs[0][...] = accs[0].astype(BF16)
        outs[1][...] = (_silu(ins[2][...].astype(F32)) * accs[0]).astype(BF16)

    return _mm_call(name, grid, "nt", [(0, 1, 0)], (n, w, gate), [n_spec, w_spec, o_spec], [o_shape] * 2, [o_spec] * 2,
                    None, up_out, after=after)


def _ffn_down(act, wd, resid, scale, after=None):
    _, t, f = act.shape
    d = wd.shape[2]
    tm, tn = _tile(t, 1024, 16), _tile(d, 1024, 128)

    def epilogue(ins, outs, accs):
        outs[0][...] = ins[2][...] + scale * accs[0]

    rc = pl.BlockSpec((tm, tn), lambda m, n, s: (m, n))
    return _mm_call(
        "ffn_down", (t // tm, d // tn, N_DEV), "nn", [(0, 1, 0)], (act, wd, resid),
        [pl.BlockSpec((None, tm, f), lambda m, n, s: (s, m, 0)), pl.BlockSpec((None, f, tn), lambda m, n, s: (s, 0, n)), rc],
        [SDS((t, d), F32)], [rc], [(tm, tn)], epilogue, after=after)[0]


def _in_proj(n, w):
    t, d = n.shape
    p = w.shape[1]
    tm = _tile(t, 256, 16)
    return _mm_call(
        "in_proj", (N_DEV, t // tm, 1), "nt", [(0, 1, 0)], (n, w),
        [pl.BlockSpec((tm, d), lambda s, m, k: (m, 0)), pl.BlockSpec((None, p, d), lambda s, m, k: (s, 0, 0))],
        [SDS((N_DEV, t, p), F32)], [pl.BlockSpec((None, tm, p), lambda s, m, k: (s, m, 0))], None, _store_f32)[0]


def _out_proj(cat, w, resid, after=None):
    t, dm = cat.shape
    d = w.shape[1]
    tm, tn = _tile(t, 512, 16), _tile(d, 1024, 128)

    def epilogue(ins, outs, accs):
        outs[0][...] = ins[2][...] + accs[0]

    rc = pl.BlockSpec((tm, tn), lambda n, m, k: (m, n))
    return _mm_call(
        "out_proj", (d // tn, t // tm, 1), "nn", [(0, 1, 0)], (cat, w, resid),
        [pl.BlockSpec((tm, dm), lambda n, m, k: (m, 0)), pl.BlockSpec((dm, tn), lambda n, m, k: (0, n)), rc],
        [SDS((t, d), F32)], [rc], None, epilogue, after=after)[0]


def _ffn_bwd_act(dy, wd, gate, up, after=None):
    t, d = dy.shape
    f = wd.shape[1]
    tm = _tile(t, 256, 16)

    def epilogue(ins, outs, accs):
        g, u = ins[2][...].astype(F32), ins[3][...].astype(F32)
        sg = _sigmoid(g)
        outs[0][...] = (accs[0] * u * sg * (1.0 + g * (1.0 - sg))).astype(BF16)
        outs[1][...] = (accs[0] * g * sg).astype(BF16)

    o_spec = pl.BlockSpec((None, tm, f), lambda s, m, k: (s, m, 0))
    return _mm_call(
        "ffn_bwd_act", (N_DEV, t // tm, 1), "nt", [(0, 1, 0)], (dy, wd, gate, up),
        [pl.BlockSpec((tm, d), lambda s, m, k: (m, 0)), pl.BlockSpec((None, f, d), lambda s, m, k: (s, 0, 0)), o_spec, o_spec],
        [SDS((N_DEV, t, f), BF16)] * 2, [o_spec] * 2, None, epilogue, after=after)


def _store_bf16(ins, outs, accs):
    outs[0][...] = accs[0].astype(BF16)


def _store_f32(ins, outs, accs):
    outs[0][...] = accs[0]


def _wgrad_rows(name, a, b, after=None):
    _, t, m = a.shape
    n = b.shape[1]
    tn, tk = _tile(n, 512 if m <= 1408 else 256, 128), t
    return _mm_call(
        name, (N_DEV, n // tn, t // tk), "tn", [(0, 1, 0)], (a, b),
        [pl.BlockSpec((None, tk, m), lambda s, j, k: (s, k, 0)), pl.BlockSpec((tk, tn), lambda s, j, k: (k, j))],
        [SDS((N_DEV, m, n), BF16)], [pl.BlockSpec((None, m, tn), lambda s, j, k: (s, 0, j))], None, _store_bf16,
        after=after)[0]


def _wgrad_full(name, a, b, after=None):
    t, m = a.shape
    n = b.shape[1]
    tm, tn, tk = _tile(m, 512, 128), _tile(n, 1024, 128), t
    return _mm_call(
        name, (m // tm, n // tn, t // tk), "tn", [(0, 1, 0)], (a, b),
        [pl.BlockSpec((tk, tm), lambda i, j, k: (k, i)), pl.BlockSpec((tk, tn), lambda i, j, k: (k, j))],
        [SDS((m, n), BF16)], [pl.BlockSpec((tm, tn), lambda i, j, k: (i, j))], None, _store_bf16, after=after)[0]


def _dgrad_cols(name, grads, weights, after=None):
    _, t, n = grads[0].shape
    m = weights[0].shape[2]
    tm, tn = _tile(t, 1024, 16), _tile(m, 1024, 128)
    k = len(grads)
    return _mm_call(
        name, (t // tm, m // tn, N_DEV), "nn", [(i, k + i, 0) for i in range(k)], (*grads, *weights),
        [pl.BlockSpec((None, tm, n), lambda i, j, s: (s, i, 0))] * k + [pl.BlockSpec((None, n, tn), lambda i, j, s: (s, 0, j))] * k,
        [SDS((t, m), F32)], [pl.BlockSpec((tm, tn), lambda i, j, s: (i, j))], [(tm, tn)], _store_f32, after=after)[0]


def _dgrad_full(name, g, w, after=None):
    t, n = g.shape
    m = w.shape[0]
    tm, tn, tk = _tile(t, 512, 16), _tile(m, 1024, 128), n
    return _mm_call(
        name, (m // tn, t // tm, n // tk), "nt", [(0, 1, 0)], (g, w),
        [pl.BlockSpec((tm, tk), lambda j, i, k: (i, k)), pl.BlockSpec((tn, tk), lambda j, i, k: (j, k))],
        [SDS((t, m), F32)], [pl.BlockSpec((tm, tn), lambda j, i, k: (i, j))], None, _store_f32, after=after)[0]


def _rms_fwd(name, h, w):
    t, d = h.shape
    tm = _tile(t, 256, 16)

    def body(h_ref, w_ref, o_ref):
        x = h_ref[...]
        o_ref[...] = (x * lax.rsqrt(jnp.mean(x * x, axis=1, keepdims=True) + EPS) * w_ref[...]).astype(BF16)

    row = pl.BlockSpec((tm, d), lambda i: (i, 0))
    return pl.pallas_call(
        body, name=name, grid=(t // tm,), in_specs=[row, pl.BlockSpec((1, d), lambda i: (0, 0))], out_specs=row,
        out_shape=SDS((t, d), BF16), compiler_params=_params(1))(h, w)


def _rms_bwd(name, h, dn, w, dres, scale):
    t, d = h.shape
    tm = _tile(t, 128, 16)

    def body(h_ref, dn_ref, w_ref, dres_ref, dh_ref, dhb_ref, dw_ref):
        @pl.when(pl.program_id(0) == 0)
        def _():
            dw_ref[...] = jnp.zeros_like(dw_ref)

        x = h_ref[...]
        rstd = lax.rsqrt(jnp.mean(x * x, axis=1, keepdims=True) + EPS)
        nhat = x * rstd
        g = dn_ref[...]
        gw = g * w_ref[...]
        dh = dres_ref[...] + rstd * (gw - nhat * jnp.mean(gw * nhat, axis=1, keepdims=True))
        dh_ref[...] = dh
        dhb_ref[...] = (scale * dh).astype(BF16)
        dw_ref[...] += jnp.sum(g * nhat, axis=0, keepdims=True)

    row = pl.BlockSpec((tm, d), lambda i: (i, 0))
    vec = pl.BlockSpec((1, d), lambda i: (0, 0))
    return pl.pallas_call(
        body, name=name, grid=(t // tm,), in_specs=[row, row, vec, row], out_specs=[row, row, vec],
        out_shape=[SDS((t, d), F32), SDS((t, d), BF16), SDS((1, d), F32)], compiler_params=_params(1))(h, dn, w, dres)


def _loss_head(h, w, target, scale):
    t, d = h.shape
    tm = _tile(t, 128, 16)

    def body(h_ref, w_ref, tg_ref, loss_ref, dh_ref, dhb_ref, dw_ref):
        @pl.when(pl.program_id(0) == 0)
        def _():
            dw_ref[...] = jnp.zeros_like(dw_ref)
            loss_ref[...] = jnp.zeros_like(loss_ref)

        x = h_ref[...]
        rstd = lax.rsqrt(jnp.mean(x * x, axis=1, keepdims=True) + EPS)
        nhat = x * rstd
        wv = w_ref[...]
        err = nhat * wv - tg_ref[...]
        loss_ref[...] += 0.5 * jnp.sum(jnp.mean(err * err, axis=1, keepdims=True), axis=0, keepdims=True)
        g = err * (1.0 / d)
        gw = g * wv
        dh = rstd * (gw - nhat * jnp.mean(gw * nhat, axis=1, keepdims=True))
        dh_ref[...] = dh
        dhb_ref[...] = (scale * dh).astype(BF16)
        dw_ref[...] += jnp.sum(g * nhat, axis=0, keepdims=True)

    row = pl.BlockSpec((tm, d), lambda i: (i, 0))
    vec = pl.BlockSpec((1, d), lambda i: (0, 0))
    return pl.pallas_call(
        body, name="loss_head", grid=(t // tm,), in_specs=[row, vec, row],
        out_specs=[pl.BlockSpec((1, 128), lambda i: (0, 0)), row, row, vec],
        out_shape=[SDS((1, 128), F32), SDS((t, d), F32), SDS((t, d), BF16), SDS((1, d), F32)],
        compiler_params=_params(1))(h, w, target)


def _attn_bias(delta, blk):
    dist = (lax.broadcasted_iota(jnp.int32, (blk, blk), 0) - lax.broadcasted_iota(jnp.int32, (blk, blk), 1)
            + delta * blk)
    count = jnp.zeros((blk, blk), F32)
    for window, dil in DILATED_CONFIGS:
        assert dil & (dil - 1) == 0
        seen = (dist >= 0) & (dist <= window) & ((dist & (dil - 1)) == 0)
        count = count + jnp.where(seen, 1.0, 0.0)
    return jnp.where(count > 0.0, jnp.log(jnp.maximum(count, 1.0)), NEG)


def _fill_bias_table(table, blk):
    @pl.when((pl.program_id(0) == 0) & (pl.program_id(1) == 0))
    def _():
        for delta in range(table.shape[0]):
            table[delta] = _attn_bias(delta, blk)


def _attn_fwd(qkv, *, batch, seq):
    t, da3 = qkv.shape
    da = da3 // 3
    n_heads = da // HEAD
    blk = _tile(seq, ATTN_BLOCK, 16)
    nq = seq // blk
    sm_scale = HEAD ** -0.5

    def body(q_ref, k_ref, v_ref, o_ref, lse_ref, bias):
        _fill_bias_table(bias, blk)

        def q_step(qi, _):
            rows = pl.ds(pl.multiple_of(qi * blk, blk), blk)
            q = q_ref[rows, :]

            def kv_step(ki, carry):
                m, l, acc = carry
                cols = pl.ds(pl.multiple_of(ki * blk, blk), blk)
                s = lax.dot_general(q, k_ref[cols, :], _DIMS["nt"], preferred_element_type=F32) * sm_scale
                s = s + bias[qi - ki]
                m_new = jnp.maximum(m, jnp.max(s, axis=1, keepdims=True))
                alpha = jnp.exp(m - m_new)
                p = jnp.exp(s - m_new)
                l = alpha * l + jnp.sum(p, axis=1, keepdims=True)
                acc = alpha * acc + jnp.dot(p.astype(BF16), v_ref[cols, :], preferred_element_type=F32)
                return m_new, l, acc

            m, l, acc = lax.fori_loop(0, qi + 1, kv_step, (jnp.full((blk, 1), NEG, F32), jnp.zeros((blk, 1), F32),
                                                           jnp.zeros((blk, HEAD), F32)))
            o_ref[rows, :] = acc / l
            lse_ref[rows, :] = jnp.broadcast_to(m + jnp.log(l), (blk, HEAD))
            return 0

        lax.fori_loop(0, nq, q_step, 0)

    col = lambda off: pl.BlockSpec((seq, HEAD), lambda b, h: (b, off + h))
    return pl.pallas_call(
        body, name="attn_fwd", grid=(batch, n_heads), in_specs=[col(0), col(n_heads), col(2 * n_heads)],
        out_specs=[col(0), col(0)], out_shape=[SDS((t, da), F32), SDS((t, da), F32)],
        scratch_shapes=[pltpu.VMEM((nq, blk, blk), F32)], compiler_params=_params(2),
    )(qkv, qkv, qkv)


def _attn_bwd(qkv, out, lse, d_out, *, batch, seq):
    t, da = out.shape
    n_heads = da // HEAD
    blk = _tile(seq, ATTN_BLOCK, 16)
    nq = seq // blk
    sm_scale = HEAD ** -0.5

    def body(q_ref, k_ref, v_ref, o_ref, lse_ref, do_ref, dq_ref, dk_ref, dv_ref, bias):
        _fill_bias_table(bias, blk)
        dk_ref[...] = jnp.zeros_like(dk_ref)
        dv_ref[...] = jnp.zeros_like(dv_ref)

        def q_step(qi, _):
            rows = pl.ds(pl.multiple_of(qi * blk, blk), blk)
            q = q_ref[rows, :]
            do = do_ref[rows, :]
            do_b = do.astype(BF16)
            lse_q = lse_ref[rows, :][:, :1]
            delta = jnp.sum(do * o_ref[rows, :], axis=1, keepdims=True)

            def kv_step(ki, dq):
                cols = pl.ds(pl.multiple_of(ki * blk, blk), blk)
                k = k_ref[cols, :]
                s = lax.dot_general(q, k, _DIMS["nt"], preferred_element_type=F32) * sm_scale
                p = jnp.exp(s + bias[qi - ki] - lse_q)
                dp = lax.dot_general(do_b, v_ref[cols, :], _DIMS["nt"], preferred_element_type=F32)
                ds = (p * (dp - delta) * sm_scale).astype(BF16)
                dv_ref[cols, :] += lax.dot_general(p.astype(BF16), do_b, _DIMS["tn"], preferred_element_type=F32)
                dk_ref[cols, :] += lax.dot_general(ds, q, _DIMS["tn"], preferred_element_type=F32)
                return dq + jnp.dot(ds, k, preferred_element_type=F32)

            dq_ref[rows, :] = lax.fori_loop(0, qi + 1, kv_step, jnp.zeros((blk, HEAD), F32))
            return 0

        lax.fori_loop(0, nq, q_step, 0)

    col = lambda off: pl.BlockSpec((seq, HEAD), lambda b, h: (b, off + h))
    return pl.pallas_call(
        body, name="attn_bwd", grid=(batch, n_heads),
        in_specs=[col(0), col(n_heads), col(2 * n_heads), col(0), col(0), col(0)], out_specs=[col(0)] * 3,
        out_shape=[SDS((t, da), F32)] * 3, scratch_shapes=[pltpu.VMEM((nq, blk, blk), F32)], compiler_params=_params(2),
    )(qkv, qkv, qkv, out, lse, d_out)


def _shift_down(x, k, row):
    return x if k == 0 else jnp.where(row >= k, pltpu.roll(x, k, axis=0), 0.0)


def _shift_up(x, k, row):
    n = x.shape[0]
    return x if k == 0 else jnp.where(row < n - k, pltpu.roll(x, n - k, axis=0), 0.0)


def _conv_silu_fwd(x, col0, w, *, batch, seq):
    t, c = x.shape[0], w.shape[1]
    first = col0 // HEAD

    def body(x_ref, w_ref, o_ref):
        xv = x_ref[...]
        row = lax.broadcasted_iota(jnp.int32, xv.shape, 0)
        acc = jnp.zeros_like(xv)
        for i in range(CONV_WIDTH):
            acc = acc + w_ref[i:i + 1, :] * _shift_down(xv, CONV_WIDTH - 1 - i, row)
        o_ref[...] = _silu(acc)

    blk = pl.BlockSpec((seq, HEAD), lambda j, b: (b, j))
    return pl.pallas_call(
        body, name="conv_silu_fwd", grid=(c // HEAD, batch),
        in_specs=[pl.BlockSpec((seq, HEAD), lambda j, b: (b, first + j)), pl.BlockSpec((CONV_WIDTH, HEAD), lambda j, b: (0, j))],
        out_specs=blk, out_shape=SDS((t, c), F32), compiler_params=_params(2))(x, w)


def _conv_silu_bwd(x, col0, w, dy, *, batch, seq):
    t, c = x.shape[0], w.shape[1]
    first = col0 // HEAD

    def body(x_ref, w_ref, dy_ref, dx_ref, dw_ref):
        @pl.when(pl.program_id(1) == 0)
        def _():
            dw_ref[...] = jnp.zeros_like(dw_ref)

        xv = x_ref[...]
        row = lax.broadcasted_iota(jnp.int32, xv.shape, 0)
        shifted = [_shift_down(xv, CONV_WIDTH - 1 - i, row) for i in range(CONV_WIDTH)]
        acc = jnp.zeros_like(xv)
        for i in range(CONV_WIDTH):
            acc = acc + w_ref[i:i + 1, :] * shifted[i]
        sg = _sigmoid(acc)
        dc = dy_ref[...] * sg * (1.0 + acc * (1.0 - sg))
        dx = jnp.zeros_like(xv)
        for i in range(CONV_WIDTH):
            dx = dx + w_ref[i:i + 1, :] * _shift_up(dc, CONV_WIDTH - 1 - i, row)
            dw_ref[i:i + 1, :] += jnp.sum(dc * shifted[i], axis=0, keepdims=True)
        dx_ref[...] = dx

    blk = pl.BlockSpec((seq, HEAD), lambda j, b: (b, j))
    wblk = pl.BlockSpec((CONV_WIDTH, HEAD), lambda j, b: (0, j))
    return pl.pallas_call(
        body, name="conv_silu_bwd", grid=(c // HEAD, batch),
        in_specs=[pl.BlockSpec((seq, HEAD), lambda j, b: (b, first + j)), wblk, blk], out_specs=[blk, wblk],
        out_shape=[SDS((t, c), F32), SDS((CONV_WIDTH, c), F32)], compiler_params=_params(2))(x, w, dy)


def _dot(a, b, mode="nn"):
    return lax.dot_general(a.astype(BF16), b.astype(BF16), _DIMS[mode], preferred_element_type=F32)


def _dot3(a, b):
    a_hi, b_hi = a.astype(BF16), b.astype(BF16)
    a_lo, b_lo = (a - a_hi.astype(F32)).astype(BF16), (b - b_hi.astype(F32)).astype(BF16)
    pass_ = lambda x, y: jnp.dot(x, y, preferred_element_type=F32)
    return pass_(a_hi, b_hi) + pass_(a_hi, b_lo) + pass_(a_lo, b_hi)


@jax.custom_vjp
def _nilpotent_inverse(m):
    r = m.shape[0]
    x = jnp.where(lax.broadcasted_iota(jnp.int32, (r, r), 0) == lax.broadcasted_iota(jnp.int32, (r, r), 1), 1.0, 0.0) + m
    p = m
    for _ in range(CHUNK_BITS - 1):
        p = _dot3(p, p)
        x = x + _dot3(x, p)
    return x


def _nilpotent_inverse_fwd(m):
    x = _nilpotent_inverse(m)
    return x, x


def _nilpotent_inverse_bwd(x, g):
    return (_dot(x, _dot(g, x, "nt"), "tn"),)


_nilpotent_inverse.defvjp(_nilpotent_inverse_fwd, _nilpotent_inverse_bwd)


def _dot_nt(a, b):
    return _dot(a, b, "nt")


def _dot_tn(a, b):
    return _dot(a, b, "tn")


def _dn_chunk(head, n_heads, aq, ak, v, z, dbda, a_log, dt_bias, dn_norm, state):
    r = aq.shape[0]
    lane_g = lax.broadcasted_iota(jnp.int32, dbda.shape, 1)
    db = jnp.sum(jnp.where(lane_g == head, dbda, 0.0), axis=1, keepdims=True)
    da = jnp.sum(jnp.where(lane_g == head + n_heads, dbda, 0.0), axis=1, keepdims=True)
    lane_h = lax.broadcasted_iota(jnp.int32, a_log.shape, 1)
    al = jnp.sum(jnp.where(lane_h == head, a_log, 0.0), axis=1, keepdims=True)
    dtb = jnp.sum(jnp.where(lane_h == head, dt_bias, 0.0), axis=1, keepdims=True)
    beta = _sigmoid(db)
    g = -jnp.exp(al) * _softplus(da + dtb)
    q = aq * lax.rsqrt(jnp.sum(aq * aq, axis=1, keepdims=True) + EPS) * (HEAD ** -0.5)
    k = ak * lax.rsqrt(jnp.sum(ak * ak, axis=1, keepdims=True) + EPS)
    ri = lax.broadcasted_iota(jnp.int32, (r, r), 0)
    ci = lax.broadcasted_iota(jnp.int32, (r, r), 1)
    same = (ri >> CHUNK_BITS) == (ci >> CHUNK_BITS)
    incl = same & (ri >= ci)
    g_row = jnp.sum(jnp.where(ri == ci, g, 0.0), axis=0, keepdims=True)
    gc_col = jnp.sum(jnp.where(incl, g_row, 0.0), axis=1, keepdims=True)
    gc_row = jnp.sum(jnp.where(same & (ri <= ci), g, 0.0), axis=0, keepdims=True)
    g_all = jnp.sum(jnp.where(same, g_row, 0.0), axis=1, keepdims=True)
    decay = jnp.where(incl, jnp.exp(jnp.where(incl, gc_col - gc_row, 0.0)), 0.0)
    kb = k * beta
    m = -jnp.where(same & (ri > ci), _dot_nt(kb, k) * decay, 0.0)
    x = _nilpotent_inverse(m)
    egc = jnp.exp(gc_col)
    wu_g = _dot(x, jnp.concatenate([kb * egc, v * beta], axis=1))
    w_g, u_g = wu_g[:, :HEAD], wu_g[:, HEAD:]
    qk = _dot_nt(q, k) * decay
    q_dec = q * egc
    k_dec = k * jnp.exp(g_all - gc_col)
    carry = jnp.exp(g_all)
    v_new, o_state = [], []
    for c in range(r // CHUNK):
        rows = slice(c * CHUNK, (c + 1) * CHUNK)
        v_new.append(u_g[rows] - _dot(w_g[rows], state))
        o_state.append(_dot(q_dec[rows], state))
        state = state * carry[c * CHUNK:c * CHUNK + 1] + _dot_tn(k_dec[rows], v_new[-1])
    o = jnp.concatenate(o_state, axis=0) + _dot(qk, jnp.concatenate(v_new, axis=0))
    o = o * lax.rsqrt(jnp.mean(o * o, axis=1, keepdims=True) + EPS) * dn_norm
    return o * _silu(z), state


def _dn_chunks(head, n_heads):
    return jax.vmap(functools.partial(_dn_chunk, head, n_heads), in_axes=(0, 0, 0, 0, 0, None, None, None, 0))


def _dn_layout(batch, seq, n_heads, small, reverse):
    grp = _tile(seq, DN_ROWS, CHUNK)
    rows = _tile(seq, DN_SEQ_BLOCK, grp)
    n_blocks, per = seq // rows, rows // grp
    at = (lambda j: n_blocks - 1 - j) if reverse else (lambda j: j)
    col = lambda off: pl.BlockSpec((batch, rows, HEAD), lambda j, h: (0, at(j), off + h))
    gates = pl.BlockSpec((batch, rows, 2 * n_heads), lambda j, h: (0, at(j), 0))
    states = pl.BlockSpec((batch, None, per, HEAD, HEAD), lambda j, h: (0, h, at(j), 0, 0))
    full = [pl.BlockSpec(a.shape, lambda j, h: (0, 0)) for a in small]
    return grp, n_blocks, per, col, gates, states, full


def _dn_fwd(y, z, dbda, a_log, dt_bias, dn_norm):
    batch, seq, dd = z.shape
    n_heads = dd // HEAD
    grp, n_blocks, per, col, gates, st_spec, full = _dn_layout(batch, seq, n_heads, (a_log, dt_bias, dn_norm), False)

    def body(q_ref, k_ref, v_ref, z_ref, g_ref, al_ref, dt_ref, nw_ref, o_ref, st_ref, carry):
        head = pl.program_id(1)
        al, dtb, nw = al_ref[...], dt_ref[...], nw_ref[...]

        @pl.when(pl.program_id(0) == 0)
        def _():
            carry[head] = jnp.zeros((batch, HEAD, HEAD), F32)

        def step(n, states):
            rows = pl.ds(pl.multiple_of(n * grp, grp), grp)
            for b in range(batch):
                st_ref[b, n] = states[b]
            out, states = _dn_chunks(head, n_heads)(q_ref[:, rows, :], k_ref[:, rows, :], v_ref[:, rows, :],
                                                    z_ref[:, rows, :], g_ref[:, rows, :], al, dtb, nw, states)
            o_ref[:, rows, :] = out
            return states

        carry[head] = lax.fori_loop(0, per, step, carry[head])

    return pl.pallas_call(
        body, name="dn_fwd", grid=(n_blocks, n_heads),
        in_specs=[col(0), col(n_heads), col(2 * n_heads), col(0), gates, *full], out_specs=[col(0), st_spec],
        out_shape=[SDS((batch, seq, dd), F32), SDS((batch, n_heads, seq // grp, HEAD, HEAD), F32)],
        scratch_shapes=[pltpu.VMEM((n_heads, batch, HEAD, HEAD), F32)], compiler_params=_params(2),
    )(y, y, y, z, dbda, a_log, dt_bias, dn_norm)


def _dn_bwd(y, z, dbda, a_log, dt_bias, dn_norm, states, d_out):
    batch, seq, dd = z.shape
    n_heads = dd // HEAD
    grp, n_blocks, per, col, gates, st_spec, full = _dn_layout(batch, seq, n_heads, (a_log, dt_bias, dn_norm), True)

    def body(q_ref, k_ref, v_ref, z_ref, g_ref, al_ref, dt_ref, nw_ref, st_ref, do_ref,
             dq_ref, dk_ref, dv_ref, dz_ref, dg_ref, dal_ref, ddt_ref, dnw_ref, carry):
        first, head = pl.program_id(0) == 0, pl.program_id(1)
        al, dtb, nw = al_ref[...], dt_ref[...], nw_ref[...]

        @pl.when(first & (head == 0))
        def _():
            dal_ref[...] = jnp.zeros_like(dal_ref)
            ddt_ref[...] = jnp.zeros_like(ddt_ref)
            dnw_ref[...] = jnp.zeros_like(dnw_ref)

        @pl.when(head == 0)
        def _():
            dg_ref[...] = jnp.zeros_like(dg_ref)

        @pl.when(first)
        def _():
            carry[head] = jnp.zeros((batch, HEAD, HEAD), F32)

        def step(i, acc):
            d_states, d_al, d_dt, d_nw = acc
            n = per - 1 - i
            rows = pl.ds(pl.multiple_of(n * grp, grp), grp)
            states = jnp.stack([st_ref[b, n] for b in range(batch)])
            _, vjp = jax.vjp(_dn_chunks(head, n_heads), q_ref[:, rows, :], k_ref[:, rows, :], v_ref[:, rows, :],
                             z_ref[:, rows, :], g_ref[:, rows, :], al, dtb, nw, states)
            gq, gk, gv, gz, gg, gal, gdt, gnw, d_states = vjp((do_ref[:, rows, :], d_states))
            dq_ref[:, rows, :] = gq
            dk_ref[:, rows, :] = gk
            dv_ref[:, rows, :] = gv
            dz_ref[:, rows, :] = gz
            dg_ref[:, rows, :] += gg
            return d_states, d_al + gal, d_dt + gdt, d_nw + gnw

        zero = lambda a: jnp.zeros(a.shape, F32)
        d_states, d_al, d_dt, d_nw = lax.fori_loop(0, per, step, (carry[head], zero(al), zero(dtb), zero(nw)))
        carry[head] = d_states
        dal_ref[...] += d_al
        ddt_ref[...] += d_dt
        dnw_ref[...] += d_nw

    out3 = SDS((batch, seq, dd), F32)
    return pl.pallas_call(
        body, name="dn_bwd", grid=(n_blocks, n_heads),
        in_specs=[col(0), col(n_heads), col(2 * n_heads), col(0), gates, *full, st_spec, col(0)],
        out_specs=[col(0), col(0), col(0), col(0), gates, *full],
        out_shape=[out3] * 4 + [SDS(dbda.shape, F32), SDS(a_log.shape, F32), SDS(dt_bias.shape, F32), SDS(dn_norm.shape, F32)],
        scratch_shapes=[pltpu.VMEM((n_heads, batch, HEAD, HEAD), F32)], compiler_params=_params(2),
    )(y, y, y, z, dbda, a_log, dt_bias, dn_norm, states, d_out)


def _my_slot():
    return 4 * lax.axis_index("x") + 2 * lax.axis_index("y") + lax.axis_index("c")


def _peer(k):
    x, y, c = lax.axis_index("x"), lax.axis_index("y"), lax.axis_index("c")
    return (x ^ (k >> 2), y ^ ((k >> 1) & 1), c ^ (k & 1)), (4 * x + 2 * y + c) ^ k


def _all_gather(name, block, after):
    def body(src, after_ref, dst, send_sems, recv_sems, local_sem):
        me = _my_slot()
        own = pltpu.make_async_copy(src, dst.at[me], local_sem)
        own.start()
        copies = []
        for k in range(1, N_DEV):
            peer, _ = _peer(k)
            copies.append(pltpu.make_async_remote_copy(
                src_ref=src, dst_ref=dst.at[me], send_sem=send_sems.at[k - 1], recv_sem=recv_sems.at[k - 1],
                device_id=peer, device_id_type=MESH))
            copies[-1].start()
        for k in range(1, N_DEV):
            peer, slot = _peer(k)
            pltpu.make_async_remote_copy(
                src_ref=src, dst_ref=dst.at[slot], send_sem=send_sems.at[k - 1], recv_sem=recv_sems.at[k - 1],
                device_id=peer, device_id_type=MESH).wait_recv()
        for cp in copies:
            cp.wait_send()
        own.wait()

    return pl.pallas_call(
        body, name=name, in_specs=[pl.BlockSpec(memory_space=pl.ANY)] * 2, out_specs=pl.BlockSpec(memory_space=pl.ANY),
        out_shape=SDS((N_DEV, *block.shape), block.dtype),
        scratch_shapes=[pltpu.SemaphoreType.DMA((N_DEV - 1,)), pltpu.SemaphoreType.DMA((N_DEV - 1,)), pltpu.SemaphoreType.DMA],
    )(block, after)


def _exchange_slices(name, parts, after):
    def body(src, after_ref, dst, send_sems, recv_sems, local_sem):
        me = _my_slot()
        own = pltpu.make_async_copy(src.at[me], dst.at[me], local_sem)
        own.start()
        copies = []
        for k in range(1, N_DEV):
            peer, slot = _peer(k)
            copies.append(pltpu.make_async_remote_copy(
                src_ref=src.at[slot], dst_ref=dst.at[me], send_sem=send_sems.at[k - 1], recv_sem=recv_sems.at[k - 1],
                device_id=peer, device_id_type=MESH))
            copies[-1].start()
        for k in range(1, N_DEV):
            peer, slot = _peer(k)
            pltpu.make_async_remote_copy(
                src_ref=src.at[me], dst_ref=dst.at[slot], send_sem=send_sems.at[k - 1], recv_sem=recv_sems.at[k - 1],
                device_id=peer, device_id_type=MESH).wait_recv()
        for cp in copies:
            cp.wait_send()
        own.wait()

    return pl.pallas_call(
        body, name=name, in_specs=[pl.BlockSpec(memory_space=pl.ANY)] * 2, out_specs=pl.BlockSpec(memory_space=pl.ANY),
        out_shape=SDS(parts.shape, parts.dtype),
        scratch_shapes=[pltpu.SemaphoreType.DMA((N_DEV - 1,)), pltpu.SemaphoreType.DMA((N_DEV - 1,)), pltpu.SemaphoreType.DMA],
    )(parts, after)


_HBM = pl.BlockSpec(memory_space=pltpu.HBM)
_SEM = pl.BlockSpec(memory_space=pltpu.SEMAPHORE)
_EFFECT = pltpu.SideEffectType.DATAFLOW_SIDE_EFFECTING


def _slot_operand():
    return _my_slot().astype(jnp.int32).reshape(1)


def _col_tile(r, c, bytes_per_element):
    return _tile(c, max(128, (12 * 2 ** 20) // (bytes_per_element * r) // 128 * 128), 128)


def _cast_place(name, block, dtype):
    r, c = block.shape
    tc = _col_tile(r, c, 6)

    def body(me_ref, src_ref, dst_ref):
        dst_ref[...] = src_ref[...].astype(dtype)

    return pl.pallas_call(
        body, name=name, out_shape=SDS((N_DEV, r, c), dtype), compiler_params=_params(1),
        grid_spec=pltpu.PrefetchScalarGridSpec(
            num_scalar_prefetch=1, grid=(c // tc,), in_specs=[pl.BlockSpec((r, tc), lambda i, me: (0, i))],
            out_specs=pl.BlockSpec((None, r, tc), lambda i, me: (me[0], 0, i))),
    )(_slot_operand(), block)


_SIBLING = 1
_SAME_CORE = (2, 4, 6)
_OTHER_CORE = (3, 5, 7)


def _gather_start(name, lands):
    n = len(lands)

    def body(*refs):
        lnds, outs = refs[:n], refs[n:]
        me = _my_slot()
        for i in range(n):
            for k in (*_SAME_CORE, _SIBLING):
                peer, _ = _peer(k)
                pltpu.make_async_remote_copy(
                    src_ref=lnds[i].at[me], dst_ref=lnds[i].at[me], send_sem=outs[2 * i].at[k - 1],
                    recv_sem=outs[2 * i + 1].at[k - 1], device_id=peer, device_id_type=MESH).start()
        outs[-1][...] = jnp.zeros_like(outs[-1])

    res = pl.pallas_call(
        body, name=name, in_specs=[_HBM] * n,
        out_specs=[_SEM] * (2 * n) + [_HBM] * n + [pl.BlockSpec(memory_space=pltpu.VMEM)],
        out_shape=[pltpu.SemaphoreType.DMA((N_DEV - 1,))] * (2 * n) + [pltpu.HBM(a.shape, a.dtype) for a in lands]
        + [SDS((8, 128), F32)],
        input_output_aliases={i: 2 * n + i for i in range(n)},
        compiler_params=pltpu.CompilerParams(has_side_effects=_EFFECT),
    )(*[pltpu.with_memory_space_constraint(a, pltpu.HBM) for a in lands])
    return [(res[2 * i], res[2 * i + 1], res[2 * n + i]) for i in range(n)], res[-1]


def _gather_copy(land_ref, send_ref, recv_ref, k, slot, to):
    return pltpu.make_async_remote_copy(
        src_ref=land_ref.at[slot], dst_ref=land_ref.at[slot], send_sem=send_ref.at[k - 1], recv_sem=recv_ref.at[k - 1],
        device_id=to, device_id_type=MESH)


def _gather_arrived(name, started, after):
    send_sems, recv_sems, land = started

    def body(land_ref, send_ref, recv_ref, after_ref, land_out):
        for k in _SAME_CORE:
            peer, slot = _peer(k)
            _gather_copy(land_ref, send_ref, recv_ref, k, slot, peer).wait_recv()

    return pl.pallas_call(
        body, name=name, in_specs=[_HBM, _SEM, _SEM, pl.BlockSpec(memory_space=pl.ANY)], out_specs=[_HBM],
        out_shape=[pltpu.HBM(land.shape, land.dtype)], input_output_aliases={0: 0},
        compiler_params=pltpu.CompilerParams(has_side_effects=_EFFECT),
    )(land, send_sems, recv_sems, after)[0]


def _gather_forward(name, land):
    def body(land_ref, send_ref, recv_ref, land_out, token):
        sibling, _ = _peer(_SIBLING)
        for j, k in enumerate(_SAME_CORE):
            _, slot = _peer(k)
            _gather_copy(land_ref, send_ref, recv_ref, j + 1, slot, sibling).start()
        token[...] = jnp.zeros_like(token)

    res = pl.pallas_call(
        body, name=name, in_specs=[_HBM], out_specs=[_SEM, _SEM, _HBM, pl.BlockSpec(memory_space=pltpu.VMEM)],
        out_shape=[pltpu.SemaphoreType.DMA((len(_SAME_CORE),))] * 2 + [pltpu.HBM(land.shape, land.dtype), SDS((8, 128), F32)],
        input_output_aliases={0: 2}, compiler_params=pltpu.CompilerParams(has_side_effects=_EFFECT),
    )(land)
    return tuple(res[:3]), res[3]


def _gather_wait(name, started, forwarded, after):
    send_sems, recv_sems, _ = started
    send_fwd, recv_fwd, land = forwarded

    def body(land_ref, send_ref, recv_ref, send2_ref, recv2_ref, after_ref, land_out):
        sibling, slot = _peer(_SIBLING)
        _gather_copy(land_ref, send_ref, recv_ref, _SIBLING, slot, sibling).wait_recv()
        for j, k in enumerate(_OTHER_CORE):
            _, slot = _peer(k)
            _gather_copy(land_ref, send2_ref, recv2_ref, j + 1, slot, sibling).wait_recv()
        for k in (_SIBLING, *_SAME_CORE):
            peer, slot = _peer(k)
            _gather_copy(land_ref, send_ref, recv_ref, k, slot, peer).wait_send()
        for j, k in enumerate(_SAME_CORE):
            _, slot = _peer(k)
            _gather_copy(land_ref, send2_ref, recv2_ref, j + 1, slot, sibling).wait_send()

    return pl.pallas_call(
        body, name=name, in_specs=[_HBM, _SEM, _SEM, _SEM, _SEM, pl.BlockSpec(memory_space=pl.ANY)], out_specs=[_HBM],
        out_shape=[pltpu.HBM(land.shape, land.dtype)], input_output_aliases={0: 0},
        compiler_params=pltpu.CompilerParams(has_side_effects=_EFFECT),
    )(land, send_sems, recv_sems, send_fwd, recv_fwd, after)[0]


def _scatter_start(name, parts):
    land = lax.empty(parts.shape, parts.dtype)

    def body(src, lnd, send_ref, recv_ref, src_out, lnd_out, token):
        me = _my_slot()
        for k in range(1, N_DEV):
            peer, slot = _peer(k)
            pltpu.make_async_remote_copy(
                src_ref=src.at[slot], dst_ref=lnd.at[me], send_sem=send_ref.at[k - 1], recv_sem=recv_ref.at[k - 1],
                device_id=peer, device_id_type=MESH).start()
        token[...] = jnp.zeros_like(token)

    res = pl.pallas_call(
        body, name=name, in_specs=[_HBM, _HBM],
        out_specs=[_SEM, _SEM, _HBM, _HBM, pl.BlockSpec(memory_space=pltpu.VMEM)],
        out_shape=[pltpu.SemaphoreType.DMA((N_DEV - 1,))] * 2 + [pltpu.HBM(parts.shape, parts.dtype)] * 2 + [SDS((8, 128), F32)],
        input_output_aliases={0: 2, 1: 3}, compiler_params=pltpu.CompilerParams(has_side_effects=_EFFECT),
    )(pltpu.with_memory_space_constraint(parts, pltpu.HBM), pltpu.with_memory_space_constraint(land, pltpu.HBM))
    return tuple(res[:4]), res[4]


def _scatter_wait(name, started, after):
    send_sems, recv_sems, parts, land = started

    def body(src_ref, land_ref, send_ref, recv_ref, after_ref, src_out, land_out):
        me = _my_slot()
        for k in range(1, N_DEV):
            peer, slot = _peer(k)
            copy = pltpu.make_async_remote_copy(
                src_ref=src_ref.at[me], dst_ref=land_ref.at[slot], send_sem=send_ref.at[k - 1],
                recv_sem=recv_ref.at[k - 1], device_id=peer, device_id_type=MESH)
            copy.wait_send()
            copy.wait_recv()

    return pl.pallas_call(
        body, name=name, in_specs=[_HBM, _HBM, _SEM, _SEM, pl.BlockSpec(memory_space=pl.ANY)], out_specs=[_HBM, _HBM],
        out_shape=[pltpu.HBM(parts.shape, parts.dtype), pltpu.HBM(land.shape, land.dtype)], input_output_aliases={0: 0, 1: 1},
        compiler_params=pltpu.CompilerParams(has_side_effects=_EFFECT),
    )(parts, land, send_sems, recv_sems, after)


def _adamw(name, landed, own, w, m, v):
    r, c = w.shape
    tc = _col_tile(r, c, 46)
    bc1 = 1.0 / (1.0 - ADAM_B1 ** ADAM_STEP)
    bc2 = 1.0 / (1.0 - ADAM_B2 ** ADAM_STEP)

    def body(me_ref, p_ref, own_ref, w_ref, m_ref, v_ref, g_ref, d_ref, nm_ref, nv_ref):
        me = me_ref[0]
        g = jnp.zeros(w_ref.shape, F32)
        for s in range(N_DEV):
            g = g + jnp.where(me == s, own_ref[...], p_ref[s]).astype(F32)
        nm = ADAM_B1 * m_ref[...] + (1.0 - ADAM_B1) * g
        nv = ADAM_B2 * v_ref[...] + (1.0 - ADAM_B2) * (g * g)
        g_ref[...] = g
        nm_ref[...] = nm
        nv_ref[...] = nv
        d_ref[...] = -ADAM_LR * ((nm * bc1) / (jnp.sqrt(nv * bc2) + ADAM_EPS) + ADAM_WD * w_ref[...])

    blk = pl.BlockSpec((r, tc), lambda i, me: (0, i))
    return pl.pallas_call(
        body, name=name, out_shape=[SDS((r, c), F32)] * 4, compiler_params=_params(1),
        grid_spec=pltpu.PrefetchScalarGridSpec(
            num_scalar_prefetch=1, grid=(c // tc,),
            in_specs=[pl.BlockSpec((N_DEV, r, tc), lambda i, me: (0, 0, i)), pl.BlockSpec((None, r, tc), lambda i, me: (me[0], 0, i)),
                      blk, blk, blk],
            out_specs=[blk] * 4),
    )(_slot_operand(), landed, own, w, m, v)


def _ffn_fwd(name, h, norm, fetch, prefetch, landed, ahead=None):
    n = _rms_fwd(name + "_norm", h, norm)
    wg = fetch(name + "_w_gate", n)
    gate = _ffn_proj(name + "_gate", n, wg)
    sent = prefetch(name + "_w_down", gate) if landed else None
    wu = fetch(name + "_w_up", gate)
    up, act = _ffn_proj(name + "_up", n, wu, gate, after=sent)
    sent = prefetch(ahead, act) if ahead else None
    wd = fetch(name + "_w_down", act)
    return _ffn_down(act, wd, h, 0.5, after=sent), (n, gate, up, act, wg, wu, wd)


def _ffn_bwd(name, h, norm, saved, dh, dy_b, scale_out, emit):
    n, gate, up, act, wg, wu, wd = saved
    sent = emit(name + "_w_down", _wgrad_rows(name + "_dwd", act, dy_b))
    d_gate, d_up = _ffn_bwd_act(dy_b, wd, gate, up, after=sent)
    sent = emit(name + "_w_gate", _wgrad_rows(name + "_dwg", d_gate, n, after=sent))
    sent = emit(name + "_w_up", _wgrad_rows(name + "_dwu", d_up, n, after=sent))
    dn = _dgrad_cols(name + "_dn", (d_gate, d_up), (wg, wu), after=sent)
    return _rms_bwd(name + "_norm_bwd", h, dn, norm, dh, scale_out)


def _local_step(x, target, norms, small, fetch, prefetch, emit, *, batch, seq):
    n1w, nmw, n2w, nfw = norms
    a_log, dt_bias, dn_norm = small
    t, d = x.shape

    h1, saved1 = _ffn_fwd("ffn1", x, n1w, fetch, prefetch, False)
    nm = _rms_fwd("mix_norm", h1, nmw)
    w_in = fetch("w_in", nm)
    p = w_in.shape[1]
    proj = _in_proj(nm, w_in)
    conv_all = fetch("conv_w", proj)
    proj = jnp.swapaxes(proj, 0, 1).reshape(t, N_DEV * p)
    conv_w = jnp.swapaxes(conv_all, 0, 1).reshape(CONV_WIDTH, N_DEV * conv_all.shape[2])
    dd = conv_w.shape[1] // 3
    da = (N_DEV * p - 4 * dd - 2 * (dd // HEAD)) // 3
    qkv = proj[:, :3 * da].astype(BF16)
    z = proj[:, 3 * da + 3 * dd:3 * da + 4 * dd]
    dbda = proj[:, 3 * da + 4 * dd:]
    attn, lse = _attn_fwd(qkv, batch=batch, seq=seq)
    xd = proj[:, 3 * da:3 * da + 3 * dd]
    yd = _conv_silu_fwd(xd, 0, conv_w, batch=batch, seq=seq)
    per_seq = lambda a: a.reshape(batch, seq, a.shape[1])
    dn_in = (per_seq(yd), per_seq(z), per_seq(dbda), a_log, dt_bias, dn_norm)
    dn_out, dn_states = _dn_fwd(*dn_in)
    cat = jnp.concatenate([attn, dn_out.reshape(t, dd)], axis=1).astype(BF16)
    w_out = fetch("w_out", cat)
    w_out2 = w_out.reshape(da + dd, d)
    sent = prefetch("ffn2_w_up", prefetch("ffn2_w_gate", cat))
    h2 = _out_proj(cat, w_out2, h1, after=sent)
    h3, saved2 = _ffn_fwd("ffn2", h2, n2w, fetch, prefetch, True)

    loss, dh3, dh3_b, g_nf = _loss_head(h3, nfw, target, 0.5)
    dh2, dh2_b, g_n2 = _ffn_bwd("ffn2", h2, n2w, saved2, dh3, dh3_b, 1.0, emit)

    sent = emit("w_out", _wgrad_full("dw_out", cat, dh2_b).reshape(w_out.shape))
    d_attn = _dgrad_full("d_attn", dh2_b, w_out2[:da], after=sent)
    d_dn = _dgrad_full("d_dn", dh2_b, w_out2[da:])
    dq, dk, dv = _attn_bwd(qkv, attn, lse, d_attn, batch=batch, seq=seq)
    gq, gk, gv, gz, g_dbda, g_alog, g_dtb, g_dnn = _dn_bwd(*dn_in, dn_states, per_seq(d_dn))
    gq, gk, gv, gz, g_dbda = (a.reshape(t, a.shape[2]) for a in (gq, gk, gv, gz, g_dbda))
    d_xd, g_conv = _conv_silu_bwd(xd, 0, conv_w, jnp.concatenate([gq, gk, gv], axis=1), batch=batch, seq=seq)
    dproj = jnp.concatenate([dq, dk, dv, d_xd, gz, g_dbda], axis=1).astype(BF16)
    dproj = jnp.swapaxes(dproj.reshape(t, N_DEV, p), 0, 1)
    sent = emit("w_in", _wgrad_rows("dw_in", dproj, nm))
    dnm = _dgrad_cols("d_mix_in", (dproj,), (w_in,), after=sent)
    dh1, dh1_b, g_nm = _rms_bwd("mix_norm_bwd", h1, dnm, nmw, dh2, 0.5)

    dx, _, g_n1 = _ffn_bwd("ffn1", x, n1w, saved1, dh1, dh1_b, 1.0, emit)
    return loss, dx, (g_n1, g_nm, g_n2, g_nf), (g_alog, g_dtb, g_dnn), g_conv


def _pack_rows(vectors):
    rows, offsets, r = [], [], 0
    for vec in vectors:
        n = -(-vec.size // 128)
        rows.append(jnp.pad(vec.reshape(-1), (0, n * 128 - vec.size)).reshape(n, 128))
        offsets.append((r, vec.size, vec.shape))
        r += n
    pad = -r % 8
    if pad:
        rows.append(jnp.zeros((pad, 128), F32))
    return jnp.concatenate(rows, axis=0), offsets


def _unpack_rows(packed, offsets):
    return [packed[r:r + -(-size // 128)].reshape(-1)[:size].reshape(shape) for r, size, shape in offsets]


def kernel(x, ffn1_norm, ffn1_w_gate, ffn1_w_up, ffn1_w_down, mix_norm, w_in, conv_w, a_log, dt_bias, dn_norm, w_out, ffn2_norm, ffn2_w_gate, ffn2_w_up, ffn2_w_down, final_norm, loss_target, m_ffn1_norm, m_ffn1_w_gate, m_ffn1_w_up, m_ffn1_w_down, m_mix_norm, m_w_in, m_conv_w, m_a_log, m_dt_bias, m_dn_norm, m_w_out, m_ffn2_norm, m_ffn2_w_gate, m_ffn2_w_up, m_ffn2_w_down, m_final_norm, v_ffn1_norm, v_ffn1_w_gate, v_ffn1_w_up, v_ffn1_w_down, v_mix_norm, v_w_in, v_conv_w, v_a_log, v_dt_bias, v_dn_norm, v_w_out, v_ffn2_norm, v_ffn2_w_gate, v_ffn2_w_up, v_ffn2_w_down, v_final_norm):
    batch, seq, d = x.shape
    t = batch * seq
    big = dict(ffn1_w_gate=(ffn1_w_gate, m_ffn1_w_gate, v_ffn1_w_gate), ffn1_w_up=(ffn1_w_up, m_ffn1_w_up, v_ffn1_w_up),
               ffn1_w_down=(ffn1_w_down, m_ffn1_w_down, v_ffn1_w_down), w_in=(w_in, m_w_in, v_w_in),
               w_out=(w_out, m_w_out, v_w_out), ffn2_w_gate=(ffn2_w_gate, m_ffn2_w_gate, v_ffn2_w_gate),
               ffn2_w_up=(ffn2_w_up, m_ffn2_w_up, v_ffn2_w_up), ffn2_w_down=(ffn2_w_down, m_ffn2_w_down, v_ffn2_w_down))
    by_columns = ("ffn1_w_gate", "ffn1_w_up", "w_in", "ffn2_w_gate", "ffn2_w_up")
    for name in by_columns:
        big[name] = tuple(a.T for a in big[name])
    rep = dict(ffn1_norm=(ffn1_norm, m_ffn1_norm, v_ffn1_norm), mix_norm=(mix_norm, m_mix_norm, v_mix_norm),
               ffn2_norm=(ffn2_norm, m_ffn2_norm, v_ffn2_norm), final_norm=(final_norm, m_final_norm, v_final_norm),
               a_log=(a_log, m_a_log, v_a_log), dt_bias=(dt_bias, m_dt_bias, v_dt_bias), dn_norm=(dn_norm, m_dn_norm, v_dn_norm))

    lands = {"conv_w": _cast_place("place_conv_w", conv_w, F32)}
    lands.update({name: _cast_place("place_" + name, w, BF16) for name, (w, _, _) in big.items()})
    started, token = _gather_start("gather_start", list(lands.values()))
    gathering = dict(zip(lands, started))
    gathered, scattering = {}, {}

    forwarding = {}

    def prefetch(name, after):
        if name not in forwarding:
            land = _gather_arrived("gather_arrived_" + name, gathering[name], after)
            forwarding[name] = _gather_forward("gather_forward_" + name, land)
        return forwarding[name][1]

    def fetch(name, after):
        if name not in gathered:
            prefetch(name, after)
            gathered[name] = _gather_wait("gather_wait_" + name, gathering[name], forwarding[name][0], after)
        return gathered[name]

    def emit(name, grad):
        scattering[name], sent = _scatter_start("scatter_start_" + name, grad)
        return sent

    row = lambda a: a.reshape(1, -1)
    norms = [row(rep[n][0]) for n in ("ffn1_norm", "mix_norm", "ffn2_norm", "final_norm")]
    norms[0] = norms[0] + token[0, 0]
    loss, dx, g_norms, g_small, g_conv = _local_step(
        x.reshape(t, d), loss_target.reshape(t, d), norms, [row(rep[n][0]) for n in ("a_log", "dt_bias", "dn_norm")],
        fetch, prefetch, emit, batch=batch, seq=seq)

    out = {"grad_x": dx.reshape(x.shape)}
    after = dx
    for name in scattering:
        w, m, v = big[name]
        own, landed = _scatter_wait("scatter_wait_" + name, scattering[name], after)
        res = _adamw("adamw_" + name, landed, own, w, m, v)
        after = res[0]
        out["grad_" + name], out["delta_" + name], out["new_m_" + name], out["new_v_" + name] = (
            [a.T for a in res] if name in by_columns else res)
    conv_parts = jnp.swapaxes(g_conv.reshape(CONV_WIDTH, N_DEV, conv_w.shape[1]), 0, 1)
    parts = _exchange_slices("scatter_conv_w", conv_parts, after)
    out["grad_conv_w"], out["delta_conv_w"], out["new_m_conv_w"], out["new_v_conv_w"] = _adamw("adamw_conv_w", parts, parts, conv_w, m_conv_w, v_conv_w)

    rep_names = list(rep)
    g_rep = [*g_norms, *g_small]
    packed_g, offsets = _pack_rows([*g_rep, loss[:, :1]])
    packed = [_pack_rows([*[rep[n][i] for n in rep_names], jnp.zeros((1, 1), F32)])[0] for i in range(3)]
    parts = _all_gather("gather_small_grads", packed_g, out["grad_conv_w"])
    res = [_unpack_rows(a, offsets) for a in _adamw("adamw_small", parts, parts, *packed)]
    for i, name in enumerate(rep_names):
        shape = rep[name][0].shape
        out["grad_" + name], out["delta_" + name], out["new_m_" + name], out["new_v_" + name] = (r[i].reshape(shape) for r in res)
    out["loss"] = res[0][-1].reshape(())

    order = ["ffn1_norm", "ffn1_w_gate", "ffn1_w_up", "ffn1_w_down", "mix_norm", "w_in", "conv_w", "a_log", "dt_bias", "dn_norm",
             "w_out", "ffn2_norm", "ffn2_w_gate", "ffn2_w_up", "ffn2_w_down", "final_norm"]
    return (out["loss"], out["grad_x"], *[out["grad_" + n] for n in order], *[out["delta_" + n] for n in order],
            *[out["new_m_" + n] for n in order], *[out["new_v_" + n] for n in order])
```
